```python
import math
import jax, jax.numpy as jnp
from jax import lax
import numpy as np

D_MODEL = 1024
BATCH = 32
SEQ = 256
DEPTH = 4
DEC_BATCH = 2
DEC_SEQ = 1024
PAST_LEN = 256

GRID_W = 64
N_MIXERS = 4
N_GLA_L = len(range(0, DEPTH, N_MIXERS))
N_NAT_L = len(range(1, DEPTH, N_MIXERS))
N_GM_L = len(range(2, DEPTH, N_MIXERS))
N_SSD_L = len(range(3, DEPTH, N_MIXERS))
N_SUB = 3
N_MOD = 3 * N_SUB
D_FF = 2816
EPS = 1e-6
NEG_INF = -1e30
ROPE_THETA = 10000.0
GLA_H = 4
GLA_DK = 128
GLA_DV = 256
GLA_RANK = 16
GLA_TAU = 16.0
GLA_CHUNK = 16
GLA_IN = 2 * GLA_H * GLA_DK + 2 * GLA_H * GLA_DV
NAT_H = 16
NAT_HD = 64
NAT_WH = 8
NAT_WW = 16
NAT_QB = 16
NAT_KB = 32
GM_DH = 1024
GM_G = 8
GM_CHUNK = 128
GM_CG = GM_DH // GM_G
SSD_DI = 2 * D_MODEL
SSD_P = 64
SSD_H = SSD_DI // SSD_P
SSD_N = 128
SSD_G = 4
SSD_CONV = 3
SSD_CHUNK = 64
SSD_XBC = SSD_DI + 2 * SSD_G * SSD_N
SSD_IN = SSD_DI + SSD_XBC + 2 * SSD_H

kernel_name = "hybrid_flow_trunk_prefix_ctx"


def rmsnorm(x, g):
    x32 = x.astype(jnp.float32)
    y = x32 * lax.rsqrt(jnp.mean(x32 * x32, axis=-1, keepdims=True) + EPS)
    return y.astype(x.dtype) * g


def layernorm(x, g, b):
    x32 = x.astype(jnp.float32)
    xc = x32 - jnp.mean(x32, axis=-1, keepdims=True)
    y = xc * lax.rsqrt(jnp.mean(xc * xc, axis=-1, keepdims=True) + EPS)
    return y.astype(x.dtype) * g + b


def modulation(cond, w_ada, b_ada):
    m = jax.nn.silu(cond) @ w_ada + b_ada
    return m.reshape(cond.shape[0], N_MOD, D_MODEL)


def pre_mod(x, g, mod, k):
    shift = mod[:, 3 * k][:, None]
    scale = mod[:, 3 * k + 1][:, None]
    gate = mod[:, 3 * k + 2][:, None]
    return rmsnorm(x, g) * (1.0 + scale) + shift, gate


def swiglu(h, w_in, w_out):
    a, u = jnp.split(h @ w_in, 2, axis=-1)
    return (jax.nn.silu(a) * u) @ w_out


def rope_2d(x):
    T, dh = x.shape[1], x.shape[-1]
    half = dh // 2
    t = jnp.arange(T)
    inv = ROPE_THETA ** (-jnp.arange(0, half, 2, dtype=jnp.float32) / half)

    def rot(xa, pos):
        ang = pos.astype(jnp.float32)[:, None] * inv
        cos = jnp.cos(ang)[None, :, None, :]
        sin = jnp.sin(ang)[None, :, None, :]
        x1, x2 = jnp.split(xa.astype(jnp.float32), 2, axis=-1)
        return jnp.concatenate([x1 * cos - x2 * sin, x1 * sin + x2 * cos], axis=-1)

    out = jnp.concatenate([rot(x[..., :half], t // GRID_W), rot(x[..., half:], t % GRID_W)], axis=-1)
    return out.astype(x.dtype)


def flip_t(t):
    return jnp.flip(t, axis=1)


def gla_chunked(q, k, v, log_a, s0):
    B_, T, H, _ = q.shape
    dv = v.shape[-1]
    C = GLA_CHUNK
    n = T // C

    def blk(t):
        return t.astype(jnp.float32).reshape(B_, n, C, H, -1).transpose(0, 1, 3, 2, 4)

    q, k, v, la = blk(q), blk(k), blk(v), blk(log_a)
    b = jnp.cumsum(la, axis=3)
    tril = jnp.tril(jnp.ones((C, C), dtype=bool))
    diff = b[:, :, :, :, None, :] - b[:, :, :, None, :, :]
    decay = jnp.exp(jnp.where(tril[:, :, None], diff, -jnp.inf))
    attn = jnp.einsum('bnhid,bnhjd,bnhijd->bnhij', q, k, decay)
    o_intra = jnp.einsum('bnhij,bnhjv->bnhiv', attn, v)
    b_last = b[:, :, :, -1:, :]
    q_dec = q * jnp.exp(b)
    u = jnp.einsum('bnhjd,bnhjv->bnhdv', k * jnp.exp(b_last - b), v)
    g = jnp.exp(b_last[:, :, :, 0])

    def step(s, xs):
        qd, un, gn = xs
        o = jnp.einsum('bhid,bhdv->bhiv', qd, s)
        return gn[..., None] * s + un, o

    s_fin, o_inter = lax.scan(step, s0.astype(jnp.float32),
                              (jnp.swapaxes(q_dec, 0, 1), jnp.swapaxes(u, 0, 1), jnp.swapaxes(g, 0, 1)))
    o = o_intra + jnp.swapaxes(o_inter, 0, 1)
    return o.transpose(0, 1, 3, 2, 4).reshape(B_, T, H, dv), s_fin


def gla_mixer(h, s0, w_in, w_a1, w_a2, b_a, norm_g, w_out, use_rope):
    B_, T, _ = h.shape
    q, k, v, r = jnp.split(h @ w_in, [GLA_H * GLA_DK, 2 * GLA_H * GLA_DK,
                                      2 * GLA_H * GLA_DK + GLA_H * GLA_DV], axis=-1)
    q = q.reshape(B_, T, GLA_H, GLA_DK) * (GLA_DK ** -0.5)
    k = k.reshape(B_, T, GLA_H, GLA_DK)
    v = v.reshape(B_, T, GLA_H, GLA_DV)
    if use_rope:
        q, k = rope_2d(q), rope_2d(k)
    z = jnp.einsum('btd,edr->bter', h, w_a1)
    z = jnp.einsum('bter,erk->btek', z, w_a2) + b_a
    log_a = (jax.nn.log_sigmoid(z.astype(jnp.float32)) / GLA_TAU).reshape(B_, T, 2, GLA_H, GLA_DK)
    o_f, s_f = gla_chunked(q, k, v, log_a[:, :, 0], s0[:, 0])
    o_b, s_b = gla_chunked(flip_t(q), flip_t(k), flip_t(v), flip_t(log_a[:, :, 1]), s0[:, 1])
    o = o_f + flip_t(o_b)
    o = rmsnorm(o, norm_g.reshape(GLA_H, GLA_DV)).astype(h.dtype)
    o = o.reshape(B_, T, GLA_H * GLA_DV) * jax.nn.silu(r)
    return o @ w_out, jnp.stack([s_f, s_b], axis=1)


def nat_tables(rows):
    wh = min(NAT_WH, rows)
    r = np.arange(rows)
    row_idx = np.clip(r - wh // 2, 0, rows - wh)[:, None] + np.arange(wh)
    ncb = GRID_W // NAT_QB
    col_idx = np.clip(np.arange(ncb) * NAT_QB - (NAT_KB - NAT_QB) // 2, 0,
                      GRID_W - NAT_KB)[:, None] + np.arange(NAT_KB)
    qcol = np.arange(ncb)[:, None] * NAT_QB + np.arange(NAT_QB)
    c_start = np.clip(qcol - NAT_WW // 2, 0, GRID_W - NAT_WW)
    kc = col_idx[:, None, :]
    col_ok = (kc >= c_start[..., None]) & (kc < c_start[..., None] + NAT_WW)
    dc = kc - qcol[..., None]
    dr = row_idx - r[:, None]
    full = (rows, ncb, NAT_QB, wh, NAT_KB)
    flat = (rows, ncb, NAT_QB, wh * NAT_KB)
    dr_i = np.broadcast_to(dr[:, None, None, :, None] + NAT_WH - 1, full).reshape(flat)
    dc_i = np.broadcast_to(np.clip(dc + NAT_WW - 1, 0, 2 * NAT_WW - 2)[None, :, :, None, :], full).reshape(flat)
    ok = np.broadcast_to(col_ok[None, :, :, None, :], full).reshape(flat)
    return row_idx, col_idx, dr_i, dc_i, ok


def nat_context(h, w_qkv, w_out):
    B_, S, _ = h.shape
    q, k, v = jnp.split(h @ w_qkv, 3, axis=-1)
    q = q.reshape(B_, S, NAT_H, NAT_HD)
    k = k.reshape(B_, S, NAT_H, NAT_HD)
    v = v.reshape(B_, S, NAT_H, NAT_HD)
    s = jnp.einsum('bqhd,bkhd->bhqk', q, k).astype(jnp.float32) * (NAT_HD ** -0.5)
    p = jax.nn.softmax(s, axis=-1).astype(h.dtype)
    o = jnp.einsum('bhqk,bkhd->bqhd', p, v).reshape(B_, S, D_MODEL)
    return o @ w_out, k.transpose(0, 2, 1, 3), v.transpose(0, 2, 1, 3)


def nat_latent(h, ck, cv, w_qkv, rpb, w_out):
    B_, T, _ = h.shape
    rows = T // GRID_W
    ncb = GRID_W // NAT_QB
    row_idx, col_idx, dr_i, dc_i, ok = nat_tables(rows)
    q, k, v = jnp.split(h @ w_qkv, 3, axis=-1)
    q = q.reshape(B_, rows, ncb, NAT_QB, NAT_H, NAT_HD)
    k = k.reshape(B_, rows, GRID_W, NAT_H, NAT_HD)
    v = v.reshape(B_, rows, GRID_W, NAT_H, NAT_HD)
    ri = row_idx[:, None, :, None]
    ci = col_idx[None, :, None, :]
    kb = k[:, ri, ci].reshape(B_, rows, ncb, -1, NAT_H, NAT_HD)
    vb = v[:, ri, ci].reshape(B_, rows, ncb, -1, NAT_H, NAT_HD)
    nk = kb.shape[3]
    scale = NAT_HD ** -0.5
    s_lat = jnp.einsum('brnqhd,brnkhd->bhrnqk', q, kb).astype(jnp.float32) * scale
    s_lat = s_lat + rpb[:, dr_i, dc_i].astype(jnp.float32)
    s_lat = jnp.where(ok, s_lat, NEG_INF)
    s_ctx = jnp.einsum('brnqhd,bhsd->bhrnqs', q, ck).astype(jnp.float32) * scale
    p = jax.nn.softmax(jnp.concatenate([s_lat, s_ctx], axis=-1), axis=-1).astype(h.dtype)
    o = (jnp.einsum('bhrnqk,brnkhd->brnqhd', p[..., :nk], vb)
         + jnp.einsum('bhrnqs,bhsd->brnqhd', p[..., nk:], cv))
    return o.reshape(B_, T, D_MODEL) @ w_out


def gmlp_mixer(h, w_in, ln_g, ln_b, w_s, b_s, w_out):
    B_, T, _ = h.shape
    u, v = jnp.split(jax.nn.gelu(h @ w_in), 2, axis=-1)
    v = layernorm(v, ln_g, ln_b).reshape(B_, T // GM_CHUNK, GM_CHUNK, GM_G, GM_CG)
    v = jnp.einsum('gpq,bnqgc->bnpgc', w_s, v) + b_s.T[None, None, :, :, None]
    return (u * v.reshape(B_, T, GM_DH)) @ w_out


def dwconv_centred(x, w, b):
    K = w.shape[0]
    T = x.shape[1]
    pad = K // 2
    xp = jnp.pad(x, ((0, 0), (pad, pad), (0, 0)))
    out = xp[:, 0:T] * w[0]
    for i in range(1, K):
        out = out + xp[:, i:i + T] * w[i]
    return out + b


def ssd_chunked(x, dt, a, bm, cm, s0):
    B_, T = x.shape[:2]
    L = SSD_CHUNK
    n = T // L
    E = SSD_H // SSD_G
    f32 = jnp.float32
    x = x.astype(f32).reshape(B_, n, L, SSD_G, E, SSD_P)
    dt = dt.reshape(B_, n, L, SSD_G, E)
    bm = bm.astype(f32).reshape(B_, n, L, SSD_G, SSD_N)
    cm = cm.astype(f32).reshape(B_, n, L, SSD_G, SSD_N)
    cum = jnp.cumsum(dt * a.reshape(SSD_G, E), axis=2)
    tril = jnp.tril(jnp.ones((L, L), dtype=bool))
    seg = cum[:, :, :, None] - cum[:, :, None, :]
    decay = jnp.exp(jnp.where(tril[:, :, None, None], seg, -jnp.inf))
    dtx = x * dt[..., None]
    cb = jnp.einsum('bnigs,bnjgs->bnijg', cm, bm)
    y_diag = jnp.einsum('bnijg,bnijge,bnjgep->bnigep', cb, decay, dtx)
    u = jnp.einsum('bnjgs,bnjge,bnjgep->bngeps', bm, jnp.exp(cum[:, :, -1:] - cum), dtx)
    chunk_decay = jnp.exp(cum[:, :, -1])
    q_decay = jnp.exp(cum)

    def step(s, xs):
        c_n, qd_n, u_n, g_n = xs
        y = jnp.einsum('bigs,bige,bgeps->bigep', c_n, qd_n, s)
        return g_n[..., None, None] * s + u_n, y

    sw = lambda t: jnp.swapaxes(t, 0, 1)
    s_fin, y_off = lax.scan(step, s0.astype(f32).reshape(B_, SSD_G, E, SSD_P, SSD_N),
                            (sw(cm), sw(q_decay), sw(u), sw(chunk_decay)))
    y = y_diag + sw(y_off)
    return y.reshape(B_, T, SSD_H, SSD_P), s_fin.reshape(B_, SSD_H, SSD_P, SSD_N)


def ssd_mixer(h, s0, w_in, conv_w, conv_b, dt_bias, a_log, d_skip, norm_g, w_out):
    B_, T, _ = h.shape
    z, xbc, dt = jnp.split(h @ w_in, [SSD_DI, SSD_DI + SSD_XBC], axis=-1)
    xbc = jax.nn.silu(dwconv_centred(xbc, conv_w, conv_b))
    x, bm, cm = jnp.split(xbc, [SSD_DI, SSD_DI + SSD_G * SSD_N], axis=-1)
    x = x.reshape(B_, T, SSD_H, SSD_P)
    bm = bm.reshape(B_, T, SSD_G, SSD_N)
    cm = cm.reshape(B_, T, SSD_G, SSD_N)
    dt = jax.nn.softplus(dt.astype(jnp.float32).reshape(B_, T, 2, SSD_H) + dt_bias)
    a = -jnp.exp(a_log.astype(jnp.float32))
    y_f, s_f = ssd_chunked(x, dt[:, :, 0], a[0], bm, cm, s0[:, 0])
    y_b, s_b = ssd_chunked(flip_t(x), flip_t(dt[:, :, 1]), a[1], flip_t(bm), flip_t(cm), s0[:, 1])
    y = y_f + flip_t(y_b) + d_skip[:, None] * x.astype(jnp.float32)
    y = rmsnorm(y.reshape(B_, T, SSD_DI) * jax.nn.silu(z.astype(jnp.float32)), norm_g).astype(h.dtype)
    return y @ w_out, jnp.stack([s_f, s_b], axis=1)


def setup_inputs(seed: int = 0) -> dict:
    key = jax.random.key(seed)
    keys = iter(jax.random.split(key, 64))

    def nrm(shape, scale):
        return scale * jax.random.normal(next(keys), shape, jnp.float32)

    def unif(shape, lo, hi):
        return jax.random.uniform(next(keys), shape, jnp.float32, lo, hi)

    D = D_MODEL
    dt0 = jnp.exp(unif((N_SSD_L, 2, SSD_H), math.log(1e-3), math.log(1e-1)))
    return {
        'x_prompt': nrm((BATCH, SEQ, D), 1.0),
        'x_sample': nrm((DEC_BATCH, DEC_SEQ, D), 1.0),
        'state_gla': nrm((DEC_BATCH, N_GLA_L, 2, GLA_H, GLA_DK, GLA_DV), 0.1),
        'cache_nat_k': nrm((DEC_BATCH, N_NAT_L, NAT_H, PAST_LEN, NAT_HD), 1.0),
        'cache_nat_v': nrm((DEC_BATCH, N_NAT_L, NAT_H, PAST_LEN, NAT_HD), 1.0),
        'state_ssd': nrm((DEC_BATCH, N_SSD_L, 2, SSD_H, SSD_P, SSD_N), 0.1),
        'c': nrm((DEC_BATCH, D), 1.0),
        'c_ctx': nrm((D,), 1.0),
        'norm_g': 1.0 + nrm((DEPTH, N_SUB, D), 0.02),
        'w_ada': nrm((DEPTH, D, N_MOD * D), 0.5 * D ** -0.5),
        'b_ada': nrm((DEPTH, N_MOD * D), 0.02),
        'w_ffn_in': nrm((DEPTH, 2, D, 2 * D_FF), D ** -0.5),
        'w_ffn_out': nrm((DEPTH, 2, D_FF, D), D_FF ** -0.5),
        'gla_w_in': nrm((N_GLA_L, D, GLA_IN), D ** -0.5),
        'gla_w_a1': nrm((N_GLA_L, 2, D, GLA_RANK), D ** -0.5),
        'gla_w_a2': nrm((N_GLA_L, 2, GLA_RANK, GLA_H * GLA_DK), GLA_RANK ** -0.5),
        'gla_b_a': nrm((N_GLA_L, 2, GLA_H * GLA_DK), 0.5),
        'gla_norm_g': 1.0 + nrm((N_GLA_L, GLA_H * GLA_DV), 0.02),
        'gla_w_out': nrm((N_GLA_L, GLA_H * GLA_DV, D), (GLA_H * GLA_DV) ** -0.5),
        'nat_w_qkv': nrm((N_NAT_L, D, 3 * D), D ** -0.5),
        'nat_rpb': nrm((N_NAT_L, NAT_H, 2 * NAT_WH - 1, 2 * NAT_WW - 1), 0.02),
        'nat_w_out': nrm((N_NAT_L, D, D), D ** -0.5),
        'gm_w_in': nrm((N_GM_L, D, 2 * GM_DH), D ** -0.5),
        'gm_ln_g': 1.0 + nrm((N_GM_L, GM_DH), 0.02),
        'gm_ln_b': nrm((N_GM_L, GM_DH), 0.02),
        'gm_w_s': nrm((N_GM_L, GM_G, GM_CHUNK, GM_CHUNK), GM_CHUNK ** -0.5),
        'gm_b_s': 1.0 + nrm((N_GM_L, GM_G, GM_CHUNK), 0.02),
        'gm_w_out': nrm((N_GM_L, GM_DH, D), GM_DH ** -0.5),
        'ssd_w_in': nrm((N_SSD_L, D, SSD_IN), D ** -0.5),
        'ssd_conv_w': nrm((N_SSD_L, SSD_CONV, SSD_XBC), SSD_CONV ** -0.5),
        'ssd_conv_b': nrm((N_SSD_L, SSD_XBC), 0.02),
        'ssd_dt_bias': dt0 + jnp.log(-jnp.expm1(-dt0)),
        'ssd_a_log': jnp.log(unif((N_SSD_L, 2, SSD_H), 1.0, 16.0)),
        'ssd_d': 1.0 + nrm((N_SSD_L, SSD_H), 0.1),
        'ssd_norm_g': 1.0 + nrm((N_SSD_L, SSD_DI), 0.02),
        'ssd_w_out': nrm((N_SSD_L, SSD_DI, D), SSD_DI ** -0.5),
        'final_g': 1.0 + nrm((D,), 0.02),
    }


def reference(x_prompt, x_sample, state_gla, cache_nat_k, cache_nat_v, state_ssd, c,
              c_ctx, norm_g, w_ada, b_ada, w_ffn_in, w_ffn_out,
              gla_w_in, gla_w_a1, gla_w_a2, gla_b_a, gla_norm_g, gla_w_out,
              nat_w_qkv, nat_rpb, nat_w_out,
              gm_w_in, gm_ln_g, gm_ln_b, gm_w_s, gm_b_s, gm_w_out,
              ssd_w_in, ssd_conv_w, ssd_conv_b, ssd_dt_bias, ssd_a_log, ssd_d, ssd_norm_g, ssd_w_out,
              final_g):
    xp, xs = x_prompt, x_sample
    new_gla, new_k, new_v, new_ssd = [], [], [], []
    for l in range(DEPTH):
        kind, j = l % N_MIXERS, l // N_MIXERS
        mp = modulation(c_ctx[None], w_ada[l], b_ada[l])
        ms = modulation(c, w_ada[l], b_ada[l])
        hp, gp = pre_mod(xp, norm_g[l, 0], mp, 0)
        hs, gs = pre_mod(xs, norm_g[l, 0], ms, 0)
        xp = xp + 0.5 * gp * swiglu(hp, w_ffn_in[l, 0], w_ffn_out[l, 0])
        xs = xs + 0.5 * gs * swiglu(hs, w_ffn_in[l, 0], w_ffn_out[l, 0])
        hp, gp = pre_mod(xp, norm_g[l, 1], mp, 1)
        hs, gs = pre_mod(xs, norm_g[l, 1], ms, 1)
        if kind == 0:
            zero_state = jnp.zeros((hp.shape[0], 2, GLA_H, GLA_DK, GLA_DV), jnp.float32)
            op, st = gla_mixer(hp, zero_state, gla_w_in[j], gla_w_a1[j], gla_w_a2[j], gla_b_a[j],
                               gla_norm_g[j], gla_w_out[j], False)
            os_, _ = gla_mixer(hs, state_gla[:, j], gla_w_in[j], gla_w_a1[j], gla_w_a2[j], gla_b_a[j],
                               gla_norm_g[j], gla_w_out[j], True)
            new_gla.append(st)
        elif kind == 1:
            op, kc, vc = nat_context(hp, nat_w_qkv[j], nat_w_out[j])
            os_ = nat_latent(hs, cache_nat_k[:, j], cache_nat_v[:, j], nat_w_qkv[j], nat_rpb[j], nat_w_out[j])
            new_k.append(kc)
            new_v.append(vc)
        elif kind == 2:
            op = gmlp_mixer(hp, gm_w_in[j], gm_ln_g[j], gm_ln_b[j], gm_w_s[j], gm_b_s[j], gm_w_out[j])
            os_ = gmlp_mixer(hs, gm_w_in[j], gm_ln_g[j], gm_ln_b[j], gm_w_s[j], gm_b_s[j], gm_w_out[j])
        else:
            zero_state = jnp.zeros((hp.shape[0], 2, SSD_H, SSD_P, SSD_N), jnp.float32)
            op, st = ssd_mixer(hp, zero_state, ssd_w_in[j], ssd_conv_w[j], ssd_conv_b[j], ssd_dt_bias[j],
                               ssd_a_log[j], ssd_d[j], ssd_norm_g[j], ssd_w_out[j])
            os_, _ = ssd_mixer(hs, state_ssd[:, j], ssd_w_in[j], ssd_conv_w[j], ssd_conv_b[j], ssd_dt_bias[j],
                               ssd_a_log[j], ssd_d[j], ssd_norm_g[j], ssd_w_out[j])
            new_ssd.append(st)
        xp = xp + gp * op
        xs = xs + gs * os_
        hp, gp = pre_mod(xp, norm_g[l, 2], mp, 2)
        hs, gs = pre_mod(xs, norm_g[l, 2], ms, 2)
        xp = xp + 0.5 * gp * swiglu(hp, w_ffn_in[l, 1], w_ffn_out[l, 1])
        xs = xs + 0.5 * gs * swiglu(hs, w_ffn_in[l, 1], w_ffn_out[l, 1])
    y_prompt = rmsnorm(xp, final_g)
    y_sample = rmsnorm(xs, final_g)
    return (y_prompt, y_sample, jnp.stack(new_gla, axis=1), jnp.stack(new_k, axis=1),
            jnp.stack(new_v, axis=1), jnp.stack(new_ssd, axis=1))
```

```python
import functools
import math

import jax
import jax.numpy as jnp
import numpy as np
from jax import lax
from jax.experimental import pallas as pl
from jax.experimental.pallas import tpu as pltpu

D_MODEL = 1024
BATCH = 32
SEQ = 256
DEPTH = 4
DEC_BATCH = 2
DEC_SEQ = 1024
N_PROMPT = BATCH * SEQ
N_SAMPLE = DEC_BATCH * DEC_SEQ
N_TOK = N_PROMPT + N_SAMPLE
N_GROUPS = 1 + DEC_BATCH

GRID_W = 64
N_MIXERS = 4
N_SUB = 3
N_MOD = 3 * N_SUB
D_FF = 2816
EPS = 1e-6
NEG_INF = -1e30
ROPE_THETA = 10000.0
GLA_H, GLA_DK, GLA_DV, GLA_RANK, GLA_TAU, GLA_CHUNK = 4, 128, 256, 16, 16.0, 16
GLA_IN = 2 * GLA_H * GLA_DK + 2 * GLA_H * GLA_DV
NAT_H, NAT_HD, NAT_WH, NAT_WW, NAT_QB, NAT_KB = 16, 64, 8, 16, 16, 32
GM_DH, GM_G, GM_CHUNK = 1024, 8, 128
GM_CG = GM_DH // GM_G
SSD_DI = 2 * D_MODEL
SSD_P = 64
SSD_H = SSD_DI // SSD_P
SSD_N, SSD_G, SSD_CONV, SSD_CHUNK = 128, 4, 3, 64
SSD_XBC = SSD_DI + 2 * SSD_G * SSD_N
SSD_IN = SSD_DI + SSD_XBC + 2 * SSD_H

LANE = 128
VMEM_LIMIT = 56 * 1024 * 1024
BF16 = jnp.bfloat16
F32 = jnp.float32

FF_CHUNK = 256
N_FF_CHUNKS = D_FF // FF_CHUNK
TM_FFN = 512
TM_PROJ = 512
ADA_TN = 1152


def _group_of_tile(i, tm):
    n_prompt_tiles = N_PROMPT // tm
    return jnp.where(i < n_prompt_tiles, 0, 1 + (i - n_prompt_tiles) // (DEC_SEQ // tm))


def _resident(shape):
    nd = len(shape)
    return pl.BlockSpec(shape, lambda i: (0,) * nd, pipeline_mode=pl.Buffered(1))


def _premod(x, g, mod_ref, k):
    shift = mod_ref[3 * k:3 * k + 1, :]
    scale = mod_ref[3 * k + 1:3 * k + 2, :]
    gate = mod_ref[3 * k + 2:3 * k + 3, :]
    ms = jnp.mean(x * x, axis=-1, keepdims=True)
    h = x * lax.rsqrt(ms + EPS) * g
    return h * (1.0 + scale) + shift, gate


def _ada_kernel(cond_ref, w_ref, b_ref, o_ref):
    cnd = cond_ref[...]
    s = (cnd * jax.nn.sigmoid(cnd)).astype(BF16)
    o_ref[...] = jnp.dot(s, w_ref[...].astype(BF16), preferred_element_type=F32) + b_ref[...]


def _modulation_all(c, c_ctx, w_ada, b_ada):
    rows = 8
    cond = jnp.concatenate([c_ctx[None], c, jnp.zeros((rows - N_GROUPS, D_MODEL), F32)], axis=0)
    n_out = N_MOD * D_MODEL
    out = pl.pallas_call(
        _ada_kernel,
        grid=(DEPTH, n_out // ADA_TN),
        in_specs=[
            pl.BlockSpec((rows, D_MODEL), lambda l, j: (0, 0)),
            pl.BlockSpec((None, D_MODEL, ADA_TN), lambda l, j: (l, 0, j)),
            pl.BlockSpec((None, 1, ADA_TN), lambda l, j: (l, 0, j)),
        ],
        out_specs=pl.BlockSpec((None, rows, ADA_TN), lambda l, j: (l, 0, j)),
        out_shape=jax.ShapeDtypeStruct((DEPTH, rows, n_out), F32),
        compiler_params=pltpu.CompilerParams(dimension_semantics=("arbitrary", "arbitrary")),
        name="ada_modulation",
    )(cond, w_ada, b_ada.reshape(DEPTH, 1, n_out))
    return out[:, :N_GROUPS].reshape(DEPTH, N_GROUPS, N_MOD, D_MODEL)


def _ffn_kernel(x_ref, mod_ref, g_ref, win_ref, wout_ref, o_ref, acc_ref, *, k):
    x = x_ref[...]
    h, gate = _premod(x, g_ref[...], mod_ref, k)
    hb = h.astype(BF16)
    for j in range(N_FF_CHUNKS):
        a = jnp.dot(hb, win_ref[j], preferred_element_type=F32)
        u = jnp.dot(hb, win_ref[N_FF_CHUNKS + j], preferred_element_type=F32)
        t = (a * jax.nn.sigmoid(a) * u).astype(BF16)
        p = jnp.dot(t, wout_ref[j], preferred_element_type=F32)
        if j == 0:
            acc_ref[...] = p
        else:
            acc_ref[...] += p
    o_ref[...] = x + 0.5 * gate * acc_ref[...]


def _ffn(x, mod_l, g, w_in, w_out, k):
    tm = TM_FFN
    win = w_in.astype(BF16).reshape(D_MODEL, 2 * N_FF_CHUNKS, FF_CHUNK).transpose(1, 0, 2)
    wout = w_out.astype(BF16).reshape(N_FF_CHUNKS, FF_CHUNK, D_MODEL)
    return pl.pallas_call(
        functools.partial(_ffn_kernel, k=k),
        grid=(N_TOK // tm,),
        in_specs=[
            pl.BlockSpec((tm, D_MODEL), lambda i: (i, 0)),
            pl.BlockSpec((None, N_MOD, D_MODEL), lambda i: (_group_of_tile(i, tm), 0, 0)),
            pl.BlockSpec((1, D_MODEL), lambda i: (0, 0)),
            _resident((2 * N_FF_CHUNKS, D_MODEL, FF_CHUNK)),
            _resident((N_FF_CHUNKS, FF_CHUNK, D_MODEL)),
        ],
        out_specs=pl.BlockSpec((tm, D_MODEL), lambda i: (i, 0)),
        out_shape=jax.ShapeDtypeStruct((N_TOK, D_MODEL), F32),
        scratch_shapes=[pltpu.VMEM((tm, D_MODEL), F32)],
        compiler_params=pltpu.CompilerParams(dimension_semantics=("arbitrary",),
                                             vmem_limit_bytes=VMEM_LIMIT),
        name="ffn_swiglu",
    )(x, mod_l, g.reshape(1, D_MODEL), win, wout)


def _proj_in_kernel(x_ref, mod_ref, g_ref, w_ref, o_ref, *, tn, act):
    h, _ = _premod(x_ref[...], g_ref[...], mod_ref, 1)
    hb = h.astype(BF16)
    for j in range(w_ref.shape[1] // tn):
        y = jnp.dot(hb, w_ref[:, j * tn:(j + 1) * tn], preferred_element_type=F32)
        if act == "gelu":
            y = jax.nn.gelu(y)
        o_ref[:, j * tn:(j + 1) * tn] = y


def _proj_in(x, mod_l, g, w, tn, act=None):
    tm = TM_PROJ
    n_out = w.shape[1]
    return pl.pallas_call(
        functools.partial(_proj_in_kernel, tn=tn, act=act),
        grid=(N_TOK // tm,),
        in_specs=[
            pl.BlockSpec((tm, D_MODEL), lambda i: (i, 0)),
            pl.BlockSpec((None, N_MOD, D_MODEL), lambda i: (_group_of_tile(i, tm), 0, 0)),
            pl.BlockSpec((1, D_MODEL), lambda i: (0, 0)),
            _resident((D_MODEL, n_out)),
        ],
        out_specs=pl.BlockSpec((tm, n_out), lambda i: (i, 0)),
        out_shape=jax.ShapeDtypeStruct((N_TOK, n_out), F32),
        compiler_params=pltpu.CompilerParams(dimension_semantics=("arbitrary",),
                                             vmem_limit_bytes=VMEM_LIMIT),
        name="mixer_proj_in",
    )(x, mod_l, g.reshape(1, D_MODEL), w.astype(BF16))


def _proj_out_kernel(x_ref, o_in_ref, mod_ref, w_ref, o_ref):
    gate = mod_ref[5:6, :]
    y = jnp.dot(o_in_ref[...].astype(BF16), w_ref[...], preferred_element_type=F32)
    o_ref[...] = x_ref[...] + gate * y


def _proj_out(x, o_in, mod_l, w):
    tm = TM_PROJ
    kdim = w.shape[0]
    return pl.pallas_call(
        _proj_out_kernel,
        grid=(N_TOK // tm,),
        in_specs=[
            pl.BlockSpec((tm, D_MODEL), lambda i: (i, 0)),
            pl.BlockSpec((tm, kdim), lambda i: (i, 0)),
            pl.BlockSpec((None, N_MOD, D_MODEL), lambda i: (_group_of_tile(i, tm), 0, 0)),
            _resident((kdim, D_MODEL)),
        ],
        out_specs=pl.BlockSpec((tm, D_MODEL), lambda i: (i, 0)),
        out_shape=jax.ShapeDtypeStruct((N_TOK, D_MODEL), F32),
        compiler_params=pltpu.CompilerParams(dimension_semantics=("arbitrary",),
                                             vmem_limit_bytes=VMEM_LIMIT),
        name="mixer_proj_out",
    )(x, o_in, mod_l, w.astype(BF16))


def _final_norm_kernel(x_ref, g_ref, o_ref):
    x = x_ref[...]
    ms = jnp.mean(x * x, axis=-1, keepdims=True)
    o_ref[...] = x * lax.rsqrt(ms + EPS) * g_ref[...]


def _final_norm(x, g):
    tm = 1024
    return pl.pallas_call(
        _final_norm_kernel,
        grid=(N_TOK // tm,),
        in_specs=[pl.BlockSpec((tm, D_MODEL), lambda i: (i, 0)),
                  pl.BlockSpec((1, D_MODEL), lambda i: (0, 0))],
        out_specs=pl.BlockSpec((tm, D_MODEL), lambda i: (i, 0)),
        out_shape=jax.ShapeDtypeStruct((N_TOK, D_MODEL), F32),
        name="final_rmsnorm",
    )(x, g.reshape(1, D_MODEL))


def _split_streams(y):
    return y[:N_PROMPT].reshape(BATCH, SEQ, -1), y[N_PROMPT:].reshape(DEC_BATCH, DEC_SEQ, -1)


def _join_streams(yp, ys):
    return jnp.concatenate([yp.reshape(N_PROMPT, -1), ys.reshape(N_SAMPLE, -1)], axis=0)


def _rmsnorm(x, g):
    return x * lax.rsqrt(jnp.mean(x * x, axis=-1, keepdims=True) + EPS) * g


def _flip_t(t):
    return jnp.flip(t, axis=1)


def _rope_2d(x):
    T, dh = x.shape[1], x.shape[-1]
    half = dh // 2
    t = jnp.arange(T)
    inv = ROPE_THETA ** (-jnp.arange(0, half, 2, dtype=F32) / half)

    def rot(xa, pos):
        ang = pos.astype(F32)[:, None] * inv
        cos = jnp.cos(ang)[None, :, None, :]
        sin = jnp.sin(ang)[None, :, None, :]
        x1, x2 = jnp.split(xa, 2, axis=-1)
        return jnp.concatenate([x1 * cos - x2 * sin, x1 * sin + x2 * cos], axis=-1)

    return jnp.concatenate([rot(x[..., :half], t // GRID_W), rot(x[..., half:], t % GRID_W)], axis=-1)


def _gla_chunked(q, k, v, log_a, s0):
    B_, T, H, _ = q.shape
    dv = v.shape[-1]
    C = GLA_CHUNK
    n = T // C

    def blk(t):
        return t.reshape(B_, n, C, H, -1).transpose(0, 1, 3, 2, 4)

    q, k, v, la = blk(q), blk(k), blk(v), blk(log_a)
    b = jnp.cumsum(la, axis=3)
    tril = jnp.tril(jnp.ones((C, C), dtype=bool))
    diff = b[:, :, :, :, None, :] - b[:, :, :, None, :, :]
    decay = jnp.exp(jnp.where(tril[:, :, None], diff, -jnp.inf))
    attn = jnp.einsum('bnhid,bnhjd,bnhijd->bnhij', q, k, decay)
    o_intra = jnp.einsum('bnhij,bnhjv->bnhiv', attn, v)
    b_last = b[:, :, :, -1:, :]
    q_dec = q * jnp.exp(b)
    u = jnp.einsum('bnhjd,bnhjv->bnhdv', k * jnp.exp(b_last - b), v)
    g = jnp.exp(b_last[:, :, :, 0])

    def step(s, xs):
        qd, un, gn = xs
        o = jnp.einsum('bhid,bhdv->bhiv', qd, s)
        return gn[..., None] * s + un, o

    s_fin, o_inter = lax.scan(step, s0,
                              (jnp.swapaxes(q_dec, 0, 1), jnp.swapaxes(u, 0, 1), jnp.swapaxes(g, 0, 1)))
    o = o_intra + jnp.swapaxes(o_inter, 0, 1)
    return o.transpose(0, 1, 3, 2, 4).reshape(B_, T, H, dv), s_fin


def _gla_core(y, s0, w_a2, b_a, norm_g, use_rope):
    B_, T, _ = y.shape
    nq = GLA_H * GLA_DK
    nv = GLA_H * GLA_DV
    q = y[..., :nq].reshape(B_, T, GLA_H, GLA_DK) * (GLA_DK ** -0.5)
    k = y[..., nq:2 * nq].reshape(B_, T, GLA_H, GLA_DK)
    v = y[..., 2 * nq:2 * nq + nv].reshape(B_, T, GLA_H, GLA_DV)
    r = y[..., 2 * nq + nv:2 * nq + 2 * nv]
    za = y[..., GLA_IN:GLA_IN + 2 * GLA_RANK].reshape(B_, T, 2, GLA_RANK)
    if use_rope:
        q, k = _rope_2d(q), _rope_2d(k)
    z = jnp.einsum('bter,erk->btek', za, w_a2) + b_a
    log_a = (jax.nn.log_sigmoid(z) / GLA_TAU).reshape(B_, T, 2, GLA_H, GLA_DK)
    o_f, s_f = _gla_chunked(q, k, v, log_a[:, :, 0], s0[:, 0])
    o_b, s_b = _gla_chunked(_flip_t(q), _flip_t(k), _flip_t(v), _flip_t(log_a[:, :, 1]), s0[:, 1])
    o = o_f + _flip_t(o_b)
    o = _rmsnorm(o, norm_g.reshape(GLA_H, GLA_DV))
    o = o.reshape(B_, T, nv) * jax.nn.silu(r)
    return o, jnp.stack([s_f, s_b], axis=1)


def _nat_tables(rows):
    wh = min(NAT_WH, rows)
    r = np.arange(rows)
    row_idx = np.clip(r - wh // 2, 0, rows - wh)[:, None] + np.arange(wh)
    ncb = GRID_W // NAT_QB
    col_idx = np.clip(np.arange(ncb) * NAT_QB - (NAT_KB - NAT_QB) // 2, 0,
                      GRID_W - NAT_KB)[:, None] + np.arange(NAT_KB)
    qcol = np.arange(ncb)[:, None] * NAT_QB + np.arange(NAT_QB)
    c_start = np.clip(qcol - NAT_WW // 2, 0, GRID_W - NAT_WW)
    kc = col_idx[:, None, :]
    col_ok = (kc >= c_start[..., None]) & (kc < c_start[..., None] + NAT_WW)
    dc = kc - qcol[..., None]
    dr = row_idx - r[:, None]
    full = (rows, ncb, NAT_QB, wh, NAT_KB)
    flat = (rows, ncb, NAT_QB, wh * NAT_KB)
    dr_i = np.broadcast_to(dr[:, None, None, :, None] + NAT_WH - 1, full).reshape(flat)
    dc_i = np.broadcast_to(np.clip(dc + NAT_WW - 1, 0, 2 * NAT_WW - 2)[None, :, :, None, :], full).reshape(flat)
    ok = np.broadcast_to(col_ok[None, :, :, None, :], full).reshape(flat)
    return row_idx, col_idx, dr_i, dc_i, ok


def _nat_context_core(y):
    B_, S, _ = y.shape
    q, k, v = jnp.split(y, 3, axis=-1)
    q = q.reshape(B_, S, NAT_H, NAT_HD)
    k = k.reshape(B_, S, NAT_H, NAT_HD)
    v = v.reshape(B_, S, NAT_H, NAT_HD)
    s = jnp.einsum('bqhd,bkhd->bhqk', q, k) * (NAT_HD ** -0.5)
    p = jax.nn.softmax(s, axis=-1)
    o = jnp.einsum('bhqk,bkhd->bqhd', p, v).reshape(B_, S, D_MODEL)
    return o, k.transpose(0, 2, 1, 3), v.transpose(0, 2, 1, 3)


def _nat_latent_core(y, ck, cv, rpb):
    B_, T, _ = y.shape
    rows = T // GRID_W
    ncb = GRID_W // NAT_QB
    row_idx, col_idx, dr_i, dc_i, ok = _nat_tables(rows)
    q, k, v = jnp.split(y, 3, axis=-1)
    q = q.reshape(B_, rows, ncb, NAT_QB, NAT_H, NAT_HD)
    k = k.reshape(B_, rows, GRID_W, NAT_H, NAT_HD)
    v = v.reshape(B_, rows, GRID_W, NAT_H, NAT_HD)
    ri = row_idx[:, None, :, None]
    ci = col_idx[None, :, None, :]
    kb = k[:, ri, ci].reshape(B_, rows, ncb, -1, NAT_H, NAT_HD)
    vb = v[:, ri, ci].reshape(B_, rows, ncb, -1, NAT_H, NAT_HD)
    nk = kb.shape[3]
    scale = NAT_HD ** -0.5
    s_lat = jnp.einsum('brnqhd,brnkhd->bhrnqk', q, kb) * scale
    s_lat = s_lat + rpb[:, dr_i, dc_i]
    s_lat = jnp.where(ok, s_lat, NEG_INF)
    s_ctx = jnp.einsum('brnqhd,bhsd->bhrnqs', q, ck) * scale
    p = jax.nn.softmax(jnp.concatenate([s_lat, s_ctx], axis=-1), axis=-1)
    o = (jnp.einsum('bhrnqk,brnkhd->brnqhd', p[..., :nk], vb)
         + jnp.einsum('bhrnqs,bhsd->brnqhd', p[..., nk:], cv))
    return o.reshape(B_, T, D_MODEL)


def _gmlp_core(y, ln_g, ln_b, w_s, b_s):
    B_, T, _ = y.shape
    u, v = jnp.split(y, 2, axis=-1)
    vc = v - jnp.mean(v, axis=-1, keepdims=True)
    v = vc * lax.rsqrt(jnp.mean(vc * vc, axis=-1, keepdims=True) + EPS) * ln_g + ln_b
    v = v.reshape(B_, T // GM_CHUNK, GM_CHUNK, GM_G, GM_CG)
    v = jnp.einsum('gpq,bnqgc->bnpgc', w_s, v) + b_s.T[None, None, :, :, None]
    return u * v.reshape(B_, T, GM_DH)


def _dwconv_centred(x, w, b):
    K = w.shape[0]
    T = x.shape[1]
    pad = K // 2
    xp = jnp.pad(x, ((0, 0), (pad, pad), (0, 0)))
    out = xp[:, 0:T] * w[0]
    for i in range(1, K):
        out = out + xp[:, i:i + T] * w[i]
    return out + b


def _ssd_chunked(x, dt, a, bm, cm, s0):
    B_, T = x.shape[:2]
    L = SSD_CHUNK
    n = T // L
    E = SSD_H // SSD_G
    x = x.reshape(B_, n, L, SSD_G, E, SSD_P)
    dt = dt.reshape(B_, n, L, SSD_G, E)
    bm = bm.reshape(B_, n, L, SSD_G, SSD_N)
    cm = cm.reshape(B_, n, L, SSD_G, SSD_N)
    cum = jnp.cumsum(dt * a.reshape(SSD_G, E), axis=2)
    tril = jnp.tril(jnp.ones((L, L), dtype=bool))
    seg = cum[:, :, :, None] - cum[:, :, None, :]
    decay = jnp.exp(jnp.where(tril[:, :, None, None], seg, -jnp.inf))
    dtx = x * dt[..., None]
    cb = jnp.einsum('bnigs,bnjgs->bnijg', cm, bm)
    y_diag = jnp.einsum('bnijg,bnijge,bnjgep->bnigep', cb, decay, dtx)
    u = jnp.einsum('bnjgs,bnjge,bnjgep->bngeps', bm, jnp.exp(cum[:, :, -1:] - cum), dtx)
    chunk_decay = jnp.exp(cum[:, :, -1])
    q_decay = jnp.exp(cum)

    def step(s, xs):
        c_n, qd_n, u_n, g_n = xs
        y = jnp.einsum('bigs,bige,bgeps->bigep', c_n, qd_n, s)
        return g_n[..., None, None] * s + u_n, y

    sw = lambda t: jnp.swapaxes(t, 0, 1)
    s_fin, y_off = lax.scan(step, s0.reshape(B_, SSD_G, E, SSD_P, SSD_N),
                            (sw(cm), sw(q_decay), sw(u), sw(chunk_decay)))
    y = y_diag + sw(y_off)
    return y.reshape(B_, T, SSD_H, SSD_P), s_fin.reshape(B_, SSD_H, SSD_P, SSD_N)


def _ssd_core(y, s0, conv_w, conv_b, dt_bias, a_log, d_skip, norm_g):
    B_, T, _ = y.shape
    z = y[..., :SSD_DI]
    xbc = y[..., SSD_DI:SSD_DI + SSD_XBC]
    dt = y[..., SSD_DI + SSD_XBC:SSD_IN]
    xbc = jax.nn.silu(_dwconv_centred(xbc, conv_w, conv_b))
    x = xbc[..., :SSD_DI].reshape(B_, T, SSD_H, SSD_P)
    bm = xbc[..., SSD_DI:SSD_DI + SSD_G * SSD_N].reshape(B_, T, SSD_G, SSD_N)
    cm = xbc[..., SSD_DI + SSD_G * SSD_N:].reshape(B_, T, SSD_G, SSD_N)
    dt = jax.nn.softplus(dt.reshape(B_, T, 2, SSD_H) + dt_bias)
    a = -jnp.exp(a_log)
    y_f, s_f = _ssd_chunked(x, dt[:, :, 0], a[0], bm, cm, s0[:, 0])
    y_b, s_b = _ssd_chunked(_flip_t(x), _flip_t(dt[:, :, 1]), a[1], _flip_t(bm), _flip_t(cm), s0[:, 1])
    yy = y_f + _flip_t(y_b) + d_skip[:, None] * x
    yy = _rmsnorm(yy.reshape(B_, T, SSD_DI) * jax.nn.silu(z), norm_g)
    return yy, jnp.stack([s_f, s_b], axis=1)


def _pad_cols(w, mult):
    n = w.shape[1]
    n_pad = -n % mult
    return jnp.pad(w, ((0, 0), (0, n_pad))) if n_pad else w


def kernel(x_prompt, x_sample, state_gla, cache_nat_k, cache_nat_v, state_ssd, c,
           c_ctx, norm_g, w_ada, b_ada, w_ffn_in, w_ffn_out,
           gla_w_in, gla_w_a1, gla_w_a2, gla_b_a, gla_norm_g, gla_w_out,
           nat_w_qkv, nat_rpb, nat_w_out,
           gm_w_in, gm_ln_g, gm_ln_b, gm_w_s, gm_b_s, gm_w_out,
           ssd_w_in, ssd_conv_w, ssd_conv_b, ssd_dt_bias, ssd_a_log, ssd_d, ssd_norm_g, ssd_w_out,
           final_g):
    x = _join_streams(x_prompt, x_sample)
    mod = _modulation_all(c, c_ctx, w_ada, b_ada)
    new_gla, new_k, new_v, new_ssd = [], [], [], []
    for l in range(DEPTH):
        kind, j = l % N_MIXERS, l // N_MIXERS
        x = _ffn(x, mod[l], norm_g[l, 0], w_ffn_in[l, 0], w_ffn_out[l, 0], 0)
        if kind == 0:
            w = jnp.concatenate([gla_w_in[j], gla_w_a1[j, 0], gla_w_a1[j, 1]], axis=1)
            y = _proj_in(x, mod[l], norm_g[l, 1], _pad_cols(w, 640), 640)
            yp, ys = _split_streams(y)
            zero_state = jnp.zeros((BATCH, 2, GLA_H, GLA_DK, GLA_DV), F32)
            op, st = _gla_core(yp, zero_state, gla_w_a2[j], gla_b_a[j], gla_norm_g[j], False)
            os_, _ = _gla_core(ys, state_gla[:, j], gla_w_a2[j], gla_b_a[j], gla_norm_g[j], True)
            new_gla.append(st)
            w_out = gla_w_out[j]
        elif kind == 1:
            y = _proj_in(x, mod[l], norm_g[l, 1], nat_w_qkv[j], 768)
            yp, ys = _split_streams(y)
            op, kc, vc = _nat_context_core(yp)
            os_ = _nat_latent_core(ys, cache_nat_k[:, j], cache_nat_v[:, j], nat_rpb[j])
            new_k.append(kc)
            new_v.append(vc)
            w_out = nat_w_out[j]
        elif kind == 2:
            y = _proj_in(x, mod[l], norm_g[l, 1], gm_w_in[j], 512, act="gelu")
            yp, ys = _split_streams(y)
            op = _gmlp_core(yp, gm_ln_g[j], gm_ln_b[j], gm_w_s[j], gm_b_s[j])
            os_ = _gmlp_core(ys, gm_ln_g[j], gm_ln_b[j], gm_w_s[j], gm_b_s[j])
            w_out = gm_w_out[j]
        else:
            y = _proj_in(x, mod[l], norm_g[l, 1], _pad_cols(ssd_w_in[j], 768), 768)
            yp, ys = _split_streams(y)
            zero_state = jnp.zeros((BATCH, 2, SSD_H, SSD_P, SSD_N), F32)
            op, st = _ssd_core(yp, zero_state, ssd_conv_w[j], ssd_conv_b[j], ssd_dt_bias[j],
                               ssd_a_log[j], ssd_d[j], ssd_norm_g[j])
            os_, _ = _ssd_core(ys, state_ssd[:, j], ssd_conv_w[j], ssd_conv_b[j], ssd_dt_bias[j],
                               ssd_a_log[j], ssd_d[j], ssd_norm_g[j])
            new_ssd.append(st)
            w_out = ssd_w_out[j]
        x = _proj_out(x, _join_streams(op, os_), mod[l], w_out)
        x = _ffn(x, mod[l], norm_g[l, 2], w_ffn_in[l, 1], w_ffn_out[l, 1], 2)
    yn = _final_norm(x, final_g)
    y_prompt, y_sample = _split_streams(yn)
    return (y_prompt, y_sample, jnp.stack(new_gla, axis=1), jnp.stack(new_k, axis=1),
            jnp.stack(new_v, axis=1), jnp.stack(new_ssd, axis=1))
```

```python
import functools
import math

import jax
import jax.numpy as jnp
import numpy as np
from jax import lax
from jax.experimental import pallas as pl
from jax.experimental.pallas import tpu as pltpu

D_MODEL = 1024
BATCH = 32
SEQ = 256
DEPTH = 4
DEC_BATCH = 2
DEC_SEQ = 1024
N_PROMPT = BATCH * SEQ
N_SAMPLE = DEC_BATCH * DEC_SEQ
N_TOK = N_PROMPT + N_SAMPLE
N_GROUPS = 1 + DEC_BATCH

GRID_W = 64
N_MIXERS = 4
N_SUB = 3
N_MOD = 3 * N_SUB
D_FF = 2816
EPS = 1e-6
NEG_INF = -1e30
ROPE_THETA = 10000.0
GLA_H, GLA_DK, GLA_DV, GLA_RANK, GLA_TAU, GLA_CHUNK = 4, 128, 256, 16, 16.0, 16
GLA_IN = 2 * GLA_H * GLA_DK + 2 * GLA_H * GLA_DV
NAT_H, NAT_HD, NAT_WH, NAT_WW, NAT_QB, NAT_KB = 16, 64, 8, 16, 16, 32
GM_DH, GM_G, GM_CHUNK = 1024, 8, 128
GM_CG = GM_DH // GM_G
SSD_DI = 2 * D_MODEL
SSD_P = 64
SSD_H = SSD_DI // SSD_P
SSD_N, SSD_G, SSD_CONV, SSD_CHUNK = 128, 4, 3, 64
SSD_XBC = SSD_DI + 2 * SSD_G * SSD_N
SSD_IN = SSD_DI + SSD_XBC + 2 * SSD_H

LANE = 128
VMEM_LIMIT = 56 * 1024 * 1024
BF16 = jnp.bfloat16
F32 = jnp.float32

FF_CHUNK = 256
N_FF_CHUNKS = D_FF // FF_CHUNK
TM_FFN = 512
TM_PROJ = 512
ADA_TN = 1152


def _group_of_tile(i, tm):
    n_prompt_tiles = N_PROMPT // tm
    return jnp.where(i < n_prompt_tiles, 0, 1 + (i - n_prompt_tiles) // (DEC_SEQ // tm))


def _resident(shape):
    nd = len(shape)
    return pl.BlockSpec(shape, lambda i: (0,) * nd, pipeline_mode=pl.Buffered(1))


def _premod(x, g, mod_ref, k):
    shift = mod_ref[3 * k:3 * k + 1, :]
    scale = mod_ref[3 * k + 1:3 * k + 2, :]
    gate = mod_ref[3 * k + 2:3 * k + 3, :]
    ms = jnp.mean(x * x, axis=-1, keepdims=True)
    h = x * lax.rsqrt(ms + EPS) * g
    return h * (1.0 + scale) + shift, gate


def _ada_kernel(cond_ref, w_ref, b_ref, o_ref):
    cnd = cond_ref[...]
    s = (cnd * jax.nn.sigmoid(cnd)).astype(BF16)
    o_ref[...] = jnp.dot(s, w_ref[...].astype(BF16), preferred_element_type=F32) + b_ref[...]


def _modulation_all(c, c_ctx, w_ada, b_ada):
    rows = 8
    cond = jnp.concatenate([c_ctx[None], c, jnp.zeros((rows - N_GROUPS, D_MODEL), F32)], axis=0)
    n_out = N_MOD * D_MODEL
    out = pl.pallas_call(
        _ada_kernel,
        grid=(DEPTH, n_out // ADA_TN),
        in_specs=[
            pl.BlockSpec((rows, D_MODEL), lambda l, j: (0, 0)),
            pl.BlockSpec((None, D_MODEL, ADA_TN), lambda l, j: (l, 0, j)),
            pl.BlockSpec((None, 1, ADA_TN), lambda l, j: (l, 0, j)),
        ],
        out_specs=pl.BlockSpec((None, rows, ADA_TN), lambda l, j: (l, 0, j)),
        out_shape=jax.ShapeDtypeStruct((DEPTH, rows, n_out), F32),
        compiler_params=pltpu.CompilerParams(dimension_semantics=("arbitrary", "arbitrary")),
        name="ada_modulation",
    )(cond, w_ada, b_ada.reshape(DEPTH, 1, n_out))
    return out[:, :N_GROUPS].reshape(DEPTH, N_GROUPS, N_MOD, D_MODEL)


def _ffn_kernel(x_ref, mod_ref, g_ref, win_ref, wout_ref, o_ref, acc_ref, *, k):
    x = x_ref[...]
    h, gate = _premod(x, g_ref[...], mod_ref, k)
    hb = h.astype(BF16)
    for j in range(N_FF_CHUNKS):
        a = jnp.dot(hb, win_ref[j], preferred_element_type=F32)
        u = jnp.dot(hb, win_ref[N_FF_CHUNKS + j], preferred_element_type=F32)
        t = (a * jax.nn.sigmoid(a) * u).astype(BF16)
        p = jnp.dot(t, wout_ref[j], preferred_element_type=F32)
        if j == 0:
            acc_ref[...] = p
        else:
            acc_ref[...] += p
    o_ref[...] = x + 0.5 * gate * acc_ref[...]


def _ffn(x, mod_l, g, w_in, w_out, k):
    tm = TM_FFN
    win = w_in.astype(BF16).reshape(D_MODEL, 2 * N_FF_CHUNKS, FF_CHUNK).transpose(1, 0, 2)
    wout = w_out.astype(BF16).reshape(N_FF_CHUNKS, FF_CHUNK, D_MODEL)
    return pl.pallas_call(
        functools.partial(_ffn_kernel, k=k),
        grid=(N_TOK // tm,),
        in_specs=[
            pl.BlockSpec((tm, D_MODEL), lambda i: (i, 0)),
            pl.BlockSpec((None, N_MOD, D_MODEL), lambda i: (_group_of_tile(i, tm), 0, 0)),
            pl.BlockSpec((1, D_MODEL), lambda i: (0, 0)),
            _resident((2 * N_FF_CHUNKS, D_MODEL, FF_CHUNK)),
            _resident((N_FF_CHUNKS, FF_CHUNK, D_MODEL)),
        ],
        out_specs=pl.BlockSpec((tm, D_MODEL), lambda i: (i, 0)),
        out_shape=jax.ShapeDtypeStruct((N_TOK, D_MODEL), F32),
        scratch_shapes=[pltpu.VMEM((tm, D_MODEL), F32)],
        compiler_params=pltpu.CompilerParams(dimension_semantics=("arbitrary",),
                                             vmem_limit_bytes=VMEM_LIMIT),
        name="ffn_swiglu",
    )(x, mod_l, g.reshape(1, D_MODEL), win, wout)


def _proj_in_kernel(x_ref, mod_ref, g_ref, w_ref, o_ref, *, tn, act):
    h, _ = _premod(x_ref[...], g_ref[...], mod_ref, 1)
    hb = h.astype(BF16)
    for j in range(w_ref.shape[1] // tn):
        y = jnp.dot(hb, w_ref[:, j * tn:(j + 1) * tn], preferred_element_type=F32)
        if act == "gelu":
            y = jax.nn.gelu(y)
        o_ref[:, j * tn:(j + 1) * tn] = y


def _proj_in(x, mod_l, g, w, tn, act=None):
    tm = TM_PROJ
    n_out = w.shape[1]
    return pl.pallas_call(
        functools.partial(_proj_in_kernel, tn=tn, act=act),
        grid=(N_TOK // tm,),
        in_specs=[
            pl.BlockSpec((tm, D_MODEL), lambda i: (i, 0)),
            pl.BlockSpec((None, N_MOD, D_MODEL), lambda i: (_group_of_tile(i, tm), 0, 0)),
            pl.BlockSpec((1, D_MODEL), lambda i: (0, 0)),
            _resident((D_MODEL, n_out)),
        ],
        out_specs=pl.BlockSpec((tm, n_out), lambda i: (i, 0)),
        out_shape=jax.ShapeDtypeStruct((N_TOK, n_out), F32),
        compiler_params=pltpu.CompilerParams(dimension_semantics=("arbitrary",),
                                             vmem_limit_bytes=VMEM_LIMIT),
        name="mixer_proj_in",
    )(x, mod_l, g.reshape(1, D_MODEL), w.astype(BF16))


def _proj_out_kernel(x_ref, o_in_ref, mod_ref, w_ref, o_ref):
    gate = mod_ref[5:6, :]
    y = jnp.dot(o_in_ref[...].astype(BF16), w_ref[...], preferred_element_type=F32)
    o_ref[...] = x_ref[...] + gate * y


def _proj_out(x, o_in, mod_l, w):
    tm = TM_PROJ
    kdim = w.shape[0]
    return pl.pallas_call(
        _proj_out_kernel,
        grid=(N_TOK // tm,),
        in_specs=[
            pl.BlockSpec((tm, D_MODEL), lambda i: (i, 0)),
            pl.BlockSpec((tm, kdim), lambda i: (i, 0)),
            pl.BlockSpec((None, N_MOD, D_MODEL), lambda i: (_group_of_tile(i, tm), 0, 0)),
            _resident((kdim, D_MODEL)),
        ],
        out_specs=pl.BlockSpec((tm, D_MODEL), lambda i: (i, 0)),
        out_shape=jax.ShapeDtypeStruct((N_TOK, D_MODEL), F32),
        compiler_params=pltpu.CompilerParams(dimension_semantics=("arbitrary",),
                                             vmem_limit_bytes=VMEM_LIMIT),
        name="mixer_proj_out",
    )(x, o_in, mod_l, w.astype(BF16))


def _final_norm_kernel(x_ref, g_ref, o_ref):
    x = x_ref[...]
    ms = jnp.mean(x * x, axis=-1, keepdims=True)
    o_ref[...] = x * lax.rsqrt(ms + EPS) * g_ref[...]


def _final_norm(x, g):
    tm = 1024
    return pl.pallas_call(
        _final_norm_kernel,
        grid=(N_TOK // tm,),
        in_specs=[pl.BlockSpec((tm, D_MODEL), lambda i: (i, 0)),
                  pl.BlockSpec((1, D_MODEL), lambda i: (0, 0))],
        out_specs=pl.BlockSpec((tm, D_MODEL), lambda i: (i, 0)),
        out_shape=jax.ShapeDtypeStruct((N_TOK, D_MODEL), F32),
        name="final_rmsnorm",
    )(x, g.reshape(1, D_MODEL))


def _split_streams(y):
    return y[:N_PROMPT].reshape(BATCH, SEQ, -1), y[N_PROMPT:].reshape(DEC_BATCH, DEC_SEQ, -1)


def _join_streams(yp, ys):
    return jnp.concatenate([yp.reshape(N_PROMPT, -1), ys.reshape(N_SAMPLE, -1)], axis=0)


def _rmsnorm(x, g):
    return x * lax.rsqrt(jnp.mean(x * x, axis=-1, keepdims=True) + EPS) * g


def _flip_t(t):
    return jnp.flip(t, axis=1)


def _rope_2d(x):
    T, dh = x.shape[1], x.shape[-1]
    half = dh // 2
    t = jnp.arange(T)
    inv = ROPE_THETA ** (-jnp.arange(0, half, 2, dtype=F32) / half)

    def rot(xa, pos):
        ang = pos.astype(F32)[:, None] * inv
        cos = jnp.cos(ang)[None, :, None, :]
        sin = jnp.sin(ang)[None, :, None, :]
        x1, x2 = jnp.split(xa, 2, axis=-1)
        return jnp.concatenate([x1 * cos - x2 * sin, x1 * sin + x2 * cos], axis=-1)

    return jnp.concatenate([rot(x[..., :half], t // GRID_W), rot(x[..., half:], t % GRID_W)], axis=-1)


def _gla_chunked(q, k, v, log_a, s0):
    B_, T, H, _ = q.shape
    dv = v.shape[-1]
    C = GLA_CHUNK
    n = T // C

    def blk(t):
        return t.reshape(B_, n, C, H, -1).transpose(0, 1, 3, 2, 4)

    q, k, v, la = blk(q), blk(k), blk(v), blk(log_a)
    b = jnp.cumsum(la, axis=3)
    tril = jnp.tril(jnp.ones((C, C), dtype=bool))
    diff = b[:, :, :, :, None, :] - b[:, :, :, None, :, :]
    decay = jnp.exp(jnp.where(tril[:, :, None], diff, -jnp.inf))
    attn = jnp.einsum('bnhid,bnhjd,bnhijd->bnhij', q, k, decay)
    o_intra = jnp.einsum('bnhij,bnhjv->bnhiv', attn, v)
    b_last = b[:, :, :, -1:, :]
    q_dec = q * jnp.exp(b)
    u = jnp.einsum('bnhjd,bnhjv->bnhdv', k * jnp.exp(b_last - b), v)
    g = jnp.exp(b_last[:, :, :, 0])

    def step(s, xs):
        qd, un, gn = xs
        o = jnp.einsum('bhid,bhdv->bhiv', qd, s)
        return gn[..., None] * s + un, o

    s_fin, o_inter = lax.scan(step, s0,
                              (jnp.swapaxes(q_dec, 0, 1), jnp.swapaxes(u, 0, 1), jnp.swapaxes(g, 0, 1)))
    o = o_intra + jnp.swapaxes(o_inter, 0, 1)
    return o.transpose(0, 1, 3, 2, 4).reshape(B_, T, H, dv), s_fin


def _gla_core(y, s0, w_a2, b_a, norm_g, use_rope):
    B_, T, _ = y.shape
    nq = GLA_H * GLA_DK
    nv = GLA_H * GLA_DV
    q = y[..., :nq].reshape(B_, T, GLA_H, GLA_DK) * (GLA_DK ** -0.5)
    k = y[..., nq:2 * nq].reshape(B_, T, GLA_H, GLA_DK)
    v = y[..., 2 * nq:2 * nq + nv].reshape(B_, T, GLA_H, GLA_DV)
    r = y[..., 2 * nq + nv:2 * nq + 2 * nv]
    za = y[..., GLA_IN:GLA_IN + 2 * GLA_RANK].reshape(B_, T, 2, GLA_RANK)
    if use_rope:
        q, k = _rope_2d(q), _rope_2d(k)
    z = jnp.einsum('bter,erk->btek', za, w_a2) + b_a
    log_a = (jax.nn.log_sigmoid(z) / GLA_TAU).reshape(B_, T, 2, GLA_H, GLA_DK)
    o_f, s_f = _gla_chunked(q, k, v, log_a[:, :, 0], s0[:, 0])
    o_b, s_b = _gla_chunked(_flip_t(q), _flip_t(k), _flip_t(v), _flip_t(log_a[:, :, 1]), s0[:, 1])
    o = o_f + _flip_t(o_b)
    o = _rmsnorm(o, norm_g.reshape(GLA_H, GLA_DV))
    o = o.reshape(B_, T, nv) * jax.nn.silu(r)
    return o, jnp.stack([s_f, s_b], axis=1)


def _nat_tables(rows):
    wh = min(NAT_WH, rows)
    r = np.arange(rows)
    row_idx = np.clip(r - wh // 2, 0, rows - wh)[:, None] + np.arange(wh)
    ncb = GRID_W // NAT_QB
    col_idx = np.clip(np.arange(ncb) * NAT_QB - (NAT_KB - NAT_QB) // 2, 0,
                      GRID_W - NAT_KB)[:, None] + np.arange(NAT_KB)
    qcol = np.arange(ncb)[:, None] * NAT_QB + np.arange(NAT_QB)
    c_start = np.clip(qcol - NAT_WW // 2, 0, GRID_W - NAT_WW)
    kc = col_idx[:, None, :]
    col_ok = (kc >= c_start[..., None]) & (kc < c_start[..., None] + NAT_WW)
    dc = kc - qcol[..., None]
    dr = row_idx - r[:, None]
    full = (rows, ncb, NAT_QB, wh, NAT_KB)
    flat = (rows, ncb, NAT_QB, wh * NAT_KB)
    dr_i = np.broadcast_to(dr[:, None, None, :, None] + NAT_WH - 1, full).reshape(flat)
    dc_i = np.broadcast_to(np.clip(dc + NAT_WW - 1, 0, 2 * NAT_WW - 2)[None, :, :, None, :], full).reshape(flat)
    ok = np.broadcast_to(col_ok[None, :, :, None, :], full).reshape(flat)
    return row_idx, col_idx, dr_i, dc_i, ok


def _nat_context_core(y):
    B_, S, _ = y.shape
    q, k, v = jnp.split(y, 3, axis=-1)
    q = q.reshape(B_, S, NAT_H, NAT_HD)
    k = k.reshape(B_, S, NAT_H, NAT_HD)
    v = v.reshape(B_, S, NAT_H, NAT_HD)
    s = jnp.einsum('bqhd,bkhd->bhqk', q, k) * (NAT_HD ** -0.5)
    p = jax.nn.softmax(s, axis=-1)
    o = jnp.einsum('bhqk,bkhd->bqhd', p, v).reshape(B_, S, D_MODEL)
    return o, k.transpose(0, 2, 1, 3), v.transpose(0, 2, 1, 3)


def _nat_latent_core(y, ck, cv, rpb):
    B_, T, _ = y.shape
    rows = T // GRID_W
    ncb = GRID_W // NAT_QB
    row_idx, col_idx, dr_i, dc_i, ok = _nat_tables(rows)
    q, k, v = jnp.split(y, 3, axis=-1)
    q = q.reshape(B_, rows, ncb, NAT_QB, NAT_H, NAT_HD)
    k = k.reshape(B_, rows, GRID_W, NAT_H, NAT_HD)
    v = v.reshape(B_, rows, GRID_W, NAT_H, NAT_HD)
    ri = row_idx[:, None, :, None]
    ci = col_idx[None, :, None, :]
    kb = k[:, ri, ci].reshape(B_, rows, ncb, -1, NAT_H, NAT_HD)
    vb = v[:, ri, ci].reshape(B_, rows, ncb, -1, NAT_H, NAT_HD)
    nk = kb.shape[3]
    scale = NAT_HD ** -0.5
    s_lat = jnp.einsum('brnqhd,brnkhd->bhrnqk', q, kb) * scale
    s_lat = s_lat + rpb[:, dr_i, dc_i]
    s_lat = jnp.where(ok, s_lat, NEG_INF)
    s_ctx = jnp.einsum('brnqhd,bhsd->bhrnqs', q, ck) * scale
    p = jax.nn.softmax(jnp.concatenate([s_lat, s_ctx], axis=-1), axis=-1)
    o = (jnp.einsum('bhrnqk,brnkhd->brnqhd', p[..., :nk], vb)
         + jnp.einsum('bhrnqs,bhsd->brnqhd', p[..., nk:], cv))
    return o.reshape(B_, T, D_MODEL)


def _gmlp_core(y, ln_g, ln_b, w_s, b_s):
    B_, T, _ = y.shape
    u, v = jnp.split(y, 2, axis=-1)
    vc = v - jnp.mean(v, axis=-1, keepdims=True)
    v = vc * lax.rsqrt(jnp.mean(vc * vc, axis=-1, keepdims=True) + EPS) * ln_g + ln_b
    v = v.reshape(B_, T // GM_CHUNK, GM_CHUNK, GM_G, GM_CG)
    v = jnp.einsum('gpq,bnqgc->bnpgc', w_s, v) + b_s.T[None, None, :, :, None]
    return u * v.reshape(B_, T, GM_DH)


GLA_BLK = 128
GLA_CPB = GLA_BLK // GLA_CHUNK
GLA_ZCOL = GLA_IN // LANE


def _gla_consts():
    r = np.arange(GLA_BLK)
    same = (r[:, None] // GLA_CHUNK) == (r[None, :] // GLA_CHUNK)
    ri, ci = r[:, None] % GLA_CHUNK, r[None, :] % GLA_CHUNK
    lf = np.concatenate([same & (ci <= ri), same & (ci > ri)], axis=0)
    lb = np.concatenate([same & (ci >= ri), same & (ci < ri)], axis=0)
    rows = np.arange(GLA_CHUNK * GLA_DK)
    sel = (rows[:, None] // GLA_DK) == (np.arange(LANE)[None, :] % GLA_CHUNK)
    return (jnp.asarray(lf, BF16), jnp.asarray(lb, BF16), jnp.asarray(sel, BF16))


def _rope_tables(T):
    half = GLA_DK // 2
    t = np.arange(T)
    inv = ROPE_THETA ** (-np.arange(0, half, 2, dtype=np.float64) / half)
    lane = np.arange(GLA_DK)
    pos = np.where(lane[None, :] < half, (t // GRID_W)[:, None], (t % GRID_W)[:, None])
    ang = pos * inv[lane % (half // 2)][None, :]
    sign = np.where((lane % half) < half // 2, -1.0, 1.0)[None, :]
    return jnp.asarray(np.cos(ang), F32), jnp.asarray(np.sin(ang) * sign, F32)


def _rope_apply(x, cos, sin_signed):
    half = GLA_DK // 2
    lane = lax.broadcasted_iota(jnp.int32, (1, GLA_DK), 1)
    partner = jnp.where((lane % half) < half // 2,
                        pltpu.roll(x, GLA_DK - half // 2, axis=1), pltpu.roll(x, half // 2, axis=1))
    return x * cos + partner * sin_signed


def _split3(x):
    hi = x.astype(BF16)
    r1 = x - hi.astype(F32)
    mid = r1.astype(BF16)
    lo = (r1 - mid.astype(F32)).astype(BF16)
    return hi, mid, lo


def _dot3(m, x):
    hi, mid, lo = _split3(x)
    d = lambda p: jnp.dot(m, p, preferred_element_type=F32)
    return d(hi) + d(mid) + d(lo)


def _gla_kernel(*refs, use_rope, has_s0, want_state):
    it = iter(refs)
    q_ref, k_ref, v_ref, r_ref, za_ref, w2_ref, ba_ref, ng_ref, lf_ref, lb_ref, sel_ref = (next(it) for _ in range(11))
    cos_ref, sin_ref = (next(it), next(it)) if use_rope else (None, None)
    s0_ref = next(it) if has_s0 else None
    o_ref = next(it)
    st_ref = next(it) if want_state else None
    (q_s, k_s, bf_s, bb_s, ef_s, eb_s, qdf_s, kdf_s, qdb_s, kdb_s, v_s, w_s, o_s, sf_s, sb_s) = it

    T = q_ref.shape[0]
    nblk, nch = T // GLA_BLK, T // GLA_CHUNK

    for blk in range(nblk):
        rows = slice(blk * GLA_BLK, (blk + 1) * GLA_BLK)
        z = jnp.dot(za_ref[rows, :].astype(BF16), w2_ref[...], preferred_element_type=F32) + ba_ref[...]
        la = (jnp.minimum(z, 0.0) - jnp.log1p(jnp.exp(-jnp.abs(z)))) * (1.0 / GLA_TAU)
        cf = _dot3(lf_ref[...], la[:, :GLA_DK])
        cb = _dot3(lb_ref[...], la[:, GLA_DK:])
        bf, bb = cf[:GLA_BLK], cb[:GLA_BLK]
        ef, eb = jnp.exp(bf), jnp.exp(bb)
        q = q_ref[rows, :] * (GLA_DK ** -0.5)
        k = k_ref[rows, :]
        if use_rope:
            q = _rope_apply(q, cos_ref[rows, :], sin_ref[rows, :])
            k = _rope_apply(k, cos_ref[rows, :], sin_ref[rows, :])
        q_s[rows, :], k_s[rows, :] = q, k
        bf_s[rows, :], bb_s[rows, :] = bf, bb
        ef_s[rows, :], eb_s[rows, :] = ef, eb
        qdf_s[rows, :] = (q * ef).astype(BF16)
        qdb_s[rows, :] = (q * eb).astype(BF16)
        kdf_s[rows, :] = (k * jnp.exp(cf[GLA_BLK:])).astype(BF16)
        kdb_s[rows, :] = (k * jnp.exp(cb[GLA_BLK:])).astype(BF16)
        v_s[rows, :] = v_ref[rows, :].astype(BF16)

    irow = lax.broadcasted_iota(jnp.int32, (GLA_CHUNK, 1), 0)

    def intra_chunk(c, carry):
        rows = pl.ds(pl.multiple_of(c * GLA_CHUNK, GLA_CHUNK), GLA_CHUNK)
        qc, bfc, bbc = q_s[rows, :], bf_s[rows, :], bb_s[rows, :]
        for j in range(GLA_CHUNK):
            row = pl.ds(c * GLA_CHUNK + j, 1)
            e = (jnp.exp(jnp.where(irow >= j, bfc - bf_s[row, :], -jnp.inf))
                 + jnp.exp(jnp.where(irow <= j, bbc - bb_s[row, :], -jnp.inf)))
            w_s[rows, j * GLA_DK:(j + 1) * GLA_DK] = (qc * k_s[row, :] * e).astype(BF16)
        return carry

    lax.fori_loop(0, nch, intra_chunk, 0)

    lane_c = lax.broadcasted_iota(jnp.int32, (GLA_BLK, LANE), 1) // GLA_CHUNK
    row_c = lax.broadcasted_iota(jnp.int32, (GLA_BLK, LANE), 0) // GLA_CHUNK
    for blk in range(nblk):
        rows = slice(blk * GLA_BLK, (blk + 1) * GLA_BLK)
        a = jnp.dot(w_s[rows, :], sel_ref[...], preferred_element_type=F32)
        a = jnp.where(lane_c == row_c, a, 0.0).astype(BF16)
        o_s[rows, :] = jnp.dot(a, v_s[rows, :], preferred_element_type=F32)

    if has_s0:
        sf_s[...] = s0_ref[0].T
        sb_s[...] = s0_ref[1].T
    else:
        sf_s[...] = jnp.zeros_like(sf_s)
        sb_s[...] = jnp.zeros_like(sb_s)

    def chain(rows, g_row, qd_s, kd_s, e_s, st_s):
        s = st_s[...]
        o_s[rows, :] += _dot_nt(qd_s[rows, :], s.astype(BF16))
        u = lax.dot_general(v_s[rows, :], kd_s[rows, :], (((0,), (0,)), ((), ())), preferred_element_type=F32)
        st_s[...] = s * e_s[g_row, :] + u

    def inter_chunk(n, carry):
        cf_ = pl.multiple_of(n * GLA_CHUNK, GLA_CHUNK)
        cb_ = pl.multiple_of((nch - 1 - n) * GLA_CHUNK, GLA_CHUNK)
        chain(pl.ds(cf_, GLA_CHUNK), pl.ds(cf_ + GLA_CHUNK - 1, 1), qdf_s, kdf_s, ef_s, sf_s)
        chain(pl.ds(cb_, GLA_CHUNK), pl.ds(cb_, 1), qdb_s, kdb_s, eb_s, sb_s)
        return carry

    lax.fori_loop(0, nch, inter_chunk, 0, unroll=4)

    o = o_s[...]
    o = o * lax.rsqrt(jnp.mean(o * o, axis=-1, keepdims=True) + EPS) * ng_ref[...]
    r = r_ref[...]
    o_ref[...] = o * (r * jax.nn.sigmoid(r))
    if want_state:
        st_ref[0] = sf_s[...].T
        st_ref[1] = sb_s[...].T


def _gla_call(y, w2, ba, ng, T, n_seq, row0, use_rope, s0):
    has_s0 = s0 is not None
    want_state = not has_s0
    rb = row0 // T
    nqb = GLA_H
    col = lambda w, off: pl.BlockSpec((T, w), lambda b, h: (rb + b, off + h))
    cst = lambda shape: pl.BlockSpec(shape, lambda b, h: (0,) * len(shape))
    per_head = lambda shape: pl.BlockSpec((None,) + shape, lambda b, h: (h,) + (0,) * len(shape))
    lf, lb, sel = _gla_consts()
    in_specs = [col(GLA_DK, 0), col(GLA_DK, nqb), col(GLA_DV, nqb), col(GLA_DV, nqb + GLA_H),
                pl.BlockSpec((T, LANE), lambda b, h: (rb + b, GLA_ZCOL)),
                per_head((LANE, 2 * GLA_DK)), per_head((1, 2 * GLA_DK)), per_head((1, GLA_DV)),
                cst(lf.shape), cst(lb.shape), cst(sel.shape)]
    args = [y, y, y, y, y, w2, ba, ng, lf, lb, sel]
    if use_rope:
        cos, sin = _rope_tables(T)
        in_specs += [cst(cos.shape), cst(sin.shape)]
        args += [cos, sin]
    if has_s0:
        in_specs.append(pl.BlockSpec((None, 2, None, GLA_DK, GLA_DV), lambda b, h: (b, 0, h, 0, 0)))
        args.append(s0)
    out_specs = [pl.BlockSpec((T, GLA_DV), lambda b, h: (b, h))]
    out_shape = [jax.ShapeDtypeStruct((n_seq * T, GLA_H * GLA_DV), F32)]
    if want_state:
        out_specs.append(pl.BlockSpec((None, 2, None, GLA_DK, GLA_DV), lambda b, h: (b, 0, h, 0, 0)))
        out_shape.append(jax.ShapeDtypeStruct((n_seq, 2, GLA_H, GLA_DK, GLA_DV), F32))
    f32s = lambda w: pltpu.VMEM((T, w), F32)
    bf16s = lambda w: pltpu.VMEM((T, w), BF16)
    scratch = ([f32s(GLA_DK)] * 6 + [bf16s(GLA_DK)] * 4
               + [bf16s(GLA_DV), bf16s(GLA_CHUNK * GLA_DK), f32s(GLA_DV),
                  pltpu.VMEM((GLA_DV, GLA_DK), F32), pltpu.VMEM((GLA_DV, GLA_DK), F32)])
    outs = pl.pallas_call(
        functools.partial(_gla_kernel, use_rope=use_rope, has_s0=has_s0, want_state=want_state),
        grid=(n_seq, GLA_H),
        in_specs=in_specs, out_specs=out_specs, out_shape=out_shape,
        scratch_shapes=scratch,
        compiler_params=pltpu.CompilerParams(dimension_semantics=("arbitrary", "arbitrary"),
                                             vmem_limit_bytes=VMEM_LIMIT),
        name="gla_rope" if use_rope else "gla",
    )(*args)
    return (outs[0], outs[1]) if want_state else (outs[0], None)


def _gla_mixer(y, state, w_a2, b_a, norm_g):
    w2 = jnp.zeros((GLA_H, LANE, 2 * GLA_DK), F32)
    for e in range(2):
        we = w_a2[e].reshape(GLA_RANK, GLA_H, GLA_DK).transpose(1, 0, 2)
        w2 = w2.at[:, e * GLA_RANK:(e + 1) * GLA_RANK, e * GLA_DK:(e + 1) * GLA_DK].set(we)
    ba = b_a.reshape(2, GLA_H, GLA_DK).transpose(1, 0, 2).reshape(GLA_H, 1, 2 * GLA_DK)
    ng = norm_g.reshape(GLA_H, 1, GLA_DV)
    op, st = _gla_call(y, w2.astype(BF16), ba, ng, SEQ, BATCH, 0, False, None)
    os_, _ = _gla_call(y, w2.astype(BF16), ba, ng, DEC_SEQ, DEC_BATCH, N_PROMPT, True, state)
    return jnp.concatenate([op, os_], axis=0), st


N_HEAD_PAIRS = NAT_H // 2
NAT_ROWS = DEC_SEQ // GRID_W
NAT_WIN = NAT_WH * GRID_W
NAT_CLS = NAT_WH


def _nat_row_window(r):
    rs = min(max(r - NAT_WH // 2, 0), NAT_ROWS - NAT_WH)
    return rs, r - rs


def _nat_bias_table(rpb):
    qc = np.arange(GRID_W)[:, None]
    kc = np.arange(GRID_W)[None, :]
    c_start = np.clip(qc - NAT_WW // 2, 0, GRID_W - NAT_WW)
    ok = (kc >= c_start) & (kc < c_start + NAT_WW)
    dc = np.clip(kc - qc + NAT_WW - 1, 0, 2 * NAT_WW - 2)
    cls = np.arange(NAT_CLS)[:, None]
    w = np.arange(NAT_WH)[None, :]
    dr = w - cls + NAT_WH - 1
    t = rpb[:, dr[:, None, :, None], dc[None, :, None, :]]
    t = jnp.where(ok[None, None, :, None, :], t, NEG_INF)
    return t.reshape(NAT_H, NAT_CLS, GRID_W, NAT_WIN)


def _head_mask(hh):
    lane = lax.broadcasted_iota(jnp.int32, (1, LANE), 1)
    return (lane < NAT_HD) if hh == 0 else (lane >= NAT_HD)


def _dot_nt(a, b):
    return lax.dot_general(a, b, (((1,), (1,)), ((), ())), preferred_element_type=F32)


def _nat_ctx_kernel(q_ref, k_ref, v_ref, o_ref):
    q = q_ref[...] * (NAT_HD ** -0.5)
    kb = k_ref[...].astype(BF16)
    vb = v_ref[...].astype(BF16)
    outs = []
    for hh in range(2):
        qm = jnp.where(_head_mask(hh), q, 0.0).astype(BF16)
        s = _dot_nt(qm, kb)
        p = jnp.exp(s - jnp.max(s, axis=-1, keepdims=True))
        l = jnp.sum(p, axis=-1, keepdims=True)
        outs.append(jnp.dot(p.astype(BF16), vb, preferred_element_type=F32) / l)
    o_ref[...] = jnp.where(_head_mask(0), outs[0], outs[1])


def _nat_context(y):
    blk = lambda off: pl.BlockSpec((SEQ, LANE), lambda b, hp: (b, off + hp))
    return pl.pallas_call(
        _nat_ctx_kernel,
        grid=(BATCH, N_HEAD_PAIRS),
        in_specs=[blk(0), blk(N_HEAD_PAIRS), blk(2 * N_HEAD_PAIRS)],
        out_specs=pl.BlockSpec((SEQ, LANE), lambda b, hp: (b, hp)),
        out_shape=jax.ShapeDtypeStruct((N_TOK, D_MODEL), F32),
        compiler_params=pltpu.CompilerParams(dimension_semantics=("arbitrary", "arbitrary")),
        name="nat_context",
    )(y, y, y)


def _nat_lat_kernel(q_ref, k_ref, v_ref, ck_ref, cv_ref, tab_ref, o_in_ref, o_ref):
    del o_in_ref
    q = q_ref[...] * (NAT_HD ** -0.5)
    qm = [jnp.where(_head_mask(hh), q, 0.0).astype(BF16) for hh in range(2)]
    ckb = ck_ref[...].astype(BF16)
    cvb = cv_ref[...].astype(BF16)
    for r in range(NAT_ROWS):
        rs, cls = _nat_row_window(r)
        kw = k_ref[rs * GRID_W:rs * GRID_W + NAT_WIN, :].astype(BF16)
        vw = v_ref[rs * GRID_W:rs * GRID_W + NAT_WIN, :].astype(BF16)
        outs = []
        for hh in range(2):
            qr = qm[hh][r * GRID_W:(r + 1) * GRID_W]
            s_lat = _dot_nt(qr, kw) + tab_ref[hh, cls]
            s_ctx = _dot_nt(qr, ckb)
            m = jnp.maximum(jnp.max(s_lat, axis=-1, keepdims=True), jnp.max(s_ctx, axis=-1, keepdims=True))
            p_lat = jnp.exp(s_lat - m)
            p_ctx = jnp.exp(s_ctx - m)
            l = jnp.sum(p_lat, axis=-1, keepdims=True) + jnp.sum(p_ctx, axis=-1, keepdims=True)
            o = (jnp.dot(p_lat.astype(BF16), vw, preferred_element_type=F32)
                 + jnp.dot(p_ctx.astype(BF16), cvb, preferred_element_type=F32))
            outs.append(o / l)
        o_ref[r * GRID_W:(r + 1) * GRID_W, :] = jnp.where(_head_mask(0), outs[0], outs[1])


def _nat_latent(y, o_ctx, ck, cv, rpb):
    row0 = N_PROMPT // DEC_SEQ
    blk = lambda off: pl.BlockSpec((DEC_SEQ, LANE), lambda b, hp: (row0 + b, off + hp))
    ctx = pl.BlockSpec((None, ck.shape[1], LANE), lambda b, hp: (b, 0, hp))
    return pl.pallas_call(
        _nat_lat_kernel,
        grid=(DEC_BATCH, N_HEAD_PAIRS),
        in_specs=[blk(0), blk(N_HEAD_PAIRS), blk(2 * N_HEAD_PAIRS), ctx, ctx,
                  pl.BlockSpec((2, NAT_CLS, GRID_W, NAT_WIN), lambda b, hp: (hp, 0, 0, 0)),
                  pl.BlockSpec(memory_space=pl.ANY)],
        out_specs=pl.BlockSpec((DEC_SEQ, LANE), lambda b, hp: (row0 + b, hp)),
        out_shape=jax.ShapeDtypeStruct((N_TOK, D_MODEL), F32),
        input_output_aliases={6: 0},
        compiler_params=pltpu.CompilerParams(dimension_semantics=("arbitrary", "arbitrary")),
        name="nat_latent",
    )(y, y, y, ck, cv, _nat_bias_table(rpb), o_ctx)


def _heads_last(t):
    b, h, s, d = t.shape
    return t.transpose(0, 2, 1, 3).reshape(b, s, h * d)


def _heads_first(t):
    b, s, _ = t.shape
    return t.reshape(b, s, NAT_H, NAT_HD).transpose(0, 2, 1, 3)


def _gmlp_kernel(x_ref, y_ref, mod_ref, lng_ref, lnb_ref, ws_ref, bs_ref, w_ref, o_ref, t_ref):
    tm = x_ref.shape[0]
    v = y_ref[:, GM_DH:]
    vc = v - jnp.mean(v, axis=-1, keepdims=True)
    vn = vc * lax.rsqrt(jnp.mean(vc * vc, axis=-1, keepdims=True) + EPS) * lng_ref[...] + lnb_ref[...]
    vn = vn.astype(BF16)
    for n in range(tm // GM_CHUNK):
        rows = slice(n * GM_CHUNK, (n + 1) * GM_CHUNK)
        for g in range(GM_G):
            cols = slice(g * GM_CG, (g + 1) * GM_CG)
            sp = jnp.dot(ws_ref[g], vn[rows, cols], preferred_element_type=F32) + bs_ref[:, cols]
            t_ref[rows, cols] = (y_ref[rows, cols] * sp).astype(BF16)
    gate = mod_ref[5:6, :]
    o_ref[...] = x_ref[...] + gate * jnp.dot(t_ref[...], w_ref[...], preferred_element_type=F32)


def _gmlp_out(x, y, mod_l, ln_g, ln_b, w_s, b_s, w_out):
    tm = TM_PROJ
    bias = jnp.repeat(b_s.T, GM_CG, axis=1)
    return pl.pallas_call(
        _gmlp_kernel,
        grid=(N_TOK // tm,),
        in_specs=[
            pl.BlockSpec((tm, D_MODEL), lambda i: (i, 0)),
            pl.BlockSpec((tm, 2 * GM_DH), lambda i: (i, 0)),
            pl.BlockSpec((None, N_MOD, D_MODEL), lambda i: (_group_of_tile(i, tm), 0, 0)),
            pl.BlockSpec((1, GM_DH), lambda i: (0, 0)),
            pl.BlockSpec((1, GM_DH), lambda i: (0, 0)),
            _resident((GM_G, GM_CHUNK, GM_CHUNK)),
            _resident((GM_CHUNK, GM_DH)),
            _resident((GM_DH, D_MODEL)),
        ],
        out_specs=pl.BlockSpec((tm, D_MODEL), lambda i: (i, 0)),
        out_shape=jax.ShapeDtypeStruct((N_TOK, D_MODEL), F32),
        scratch_shapes=[pltpu.VMEM((tm, GM_DH), BF16)],
        compiler_params=pltpu.CompilerParams(dimension_semantics=("arbitrary",),
                                             vmem_limit_bytes=VMEM_LIMIT),
        name="gmlp_gate_out",
    )(x, y, mod_l, ln_g.reshape(1, GM_DH), ln_b.reshape(1, GM_DH), w_s.astype(BF16), bias,
      w_out.astype(BF16))


def _dwconv_centred(x, w, b):
    K = w.shape[0]
    T = x.shape[1]
    pad = K // 2
    xp = jnp.pad(x, ((0, 0), (pad, pad), (0, 0)))
    out = xp[:, 0:T] * w[0]
    for i in range(1, K):
        out = out + xp[:, i:i + T] * w[i]
    return out + b


def _ssd_chunked(x, dt, a, bm, cm, s0):
    B_, T = x.shape[:2]
    L = SSD_CHUNK
    n = T // L
    E = SSD_H // SSD_G
    x = x.reshape(B_, n, L, SSD_G, E, SSD_P)
    dt = dt.reshape(B_, n, L, SSD_G, E)
    bm = bm.reshape(B_, n, L, SSD_G, SSD_N)
    cm = cm.reshape(B_, n, L, SSD_G, SSD_N)
    cum = jnp.cumsum(dt * a.reshape(SSD_G, E), axis=2)
    tril = jnp.tril(jnp.ones((L, L), dtype=bool))
    seg = cum[:, :, :, None] - cum[:, :, None, :]
    decay = jnp.exp(jnp.where(tril[:, :, None, None], seg, -jnp.inf))
    dtx = x * dt[..., None]
    cb = jnp.einsum('bnigs,bnjgs->bnijg', cm, bm)
    y_diag = jnp.einsum('bnijg,bnijge,bnjgep->bnigep', cb, decay, dtx)
    u = jnp.einsum('bnjgs,bnjge,bnjgep->bngeps', bm, jnp.exp(cum[:, :, -1:] - cum), dtx)
    chunk_decay = jnp.exp(cum[:, :, -1])
    q_decay = jnp.exp(cum)

    def step(s, xs):
        c_n, qd_n, u_n, g_n = xs
        y = jnp.einsum('bigs,bige,bgeps->bigep', c_n, qd_n, s)
        return g_n[..., None, None] * s + u_n, y

    sw = lambda t: jnp.swapaxes(t, 0, 1)
    s_fin, y_off = lax.scan(step, s0.reshape(B_, SSD_G, E, SSD_P, SSD_N),
                            (sw(cm), sw(q_decay), sw(u), sw(chunk_decay)))
    y = y_diag + sw(y_off)
    return y.reshape(B_, T, SSD_H, SSD_P), s_fin.reshape(B_, SSD_H, SSD_P, SSD_N)


def _ssd_core(y, s0, conv_w, conv_b, dt_bias, a_log, d_skip, norm_g):
    B_, T, _ = y.shape
    z = y[..., :SSD_DI]
    xbc = y[..., SSD_DI:SSD_DI + SSD_XBC]
    dt = y[..., SSD_DI + SSD_XBC:SSD_IN]
    xbc = jax.nn.silu(_dwconv_centred(xbc, conv_w, conv_b))
    x = xbc[..., :SSD_DI].reshape(B_, T, SSD_H, SSD_P)
    bm = xbc[..., SSD_DI:SSD_DI + SSD_G * SSD_N].reshape(B_, T, SSD_G, SSD_N)
    cm = xbc[..., SSD_DI + SSD_G * SSD_N:].reshape(B_, T, SSD_G, SSD_N)
    dt = jax.nn.softplus(dt.reshape(B_, T, 2, SSD_H) + dt_bias)
    a = -jnp.exp(a_log)
    y_f, s_f = _ssd_chunked(x, dt[:, :, 0], a[0], bm, cm, s0[:, 0])
    y_b, s_b = _ssd_chunked(_flip_t(x), _flip_t(dt[:, :, 1]), a[1], _flip_t(bm), _flip_t(cm), s0[:, 1])
    yy = y_f + _flip_t(y_b) + d_skip[:, None] * x
    yy = _rmsnorm(yy.reshape(B_, T, SSD_DI) * jax.nn.silu(z), norm_g)
    return yy, jnp.stack([s_f, s_b], axis=1)


def _pad_cols(w, mult):
    n = w.shape[1]
    n_pad = -n % mult
    return jnp.pad(w, ((0, 0), (0, n_pad))) if n_pad else w


def kernel(x_prompt, x_sample, state_gla, cache_nat_k, cache_nat_v, state_ssd, c,
           c_ctx, norm_g, w_ada, b_ada, w_ffn_in, w_ffn_out,
           gla_w_in, gla_w_a1, gla_w_a2, gla_b_a, gla_norm_g, gla_w_out,
           nat_w_qkv, nat_rpb, nat_w_out,
           gm_w_in, gm_ln_g, gm_ln_b, gm_w_s, gm_b_s, gm_w_out,
           ssd_w_in, ssd_conv_w, ssd_conv_b, ssd_dt_bias, ssd_a_log, ssd_d, ssd_norm_g, ssd_w_out,
           final_g):
    x = _join_streams(x_prompt, x_sample)
    mod = _modulation_all(c, c_ctx, w_ada, b_ada)
    new_gla, new_k, new_v, new_ssd = [], [], [], []
    for l in range(DEPTH):
        kind, j = l % N_MIXERS, l // N_MIXERS
        x = _ffn(x, mod[l], norm_g[l, 0], w_ffn_in[l, 0], w_ffn_out[l, 0], 0)
        if kind == 0:
            w = jnp.concatenate([gla_w_in[j], gla_w_a1[j, 0], gla_w_a1[j, 1]], axis=1)
            y = _proj_in(x, mod[l], norm_g[l, 1], _pad_cols(w, 640), 640)
            o, st = _gla_mixer(y, state_gla[:, j], gla_w_a2[j], gla_b_a[j], gla_norm_g[j])
            new_gla.append(st)
            x = _proj_out(x, o, mod[l], gla_w_out[j])
        elif kind == 1:
            y = _proj_in(x, mod[l], norm_g[l, 1], nat_w_qkv[j], 768)
            o = _nat_context(y)
            o = _nat_latent(y, o, _heads_last(cache_nat_k[:, j]), _heads_last(cache_nat_v[:, j]), nat_rpb[j])
            yp = y[:N_PROMPT].reshape(BATCH, SEQ, 3 * D_MODEL)
            new_k.append(_heads_first(yp[..., D_MODEL:2 * D_MODEL]))
            new_v.append(_heads_first(yp[..., 2 * D_MODEL:]))
            x = _proj_out(x, o, mod[l], nat_w_out[j])
        elif kind == 2:
            y = _proj_in(x, mod[l], norm_g[l, 1], gm_w_in[j], 512, act="gelu")
            x = _gmlp_out(x, y, mod[l], gm_ln_g[j], gm_ln_b[j], gm_w_s[j], gm_b_s[j], gm_w_out[j])
        else:
            y = _proj_in(x, mod[l], norm_g[l, 1], _pad_cols(ssd_w_in[j], 768), 768)
            yp, ys = _split_streams(y)
            zero_state = jnp.zeros((BATCH, 2, SSD_H, SSD_P, SSD_N), F32)
            op, st = _ssd_core(yp, zero_state, ssd_conv_w[j], ssd_conv_b[j], ssd_dt_bias[j],
                               ssd_a_log[j], ssd_d[j], ssd_norm_g[j])
            os_, _ = _ssd_core(ys, state_ssd[:, j], ssd_conv_w[j], ssd_conv_b[j], ssd_dt_bias[j],
                               ssd_a_log[j], ssd_d[j], ssd_norm_g[j])
            new_ssd.append(st)
            x = _proj_out(x, _join_streams(op, os_), mod[l], ssd_w_out[j])
        x = _ffn(x, mod[l], norm_g[l, 2], w_ffn_in[l, 1], w_ffn_out[l, 1], 2)
    yn = _final_norm(x, final_g)
    y_prompt, y_sample = _split_streams(yn)
    return (y_prompt, y_sample, jnp.stack(new_gla, axis=1), jnp.stack(new_k, axis=1),
            jnp.stack(new_v, axis=1), jnp.stack(new_ssd, axis=1))
```

```python
import functools
import math

import jax
import jax.numpy as jnp
import numpy as np
from jax import lax
from jax.experimental import pallas as pl
from jax.experimental.pallas import tpu as pltpu

D_MODEL = 1024
BATCH = 32
SEQ = 256
DEPTH = 4
DEC_BATCH = 2
DEC_SEQ = 1024
N_PROMPT = BATCH * SEQ
N_SAMPLE = DEC_BATCH * DEC_SEQ
N_TOK = N_PROMPT + N_SAMPLE
N_GROUPS = 1 + DEC_BATCH

GRID_W = 64
N_MIXERS = 4
N_SUB = 3
N_MOD = 3 * N_SUB
D_FF = 2816
EPS = 1e-6
NEG_INF = -1e30
ROPE_THETA = 10000.0
GLA_H, GLA_DK, GLA_DV, GLA_RANK, GLA_TAU, GLA_CHUNK = 4, 128, 256, 16, 16.0, 16
GLA_IN = 2 * GLA_H * GLA_DK + 2 * GLA_H * GLA_DV
NAT_H, NAT_HD, NAT_WH, NAT_WW, NAT_QB, NAT_KB = 16, 64, 8, 16, 16, 32
GM_DH, GM_G, GM_CHUNK = 1024, 8, 128
GM_CG = GM_DH // GM_G
SSD_DI = 2 * D_MODEL
SSD_P = 64
SSD_H = SSD_DI // SSD_P
SSD_N, SSD_G, SSD_CONV, SSD_CHUNK = 128, 4, 3, 64
SSD_XBC = SSD_DI + 2 * SSD_G * SSD_N
SSD_IN = SSD_DI + SSD_XBC + 2 * SSD_H

LANE = 128
VMEM_LIMIT = 56 * 1024 * 1024
BF16 = jnp.bfloat16
F32 = jnp.float32

FF_CHUNK = 256
N_FF_CHUNKS = D_FF // FF_CHUNK
TM_FFN = 512
TM_PROJ = 512
ADA_TN = 1152


def _group_of_tile(i, tm):
    n_prompt_tiles = N_PROMPT // tm
    return jnp.where(i < n_prompt_tiles, 0, 1 + (i - n_prompt_tiles) // (DEC_SEQ // tm))


def _resident(shape):
    nd = len(shape)
    return pl.BlockSpec(shape, lambda i: (0,) * nd, pipeline_mode=pl.Buffered(1))


def _premod(x, g, mod_ref, k):
    shift = mod_ref[3 * k:3 * k + 1, :]
    scale = mod_ref[3 * k + 1:3 * k + 2, :]
    gate = mod_ref[3 * k + 2:3 * k + 3, :]
    ms = jnp.mean(x * x, axis=-1, keepdims=True)
    h = x * lax.rsqrt(ms + EPS) * g
    return h * (1.0 + scale) + shift, gate


def _ada_kernel(cond_ref, w_ref, b_ref, o_ref):
    cnd = cond_ref[...]
    s = (cnd * jax.nn.sigmoid(cnd)).astype(BF16)
    o_ref[...] = jnp.dot(s, w_ref[...].astype(BF16), preferred_element_type=F32) + b_ref[...]


def _modulation_all(c, c_ctx, w_ada, b_ada):
    rows = 8
    cond = jnp.concatenate([c_ctx[None], c, jnp.zeros((rows - N_GROUPS, D_MODEL), F32)], axis=0)
    n_out = N_MOD * D_MODEL
    out = pl.pallas_call(
        _ada_kernel,
        grid=(DEPTH, n_out // ADA_TN),
        in_specs=[
            pl.BlockSpec((rows, D_MODEL), lambda l, j: (0, 0)),
            pl.BlockSpec((None, D_MODEL, ADA_TN), lambda l, j: (l, 0, j)),
            pl.BlockSpec((None, 1, ADA_TN), lambda l, j: (l, 0, j)),
        ],
        out_specs=pl.BlockSpec((None, rows, ADA_TN), lambda l, j: (l, 0, j)),
        out_shape=jax.ShapeDtypeStruct((DEPTH, rows, n_out), F32),
        compiler_params=pltpu.CompilerParams(dimension_semantics=("arbitrary", "arbitrary")),
        name="ada_modulation",
    )(cond, w_ada, b_ada.reshape(DEPTH, 1, n_out))
    return out[:, :N_GROUPS].reshape(DEPTH, N_GROUPS, N_MOD, D_MODEL)


def _ffn_kernel(x_ref, mod_ref, g_ref, win_ref, wout_ref, o_ref, acc_ref, *, k):
    x = x_ref[...]
    h, gate = _premod(x, g_ref[...], mod_ref, k)
    hb = h.astype(BF16)
    for j in range(N_FF_CHUNKS):
        a = jnp.dot(hb, win_ref[j], preferred_element_type=F32)
        u = jnp.dot(hb, win_ref[N_FF_CHUNKS + j], preferred_element_type=F32)
        t = (a * jax.nn.sigmoid(a) * u).astype(BF16)
        p = jnp.dot(t, wout_ref[j], preferred_element_type=F32)
        if j == 0:
            acc_ref[...] = p
        else:
            acc_ref[...] += p
    o_ref[...] = x + 0.5 * gate * acc_ref[...]


def _ffn(x, mod_l, g, w_in, w_out, k):
    tm = TM_FFN
    win = w_in.astype(BF16).reshape(D_MODEL, 2 * N_FF_CHUNKS, FF_CHUNK).transpose(1, 0, 2)
    wout = w_out.astype(BF16).reshape(N_FF_CHUNKS, FF_CHUNK, D_MODEL)
    return pl.pallas_call(
        functools.partial(_ffn_kernel, k=k),
        grid=(N_TOK // tm,),
        in_specs=[
            pl.BlockSpec((tm, D_MODEL), lambda i: (i, 0)),
            pl.BlockSpec((None, N_MOD, D_MODEL), lambda i: (_group_of_tile(i, tm), 0, 0)),
            pl.BlockSpec((1, D_MODEL), lambda i: (0, 0)),
            _resident((2 * N_FF_CHUNKS, D_MODEL, FF_CHUNK)),
            _resident((N_FF_CHUNKS, FF_CHUNK, D_MODEL)),
        ],
        out_specs=pl.BlockSpec((tm, D_MODEL), lambda i: (i, 0)),
        out_shape=jax.ShapeDtypeStruct((N_TOK, D_MODEL), F32),
        scratch_shapes=[pltpu.VMEM((tm, D_MODEL), F32)],
        compiler_params=pltpu.CompilerParams(dimension_semantics=("arbitrary",),
                                             vmem_limit_bytes=VMEM_LIMIT),
        name="ffn_swiglu",
    )(x, mod_l, g.reshape(1, D_MODEL), win, wout)


def _proj_in_kernel(x_ref, mod_ref, g_ref, w_ref, o_ref, *, tn, act):
    h, _ = _premod(x_ref[...], g_ref[...], mod_ref, 1)
    hb = h.astype(BF16)
    for j in range(w_ref.shape[1] // tn):
        y = jnp.dot(hb, w_ref[:, j * tn:(j + 1) * tn], preferred_element_type=F32)
        if act == "gelu":
            y = jax.nn.gelu(y)
        o_ref[:, j * tn:(j + 1) * tn] = y


def _proj_in(x, mod_l, g, w, tn, act=None):
    tm = TM_PROJ
    n_out = w.shape[1]
    return pl.pallas_call(
        functools.partial(_proj_in_kernel, tn=tn, act=act),
        grid=(N_TOK // tm,),
        in_specs=[
            pl.BlockSpec((tm, D_MODEL), lambda i: (i, 0)),
            pl.BlockSpec((None, N_MOD, D_MODEL), lambda i: (_group_of_tile(i, tm), 0, 0)),
            pl.BlockSpec((1, D_MODEL), lambda i: (0, 0)),
            _resident((D_MODEL, n_out)),
        ],
        out_specs=pl.BlockSpec((tm, n_out), lambda i: (i, 0)),
        out_shape=jax.ShapeDtypeStruct((N_TOK, n_out), F32),
        compiler_params=pltpu.CompilerParams(dimension_semantics=("arbitrary",),
                                             vmem_limit_bytes=VMEM_LIMIT),
        name="mixer_proj_in",
    )(x, mod_l, g.reshape(1, D_MODEL), w.astype(BF16))


def _proj_out_kernel(x_ref, o_in_ref, mod_ref, ng_ref, w_ref, o_ref, *, norm):
    gate = mod_ref[5:6, :]
    o_in = o_in_ref[...]
    if norm:
        o_in = o_in * lax.rsqrt(jnp.mean(o_in * o_in, axis=-1, keepdims=True) + EPS) * ng_ref[...]
    y = jnp.dot(o_in.astype(BF16), w_ref[...], preferred_element_type=F32)
    o_ref[...] = x_ref[...] + gate * y


def _proj_out(x, o_in, mod_l, w, norm_g=None):
    tm = TM_PROJ
    kdim = w.shape[0]
    ng = jnp.ones((1, kdim), F32) if norm_g is None else norm_g.reshape(1, kdim)
    return pl.pallas_call(
        functools.partial(_proj_out_kernel, norm=norm_g is not None),
        grid=(N_TOK // tm,),
        in_specs=[
            pl.BlockSpec((tm, D_MODEL), lambda i: (i, 0)),
            pl.BlockSpec((tm, kdim), lambda i: (i, 0)),
            pl.BlockSpec((None, N_MOD, D_MODEL), lambda i: (_group_of_tile(i, tm), 0, 0)),
            pl.BlockSpec((1, kdim), lambda i: (0, 0)),
            _resident((kdim, D_MODEL)),
        ],
        out_specs=pl.BlockSpec((tm, D_MODEL), lambda i: (i, 0)),
        out_shape=jax.ShapeDtypeStruct((N_TOK, D_MODEL), F32),
        compiler_params=pltpu.CompilerParams(dimension_semantics=("arbitrary",),
                                             vmem_limit_bytes=VMEM_LIMIT),
        name="mixer_proj_out",
    )(x, o_in, mod_l, ng, w.astype(BF16))


def _final_norm_kernel(x_ref, g_ref, o_ref):
    x = x_ref[...]
    ms = jnp.mean(x * x, axis=-1, keepdims=True)
    o_ref[...] = x * lax.rsqrt(ms + EPS) * g_ref[...]


def _final_norm(x, g):
    tm = 1024
    return pl.pallas_call(
        _final_norm_kernel,
        grid=(N_TOK // tm,),
        in_specs=[pl.BlockSpec((tm, D_MODEL), lambda i: (i, 0)),
                  pl.BlockSpec((1, D_MODEL), lambda i: (0, 0))],
        out_specs=pl.BlockSpec((tm, D_MODEL), lambda i: (i, 0)),
        out_shape=jax.ShapeDtypeStruct((N_TOK, D_MODEL), F32),
        name="final_rmsnorm",
    )(x, g.reshape(1, D_MODEL))


def _split_streams(y):
    return y[:N_PROMPT].reshape(BATCH, SEQ, -1), y[N_PROMPT:].reshape(DEC_BATCH, DEC_SEQ, -1)


def _join_streams(yp, ys):
    return jnp.concatenate([yp.reshape(N_PROMPT, -1), ys.reshape(N_SAMPLE, -1)], axis=0)


def _rmsnorm(x, g):
    return x * lax.rsqrt(jnp.mean(x * x, axis=-1, keepdims=True) + EPS) * g


def _flip_t(t):
    return jnp.flip(t, axis=1)


def _rope_2d(x):
    T, dh = x.shape[1], x.shape[-1]
    half = dh // 2
    t = jnp.arange(T)
    inv = ROPE_THETA ** (-jnp.arange(0, half, 2, dtype=F32) / half)

    def rot(xa, pos):
        ang = pos.astype(F32)[:, None] * inv
        cos = jnp.cos(ang)[None, :, None, :]
        sin = jnp.sin(ang)[None, :, None, :]
        x1, x2 = jnp.split(xa, 2, axis=-1)
        return jnp.concatenate([x1 * cos - x2 * sin, x1 * sin + x2 * cos], axis=-1)

    return jnp.concatenate([rot(x[..., :half], t // GRID_W), rot(x[..., half:], t % GRID_W)], axis=-1)


def _gla_chunked(q, k, v, log_a, s0):
    B_, T, H, _ = q.shape
    dv = v.shape[-1]
    C = GLA_CHUNK
    n = T // C

    def blk(t):
        return t.reshape(B_, n, C, H, -1).transpose(0, 1, 3, 2, 4)

    q, k, v, la = blk(q), blk(k), blk(v), blk(log_a)
    b = jnp.cumsum(la, axis=3)
    tril = jnp.tril(jnp.ones((C, C), dtype=bool))
    diff = b[:, :, :, :, None, :] - b[:, :, :, None, :, :]
    decay = jnp.exp(jnp.where(tril[:, :, None], diff, -jnp.inf))
    attn = jnp.einsum('bnhid,bnhjd,bnhijd->bnhij', q, k, decay)
    o_intra = jnp.einsum('bnhij,bnhjv->bnhiv', attn, v)
    b_last = b[:, :, :, -1:, :]
    q_dec = q * jnp.exp(b)
    u = jnp.einsum('bnhjd,bnhjv->bnhdv', k * jnp.exp(b_last - b), v)
    g = jnp.exp(b_last[:, :, :, 0])

    def step(s, xs):
        qd, un, gn = xs
        o = jnp.einsum('bhid,bhdv->bhiv', qd, s)
        return gn[..., None] * s + un, o

    s_fin, o_inter = lax.scan(step, s0,
                              (jnp.swapaxes(q_dec, 0, 1), jnp.swapaxes(u, 0, 1), jnp.swapaxes(g, 0, 1)))
    o = o_intra + jnp.swapaxes(o_inter, 0, 1)
    return o.transpose(0, 1, 3, 2, 4).reshape(B_, T, H, dv), s_fin


def _gla_core(y, s0, w_a2, b_a, norm_g, use_rope):
    B_, T, _ = y.shape
    nq = GLA_H * GLA_DK
    nv = GLA_H * GLA_DV
    q = y[..., :nq].reshape(B_, T, GLA_H, GLA_DK) * (GLA_DK ** -0.5)
    k = y[..., nq:2 * nq].reshape(B_, T, GLA_H, GLA_DK)
    v = y[..., 2 * nq:2 * nq + nv].reshape(B_, T, GLA_H, GLA_DV)
    r = y[..., 2 * nq + nv:2 * nq + 2 * nv]
    za = y[..., GLA_IN:GLA_IN + 2 * GLA_RANK].reshape(B_, T, 2, GLA_RANK)
    if use_rope:
        q, k = _rope_2d(q), _rope_2d(k)
    z = jnp.einsum('bter,erk->btek', za, w_a2) + b_a
    log_a = (jax.nn.log_sigmoid(z) / GLA_TAU).reshape(B_, T, 2, GLA_H, GLA_DK)
    o_f, s_f = _gla_chunked(q, k, v, log_a[:, :, 0], s0[:, 0])
    o_b, s_b = _gla_chunked(_flip_t(q), _flip_t(k), _flip_t(v), _flip_t(log_a[:, :, 1]), s0[:, 1])
    o = o_f + _flip_t(o_b)
    o = _rmsnorm(o, norm_g.reshape(GLA_H, GLA_DV))
    o = o.reshape(B_, T, nv) * jax.nn.silu(r)
    return o, jnp.stack([s_f, s_b], axis=1)


def _nat_tables(rows):
    wh = min(NAT_WH, rows)
    r = np.arange(rows)
    row_idx = np.clip(r - wh // 2, 0, rows - wh)[:, None] + np.arange(wh)
    ncb = GRID_W // NAT_QB
    col_idx = np.clip(np.arange(ncb) * NAT_QB - (NAT_KB - NAT_QB) // 2, 0,
                      GRID_W - NAT_KB)[:, None] + np.arange(NAT_KB)
    qcol = np.arange(ncb)[:, None] * NAT_QB + np.arange(NAT_QB)
    c_start = np.clip(qcol - NAT_WW // 2, 0, GRID_W - NAT_WW)
    kc = col_idx[:, None, :]
    col_ok = (kc >= c_start[..., None]) & (kc < c_start[..., None] + NAT_WW)
    dc = kc - qcol[..., None]
    dr = row_idx - r[:, None]
    full = (rows, ncb, NAT_QB, wh, NAT_KB)
    flat = (rows, ncb, NAT_QB, wh * NAT_KB)
    dr_i = np.broadcast_to(dr[:, None, None, :, None] + NAT_WH - 1, full).reshape(flat)
    dc_i = np.broadcast_to(np.clip(dc + NAT_WW - 1, 0, 2 * NAT_WW - 2)[None, :, :, None, :], full).reshape(flat)
    ok = np.broadcast_to(col_ok[None, :, :, None, :], full).reshape(flat)
    return row_idx, col_idx, dr_i, dc_i, ok


def _nat_context_core(y):
    B_, S, _ = y.shape
    q, k, v = jnp.split(y, 3, axis=-1)
    q = q.reshape(B_, S, NAT_H, NAT_HD)
    k = k.reshape(B_, S, NAT_H, NAT_HD)
    v = v.reshape(B_, S, NAT_H, NAT_HD)
    s = jnp.einsum('bqhd,bkhd->bhqk', q, k) * (NAT_HD ** -0.5)
    p = jax.nn.softmax(s, axis=-1)
    o = jnp.einsum('bhqk,bkhd->bqhd', p, v).reshape(B_, S, D_MODEL)
    return o, k.transpose(0, 2, 1, 3), v.transpose(0, 2, 1, 3)


def _nat_latent_core(y, ck, cv, rpb):
    B_, T, _ = y.shape
    rows = T // GRID_W
    ncb = GRID_W // NAT_QB
    row_idx, col_idx, dr_i, dc_i, ok = _nat_tables(rows)
    q, k, v = jnp.split(y, 3, axis=-1)
    q = q.reshape(B_, rows, ncb, NAT_QB, NAT_H, NAT_HD)
    k = k.reshape(B_, rows, GRID_W, NAT_H, NAT_HD)
    v = v.reshape(B_, rows, GRID_W, NAT_H, NAT_HD)
    ri = row_idx[:, None, :, None]
    ci = col_idx[None, :, None, :]
    kb = k[:, ri, ci].reshape(B_, rows, ncb, -1, NAT_H, NAT_HD)
    vb = v[:, ri, ci].reshape(B_, rows, ncb, -1, NAT_H, NAT_HD)
    nk = kb.shape[3]
    scale = NAT_HD ** -0.5
    s_lat = jnp.einsum('brnqhd,brnkhd->bhrnqk', q, kb) * scale
    s_lat = s_lat + rpb[:, dr_i, dc_i]
    s_lat = jnp.where(ok, s_lat, NEG_INF)
    s_ctx = jnp.einsum('brnqhd,bhsd->bhrnqs', q, ck) * scale
    p = jax.nn.softmax(jnp.concatenate([s_lat, s_ctx], axis=-1), axis=-1)
    o = (jnp.einsum('bhrnqk,brnkhd->brnqhd', p[..., :nk], vb)
         + jnp.einsum('bhrnqs,bhsd->brnqhd', p[..., nk:], cv))
    return o.reshape(B_, T, D_MODEL)


def _gmlp_core(y, ln_g, ln_b, w_s, b_s):
    B_, T, _ = y.shape
    u, v = jnp.split(y, 2, axis=-1)
    vc = v - jnp.mean(v, axis=-1, keepdims=True)
    v = vc * lax.rsqrt(jnp.mean(vc * vc, axis=-1, keepdims=True) + EPS) * ln_g + ln_b
    v = v.reshape(B_, T // GM_CHUNK, GM_CHUNK, GM_G, GM_CG)
    v = jnp.einsum('gpq,bnqgc->bnpgc', w_s, v) + b_s.T[None, None, :, :, None]
    return u * v.reshape(B_, T, GM_DH)


GLA_BLK = 128
GLA_CPB = GLA_BLK // GLA_CHUNK
GLA_ZCOL = GLA_IN // LANE


def _gla_consts():
    r = np.arange(GLA_BLK)
    same = (r[:, None] // GLA_CHUNK) == (r[None, :] // GLA_CHUNK)
    ri, ci = r[:, None] % GLA_CHUNK, r[None, :] % GLA_CHUNK
    lf = np.concatenate([same & (ci <= ri), same & (ci > ri)], axis=0)
    lb = np.concatenate([same & (ci >= ri), same & (ci < ri)], axis=0)
    rows = np.arange(GLA_CHUNK * GLA_DK)
    sel = (rows[:, None] // GLA_DK) == (np.arange(LANE)[None, :] % GLA_CHUNK)
    return (jnp.asarray(lf, BF16), jnp.asarray(lb, BF16), jnp.asarray(sel, BF16))


def _rope_tables(T):
    half = GLA_DK // 2
    t = np.arange(T)
    inv = ROPE_THETA ** (-np.arange(0, half, 2, dtype=np.float64) / half)
    lane = np.arange(GLA_DK)
    pos = np.where(lane[None, :] < half, (t // GRID_W)[:, None], (t % GRID_W)[:, None])
    ang = pos * inv[lane % (half // 2)][None, :]
    sign = np.where((lane % half) < half // 2, -1.0, 1.0)[None, :]
    return jnp.asarray(np.cos(ang), F32), jnp.asarray(np.sin(ang) * sign, F32)


def _rope_apply(x, cos, sin_signed):
    half = GLA_DK // 2
    lane = lax.broadcasted_iota(jnp.int32, (1, GLA_DK), 1)
    partner = jnp.where((lane % half) < half // 2,
                        pltpu.roll(x, GLA_DK - half // 2, axis=1), pltpu.roll(x, half // 2, axis=1))
    return x * cos + partner * sin_signed


def _split3(x):
    hi = x.astype(BF16)
    r1 = x - hi.astype(F32)
    mid = r1.astype(BF16)
    lo = (r1 - mid.astype(F32)).astype(BF16)
    return hi, mid, lo


def _dot3(m, x):
    hi, mid, lo = _split3(x)
    d = lambda p: jnp.dot(m, p, preferred_element_type=F32)
    return d(hi) + d(mid) + d(lo)


def _gla_kernel(*refs, use_rope, has_s0, want_state):
    it = iter(refs)
    q_ref, k_ref, v_ref, r_ref, za_ref, w2_ref, ba_ref, ng_ref, lf_ref, lb_ref, sel_ref = (next(it) for _ in range(11))
    cos_ref, sin_ref = (next(it), next(it)) if use_rope else (None, None)
    s0_ref = next(it) if has_s0 else None
    o_ref = next(it)
    st_ref = next(it) if want_state else None
    (q_s, k_s, bf_s, bb_s, ef_s, eb_s, qdf_s, kdf_s, qdb_s, kdb_s, v_s, w_s, o_s, sf_s, sb_s) = it

    T = q_ref.shape[0]
    nblk, nch = T // GLA_BLK, T // GLA_CHUNK

    for blk in range(nblk):
        rows = slice(blk * GLA_BLK, (blk + 1) * GLA_BLK)
        z = jnp.dot(za_ref[rows, :].astype(BF16), w2_ref[...], preferred_element_type=F32) + ba_ref[...]
        la = (jnp.minimum(z, 0.0) - jnp.log1p(jnp.exp(-jnp.abs(z)))) * (1.0 / GLA_TAU)
        cf = _dot3(lf_ref[...], la[:, :GLA_DK])
        cb = _dot3(lb_ref[...], la[:, GLA_DK:])
        bf, bb = cf[:GLA_BLK], cb[:GLA_BLK]
        ef, eb = jnp.exp(bf), jnp.exp(bb)
        q = q_ref[rows, :] * (GLA_DK ** -0.5)
        k = k_ref[rows, :]
        if use_rope:
            q = _rope_apply(q, cos_ref[rows, :], sin_ref[rows, :])
            k = _rope_apply(k, cos_ref[rows, :], sin_ref[rows, :])
        q_s[rows, :], k_s[rows, :] = q, k
        bf_s[rows, :], bb_s[rows, :] = bf, bb
        ef_s[rows, :], eb_s[rows, :] = ef, eb
        qdf_s[rows, :] = (q * ef).astype(BF16)
        qdb_s[rows, :] = (q * eb).astype(BF16)
        kdf_s[rows, :] = (k * jnp.exp(cf[GLA_BLK:])).astype(BF16)
        kdb_s[rows, :] = (k * jnp.exp(cb[GLA_BLK:])).astype(BF16)
        v_s[rows, :] = v_ref[rows, :].astype(BF16)

    irow = lax.broadcasted_iota(jnp.int32, (GLA_CHUNK, 1), 0)

    def intra_chunk(c, carry):
        rows = pl.ds(pl.multiple_of(c * GLA_CHUNK, GLA_CHUNK), GLA_CHUNK)
        qc, bfc, bbc = q_s[rows, :], bf_s[rows, :], bb_s[rows, :]
        for j in range(GLA_CHUNK):
            row = pl.ds(c * GLA_CHUNK + j, 1)
            e = (jnp.exp(jnp.where(irow >= j, bfc - bf_s[row, :], -jnp.inf))
                 + jnp.exp(jnp.where(irow <= j, bbc - bb_s[row, :], -jnp.inf)))
            w_s[rows, j * GLA_DK:(j + 1) * GLA_DK] = (qc * k_s[row, :] * e).astype(BF16)
        return carry

    lax.fori_loop(0, nch, intra_chunk, 0)

    lane_c = lax.broadcasted_iota(jnp.int32, (GLA_BLK, LANE), 1) // GLA_CHUNK
    row_c = lax.broadcasted_iota(jnp.int32, (GLA_BLK, LANE), 0) // GLA_CHUNK
    for blk in range(nblk):
        rows = slice(blk * GLA_BLK, (blk + 1) * GLA_BLK)
        a = jnp.dot(w_s[rows, :], sel_ref[...], preferred_element_type=F32)
        a = jnp.where(lane_c == row_c, a, 0.0).astype(BF16)
        o_s[rows, :] = jnp.dot(a, v_s[rows, :], preferred_element_type=F32)

    if has_s0:
        sf_s[...] = s0_ref[0].T
        sb_s[...] = s0_ref[1].T
    else:
        sf_s[...] = jnp.zeros_like(sf_s)
        sb_s[...] = jnp.zeros_like(sb_s)

    def chain(rows, g_row, qd_s, kd_s, e_s, st_s):
        s = st_s[...]
        o_s[rows, :] += _dot_nt(qd_s[rows, :], s.astype(BF16))
        u = lax.dot_general(v_s[rows, :], kd_s[rows, :], (((0,), (0,)), ((), ())), preferred_element_type=F32)
        st_s[...] = s * e_s[g_row, :] + u

    def inter_chunk(n, carry):
        cf_ = pl.multiple_of(n * GLA_CHUNK, GLA_CHUNK)
        cb_ = pl.multiple_of((nch - 1 - n) * GLA_CHUNK, GLA_CHUNK)
        chain(pl.ds(cf_, GLA_CHUNK), pl.ds(cf_ + GLA_CHUNK - 1, 1), qdf_s, kdf_s, ef_s, sf_s)
        chain(pl.ds(cb_, GLA_CHUNK), pl.ds(cb_, 1), qdb_s, kdb_s, eb_s, sb_s)
        return carry

    lax.fori_loop(0, nch, inter_chunk, 0, unroll=4)

    o = o_s[...]
    o = o * lax.rsqrt(jnp.mean(o * o, axis=-1, keepdims=True) + EPS) * ng_ref[...]
    r = r_ref[...]
    o_ref[...] = o * (r * jax.nn.sigmoid(r))
    if want_state:
        st_ref[0] = sf_s[...].T
        st_ref[1] = sb_s[...].T


def _gla_call(y, w2, ba, ng, T, n_seq, row0, use_rope, s0):
    has_s0 = s0 is not None
    want_state = not has_s0
    rb = row0 // T
    nqb = GLA_H
    col = lambda w, off: pl.BlockSpec((T, w), lambda b, h: (rb + b, off + h))
    cst = lambda shape: pl.BlockSpec(shape, lambda b, h: (0,) * len(shape))
    per_head = lambda shape: pl.BlockSpec((None,) + shape, lambda b, h: (h,) + (0,) * len(shape))
    lf, lb, sel = _gla_consts()
    in_specs = [col(GLA_DK, 0), col(GLA_DK, nqb), col(GLA_DV, nqb), col(GLA_DV, nqb + GLA_H),
                pl.BlockSpec((T, LANE), lambda b, h: (rb + b, GLA_ZCOL)),
                per_head((LANE, 2 * GLA_DK)), per_head((1, 2 * GLA_DK)), per_head((1, GLA_DV)),
                cst(lf.shape), cst(lb.shape), cst(sel.shape)]
    args = [y, y, y, y, y, w2, ba, ng, lf, lb, sel]
    if use_rope:
        cos, sin = _rope_tables(T)
        in_specs += [cst(cos.shape), cst(sin.shape)]
        args += [cos, sin]
    if has_s0:
        in_specs.append(pl.BlockSpec((None, 2, None, GLA_DK, GLA_DV), lambda b, h: (b, 0, h, 0, 0)))
        args.append(s0)
    out_specs = [pl.BlockSpec((T, GLA_DV), lambda b, h: (b, h))]
    out_shape = [jax.ShapeDtypeStruct((n_seq * T, GLA_H * GLA_DV), F32)]
    if want_state:
        out_specs.append(pl.BlockSpec((None, 2, None, GLA_DK, GLA_DV), lambda b, h: (b, 0, h, 0, 0)))
        out_shape.append(jax.ShapeDtypeStruct((n_seq, 2, GLA_H, GLA_DK, GLA_DV), F32))
    f32s = lambda w: pltpu.VMEM((T, w), F32)
    bf16s = lambda w: pltpu.VMEM((T, w), BF16)
    scratch = ([f32s(GLA_DK)] * 6 + [bf16s(GLA_DK)] * 4
               + [bf16s(GLA_DV), bf16s(GLA_CHUNK * GLA_DK), f32s(GLA_DV),
                  pltpu.VMEM((GLA_DV, GLA_DK), F32), pltpu.VMEM((GLA_DV, GLA_DK), F32)])
    outs = pl.pallas_call(
        functools.partial(_gla_kernel, use_rope=use_rope, has_s0=has_s0, want_state=want_state),
        grid=(n_seq, GLA_H),
        in_specs=in_specs, out_specs=out_specs, out_shape=out_shape,
        scratch_shapes=scratch,
        compiler_params=pltpu.CompilerParams(dimension_semantics=("arbitrary", "arbitrary"),
                                             vmem_limit_bytes=VMEM_LIMIT),
        name="gla_rope" if use_rope else "gla",
    )(*args)
    return (outs[0], outs[1]) if want_state else (outs[0], None)


def _gla_mixer(y, state, w_a2, b_a, norm_g):
    w2 = jnp.zeros((GLA_H, LANE, 2 * GLA_DK), F32)
    for e in range(2):
        we = w_a2[e].reshape(GLA_RANK, GLA_H, GLA_DK).transpose(1, 0, 2)
        w2 = w2.at[:, e * GLA_RANK:(e + 1) * GLA_RANK, e * GLA_DK:(e + 1) * GLA_DK].set(we)
    ba = b_a.reshape(2, GLA_H, GLA_DK).transpose(1, 0, 2).reshape(GLA_H, 1, 2 * GLA_DK)
    ng = norm_g.reshape(GLA_H, 1, GLA_DV)
    op, st = _gla_call(y, w2.astype(BF16), ba, ng, SEQ, BATCH, 0, False, None)
    os_, _ = _gla_call(y, w2.astype(BF16), ba, ng, DEC_SEQ, DEC_BATCH, N_PROMPT, True, state)
    return jnp.concatenate([op, os_], axis=0), st


N_HEAD_PAIRS = NAT_H // 2
NAT_ROWS = DEC_SEQ // GRID_W
NAT_WIN = NAT_WH * GRID_W
NAT_CLS = NAT_WH


def _nat_row_window(r):
    rs = min(max(r - NAT_WH // 2, 0), NAT_ROWS - NAT_WH)
    return rs, r - rs


def _nat_bias_table(rpb):
    qc = np.arange(GRID_W)[:, None]
    kc = np.arange(GRID_W)[None, :]
    c_start = np.clip(qc - NAT_WW // 2, 0, GRID_W - NAT_WW)
    ok = (kc >= c_start) & (kc < c_start + NAT_WW)
    dc = np.clip(kc - qc + NAT_WW - 1, 0, 2 * NAT_WW - 2)
    cls = np.arange(NAT_CLS)[:, None]
    w = np.arange(NAT_WH)[None, :]
    dr = w - cls + NAT_WH - 1
    t = rpb[:, dr[:, None, :, None], dc[None, :, None, :]]
    t = jnp.where(ok[None, None, :, None, :], t, NEG_INF)
    return t.reshape(NAT_H, NAT_CLS, GRID_W, NAT_WIN)


def _head_mask(hh):
    lane = lax.broadcasted_iota(jnp.int32, (1, LANE), 1)
    return (lane < NAT_HD) if hh == 0 else (lane >= NAT_HD)


def _dot_nt(a, b):
    return lax.dot_general(a, b, (((1,), (1,)), ((), ())), preferred_element_type=F32)


def _nat_ctx_kernel(q_ref, k_ref, v_ref, o_ref):
    q = q_ref[...] * (NAT_HD ** -0.5)
    kb = k_ref[...].astype(BF16)
    vb = v_ref[...].astype(BF16)
    outs = []
    for hh in range(2):
        qm = jnp.where(_head_mask(hh), q, 0.0).astype(BF16)
        s = _dot_nt(qm, kb)
        p = jnp.exp(s - jnp.max(s, axis=-1, keepdims=True))
        l = jnp.sum(p, axis=-1, keepdims=True)
        outs.append(jnp.dot(p.astype(BF16), vb, preferred_element_type=F32) / l)
    o_ref[...] = jnp.where(_head_mask(0), outs[0], outs[1])


def _nat_context(y):
    blk = lambda off: pl.BlockSpec((SEQ, LANE), lambda b, hp: (b, off + hp))
    return pl.pallas_call(
        _nat_ctx_kernel,
        grid=(BATCH, N_HEAD_PAIRS),
        in_specs=[blk(0), blk(N_HEAD_PAIRS), blk(2 * N_HEAD_PAIRS)],
        out_specs=pl.BlockSpec((SEQ, LANE), lambda b, hp: (b, hp)),
        out_shape=jax.ShapeDtypeStruct((N_TOK, D_MODEL), F32),
        compiler_params=pltpu.CompilerParams(dimension_semantics=("arbitrary", "arbitrary")),
        name="nat_context",
    )(y, y, y)


def _nat_lat_kernel(q_ref, k_ref, v_ref, ck_ref, cv_ref, tab_ref, o_in_ref, o_ref):
    del o_in_ref
    q = q_ref[...] * (NAT_HD ** -0.5)
    qm = [jnp.where(_head_mask(hh), q, 0.0).astype(BF16) for hh in range(2)]
    ckb = ck_ref[...].astype(BF16)
    cvb = cv_ref[...].astype(BF16)
    for r in range(NAT_ROWS):
        rs, cls = _nat_row_window(r)
        kw = k_ref[rs * GRID_W:rs * GRID_W + NAT_WIN, :].astype(BF16)
        vw = v_ref[rs * GRID_W:rs * GRID_W + NAT_WIN, :].astype(BF16)
        outs = []
        for hh in range(2):
            qr = qm[hh][r * GRID_W:(r + 1) * GRID_W]
            s_lat = _dot_nt(qr, kw) + tab_ref[hh, cls]
            s_ctx = _dot_nt(qr, ckb)
            m = jnp.maximum(jnp.max(s_lat, axis=-1, keepdims=True), jnp.max(s_ctx, axis=-1, keepdims=True))
            p_lat = jnp.exp(s_lat - m)
            p_ctx = jnp.exp(s_ctx - m)
            l = jnp.sum(p_lat, axis=-1, keepdims=True) + jnp.sum(p_ctx, axis=-1, keepdims=True)
            o = (jnp.dot(p_lat.astype(BF16), vw, preferred_element_type=F32)
                 + jnp.dot(p_ctx.astype(BF16), cvb, preferred_element_type=F32))
            outs.append(o / l)
        o_ref[r * GRID_W:(r + 1) * GRID_W, :] = jnp.where(_head_mask(0), outs[0], outs[1])


def _nat_latent(y, o_ctx, ck, cv, rpb):
    row0 = N_PROMPT // DEC_SEQ
    blk = lambda off: pl.BlockSpec((DEC_SEQ, LANE), lambda b, hp: (row0 + b, off + hp))
    ctx = pl.BlockSpec((None, ck.shape[1], LANE), lambda b, hp: (b, 0, hp))
    return pl.pallas_call(
        _nat_lat_kernel,
        grid=(DEC_BATCH, N_HEAD_PAIRS),
        in_specs=[blk(0), blk(N_HEAD_PAIRS), blk(2 * N_HEAD_PAIRS), ctx, ctx,
                  pl.BlockSpec((2, NAT_CLS, GRID_W, NAT_WIN), lambda b, hp: (hp, 0, 0, 0)),
                  pl.BlockSpec(memory_space=pl.ANY)],
        out_specs=pl.BlockSpec((DEC_SEQ, LANE), lambda b, hp: (row0 + b, hp)),
        out_shape=jax.ShapeDtypeStruct((N_TOK, D_MODEL), F32),
        input_output_aliases={6: 0},
        compiler_params=pltpu.CompilerParams(dimension_semantics=("arbitrary", "arbitrary")),
        name="nat_latent",
    )(y, y, y, ck, cv, _nat_bias_table(rpb), o_ctx)


def _heads_last(t):
    b, h, s, d = t.shape
    return t.transpose(0, 2, 1, 3).reshape(b, s, h * d)


def _heads_first(t):
    b, s, _ = t.shape
    return t.reshape(b, s, NAT_H, NAT_HD).transpose(0, 2, 1, 3)


def _gmlp_kernel(x_ref, y_ref, mod_ref, lng_ref, lnb_ref, ws_ref, bs_ref, w_ref, o_ref, t_ref):
    tm = x_ref.shape[0]
    v = y_ref[:, GM_DH:]
    vc = v - jnp.mean(v, axis=-1, keepdims=True)
    vn = vc * lax.rsqrt(jnp.mean(vc * vc, axis=-1, keepdims=True) + EPS) * lng_ref[...] + lnb_ref[...]
    vn = vn.astype(BF16)
    for n in range(tm // GM_CHUNK):
        rows = slice(n * GM_CHUNK, (n + 1) * GM_CHUNK)
        for g in range(GM_G):
            cols = slice(g * GM_CG, (g + 1) * GM_CG)
            sp = jnp.dot(ws_ref[g], vn[rows, cols], preferred_element_type=F32) + bs_ref[:, cols]
            t_ref[rows, cols] = (y_ref[rows, cols] * sp).astype(BF16)
    gate = mod_ref[5:6, :]
    o_ref[...] = x_ref[...] + gate * jnp.dot(t_ref[...], w_ref[...], preferred_element_type=F32)


def _gmlp_out(x, y, mod_l, ln_g, ln_b, w_s, b_s, w_out):
    tm = TM_PROJ
    bias = jnp.repeat(b_s.T, GM_CG, axis=1)
    return pl.pallas_call(
        _gmlp_kernel,
        grid=(N_TOK // tm,),
        in_specs=[
            pl.BlockSpec((tm, D_MODEL), lambda i: (i, 0)),
            pl.BlockSpec((tm, 2 * GM_DH), lambda i: (i, 0)),
            pl.BlockSpec((None, N_MOD, D_MODEL), lambda i: (_group_of_tile(i, tm), 0, 0)),
            pl.BlockSpec((1, GM_DH), lambda i: (0, 0)),
            pl.BlockSpec((1, GM_DH), lambda i: (0, 0)),
            _resident((GM_G, GM_CHUNK, GM_CHUNK)),
            _resident((GM_CHUNK, GM_DH)),
            _resident((GM_DH, D_MODEL)),
        ],
        out_specs=pl.BlockSpec((tm, D_MODEL), lambda i: (i, 0)),
        out_shape=jax.ShapeDtypeStruct((N_TOK, D_MODEL), F32),
        scratch_shapes=[pltpu.VMEM((tm, GM_DH), BF16)],
        compiler_params=pltpu.CompilerParams(dimension_semantics=("arbitrary",),
                                             vmem_limit_bytes=VMEM_LIMIT),
        name="gmlp_gate_out",
    )(x, y, mod_l, ln_g.reshape(1, GM_DH), ln_b.reshape(1, GM_DH), w_s.astype(BF16), bias,
      w_out.astype(BF16))


def _dwconv_centred(x, w, b):
    K = w.shape[0]
    T = x.shape[1]
    pad = K // 2
    xp = jnp.pad(x, ((0, 0), (pad, pad), (0, 0)))
    out = xp[:, 0:T] * w[0]
    for i in range(1, K):
        out = out + xp[:, i:i + T] * w[i]
    return out + b


def _ssd_chunked(x, dt, a, bm, cm, s0):
    B_, T = x.shape[:2]
    L = SSD_CHUNK
    n = T // L
    E = SSD_H // SSD_G
    x = x.reshape(B_, n, L, SSD_G, E, SSD_P)
    dt = dt.reshape(B_, n, L, SSD_G, E)
    bm = bm.reshape(B_, n, L, SSD_G, SSD_N)
    cm = cm.reshape(B_, n, L, SSD_G, SSD_N)
    cum = jnp.cumsum(dt * a.reshape(SSD_G, E), axis=2)
    tril = jnp.tril(jnp.ones((L, L), dtype=bool))
    seg = cum[:, :, :, None] - cum[:, :, None, :]
    decay = jnp.exp(jnp.where(tril[:, :, None, None], seg, -jnp.inf))
    dtx = x * dt[..., None]
    cb = jnp.einsum('bnigs,bnjgs->bnijg', cm, bm)
    y_diag = jnp.einsum('bnijg,bnijge,bnjgep->bnigep', cb, decay, dtx)
    u = jnp.einsum('bnjgs,bnjge,bnjgep->bngeps', bm, jnp.exp(cum[:, :, -1:] - cum), dtx)
    chunk_decay = jnp.exp(cum[:, :, -1])
    q_decay = jnp.exp(cum)

    def step(s, xs):
        c_n, qd_n, u_n, g_n = xs
        y = jnp.einsum('bigs,bige,bgeps->bigep', c_n, qd_n, s)
        return g_n[..., None, None] * s + u_n, y

    sw = lambda t: jnp.swapaxes(t, 0, 1)
    s_fin, y_off = lax.scan(step, s0.reshape(B_, SSD_G, E, SSD_P, SSD_N),
                            (sw(cm), sw(q_decay), sw(u), sw(chunk_decay)))
    y = y_diag + sw(y_off)
    return y.reshape(B_, T, SSD_H, SSD_P), s_fin.reshape(B_, SSD_H, SSD_P, SSD_N)


def _ssd_core(y, s0, conv_w, conv_b, dt_bias, a_log, d_skip, norm_g):
    B_, T, _ = y.shape
    z = y[..., :SSD_DI]
    xbc = y[..., SSD_DI:SSD_DI + SSD_XBC]
    dt = y[..., SSD_DI + SSD_XBC:SSD_IN]
    xbc = jax.nn.silu(_dwconv_centred(xbc, conv_w, conv_b))
    x = xbc[..., :SSD_DI].reshape(B_, T, SSD_H, SSD_P)
    bm = xbc[..., SSD_DI:SSD_DI + SSD_G * SSD_N].reshape(B_, T, SSD_G, SSD_N)
    cm = xbc[..., SSD_DI + SSD_G * SSD_N:].reshape(B_, T, SSD_G, SSD_N)
    dt = jax.nn.softplus(dt.reshape(B_, T, 2, SSD_H) + dt_bias)
    a = -jnp.exp(a_log)
    y_f, s_f = _ssd_chunked(x, dt[:, :, 0], a[0], bm, cm, s0[:, 0])
    y_b, s_b = _ssd_chunked(_flip_t(x), _flip_t(dt[:, :, 1]), a[1], _flip_t(bm), _flip_t(cm), s0[:, 1])
    yy = y_f + _flip_t(y_b) + d_skip[:, None] * x
    yy = _rmsnorm(yy.reshape(B_, T, SSD_DI) * jax.nn.silu(z), norm_g)
    return yy, jnp.stack([s_f, s_b], axis=1)


SSD_E = SSD_H // SSD_G
SSD_GP = SSD_E * SSD_P
SSD_BLK = 2 * SSD_CHUNK
SSD_COL_X = SSD_DI
SSD_COL_B = 2 * SSD_DI
SSD_COL_C = SSD_COL_B + SSD_G * SSD_N
SSD_COL_DT = SSD_COL_C + SSD_G * SSD_N
SSD_PROJ = SSD_COL_DT + SSD_G * LANE


def _ssd_w_in(w_in):
    base = SSD_DI + SSD_XBC
    w_dt = jnp.zeros((D_MODEL, SSD_G, LANE), F32)
    for g in range(SSD_G):
        cols = jnp.concatenate([w_in[:, base + g * SSD_E:base + (g + 1) * SSD_E],
                                w_in[:, base + SSD_H + g * SSD_E:base + SSD_H + (g + 1) * SSD_E]], axis=1)
        w_dt = w_dt.at[:, g, :2 * SSD_E].set(cols).at[:, g, 2 * SSD_E:4 * SSD_E].set(cols)
    return jnp.concatenate([w_in[:, :base], w_dt.reshape(D_MODEL, SSD_G * LANE)], axis=1)


def _ssd_consts():
    r = np.arange(SSD_BLK)
    same = (r[:, None] // SSD_CHUNK) == (r[None, :] // SSD_CHUNK)
    cum = np.concatenate([same & (r[None, :] <= r[:, None]), same & (r[None, :] >= r[:, None])], axis=0)
    lane = np.arange(LANE)[:, None]
    col = np.arange(SSD_GP)[None, :] // SSD_P
    exp_f = lane == col
    exp_b = lane == col + SSD_E
    return jnp.asarray(cum, BF16), jnp.asarray(exp_f, BF16), jnp.asarray(exp_b, BF16)


def _dot3_r(x, m):
    hi, mid, lo = _split3(x)
    d = lambda p: jnp.dot(p, m, preferred_element_type=F32)
    return d(hi) + d(mid) + d(lo)


def _softplus(x):
    return jnp.maximum(x, 0.0) + jnp.log1p(jnp.exp(-jnp.abs(x)))


def _ssd_kernel(*refs, has_s0, want_state):
    it = iter(refs)
    (z_ref, x_ref, b_ref, c_ref, dt_ref, cwx_ref, cwb_ref, cwc_ref, cbx_ref, cbb_ref, cbc_ref,
     dtb_ref, alog_ref, dsk_ref, cum_ref, ef_ref, eb_ref) = (next(it) for _ in range(17))
    s0_ref = next(it) if has_s0 else None
    o_ref = next(it)
    st_ref = next(it) if want_state else None
    xs_s, xb_s, bm_s, cm_s, cu_s, dt_s, y_s, sf_s, sb_s = it

    T = x_ref.shape[0]
    nch = T // SSD_CHUNK
    L = SSD_CHUNK

    trow = lax.broadcasted_iota(jnp.int32, (T, 1), 0)

    def conv_silu(v_ref, w_ref, bias_ref):
        v = v_ref[...]
        prev = jnp.where(trow == 0, 0.0, pltpu.roll(v, 1, axis=0))
        nxt = jnp.where(trow == T - 1, 0.0, pltpu.roll(v, T - 1, axis=0))
        y = prev * w_ref[0:1, :] + v * w_ref[1:2, :] + nxt * w_ref[2:3, :] + bias_ref[...]
        return y * jax.nn.sigmoid(y)

    xs = conv_silu(x_ref, cwx_ref, cbx_ref)
    xs_s[...] = xs
    xb_s[...] = xs.astype(BF16)
    bm_s[...] = conv_silu(b_ref, cwb_ref, cbb_ref).astype(BF16)
    cm_s[...] = conv_silu(c_ref, cwc_ref, cbc_ref).astype(BF16)

    lane1 = lax.broadcasted_iota(jnp.int32, (1, LANE), 1)
    a_row = jnp.where(lane1 < 2 * SSD_E, -jnp.exp(alog_ref[...]), 0.0)
    for blk in range(T // SSD_BLK):
        rows = slice(blk * SSD_BLK, (blk + 1) * SSD_BLK)
        dt = _softplus(dt_ref[rows, :] + dtb_ref[...])
        c2 = _dot3(cum_ref[...], dt * a_row)
        cu_s[rows, :] = jnp.where(lane1 < SSD_E, c2[:SSD_BLK], c2[SSD_BLK:])
        dt_s[rows, :] = dt

    ii = lax.broadcasted_iota(jnp.int32, (L, LANE), 0)
    jj = lax.broadcasted_iota(jnp.int32, (L, LANE), 1)
    fwd_half = jj < L
    fwd_half1 = lane1 < L
    tri = (fwd_half & (ii >= jj)) | ((jj >= L) & (ii <= jj - L))
    left = lane1 < SSD_P
    for c in range(nch):
        rows = slice(c * L, (c + 1) * L)
        cum_c, dt_c = cu_s[rows, :], dt_s[rows, :]
        bm_c, cm_c = bm_s[rows, :], cm_s[rows, :]
        cb2 = _dot_nt(cm_c, jnp.concatenate([bm_c, bm_c], axis=0))
        arr = jnp.where(lane1 < 2 * SSD_E, cum_c, dt_c)
        arr_t = jnp.concatenate([arr, arr], axis=0).T
        gs = []
        for e in range(SSD_E):
            row_c = jnp.where(fwd_half1, arr_t[e:e + 1, :], arr_t[SSD_E + e:SSD_E + e + 1, :])
            row_dt = jnp.where(fwd_half1, arr_t[2 * SSD_E + e:2 * SSD_E + e + 1, :],
                               arr_t[3 * SSD_E + e:3 * SSD_E + e + 1, :])
            col_c = jnp.where(fwd_half, jnp.broadcast_to(cum_c[:, e:e + 1], (L, LANE)),
                              jnp.broadcast_to(cum_c[:, SSD_E + e:SSD_E + e + 1], (L, LANE)))
            dec = jnp.exp(jnp.where(tri, col_c - row_c, -jnp.inf))
            gs.append((cb2 * dec * row_dt).astype(BF16))
        for pr in range(SSD_E // 2):
            cols = slice(pr * LANE, (pr + 1) * LANE)
            xp = xb_s[rows, cols]
            xl = jnp.where(left, xp, jnp.zeros_like(xp))
            xr = jnp.where(left, jnp.zeros_like(xp), xp)
            lhs = jnp.concatenate([gs[2 * pr], gs[2 * pr + 1]], axis=1)
            rhs = jnp.concatenate([xl, xl, xr, xr], axis=0)
            y_s[rows, cols] = (jnp.dot(lhs, rhs, preferred_element_type=F32)
                               + dsk_ref[:, cols] * xs_s[rows, cols])

    if has_s0:
        sf_s[...] = s0_ref[0]
        sb_s[...] = s0_ref[1]
    else:
        sf_s[...] = jnp.zeros_like(sf_s)
        sb_s[...] = jnp.zeros_like(sb_s)

    def chain(c, last, lane0, exp_ref, st_s):
        rows = slice(c * L, (c + 1) * L)
        cum_c = jnp.where((lane1 >= lane0) & (lane1 < lane0 + SSD_E), cu_s[rows, :], 0.0)
        tot = cum_c[last:last + 1, :]
        qw = jnp.concatenate([jnp.exp(cum_c), jnp.exp(tot - cum_c) * dt_s[rows, :]], axis=0)
        qw = _dot3_r(qw, exp_ref[...])
        qd, w = qw[:L], qw[L:]
        s = st_s[...]
        y_s[rows, :] += jnp.dot(cm_s[rows, :], s.astype(BF16), preferred_element_type=F32) * qd
        xw = (xs_s[rows, :] * w).astype(BF16)
        u = lax.dot_general(bm_s[rows, :], xw, (((0,), (0,)), ((), ())), preferred_element_type=F32)
        st_s[...] = s * qd[last:last + 1, :] + u

    for n in range(nch):
        chain(n, L - 1, 0, ef_ref, sf_s)
        chain(nch - 1 - n, 0, SSD_E, eb_ref, sb_s)

    z = z_ref[...]
    o_ref[...] = y_s[...] * (z * jax.nn.sigmoid(z))
    if want_state:
        st_ref[0] = sf_s[...]
        st_ref[1] = sb_s[...]


def _ssd_call(y, prm, T, n_seq, row0, s0):
    has_s0 = s0 is not None
    want_state = not has_s0
    rb = row0 // T
    col = lambda w, off: pl.BlockSpec((T, w), lambda b, g: (rb + b, off // w + g))
    cst = lambda shape: pl.BlockSpec(shape, lambda b, g: (0,) * len(shape))
    wcol = lambda rows, w, off: pl.BlockSpec((rows, w), lambda b, g: (0, off // w + g))
    per_g = lambda w: pl.BlockSpec((None, 1, w), lambda b, g: (g, 0, 0))
    cum, exp_f, exp_b = _ssd_consts()
    xoff, boff, coff = 0, SSD_DI, SSD_DI + SSD_G * SSD_N
    in_specs = [col(SSD_GP, 0), col(SSD_GP, SSD_COL_X), col(SSD_N, SSD_COL_B), col(SSD_N, SSD_COL_C),
                col(LANE, SSD_COL_DT),
                wcol(SSD_CONV, SSD_GP, xoff), wcol(SSD_CONV, SSD_N, boff), wcol(SSD_CONV, SSD_N, coff),
                wcol(1, SSD_GP, xoff), wcol(1, SSD_N, boff), wcol(1, SSD_N, coff),
                per_g(LANE), per_g(LANE), per_g(SSD_GP),
                cst(cum.shape), cst(exp_f.shape), cst(exp_b.shape)]
    args = [y, y, y, y, y, prm["conv_w"], prm["conv_w"], prm["conv_w"], prm["conv_b"], prm["conv_b"], prm["conv_b"],
            prm["dt_bias"], prm["a_log"], prm["d_skip"], cum, exp_f, exp_b]
    st_spec = pl.BlockSpec((None, 2, None, SSD_N, SSD_GP), lambda b, g: (b, 0, g, 0, 0))
    if has_s0:
        in_specs.append(st_spec)
        args.append(s0)
    out_specs = [pl.BlockSpec((T, SSD_GP), lambda b, g: (b, g))]
    out_shape = [jax.ShapeDtypeStruct((n_seq * T, SSD_DI), F32)]
    if want_state:
        out_specs.append(st_spec)
        out_shape.append(jax.ShapeDtypeStruct((n_seq, 2, SSD_G, SSD_N, SSD_GP), F32))
    scratch = [pltpu.VMEM((T, SSD_GP), F32), pltpu.VMEM((T, SSD_GP), BF16),
               pltpu.VMEM((T, SSD_N), BF16), pltpu.VMEM((T, SSD_N), BF16),
               pltpu.VMEM((T, LANE), F32), pltpu.VMEM((T, LANE), F32), pltpu.VMEM((T, SSD_GP), F32),
               pltpu.VMEM((SSD_N, SSD_GP), F32), pltpu.VMEM((SSD_N, SSD_GP), F32)]
    outs = pl.pallas_call(
        functools.partial(_ssd_kernel, has_s0=has_s0, want_state=want_state),
        grid=(n_seq, SSD_G),
        in_specs=in_specs, out_specs=out_specs, out_shape=out_shape,
        scratch_shapes=scratch,
        compiler_params=pltpu.CompilerParams(dimension_semantics=("arbitrary", "arbitrary"),
                                             vmem_limit_bytes=VMEM_LIMIT),
        name="ssd_state" if has_s0 else "ssd",
    )(*args)
    return (outs[0], outs[1]) if want_state else (outs[0], None)


def _ssd_state_to_kernel(s):
    b = s.shape[0]
    return s.reshape(b, 2, SSD_G, SSD_GP, SSD_N).transpose(0, 1, 2, 4, 3)


def _ssd_state_from_kernel(s):
    b = s.shape[0]
    return s.transpose(0, 1, 2, 4, 3).reshape(b, 2, SSD_H, SSD_P, SSD_N)


def _ssd_mixer(y, state, conv_w, conv_b, dt_bias, a_log, d_skip):
    def lanes(p):
        pg = p.reshape(2, SSD_G, SSD_E).transpose(1, 0, 2).reshape(SSD_G, 2 * SSD_E)
        return jnp.pad(jnp.concatenate([pg, pg], axis=1), ((0, 0), (0, LANE - 4 * SSD_E))).reshape(SSD_G, 1, LANE)
    prm = dict(conv_w=conv_w, conv_b=conv_b.reshape(1, SSD_XBC), dt_bias=lanes(dt_bias), a_log=lanes(a_log),
               d_skip=jnp.repeat(d_skip, SSD_P).reshape(SSD_G, 1, SSD_GP))
    op, st = _ssd_call(y, prm, SEQ, BATCH, 0, None)
    os_, _ = _ssd_call(y, prm, DEC_SEQ, DEC_BATCH, N_PROMPT, _ssd_state_to_kernel(state))
    return jnp.concatenate([op, os_], axis=0), _ssd_state_from_kernel(st)


def _pad_cols(w, mult):
    n = w.shape[1]
    n_pad = -n % mult
    return jnp.pad(w, ((0, 0), (0, n_pad))) if n_pad else w


def kernel(x_prompt, x_sample, state_gla, cache_nat_k, cache_nat_v, state_ssd, c,
           c_ctx, norm_g, w_ada, b_ada, w_ffn_in, w_ffn_out,
           gla_w_in, gla_w_a1, gla_w_a2, gla_b_a, gla_norm_g, gla_w_out,
           nat_w_qkv, nat_rpb, nat_w_out,
           gm_w_in, gm_ln_g, gm_ln_b, gm_w_s, gm_b_s, gm_w_out,
           ssd_w_in, ssd_conv_w, ssd_conv_b, ssd_dt_bias, ssd_a_log, ssd_d, ssd_norm_g, ssd_w_out,
           final_g):
    x = _join_streams(x_prompt, x_sample)
    mod = _modulation_all(c, c_ctx, w_ada, b_ada)
    new_gla, new_k, new_v, new_ssd = [], [], [], []
    for l in range(DEPTH):
        kind, j = l % N_MIXERS, l // N_MIXERS
        x = _ffn(x, mod[l], norm_g[l, 0], w_ffn_in[l, 0], w_ffn_out[l, 0], 0)
        if kind == 0:
            w = jnp.concatenate([gla_w_in[j], gla_w_a1[j, 0], gla_w_a1[j, 1]], axis=1)
            y = _proj_in(x, mod[l], norm_g[l, 1], _pad_cols(w, 640), 640)
            o, st = _gla_mixer(y, state_gla[:, j], gla_w_a2[j], gla_b_a[j], gla_norm_g[j])
            new_gla.append(st)
            x = _proj_out(x, o, mod[l], gla_w_out[j])
        elif kind == 1:
            y = _proj_in(x, mod[l], norm_g[l, 1], nat_w_qkv[j], 768)
            o = _nat_context(y)
            o = _nat_latent(y, o, _heads_last(cache_nat_k[:, j]), _heads_last(cache_nat_v[:, j]), nat_rpb[j])
            yp = y[:N_PROMPT].reshape(BATCH, SEQ, 3 * D_MODEL)
            new_k.append(_heads_first(yp[..., D_MODEL:2 * D_MODEL]))
            new_v.append(_heads_first(yp[..., 2 * D_MODEL:]))
            x = _proj_out(x, o, mod[l], nat_w_out[j])
        elif kind == 2:
            y = _proj_in(x, mod[l], norm_g[l, 1], gm_w_in[j], 512, act="gelu")
            x = _gmlp_out(x, y, mod[l], gm_ln_g[j], gm_ln_b[j], gm_w_s[j], gm_b_s[j], gm_w_out[j])
        else:
            y = _proj_in(x, mod[l], norm_g[l, 1], _ssd_w_in(ssd_w_in[j]), 512)
            o, st = _ssd_mixer(y, state_ssd[:, j], ssd_conv_w[j], ssd_conv_b[j], ssd_dt_bias[j],
                               ssd_a_log[j], ssd_d[j])
            new_ssd.append(st)
            x = _proj_out(x, o, mod[l], ssd_w_out[j], norm_g=ssd_norm_g[j])
        x = _ffn(x, mod[l], norm_g[l, 2], w_ffn_in[l, 1], w_ffn_out[l, 1], 2)
    yn = _final_norm(x, final_g)
    y_prompt, y_sample = _split_streams(yn)
    return (y_prompt, y_sample, jnp.stack(new_gla, axis=1), jnp.stack(new_k, axis=1),
            jnp.stack(new_v, axis=1), jnp.stack(new_ssd, axis=1))
```

```python
import functools
import math

import jax
import jax.numpy as jnp
import numpy as np
from jax import lax
from jax.experimental import pallas as pl
from jax.experimental.pallas import tpu as pltpu

D_MODEL = 1024
BATCH = 32
SEQ = 256
DEPTH = 4
DEC_BATCH = 2
DEC_SEQ = 1024
N_PROMPT = BATCH * SEQ
N_SAMPLE = DEC_BATCH * DEC_SEQ
N_TOK = N_PROMPT + N_SAMPLE
N_GROUPS = 1 + DEC_BATCH

GRID_W = 64
N_MIXERS = 4
N_SUB = 3
N_MOD = 3 * N_SUB
D_FF = 2816
EPS = 1e-6
NEG_INF = -1e30
ROPE_THETA = 10000.0
GLA_H, GLA_DK, GLA_DV, GLA_RANK, GLA_TAU, GLA_CHUNK = 4, 128, 256, 16, 16.0, 16
GLA_IN = 2 * GLA_H * GLA_DK + 2 * GLA_H * GLA_DV
NAT_H, NAT_HD, NAT_WH, NAT_WW, NAT_QB, NAT_KB = 16, 64, 8, 16, 16, 32
GM_DH, GM_G, GM_CHUNK = 1024, 8, 128
GM_CG = GM_DH // GM_G
SSD_DI = 2 * D_MODEL
SSD_P = 64
SSD_H = SSD_DI // SSD_P
SSD_N, SSD_G, SSD_CONV, SSD_CHUNK = 128, 4, 3, 64
SSD_XBC = SSD_DI + 2 * SSD_G * SSD_N
SSD_IN = SSD_DI + SSD_XBC + 2 * SSD_H

LANE = 128
VMEM_LIMIT = 56 * 1024 * 1024
BF16 = jnp.bfloat16
F32 = jnp.float32

FF_CHUNK = 256
N_FF_CHUNKS = D_FF // FF_CHUNK
TM_FFN = 512
TM_PROJ = 512
ADA_TN = 1152


def _group_of_tile(i, tm):
    n_prompt_tiles = N_PROMPT // tm
    return jnp.where(i < n_prompt_tiles, 0, 1 + (i - n_prompt_tiles) // (DEC_SEQ // tm))


def _resident(shape):
    nd = len(shape)
    return pl.BlockSpec(shape, lambda i: (0,) * nd, pipeline_mode=pl.Buffered(1))


def _premod(x, g, mod_ref, k):
    shift = mod_ref[3 * k:3 * k + 1, :]
    scale = mod_ref[3 * k + 1:3 * k + 2, :]
    gate = mod_ref[3 * k + 2:3 * k + 3, :]
    ms = jnp.mean(x * x, axis=-1, keepdims=True)
    h = x * lax.rsqrt(ms + EPS) * g
    return h * (1.0 + scale) + shift, gate


def _ada_kernel(cond_ref, w_ref, b_ref, o_ref):
    cnd = cond_ref[...]
    s = (cnd * jax.nn.sigmoid(cnd)).astype(BF16)
    o_ref[...] = jnp.dot(s, w_ref[...].astype(BF16), preferred_element_type=F32) + b_ref[...]


def _modulation_all(c, c_ctx, w_ada, b_ada):
    rows = 8
    cond = jnp.concatenate([c_ctx[None], c, jnp.zeros((rows - N_GROUPS, D_MODEL), F32)], axis=0)
    n_out = N_MOD * D_MODEL
    out = pl.pallas_call(
        _ada_kernel,
        grid=(DEPTH, n_out // ADA_TN),
        in_specs=[
            pl.BlockSpec((rows, D_MODEL), lambda l, j: (0, 0)),
            pl.BlockSpec((None, D_MODEL, ADA_TN), lambda l, j: (l, 0, j)),
            pl.BlockSpec((None, 1, ADA_TN), lambda l, j: (l, 0, j)),
        ],
        out_specs=pl.BlockSpec((None, rows, ADA_TN), lambda l, j: (l, 0, j)),
        out_shape=jax.ShapeDtypeStruct((DEPTH, rows, n_out), F32),
        compiler_params=pltpu.CompilerParams(dimension_semantics=("arbitrary", "arbitrary")),
        name="ada_modulation",
    )(cond, w_ada, b_ada.reshape(DEPTH, 1, n_out))
    return out[:, :N_GROUPS].reshape(DEPTH, N_GROUPS, N_MOD, D_MODEL)


def _ffn_kernel(x_ref, mod_ref, g_ref, win_ref, wout_ref, o_ref, acc_ref, *, k):
    x = x_ref[...]
    h, gate = _premod(x, g_ref[...], mod_ref, k)
    hb = h.astype(BF16)
    for j in range(N_FF_CHUNKS):
        a = jnp.dot(hb, win_ref[j], preferred_element_type=F32)
        u = jnp.dot(hb, win_ref[N_FF_CHUNKS + j], preferred_element_type=F32)
        t = (a * jax.nn.sigmoid(a) * u).astype(BF16)
        p = jnp.dot(t, wout_ref[j], preferred_element_type=F32)
        if j == 0:
            acc_ref[...] = p
        else:
            acc_ref[...] += p
    o_ref[...] = x + 0.5 * gate * acc_ref[...]


def _ffn(x, mod_l, g, w_in, w_out, k):
    tm = TM_FFN
    win = w_in.astype(BF16).reshape(D_MODEL, 2 * N_FF_CHUNKS, FF_CHUNK).transpose(1, 0, 2)
    wout = w_out.astype(BF16).reshape(N_FF_CHUNKS, FF_CHUNK, D_MODEL)
    return pl.pallas_call(
        functools.partial(_ffn_kernel, k=k),
        grid=(N_TOK // tm,),
        in_specs=[
            pl.BlockSpec((tm, D_MODEL), lambda i: (i, 0)),
            pl.BlockSpec((None, N_MOD, D_MODEL), lambda i: (_group_of_tile(i, tm), 0, 0)),
            pl.BlockSpec((1, D_MODEL), lambda i: (0, 0)),
            _resident((2 * N_FF_CHUNKS, D_MODEL, FF_CHUNK)),
            _resident((N_FF_CHUNKS, FF_CHUNK, D_MODEL)),
        ],
        out_specs=pl.BlockSpec((tm, D_MODEL), lambda i: (i, 0)),
        out_shape=jax.ShapeDtypeStruct((N_TOK, D_MODEL), F32),
        scratch_shapes=[pltpu.VMEM((tm, D_MODEL), F32)],
        compiler_params=pltpu.CompilerParams(dimension_semantics=("arbitrary",),
                                             vmem_limit_bytes=VMEM_LIMIT),
        name="ffn_swiglu",
    )(x, mod_l, g.reshape(1, D_MODEL), win, wout)


def _proj_in_kernel(x_ref, mod_ref, g_ref, w_ref, o_ref, *, tn, act):
    h, _ = _premod(x_ref[...], g_ref[...], mod_ref, 1)
    hb = h.astype(BF16)
    for j in range(w_ref.shape[1] // tn):
        y = jnp.dot(hb, w_ref[:, j * tn:(j + 1) * tn], preferred_element_type=F32)
        if act == "gelu":
            y = jax.nn.gelu(y)
        o_ref[:, j * tn:(j + 1) * tn] = y


def _proj_in(x, mod_l, g, w, tn, act=None):
    tm = TM_PROJ
    n_out = w.shape[1]
    return pl.pallas_call(
        functools.partial(_proj_in_kernel, tn=tn, act=act),
        grid=(N_TOK // tm,),
        in_specs=[
            pl.BlockSpec((tm, D_MODEL), lambda i: (i, 0)),
            pl.BlockSpec((None, N_MOD, D_MODEL), lambda i: (_group_of_tile(i, tm), 0, 0)),
            pl.BlockSpec((1, D_MODEL), lambda i: (0, 0)),
            _resident((D_MODEL, n_out)),
        ],
        out_specs=pl.BlockSpec((tm, n_out), lambda i: (i, 0)),
        out_shape=jax.ShapeDtypeStruct((N_TOK, n_out), F32),
        compiler_params=pltpu.CompilerParams(dimension_semantics=("arbitrary",),
                                             vmem_limit_bytes=VMEM_LIMIT),
        name="mixer_proj_in",
    )(x, mod_l, g.reshape(1, D_MODEL), w.astype(BF16))


def _proj_out_kernel(x_ref, o_in_ref, mod_ref, ng_ref, w_ref, o_ref, *, norm):
    gate = mod_ref[5:6, :]
    o_in = o_in_ref[...]
    if norm:
        o_in = o_in * lax.rsqrt(jnp.mean(o_in * o_in, axis=-1, keepdims=True) + EPS) * ng_ref[...]
    y = jnp.dot(o_in.astype(BF16), w_ref[...], preferred_element_type=F32)
    o_ref[...] = x_ref[...] + gate * y


def _proj_out(x, o_in, mod_l, w, norm_g=None):
    tm = TM_PROJ
    kdim = w.shape[0]
    ng = jnp.ones((1, kdim), F32) if norm_g is None else norm_g.reshape(1, kdim)
    return pl.pallas_call(
        functools.partial(_proj_out_kernel, norm=norm_g is not None),
        grid=(N_TOK // tm,),
        in_specs=[
            pl.BlockSpec((tm, D_MODEL), lambda i: (i, 0)),
            pl.BlockSpec((tm, kdim), lambda i: (i, 0)),
            pl.BlockSpec((None, N_MOD, D_MODEL), lambda i: (_group_of_tile(i, tm), 0, 0)),
            pl.BlockSpec((1, kdim), lambda i: (0, 0)),
            _resident((kdim, D_MODEL)),
        ],
        out_specs=pl.BlockSpec((tm, D_MODEL), lambda i: (i, 0)),
        out_shape=jax.ShapeDtypeStruct((N_TOK, D_MODEL), F32),
        compiler_params=pltpu.CompilerParams(dimension_semantics=("arbitrary",),
                                             vmem_limit_bytes=VMEM_LIMIT),
        name="mixer_proj_out",
    )(x, o_in, mod_l, ng, w.astype(BF16))


def _final_norm_kernel(x_ref, g_ref, o_ref):
    x = x_ref[...]
    ms = jnp.mean(x * x, axis=-1, keepdims=True)
    o_ref[...] = x * lax.rsqrt(ms + EPS) * g_ref[...]


def _final_norm(x, g):
    tm = 1024
    return pl.pallas_call(
        _final_norm_kernel,
        grid=(N_TOK // tm,),
        in_specs=[pl.BlockSpec((tm, D_MODEL), lambda i: (i, 0)),
                  pl.BlockSpec((1, D_MODEL), lambda i: (0, 0))],
        out_specs=pl.BlockSpec((tm, D_MODEL), lambda i: (i, 0)),
        out_shape=jax.ShapeDtypeStruct((N_TOK, D_MODEL), F32),
        name="final_rmsnorm",
    )(x, g.reshape(1, D_MODEL))


def _split_streams(y):
    return y[:N_PROMPT].reshape(BATCH, SEQ, -1), y[N_PROMPT:].reshape(DEC_BATCH, DEC_SEQ, -1)


def _join_streams(yp, ys):
    return jnp.concatenate([yp.reshape(N_PROMPT, -1), ys.reshape(N_SAMPLE, -1)], axis=0)


def _rmsnorm(x, g):
    return x * lax.rsqrt(jnp.mean(x * x, axis=-1, keepdims=True) + EPS) * g


def _flip_t(t):
    return jnp.flip(t, axis=1)


def _rope_2d(x):
    T, dh = x.shape[1], x.shape[-1]
    half = dh // 2
    t = jnp.arange(T)
    inv = ROPE_THETA ** (-jnp.arange(0, half, 2, dtype=F32) / half)

    def rot(xa, pos):
        ang = pos.astype(F32)[:, None] * inv
        cos = jnp.cos(ang)[None, :, None, :]
        sin = jnp.sin(ang)[None, :, None, :]
        x1, x2 = jnp.split(xa, 2, axis=-1)
        return jnp.concatenate([x1 * cos - x2 * sin, x1 * sin + x2 * cos], axis=-1)

    return jnp.concatenate([rot(x[..., :half], t // GRID_W), rot(x[..., half:], t % GRID_W)], axis=-1)


def _gla_chunked(q, k, v, log_a, s0):
    B_, T, H, _ = q.shape
    dv = v.shape[-1]
    C = GLA_CHUNK
    n = T // C

    def blk(t):
        return t.reshape(B_, n, C, H, -1).transpose(0, 1, 3, 2, 4)

    q, k, v, la = blk(q), blk(k), blk(v), blk(log_a)
    b = jnp.cumsum(la, axis=3)
    tril = jnp.tril(jnp.ones((C, C), dtype=bool))
    diff = b[:, :, :, :, None, :] - b[:, :, :, None, :, :]
    decay = jnp.exp(jnp.where(tril[:, :, None], diff, -jnp.inf))
    attn = jnp.einsum('bnhid,bnhjd,bnhijd->bnhij', q, k, decay)
    o_intra = jnp.einsum('bnhij,bnhjv->bnhiv', attn, v)
    b_last = b[:, :, :, -1:, :]
    q_dec = q * jnp.exp(b)
    u = jnp.einsum('bnhjd,bnhjv->bnhdv', k * jnp.exp(b_last - b), v)
    g = jnp.exp(b_last[:, :, :, 0])

    def step(s, xs):
        qd, un, gn = xs
        o = jnp.einsum('bhid,bhdv->bhiv', qd, s)
        return gn[..., None] * s + un, o

    s_fin, o_inter = lax.scan(step, s0,
                              (jnp.swapaxes(q_dec, 0, 1), jnp.swapaxes(u, 0, 1), jnp.swapaxes(g, 0, 1)))
    o = o_intra + jnp.swapaxes(o_inter, 0, 1)
    return o.transpose(0, 1, 3, 2, 4).reshape(B_, T, H, dv), s_fin


def _gla_core(y, s0, w_a2, b_a, norm_g, use_rope):
    B_, T, _ = y.shape
    nq = GLA_H * GLA_DK
    nv = GLA_H * GLA_DV
    q = y[..., :nq].reshape(B_, T, GLA_H, GLA_DK) * (GLA_DK ** -0.5)
    k = y[..., nq:2 * nq].reshape(B_, T, GLA_H, GLA_DK)
    v = y[..., 2 * nq:2 * nq + nv].reshape(B_, T, GLA_H, GLA_DV)
    r = y[..., 2 * nq + nv:2 * nq + 2 * nv]
    za = y[..., GLA_IN:GLA_IN + 2 * GLA_RANK].reshape(B_, T, 2, GLA_RANK)
    if use_rope:
        q, k = _rope_2d(q), _rope_2d(k)
    z = jnp.einsum('bter,erk->btek', za, w_a2) + b_a
    log_a = (jax.nn.log_sigmoid(z) / GLA_TAU).reshape(B_, T, 2, GLA_H, GLA_DK)
    o_f, s_f = _gla_chunked(q, k, v, log_a[:, :, 0], s0[:, 0])
    o_b, s_b = _gla_chunked(_flip_t(q), _flip_t(k), _flip_t(v), _flip_t(log_a[:, :, 1]), s0[:, 1])
    o = o_f + _flip_t(o_b)
    o = _rmsnorm(o, norm_g.reshape(GLA_H, GLA_DV))
    o = o.reshape(B_, T, nv) * jax.nn.silu(r)
    return o, jnp.stack([s_f, s_b], axis=1)


def _nat_tables(rows):
    wh = min(NAT_WH, rows)
    r = np.arange(rows)
    row_idx = np.clip(r - wh // 2, 0, rows - wh)[:, None] + np.arange(wh)
    ncb = GRID_W // NAT_QB
    col_idx = np.clip(np.arange(ncb) * NAT_QB - (NAT_KB - NAT_QB) // 2, 0,
                      GRID_W - NAT_KB)[:, None] + np.arange(NAT_KB)
    qcol = np.arange(ncb)[:, None] * NAT_QB + np.arange(NAT_QB)
    c_start = np.clip(qcol - NAT_WW // 2, 0, GRID_W - NAT_WW)
    kc = col_idx[:, None, :]
    col_ok = (kc >= c_start[..., None]) & (kc < c_start[..., None] + NAT_WW)
    dc = kc - qcol[..., None]
    dr = row_idx - r[:, None]
    full = (rows, ncb, NAT_QB, wh, NAT_KB)
    flat = (rows, ncb, NAT_QB, wh * NAT_KB)
    dr_i = np.broadcast_to(dr[:, None, None, :, None] + NAT_WH - 1, full).reshape(flat)
    dc_i = np.broadcast_to(np.clip(dc + NAT_WW - 1, 0, 2 * NAT_WW - 2)[None, :, :, None, :], full).reshape(flat)
    ok = np.broadcast_to(col_ok[None, :, :, None, :], full).reshape(flat)
    return row_idx, col_idx, dr_i, dc_i, ok


def _nat_context_core(y):
    B_, S, _ = y.shape
    q, k, v = jnp.split(y, 3, axis=-1)
    q = q.reshape(B_, S, NAT_H, NAT_HD)
    k = k.reshape(B_, S, NAT_H, NAT_HD)
    v = v.reshape(B_, S, NAT_H, NAT_HD)
    s = jnp.einsum('bqhd,bkhd->bhqk', q, k) * (NAT_HD ** -0.5)
    p = jax.nn.softmax(s, axis=-1)
    o = jnp.einsum('bhqk,bkhd->bqhd', p, v).reshape(B_, S, D_MODEL)
    return o, k.transpose(0, 2, 1, 3), v.transpose(0, 2, 1, 3)


def _nat_latent_core(y, ck, cv, rpb):
    B_, T, _ = y.shape
    rows = T // GRID_W
    ncb = GRID_W // NAT_QB
    row_idx, col_idx, dr_i, dc_i, ok = _nat_tables(rows)
    q, k, v = jnp.split(y, 3, axis=-1)
    q = q.reshape(B_, rows, ncb, NAT_QB, NAT_H, NAT_HD)
    k = k.reshape(B_, rows, GRID_W, NAT_H, NAT_HD)
    v = v.reshape(B_, rows, GRID_W, NAT_H, NAT_HD)
    ri = row_idx[:, None, :, None]
    ci = col_idx[None, :, None, :]
    kb = k[:, ri, ci].reshape(B_, rows, ncb, -1, NAT_H, NAT_HD)
    vb = v[:, ri, ci].reshape(B_, rows, ncb, -1, NAT_H, NAT_HD)
    nk = kb.shape[3]
    scale = NAT_HD ** -0.5
    s_lat = jnp.einsum('brnqhd,brnkhd->bhrnqk', q, kb) * scale
    s_lat = s_lat + rpb[:, dr_i, dc_i]
    s_lat = jnp.where(ok, s_lat, NEG_INF)
    s_ctx = jnp.einsum('brnqhd,bhsd->bhrnqs', q, ck) * scale
    p = jax.nn.softmax(jnp.concatenate([s_lat, s_ctx], axis=-1), axis=-1)
    o = (jnp.einsum('bhrnqk,brnkhd->brnqhd', p[..., :nk], vb)
         + jnp.einsum('bhrnqs,bhsd->brnqhd', p[..., nk:], cv))
    return o.reshape(B_, T, D_MODEL)


def _gmlp_core(y, ln_g, ln_b, w_s, b_s):
    B_, T, _ = y.shape
    u, v = jnp.split(y, 2, axis=-1)
    vc = v - jnp.mean(v, axis=-1, keepdims=True)
    v = vc * lax.rsqrt(jnp.mean(vc * vc, axis=-1, keepdims=True) + EPS) * ln_g + ln_b
    v = v.reshape(B_, T // GM_CHUNK, GM_CHUNK, GM_G, GM_CG)
    v = jnp.einsum('gpq,bnqgc->bnpgc', w_s, v) + b_s.T[None, None, :, :, None]
    return u * v.reshape(B_, T, GM_DH)


GLA_BLK = 128
GLA_CPB = GLA_BLK // GLA_CHUNK
GLA_ZCOL = GLA_IN // LANE


def _gla_consts():
    r = np.arange(GLA_BLK)
    same = (r[:, None] // GLA_CHUNK) == (r[None, :] // GLA_CHUNK)
    ri, ci = r[:, None] % GLA_CHUNK, r[None, :] % GLA_CHUNK
    lf = np.concatenate([same & (ci <= ri), same & (ci > ri)], axis=0)
    lb = np.concatenate([same & (ci >= ri), same & (ci < ri)], axis=0)
    rows = np.arange(GLA_CHUNK * GLA_DK)
    sel = (rows[:, None] // GLA_DK) == (np.arange(LANE)[None, :] % GLA_CHUNK)
    return (jnp.asarray(lf, BF16), jnp.asarray(lb, BF16), jnp.asarray(sel, BF16))


def _rope_tables(T):
    half = GLA_DK // 2
    t = np.arange(T)
    inv = ROPE_THETA ** (-np.arange(0, half, 2, dtype=np.float64) / half)
    lane = np.arange(GLA_DK)
    pos = np.where(lane[None, :] < half, (t // GRID_W)[:, None], (t % GRID_W)[:, None])
    ang = pos * inv[lane % (half // 2)][None, :]
    sign = np.where((lane % half) < half // 2, -1.0, 1.0)[None, :]
    return jnp.asarray(np.cos(ang), F32), jnp.asarray(np.sin(ang) * sign, F32)


def _rope_apply(x, cos, sin_signed):
    half = GLA_DK // 2
    lane = lax.broadcasted_iota(jnp.int32, (1, GLA_DK), 1)
    partner = jnp.where((lane % half) < half // 2,
                        pltpu.roll(x, GLA_DK - half // 2, axis=1), pltpu.roll(x, half // 2, axis=1))
    return x * cos + partner * sin_signed


def _split3(x):
    hi = x.astype(BF16)
    r1 = x - hi.astype(F32)
    mid = r1.astype(BF16)
    lo = (r1 - mid.astype(F32)).astype(BF16)
    return hi, mid, lo


def _dot3(m, x):
    hi, mid, lo = _split3(x)
    d = lambda p: jnp.dot(m, p, preferred_element_type=F32)
    return d(hi) + d(mid) + d(lo)


def _gla_kernel(*refs, use_rope, has_s0, want_state):
    it = iter(refs)
    q_ref, k_ref, v_ref, r_ref, za_ref, w2_ref, ba_ref, ng_ref, lf_ref, lb_ref, sel_ref = (next(it) for _ in range(11))
    cos_ref, sin_ref = (next(it), next(it)) if use_rope else (None, None)
    s0_ref = next(it) if has_s0 else None
    if has_s0:
        next(it)
    o_ref = next(it)
    st_ref = next(it) if want_state else None
    (q_s, k_s, bf_s, bb_s, ef_s, eb_s, qdf_s, kdf_s, qdb_s, kdb_s, v_s, w_s, o_s, sf_s, sb_s) = it

    T = q_ref.shape[0]
    nblk, nch = T // GLA_BLK, T // GLA_CHUNK

    for blk in range(nblk):
        rows = slice(blk * GLA_BLK, (blk + 1) * GLA_BLK)
        z = jnp.dot(za_ref[rows, :].astype(BF16), w2_ref[...], preferred_element_type=F32) + ba_ref[...]
        la = (jnp.minimum(z, 0.0) - jnp.log1p(jnp.exp(-jnp.abs(z)))) * (1.0 / GLA_TAU)
        cf = _dot3(lf_ref[...], la[:, :GLA_DK])
        cb = _dot3(lb_ref[...], la[:, GLA_DK:])
        bf, bb = cf[:GLA_BLK], cb[:GLA_BLK]
        ef, eb = jnp.exp(bf), jnp.exp(bb)
        q = q_ref[rows, :] * (GLA_DK ** -0.5)
        k = k_ref[rows, :]
        if use_rope:
            q = _rope_apply(q, cos_ref[rows, :], sin_ref[rows, :])
            k = _rope_apply(k, cos_ref[rows, :], sin_ref[rows, :])
        q_s[rows, :], k_s[rows, :] = q, k
        bf_s[rows, :], bb_s[rows, :] = bf, bb
        ef_s[rows, :], eb_s[rows, :] = ef, eb
        qdf_s[rows, :] = (q * ef).astype(BF16)
        qdb_s[rows, :] = (q * eb).astype(BF16)
        kdf_s[rows, :] = (k * jnp.exp(cf[GLA_BLK:])).astype(BF16)
        kdb_s[rows, :] = (k * jnp.exp(cb[GLA_BLK:])).astype(BF16)
        v_s[rows, :] = v_ref[rows, :].astype(BF16)

    irow = lax.broadcasted_iota(jnp.int32, (GLA_CHUNK, 1), 0)

    def intra_chunk(c, carry):
        rows = pl.ds(pl.multiple_of(c * GLA_CHUNK, GLA_CHUNK), GLA_CHUNK)
        qc, bfc, bbc = q_s[rows, :], bf_s[rows, :], bb_s[rows, :]
        for j in range(GLA_CHUNK):
            row = pl.ds(c * GLA_CHUNK + j, 1)
            e = (jnp.exp(jnp.where(irow >= j, bfc - bf_s[row, :], -jnp.inf))
                 + jnp.exp(jnp.where(irow <= j, bbc - bb_s[row, :], -jnp.inf)))
            w_s[rows, j * GLA_DK:(j + 1) * GLA_DK] = (qc * k_s[row, :] * e).astype(BF16)
        return carry

    lax.fori_loop(0, nch, intra_chunk, 0)

    lane_c = lax.broadcasted_iota(jnp.int32, (GLA_BLK, LANE), 1) // GLA_CHUNK
    row_c = lax.broadcasted_iota(jnp.int32, (GLA_BLK, LANE), 0) // GLA_CHUNK
    for blk in range(nblk):
        rows = slice(blk * GLA_BLK, (blk + 1) * GLA_BLK)
        a = jnp.dot(w_s[rows, :], sel_ref[...], preferred_element_type=F32)
        a = jnp.where(lane_c == row_c, a, 0.0).astype(BF16)
        o_s[rows, :] = jnp.dot(a, v_s[rows, :], preferred_element_type=F32)

    if has_s0:
        sf_s[...] = s0_ref[0].T
        sb_s[...] = s0_ref[1].T
    else:
        sf_s[...] = jnp.zeros_like(sf_s)
        sb_s[...] = jnp.zeros_like(sb_s)

    def chain(rows, g_row, qd_s, kd_s, e_s, st_s):
        s = st_s[...]
        o_s[rows, :] += _dot_nt(qd_s[rows, :], s.astype(BF16))
        u = lax.dot_general(v_s[rows, :], kd_s[rows, :], (((0,), (0,)), ((), ())), preferred_element_type=F32)
        st_s[...] = s * e_s[g_row, :] + u

    def inter_chunk(n, carry):
        cf_ = pl.multiple_of(n * GLA_CHUNK, GLA_CHUNK)
        cb_ = pl.multiple_of((nch - 1 - n) * GLA_CHUNK, GLA_CHUNK)
        chain(pl.ds(cf_, GLA_CHUNK), pl.ds(cf_ + GLA_CHUNK - 1, 1), qdf_s, kdf_s, ef_s, sf_s)
        chain(pl.ds(cb_, GLA_CHUNK), pl.ds(cb_, 1), qdb_s, kdb_s, eb_s, sb_s)
        return carry

    lax.fori_loop(0, nch, inter_chunk, 0, unroll=4)

    o = o_s[...]
    o = o * lax.rsqrt(jnp.mean(o * o, axis=-1, keepdims=True) + EPS) * ng_ref[...]
    r = r_ref[...]
    o_ref[...] = o * (r * jax.nn.sigmoid(r))
    if want_state:
        st_ref[0] = sf_s[...].T
        st_ref[1] = sb_s[...].T


def _gla_call(y, w2, ba, ng, T, n_seq, row0, use_rope, s0, o_prev=None):
    has_s0 = s0 is not None
    want_state = not has_s0
    rb = row0 // T
    nqb = GLA_H
    col = lambda w, off: pl.BlockSpec((T, w), lambda b, h: (rb + b, off + h))
    cst = lambda shape: pl.BlockSpec(shape, lambda b, h: (0,) * len(shape))
    per_head = lambda shape: pl.BlockSpec((None,) + shape, lambda b, h: (h,) + (0,) * len(shape))
    lf, lb, sel = _gla_consts()
    in_specs = [col(GLA_DK, 0), col(GLA_DK, nqb), col(GLA_DV, nqb), col(GLA_DV, nqb + GLA_H),
                pl.BlockSpec((T, LANE), lambda b, h: (rb + b, GLA_ZCOL)),
                per_head((LANE, 2 * GLA_DK)), per_head((1, 2 * GLA_DK)), per_head((1, GLA_DV)),
                cst(lf.shape), cst(lb.shape), cst(sel.shape)]
    args = [y, y, y, y, y, w2, ba, ng, lf, lb, sel]
    if use_rope:
        cos, sin = _rope_tables(T)
        in_specs += [cst(cos.shape), cst(sin.shape)]
        args += [cos, sin]
    if has_s0:
        in_specs += [pl.BlockSpec((None, 2, None, GLA_DK, GLA_DV), lambda b, h: (b, 0, h, 0, 0)),
                     pl.BlockSpec(memory_space=pl.ANY)]
        args += [s0, o_prev]
    aliases = {len(args) - 1: 0} if has_s0 else {}
    out_specs = [pl.BlockSpec((T, GLA_DV), lambda b, h: (rb + b, h))]
    out_shape = [jax.ShapeDtypeStruct((N_TOK, GLA_H * GLA_DV), F32)]
    if want_state:
        out_specs.append(pl.BlockSpec((None, 2, None, GLA_DK, GLA_DV), lambda b, h: (b, 0, h, 0, 0)))
        out_shape.append(jax.ShapeDtypeStruct((n_seq, 2, GLA_H, GLA_DK, GLA_DV), F32))
    f32s = lambda w: pltpu.VMEM((T, w), F32)
    bf16s = lambda w: pltpu.VMEM((T, w), BF16)
    scratch = ([f32s(GLA_DK)] * 6 + [bf16s(GLA_DK)] * 4
               + [bf16s(GLA_DV), bf16s(GLA_CHUNK * GLA_DK), f32s(GLA_DV),
                  pltpu.VMEM((GLA_DV, GLA_DK), F32), pltpu.VMEM((GLA_DV, GLA_DK), F32)])
    outs = pl.pallas_call(
        functools.partial(_gla_kernel, use_rope=use_rope, has_s0=has_s0, want_state=want_state),
        grid=(n_seq, GLA_H),
        in_specs=in_specs, out_specs=out_specs, out_shape=out_shape,
        scratch_shapes=scratch,
        input_output_aliases=aliases,
        compiler_params=pltpu.CompilerParams(dimension_semantics=("arbitrary", "arbitrary"),
                                             vmem_limit_bytes=VMEM_LIMIT),
        name="gla_rope" if use_rope else "gla",
    )(*args)
    return (outs[0], outs[1]) if want_state else (outs[0], None)


def _gla_mixer(y, state, w_a2, b_a, norm_g):
    w2 = jnp.zeros((GLA_H, LANE, 2 * GLA_DK), F32)
    for e in range(2):
        we = w_a2[e].reshape(GLA_RANK, GLA_H, GLA_DK).transpose(1, 0, 2)
        w2 = w2.at[:, e * GLA_RANK:(e + 1) * GLA_RANK, e * GLA_DK:(e + 1) * GLA_DK].set(we)
    ba = b_a.reshape(2, GLA_H, GLA_DK).transpose(1, 0, 2).reshape(GLA_H, 1, 2 * GLA_DK)
    ng = norm_g.reshape(GLA_H, 1, GLA_DV)
    op, st = _gla_call(y, w2.astype(BF16), ba, ng, SEQ, BATCH, 0, False, None)
    o, _ = _gla_call(y, w2.astype(BF16), ba, ng, DEC_SEQ, DEC_BATCH, N_PROMPT, True, state, op)
    return o, st


N_HEAD_PAIRS = NAT_H // 2
NAT_ROWS = DEC_SEQ // GRID_W
NAT_WIN = NAT_WH * GRID_W
NAT_CLS = NAT_WH


def _nat_row_window(r):
    rs = min(max(r - NAT_WH // 2, 0), NAT_ROWS - NAT_WH)
    return rs, r - rs


def _nat_bias_table(rpb):
    qc = np.arange(GRID_W)[:, None]
    kc = np.arange(GRID_W)[None, :]
    c_start = np.clip(qc - NAT_WW // 2, 0, GRID_W - NAT_WW)
    ok = (kc >= c_start) & (kc < c_start + NAT_WW)
    dc = np.clip(kc - qc + NAT_WW - 1, 0, 2 * NAT_WW - 2)
    cls = np.arange(NAT_CLS)[:, None]
    w = np.arange(NAT_WH)[None, :]
    dr = w - cls + NAT_WH - 1
    pick = jnp.asarray(dc[None] == np.arange(2 * NAT_WW - 1)[:, None, None], F32)
    t = jnp.einsum('hcwd,dqk->hcqwk', rpb[:, dr], pick, precision=lax.Precision.HIGHEST)
    t = jnp.where(ok[None, None, :, None, :], t, NEG_INF)
    return t.reshape(NAT_H, NAT_CLS, GRID_W, NAT_WIN)


def _head_mask(hh):
    lane = lax.broadcasted_iota(jnp.int32, (1, LANE), 1)
    return (lane < NAT_HD) if hh == 0 else (lane >= NAT_HD)


def _dot_nt(a, b):
    return lax.dot_general(a, b, (((1,), (1,)), ((), ())), preferred_element_type=F32)


def _nat_ctx_kernel(q_ref, k_ref, v_ref, o_ref, kc_ref, vc_ref):
    for hp in range(N_HEAD_PAIRS):
        cols = slice(hp * LANE, (hp + 1) * LANE)
        q = q_ref[:, cols] * (NAT_HD ** -0.5)
        k, v = k_ref[:, cols], v_ref[:, cols]
        kb, vb = k.astype(BF16), v.astype(BF16)
        outs = []
        for hh in range(2):
            qm = jnp.where(_head_mask(hh), q, 0.0).astype(BF16)
            s = _dot_nt(qm, kb)
            p = jnp.exp(s - jnp.max(s, axis=-1, keepdims=True))
            l = jnp.sum(p, axis=-1, keepdims=True)
            outs.append(jnp.dot(p.astype(BF16), vb, preferred_element_type=F32) / l)
            kc_ref[2 * hp + hh] = k[:, hh * NAT_HD:(hh + 1) * NAT_HD]
            vc_ref[2 * hp + hh] = v[:, hh * NAT_HD:(hh + 1) * NAT_HD]
        o_ref[:, cols] = jnp.where(_head_mask(0), outs[0], outs[1])


def _nat_context(y):
    blk = lambda j: pl.BlockSpec((SEQ, D_MODEL), lambda b: (b, j))
    cache = pl.BlockSpec((None, NAT_H, SEQ, NAT_HD), lambda b: (b, 0, 0, 0))
    cache_shape = jax.ShapeDtypeStruct((BATCH, NAT_H, SEQ, NAT_HD), F32)
    return pl.pallas_call(
        _nat_ctx_kernel,
        grid=(BATCH,),
        in_specs=[blk(0), blk(1), blk(2)],
        out_specs=[pl.BlockSpec((SEQ, D_MODEL), lambda b: (b, 0)), cache, cache],
        out_shape=[jax.ShapeDtypeStruct((N_TOK, D_MODEL), F32), cache_shape, cache_shape],
        compiler_params=pltpu.CompilerParams(dimension_semantics=("arbitrary",)),
        name="nat_context",
    )(y, y, y)


def _nat_lat_kernel(q_ref, k_ref, v_ref, ck_ref, cv_ref, tab_ref, o_in_ref, o_ref):
    del o_in_ref
    q = q_ref[...] * (NAT_HD ** -0.5)
    qm = [jnp.where(_head_mask(hh), q, 0.0).astype(BF16) for hh in range(2)]
    ckb = ck_ref[...].astype(BF16)
    cvb = cv_ref[...].astype(BF16)
    for r in range(NAT_ROWS):
        rs, cls = _nat_row_window(r)
        kw = k_ref[rs * GRID_W:rs * GRID_W + NAT_WIN, :].astype(BF16)
        vw = v_ref[rs * GRID_W:rs * GRID_W + NAT_WIN, :].astype(BF16)
        outs = []
        for hh in range(2):
            qr = qm[hh][r * GRID_W:(r + 1) * GRID_W]
            s_lat = _dot_nt(qr, kw) + tab_ref[hh, cls]
            s_ctx = _dot_nt(qr, ckb)
            m = jnp.maximum(jnp.max(s_lat, axis=-1, keepdims=True), jnp.max(s_ctx, axis=-1, keepdims=True))
            p_lat = jnp.exp(s_lat - m)
            p_ctx = jnp.exp(s_ctx - m)
            l = jnp.sum(p_lat, axis=-1, keepdims=True) + jnp.sum(p_ctx, axis=-1, keepdims=True)
            o = (jnp.dot(p_lat.astype(BF16), vw, preferred_element_type=F32)
                 + jnp.dot(p_ctx.astype(BF16), cvb, preferred_element_type=F32))
            outs.append(o / l)
        o_ref[r * GRID_W:(r + 1) * GRID_W, :] = jnp.where(_head_mask(0), outs[0], outs[1])


def _nat_latent(y, o_ctx, ck, cv, rpb):
    row0 = N_PROMPT // DEC_SEQ
    blk = lambda off: pl.BlockSpec((DEC_SEQ, LANE), lambda b, hp: (row0 + b, off + hp))
    ctx = pl.BlockSpec((None, ck.shape[1], LANE), lambda b, hp: (b, 0, hp))
    return pl.pallas_call(
        _nat_lat_kernel,
        grid=(DEC_BATCH, N_HEAD_PAIRS),
        in_specs=[blk(0), blk(N_HEAD_PAIRS), blk(2 * N_HEAD_PAIRS), ctx, ctx,
                  pl.BlockSpec((2, NAT_CLS, GRID_W, NAT_WIN), lambda b, hp: (hp, 0, 0, 0)),
                  pl.BlockSpec(memory_space=pl.ANY)],
        out_specs=pl.BlockSpec((DEC_SEQ, LANE), lambda b, hp: (row0 + b, hp)),
        out_shape=jax.ShapeDtypeStruct((N_TOK, D_MODEL), F32),
        input_output_aliases={6: 0},
        compiler_params=pltpu.CompilerParams(dimension_semantics=("arbitrary", "arbitrary")),
        name="nat_latent",
    )(y, y, y, ck, cv, _nat_bias_table(rpb), o_ctx)


def _heads_last(t):
    b, h, s, d = t.shape
    return t.transpose(0, 2, 1, 3).reshape(b, s, h * d)


def _heads_first(t):
    b, s, _ = t.shape
    return t.reshape(b, s, NAT_H, NAT_HD).transpose(0, 2, 1, 3)


def _gmlp_kernel(x_ref, y_ref, mod_ref, lng_ref, lnb_ref, ws_ref, bs_ref, w_ref, o_ref, t_ref):
    tm = x_ref.shape[0]
    v = y_ref[:, GM_DH:]
    vc = v - jnp.mean(v, axis=-1, keepdims=True)
    vn = vc * lax.rsqrt(jnp.mean(vc * vc, axis=-1, keepdims=True) + EPS) * lng_ref[...] + lnb_ref[...]
    vn = vn.astype(BF16)
    for n in range(tm // GM_CHUNK):
        rows = slice(n * GM_CHUNK, (n + 1) * GM_CHUNK)
        for g in range(GM_G):
            cols = slice(g * GM_CG, (g + 1) * GM_CG)
            sp = jnp.dot(ws_ref[g], vn[rows, cols], preferred_element_type=F32) + bs_ref[:, cols]
            t_ref[rows, cols] = (y_ref[rows, cols] * sp).astype(BF16)
    gate = mod_ref[5:6, :]
    o_ref[...] = x_ref[...] + gate * jnp.dot(t_ref[...], w_ref[...], preferred_element_type=F32)


def _gmlp_out(x, y, mod_l, ln_g, ln_b, w_s, b_s, w_out):
    tm = TM_PROJ
    bias = jnp.repeat(b_s.T, GM_CG, axis=1)
    return pl.pallas_call(
        _gmlp_kernel,
        grid=(N_TOK // tm,),
        in_specs=[
            pl.BlockSpec((tm, D_MODEL), lambda i: (i, 0)),
            pl.BlockSpec((tm, 2 * GM_DH), lambda i: (i, 0)),
            pl.BlockSpec((None, N_MOD, D_MODEL), lambda i: (_group_of_tile(i, tm), 0, 0)),
            pl.BlockSpec((1, GM_DH), lambda i: (0, 0)),
            pl.BlockSpec((1, GM_DH), lambda i: (0, 0)),
            _resident((GM_G, GM_CHUNK, GM_CHUNK)),
            _resident((GM_CHUNK, GM_DH)),
            _resident((GM_DH, D_MODEL)),
        ],
        out_specs=pl.BlockSpec((tm, D_MODEL), lambda i: (i, 0)),
        out_shape=jax.ShapeDtypeStruct((N_TOK, D_MODEL), F32),
        scratch_shapes=[pltpu.VMEM((tm, GM_DH), BF16)],
        compiler_params=pltpu.CompilerParams(dimension_semantics=("arbitrary",),
                                             vmem_limit_bytes=VMEM_LIMIT),
        name="gmlp_gate_out",
    )(x, y, mod_l, ln_g.reshape(1, GM_DH), ln_b.reshape(1, GM_DH), w_s.astype(BF16), bias,
      w_out.astype(BF16))


def _dwconv_centred(x, w, b):
    K = w.shape[0]
    T = x.shape[1]
    pad = K // 2
    xp = jnp.pad(x, ((0, 0), (pad, pad), (0, 0)))
    out = xp[:, 0:T] * w[0]
    for i in range(1, K):
        out = out + xp[:, i:i + T] * w[i]
    return out + b


def _ssd_chunked(x, dt, a, bm, cm, s0):
    B_, T = x.shape[:2]
    L = SSD_CHUNK
    n = T // L
    E = SSD_H // SSD_G
    x = x.reshape(B_, n, L, SSD_G, E, SSD_P)
    dt = dt.reshape(B_, n, L, SSD_G, E)
    bm = bm.reshape(B_, n, L, SSD_G, SSD_N)
    cm = cm.reshape(B_, n, L, SSD_G, SSD_N)
    cum = jnp.cumsum(dt * a.reshape(SSD_G, E), axis=2)
    tril = jnp.tril(jnp.ones((L, L), dtype=bool))
    seg = cum[:, :, :, None] - cum[:, :, None, :]
    decay = jnp.exp(jnp.where(tril[:, :, None, None], seg, -jnp.inf))
    dtx = x * dt[..., None]
    cb = jnp.einsum('bnigs,bnjgs->bnijg', cm, bm)
    y_diag = jnp.einsum('bnijg,bnijge,bnjgep->bnigep', cb, decay, dtx)
    u = jnp.einsum('bnjgs,bnjge,bnjgep->bngeps', bm, jnp.exp(cum[:, :, -1:] - cum), dtx)
    chunk_decay = jnp.exp(cum[:, :, -1])
    q_decay = jnp.exp(cum)

    def step(s, xs):
        c_n, qd_n, u_n, g_n = xs
        y = jnp.einsum('bigs,bige,bgeps->bigep', c_n, qd_n, s)
        return g_n[..., None, None] * s + u_n, y

    sw = lambda t: jnp.swapaxes(t, 0, 1)
    s_fin, y_off = lax.scan(step, s0.reshape(B_, SSD_G, E, SSD_P, SSD_N),
                            (sw(cm), sw(q_decay), sw(u), sw(chunk_decay)))
    y = y_diag + sw(y_off)
    return y.reshape(B_, T, SSD_H, SSD_P), s_fin.reshape(B_, SSD_H, SSD_P, SSD_N)


def _ssd_core(y, s0, conv_w, conv_b, dt_bias, a_log, d_skip, norm_g):
    B_, T, _ = y.shape
    z = y[..., :SSD_DI]
    xbc = y[..., SSD_DI:SSD_DI + SSD_XBC]
    dt = y[..., SSD_DI + SSD_XBC:SSD_IN]
    xbc = jax.nn.silu(_dwconv_centred(xbc, conv_w, conv_b))
    x = xbc[..., :SSD_DI].reshape(B_, T, SSD_H, SSD_P)
    bm = xbc[..., SSD_DI:SSD_DI + SSD_G * SSD_N].reshape(B_, T, SSD_G, SSD_N)
    cm = xbc[..., SSD_DI + SSD_G * SSD_N:].reshape(B_, T, SSD_G, SSD_N)
    dt = jax.nn.softplus(dt.reshape(B_, T, 2, SSD_H) + dt_bias)
    a = -jnp.exp(a_log)
    y_f, s_f = _ssd_chunked(x, dt[:, :, 0], a[0], bm, cm, s0[:, 0])
    y_b, s_b = _ssd_chunked(_flip_t(x), _flip_t(dt[:, :, 1]), a[1], _flip_t(bm), _flip_t(cm), s0[:, 1])
    yy = y_f + _flip_t(y_b) + d_skip[:, None] * x
    yy = _rmsnorm(yy.reshape(B_, T, SSD_DI) * jax.nn.silu(z), norm_g)
    return yy, jnp.stack([s_f, s_b], axis=1)


SSD_E = SSD_H // SSD_G
SSD_GP = SSD_E * SSD_P
SSD_BLK = 2 * SSD_CHUNK
SSD_COL_X = SSD_DI
SSD_COL_B = 2 * SSD_DI
SSD_COL_C = SSD_COL_B + SSD_G * SSD_N
SSD_COL_DT = SSD_COL_C + SSD_G * SSD_N
SSD_PROJ = SSD_COL_DT + SSD_G * LANE


def _ssd_w_in(w_in):
    base = SSD_DI + SSD_XBC
    w_dt = jnp.zeros((D_MODEL, SSD_G, LANE), F32)
    for g in range(SSD_G):
        cols = jnp.concatenate([w_in[:, base + g * SSD_E:base + (g + 1) * SSD_E],
                                w_in[:, base + SSD_H + g * SSD_E:base + SSD_H + (g + 1) * SSD_E]], axis=1)
        w_dt = w_dt.at[:, g, :2 * SSD_E].set(cols).at[:, g, 2 * SSD_E:4 * SSD_E].set(cols)
    return jnp.concatenate([w_in[:, :base], w_dt.reshape(D_MODEL, SSD_G * LANE)], axis=1)


def _ssd_consts():
    r = np.arange(SSD_BLK)
    same = (r[:, None] // SSD_CHUNK) == (r[None, :] // SSD_CHUNK)
    cum = np.concatenate([same & (r[None, :] <= r[:, None]), same & (r[None, :] >= r[:, None])], axis=0)
    lane = np.arange(LANE)[:, None]
    col = np.arange(SSD_GP)[None, :] // SSD_P
    exp_f = lane == col
    exp_b = lane == col + SSD_E
    return jnp.asarray(cum, BF16), jnp.asarray(exp_f, BF16), jnp.asarray(exp_b, BF16)


def _dot3_r(x, m):
    hi, mid, lo = _split3(x)
    d = lambda p: jnp.dot(p, m, preferred_element_type=F32)
    return d(hi) + d(mid) + d(lo)


def _softplus(x):
    return jnp.maximum(x, 0.0) + jnp.log1p(jnp.exp(-jnp.abs(x)))


def _ssd_kernel(*refs, has_s0, want_state):
    it = iter(refs)
    (z_ref, x_ref, b_ref, c_ref, dt_ref, cwx_ref, cwb_ref, cwc_ref, cbx_ref, cbb_ref, cbc_ref,
     dtb_ref, alog_ref, dsk_ref, cum_ref, ef_ref, eb_ref) = (next(it) for _ in range(17))
    s0_ref = next(it) if has_s0 else None
    if has_s0:
        next(it)
    o_ref = next(it)
    st_ref = next(it) if want_state else None
    xs_s, xb_s, bm_s, cm_s, cu_s, dt_s, y_s, sf_s, sb_s = it

    T = x_ref.shape[0]
    nch = T // SSD_CHUNK
    L = SSD_CHUNK

    trow = lax.broadcasted_iota(jnp.int32, (T, 1), 0)

    def conv_silu(v_ref, w_ref, bias_ref):
        v = v_ref[...]
        prev = jnp.where(trow == 0, 0.0, pltpu.roll(v, 1, axis=0))
        nxt = jnp.where(trow == T - 1, 0.0, pltpu.roll(v, T - 1, axis=0))
        y = prev * w_ref[0:1, :] + v * w_ref[1:2, :] + nxt * w_ref[2:3, :] + bias_ref[...]
        return y * jax.nn.sigmoid(y)

    xs = conv_silu(x_ref, cwx_ref, cbx_ref)
    xs_s[...] = xs
    xb_s[...] = xs.astype(BF16)
    bm_s[...] = conv_silu(b_ref, cwb_ref, cbb_ref).astype(BF16)
    cm_s[...] = conv_silu(c_ref, cwc_ref, cbc_ref).astype(BF16)

    lane1 = lax.broadcasted_iota(jnp.int32, (1, LANE), 1)
    a_row = jnp.where(lane1 < 2 * SSD_E, -jnp.exp(alog_ref[...]), 0.0)
    for blk in range(T // SSD_BLK):
        rows = slice(blk * SSD_BLK, (blk + 1) * SSD_BLK)
        dt = _softplus(dt_ref[rows, :] + dtb_ref[...])
        c2 = _dot3(cum_ref[...], dt * a_row)
        cu_s[rows, :] = jnp.where(lane1 < SSD_E, c2[:SSD_BLK], c2[SSD_BLK:])
        dt_s[rows, :] = dt

    ii = lax.broadcasted_iota(jnp.int32, (L, LANE), 0)
    jj = lax.broadcasted_iota(jnp.int32, (L, LANE), 1)
    fwd_half = jj < L
    fwd_half1 = lane1 < L
    tri = (fwd_half & (ii >= jj)) | ((jj >= L) & (ii <= jj - L))
    left = lane1 < SSD_P
    for c in range(nch):
        rows = slice(c * L, (c + 1) * L)
        cum_c, dt_c = cu_s[rows, :], dt_s[rows, :]
        bm_c, cm_c = bm_s[rows, :], cm_s[rows, :]
        cb2 = _dot_nt(cm_c, jnp.concatenate([bm_c, bm_c], axis=0))
        arr = jnp.where(lane1 < 2 * SSD_E, cum_c, dt_c)
        arr_t = jnp.concatenate([arr, arr], axis=0).T
        gs = []
        for e in range(SSD_E):
            row_c = jnp.where(fwd_half1, arr_t[e:e + 1, :], arr_t[SSD_E + e:SSD_E + e + 1, :])
            row_dt = jnp.where(fwd_half1, arr_t[2 * SSD_E + e:2 * SSD_E + e + 1, :],
                               arr_t[3 * SSD_E + e:3 * SSD_E + e + 1, :])
            col_c = jnp.where(fwd_half, jnp.broadcast_to(cum_c[:, e:e + 1], (L, LANE)),
                              jnp.broadcast_to(cum_c[:, SSD_E + e:SSD_E + e + 1], (L, LANE)))
            dec = jnp.exp(jnp.where(tri, col_c - row_c, -jnp.inf))
            gs.append((cb2 * dec * row_dt).astype(BF16))
        for pr in range(SSD_E // 2):
            cols = slice(pr * LANE, (pr + 1) * LANE)
            xp = xb_s[rows, cols]
            xl = jnp.where(left, xp, jnp.zeros_like(xp))
            xr = jnp.where(left, jnp.zeros_like(xp), xp)
            lhs = jnp.concatenate([gs[2 * pr], gs[2 * pr + 1]], axis=1)
            rhs = jnp.concatenate([xl, xl, xr, xr], axis=0)
            y_s[rows, cols] = (jnp.dot(lhs, rhs, preferred_element_type=F32)
                               + dsk_ref[:, cols] * xs_s[rows, cols])

    if has_s0:
        sf_s[...] = s0_ref[0]
        sb_s[...] = s0_ref[1]
    else:
        sf_s[...] = jnp.zeros_like(sf_s)
        sb_s[...] = jnp.zeros_like(sb_s)

    def chain(c, last, lane0, exp_ref, st_s):
        rows = slice(c * L, (c + 1) * L)
        cum_c = jnp.where((lane1 >= lane0) & (lane1 < lane0 + SSD_E), cu_s[rows, :], 0.0)
        tot = cum_c[last:last + 1, :]
        qw = jnp.concatenate([jnp.exp(cum_c), jnp.exp(tot - cum_c) * dt_s[rows, :]], axis=0)
        qw = _dot3_r(qw, exp_ref[...])
        qd, w = qw[:L], qw[L:]
        s = st_s[...]
        y_s[rows, :] += jnp.dot(cm_s[rows, :], s.astype(BF16), preferred_element_type=F32) * qd
        xw = (xs_s[rows, :] * w).astype(BF16)
        u = lax.dot_general(bm_s[rows, :], xw, (((0,), (0,)), ((), ())), preferred_element_type=F32)
        st_s[...] = s * qd[last:last + 1, :] + u

    for n in range(nch):
        chain(n, L - 1, 0, ef_ref, sf_s)
        chain(nch - 1 - n, 0, SSD_E, eb_ref, sb_s)

    z = z_ref[...]
    o_ref[...] = y_s[...] * (z * jax.nn.sigmoid(z))
    if want_state:
        st_ref[0] = sf_s[...]
        st_ref[1] = sb_s[...]


def _ssd_call(y, prm, T, n_seq, row0, s0, o_prev=None):
    has_s0 = s0 is not None
    want_state = not has_s0
    rb = row0 // T
    col = lambda w, off: pl.BlockSpec((T, w), lambda b, g: (rb + b, off // w + g))
    cst = lambda shape: pl.BlockSpec(shape, lambda b, g: (0,) * len(shape))
    wcol = lambda rows, w, off: pl.BlockSpec((rows, w), lambda b, g: (0, off // w + g))
    per_g = lambda w: pl.BlockSpec((None, 1, w), lambda b, g: (g, 0, 0))
    cum, exp_f, exp_b = _ssd_consts()
    xoff, boff, coff = 0, SSD_DI, SSD_DI + SSD_G * SSD_N
    in_specs = [col(SSD_GP, 0), col(SSD_GP, SSD_COL_X), col(SSD_N, SSD_COL_B), col(SSD_N, SSD_COL_C),
                col(LANE, SSD_COL_DT),
                wcol(SSD_CONV, SSD_GP, xoff), wcol(SSD_CONV, SSD_N, boff), wcol(SSD_CONV, SSD_N, coff),
                wcol(1, SSD_GP, xoff), wcol(1, SSD_N, boff), wcol(1, SSD_N, coff),
                per_g(LANE), per_g(LANE), per_g(SSD_GP),
                cst(cum.shape), cst(exp_f.shape), cst(exp_b.shape)]
    args = [y, y, y, y, y, prm["conv_w"], prm["conv_w"], prm["conv_w"], prm["conv_b"], prm["conv_b"], prm["conv_b"],
            prm["dt_bias"], prm["a_log"], prm["d_skip"], cum, exp_f, exp_b]
    st_spec = pl.BlockSpec((None, 2, None, SSD_N, SSD_GP), lambda b, g: (b, 0, g, 0, 0))
    if has_s0:
        in_specs += [st_spec, pl.BlockSpec(memory_space=pl.ANY)]
        args += [s0, o_prev]
    aliases = {len(args) - 1: 0} if has_s0 else {}
    out_specs = [pl.BlockSpec((T, SSD_GP), lambda b, g: (rb + b, g))]
    out_shape = [jax.ShapeDtypeStruct((N_TOK, SSD_DI), F32)]
    if want_state:
        out_specs.append(st_spec)
        out_shape.append(jax.ShapeDtypeStruct((n_seq, 2, SSD_G, SSD_N, SSD_GP), F32))
    scratch = [pltpu.VMEM((T, SSD_GP), F32), pltpu.VMEM((T, SSD_GP), BF16),
               pltpu.VMEM((T, SSD_N), BF16), pltpu.VMEM((T, SSD_N), BF16),
               pltpu.VMEM((T, LANE), F32), pltpu.VMEM((T, LANE), F32), pltpu.VMEM((T, SSD_GP), F32),
               pltpu.VMEM((SSD_N, SSD_GP), F32), pltpu.VMEM((SSD_N, SSD_GP), F32)]
    outs = pl.pallas_call(
        functools.partial(_ssd_kernel, has_s0=has_s0, want_state=want_state),
        grid=(n_seq, SSD_G),
        in_specs=in_specs, out_specs=out_specs, out_shape=out_shape,
        scratch_shapes=scratch,
        input_output_aliases=aliases,
        compiler_params=pltpu.CompilerParams(dimension_semantics=("arbitrary", "arbitrary"),
                                             vmem_limit_bytes=VMEM_LIMIT),
        name="ssd_state" if has_s0 else "ssd",
    )(*args)
    return (outs[0], outs[1]) if want_state else (outs[0], None)


def _ssd_state_to_kernel(s):
    b = s.shape[0]
    return s.reshape(b, 2, SSD_G, SSD_GP, SSD_N).transpose(0, 1, 2, 4, 3)


def _ssd_state_from_kernel(s):
    b = s.shape[0]
    return s.transpose(0, 1, 2, 4, 3).reshape(b, 2, SSD_H, SSD_P, SSD_N)


def _ssd_mixer(y, state, conv_w, conv_b, dt_bias, a_log, d_skip):
    def lanes(p):
        pg = p.reshape(2, SSD_G, SSD_E).transpose(1, 0, 2).reshape(SSD_G, 2 * SSD_E)
        return jnp.pad(jnp.concatenate([pg, pg], axis=1), ((0, 0), (0, LANE - 4 * SSD_E))).reshape(SSD_G, 1, LANE)
    prm = dict(conv_w=conv_w, conv_b=conv_b.reshape(1, SSD_XBC), dt_bias=lanes(dt_bias), a_log=lanes(a_log),
               d_skip=jnp.repeat(d_skip, SSD_P).reshape(SSD_G, 1, SSD_GP))
    op, st = _ssd_call(y, prm, SEQ, BATCH, 0, None)
    o, _ = _ssd_call(y, prm, DEC_SEQ, DEC_BATCH, N_PROMPT, _ssd_state_to_kernel(state), op)
    return o, _ssd_state_from_kernel(st)


def _pad_cols(w, mult):
    n = w.shape[1]
    n_pad = -n % mult
    return jnp.pad(w, ((0, 0), (0, n_pad))) if n_pad else w


def kernel(x_prompt, x_sample, state_gla, cache_nat_k, cache_nat_v, state_ssd, c,
           c_ctx, norm_g, w_ada, b_ada, w_ffn_in, w_ffn_out,
           gla_w_in, gla_w_a1, gla_w_a2, gla_b_a, gla_norm_g, gla_w_out,
           nat_w_qkv, nat_rpb, nat_w_out,
           gm_w_in, gm_ln_g, gm_ln_b, gm_w_s, gm_b_s, gm_w_out,
           ssd_w_in, ssd_conv_w, ssd_conv_b, ssd_dt_bias, ssd_a_log, ssd_d, ssd_norm_g, ssd_w_out,
           final_g):
    x = _join_streams(x_prompt, x_sample)
    mod = _modulation_all(c, c_ctx, w_ada, b_ada)
    new_gla, new_k, new_v, new_ssd = [], [], [], []
    for l in range(DEPTH):
        kind, j = l % N_MIXERS, l // N_MIXERS
        x = _ffn(x, mod[l], norm_g[l, 0], w_ffn_in[l, 0], w_ffn_out[l, 0], 0)
        if kind == 0:
            w = jnp.concatenate([gla_w_in[j], gla_w_a1[j, 0], gla_w_a1[j, 1]], axis=1)
            y = _proj_in(x, mod[l], norm_g[l, 1], _pad_cols(w, 640), 640)
            o, st = _gla_mixer(y, state_gla[:, j], gla_w_a2[j], gla_b_a[j], gla_norm_g[j])
            new_gla.append(st)
            x = _proj_out(x, o, mod[l], gla_w_out[j])
        elif kind == 1:
            y = _proj_in(x, mod[l], norm_g[l, 1], nat_w_qkv[j], 768)
            o, kc, vc = _nat_context(y)
            o = _nat_latent(y, o, _heads_last(cache_nat_k[:, j]), _heads_last(cache_nat_v[:, j]), nat_rpb[j])
            new_k.append(kc)
            new_v.append(vc)
            x = _proj_out(x, o, mod[l], nat_w_out[j])
        elif kind == 2:
            y = _proj_in(x, mod[l], norm_g[l, 1], gm_w_in[j], 512, act="gelu")
            x = _gmlp_out(x, y, mod[l], gm_ln_g[j], gm_ln_b[j], gm_w_s[j], gm_b_s[j], gm_w_out[j])
        else:
            y = _proj_in(x, mod[l], norm_g[l, 1], _ssd_w_in(ssd_w_in[j]), 512)
            o, st = _ssd_mixer(y, state_ssd[:, j], ssd_conv_w[j], ssd_conv_b[j], ssd_dt_bias[j],
                               ssd_a_log[j], ssd_d[j])
            new_ssd.append(st)
            x = _proj_out(x, o, mod[l], ssd_w_out[j], norm_g=ssd_norm_g[j])
        x = _ffn(x, mod[l], norm_g[l, 2], w_ffn_in[l, 1], w_ffn_out[l, 1], 2)
    yn = _final_norm(x, final_g)
    y_prompt, y_sample = _split_streams(yn)
    return (y_prompt, y_sample, jnp.stack(new_gla, axis=1), jnp.stack(new_k, axis=1),
            jnp.stack(new_v, axis=1), jnp.stack(new_ssd, axis=1))
```

```python
import functools
import math

import jax
import jax.numpy as jnp
import numpy as np
from jax import lax
from jax.experimental import pallas as pl
from jax.experimental.pallas import tpu as pltpu

D_MODEL = 1024
BATCH = 32
SEQ = 256
DEPTH = 4
DEC_BATCH = 2
DEC_SEQ = 1024
N_PROMPT = BATCH * SEQ
N_SAMPLE = DEC_BATCH * DEC_SEQ
N_TOK = N_PROMPT + N_SAMPLE
N_GROUPS = 1 + DEC_BATCH

GRID_W = 64
N_MIXERS = 4
N_SUB = 3
N_MOD = 3 * N_SUB
D_FF = 2816
EPS = 1e-6
NEG_INF = -1e30
ROPE_THETA = 10000.0
GLA_H, GLA_DK, GLA_DV, GLA_RANK, GLA_TAU, GLA_CHUNK = 4, 128, 256, 16, 16.0, 16
GLA_IN = 2 * GLA_H * GLA_DK + 2 * GLA_H * GLA_DV
NAT_H, NAT_HD, NAT_WH, NAT_WW, NAT_QB, NAT_KB = 16, 64, 8, 16, 16, 32
GM_DH, GM_G, GM_CHUNK = 1024, 8, 128
GM_CG = GM_DH // GM_G
SSD_DI = 2 * D_MODEL
SSD_P = 64
SSD_H = SSD_DI // SSD_P
SSD_N, SSD_G, SSD_CONV, SSD_CHUNK = 128, 4, 3, 64
SSD_XBC = SSD_DI + 2 * SSD_G * SSD_N
SSD_IN = SSD_DI + SSD_XBC + 2 * SSD_H

LANE = 128
VMEM_LIMIT = 56 * 1024 * 1024
BF16 = jnp.bfloat16
F32 = jnp.float32

FF_CHUNK = 256
N_FF_CHUNKS = D_FF // FF_CHUNK
TM_FFN = 512
TM_PROJ = 512
ADA_TK = 128


def _group_of_tile(i, tm):
    n_prompt_tiles = N_PROMPT // tm
    return jnp.where(i < n_prompt_tiles, 0, 1 + (i - n_prompt_tiles) // (DEC_SEQ // tm))


def _resident(shape):
    nd = len(shape)
    return pl.BlockSpec(shape, lambda i: (0,) * nd, pipeline_mode=pl.Buffered(1))


def _premod(x, g, mod_ref, k):
    shift = mod_ref[3 * k:3 * k + 1, :]
    scale = mod_ref[3 * k + 1:3 * k + 2, :]
    gate = mod_ref[3 * k + 2:3 * k + 3, :]
    ms = jnp.mean(x * x, axis=-1, keepdims=True)
    h = x * lax.rsqrt(ms + EPS) * g
    return h * (1.0 + scale) + shift, gate


def _ada_kernel(cond_ref, w_ref, b_ref, o_ref):
    cnd = cond_ref[...]
    s = (cnd * jax.nn.sigmoid(cnd)).astype(BF16)
    p = jnp.dot(s, w_ref[...].astype(BF16), preferred_element_type=F32)

    @pl.when(pl.program_id(1) == 0)
    def _():
        o_ref[...] = p + b_ref[...]

    @pl.when(pl.program_id(1) > 0)
    def _():
        o_ref[...] += p


def _modulation_all(c, c_ctx, w_ada, b_ada):
    rows = 8
    nk = D_MODEL // ADA_TK
    cond = jnp.concatenate([c_ctx[None], c, jnp.zeros((rows - N_GROUPS, D_MODEL), F32)], axis=0)
    cond = cond.reshape(rows, nk, ADA_TK).transpose(1, 0, 2)
    n_out = N_MOD * D_MODEL
    out = pl.pallas_call(
        _ada_kernel,
        grid=(DEPTH, nk),
        in_specs=[
            pl.BlockSpec((None, rows, ADA_TK), lambda l, k: (k, 0, 0)),
            pl.BlockSpec((None, ADA_TK, n_out), lambda l, k: (l, k, 0)),
            pl.BlockSpec((None, 1, n_out), lambda l, k: (l, 0, 0)),
        ],
        out_specs=pl.BlockSpec((None, rows, n_out), lambda l, k: (l, 0, 0)),
        out_shape=jax.ShapeDtypeStruct((DEPTH, rows, n_out), F32),
        compiler_params=pltpu.CompilerParams(dimension_semantics=("arbitrary", "arbitrary")),
        name="ada_modulation",
    )(cond, w_ada, b_ada.reshape(DEPTH, 1, n_out))
    return out[:, :N_GROUPS].reshape(DEPTH, N_GROUPS, N_MOD, D_MODEL)


def _ffn_kernel(x_ref, mod_ref, g_ref, win_ref, wout_ref, o_ref, acc_ref, *, k):
    x = x_ref[...]
    h, gate = _premod(x, g_ref[...], mod_ref, k)
    hb = h.astype(BF16)
    for j in range(N_FF_CHUNKS):
        a = jnp.dot(hb, win_ref[j], preferred_element_type=F32)
        u = jnp.dot(hb, win_ref[N_FF_CHUNKS + j], preferred_element_type=F32)
        t = (a * jax.nn.sigmoid(a) * u).astype(BF16)
        p = jnp.dot(t, wout_ref[j], preferred_element_type=F32)
        if j == 0:
            acc_ref[...] = p
        else:
            acc_ref[...] += p
    o_ref[...] = x + 0.5 * gate * acc_ref[...]


def _ffn(x, mod_l, g, w_in, w_out, k):
    tm = TM_FFN
    win = w_in.astype(BF16).reshape(D_MODEL, 2 * N_FF_CHUNKS, FF_CHUNK).transpose(1, 0, 2)
    wout = w_out.astype(BF16).reshape(N_FF_CHUNKS, FF_CHUNK, D_MODEL)
    return pl.pallas_call(
        functools.partial(_ffn_kernel, k=k),
        grid=(N_TOK // tm,),
        in_specs=[
            pl.BlockSpec((tm, D_MODEL), lambda i: (i, 0)),
            pl.BlockSpec((None, N_MOD, D_MODEL), lambda i: (_group_of_tile(i, tm), 0, 0)),
            pl.BlockSpec((1, D_MODEL), lambda i: (0, 0)),
            _resident((2 * N_FF_CHUNKS, D_MODEL, FF_CHUNK)),
            _resident((N_FF_CHUNKS, FF_CHUNK, D_MODEL)),
        ],
        out_specs=pl.BlockSpec((tm, D_MODEL), lambda i: (i, 0)),
        out_shape=jax.ShapeDtypeStruct((N_TOK, D_MODEL), F32),
        scratch_shapes=[pltpu.VMEM((tm, D_MODEL), F32)],
        compiler_params=pltpu.CompilerParams(dimension_semantics=("arbitrary",),
                                             vmem_limit_bytes=VMEM_LIMIT),
        name="ffn_swiglu",
    )(x, mod_l, g.reshape(1, D_MODEL), win, wout)


def _proj_in_kernel(x_ref, mod_ref, g_ref, w_ref, o_ref, *, tn, act):
    h, _ = _premod(x_ref[...], g_ref[...], mod_ref, 1)
    hb = h.astype(BF16)
    for j in range(w_ref.shape[1] // tn):
        y = jnp.dot(hb, w_ref[:, j * tn:(j + 1) * tn], preferred_element_type=F32)
        if act == "gelu":
            y = jax.nn.gelu(y)
        o_ref[:, j * tn:(j + 1) * tn] = y


def _proj_in(x, mod_l, g, w, tn, act=None):
    tm = TM_PROJ
    n_out = w.shape[1]
    return pl.pallas_call(
        functools.partial(_proj_in_kernel, tn=tn, act=act),
        grid=(N_TOK // tm,),
        in_specs=[
            pl.BlockSpec((tm, D_MODEL), lambda i: (i, 0)),
            pl.BlockSpec((None, N_MOD, D_MODEL), lambda i: (_group_of_tile(i, tm), 0, 0)),
            pl.BlockSpec((1, D_MODEL), lambda i: (0, 0)),
            _resident((D_MODEL, n_out)),
        ],
        out_specs=pl.BlockSpec((tm, n_out), lambda i: (i, 0)),
        out_shape=jax.ShapeDtypeStruct((N_TOK, n_out), F32),
        compiler_params=pltpu.CompilerParams(dimension_semantics=("arbitrary",),
                                             vmem_limit_bytes=VMEM_LIMIT),
        name="mixer_proj_in",
    )(x, mod_l, g.reshape(1, D_MODEL), w.astype(BF16))


def _proj_out_kernel(x_ref, o_in_ref, mod_ref, ng_ref, w_ref, o_ref, *, norm):
    gate = mod_ref[5:6, :]
    o_in = o_in_ref[...]
    if norm:
        o_in = o_in * lax.rsqrt(jnp.mean(o_in * o_in, axis=-1, keepdims=True) + EPS) * ng_ref[...]
    y = jnp.dot(o_in.astype(BF16), w_ref[...], preferred_element_type=F32)
    o_ref[...] = x_ref[...] + gate * y


def _proj_out(x, o_in, mod_l, w, norm_g=None):
    tm = TM_PROJ
    kdim = w.shape[0]
    ng = jnp.ones((1, kdim), F32) if norm_g is None else norm_g.reshape(1, kdim)
    return pl.pallas_call(
        functools.partial(_proj_out_kernel, norm=norm_g is not None),
        grid=(N_TOK // tm,),
        in_specs=[
            pl.BlockSpec((tm, D_MODEL), lambda i: (i, 0)),
            pl.BlockSpec((tm, kdim), lambda i: (i, 0)),
            pl.BlockSpec((None, N_MOD, D_MODEL), lambda i: (_group_of_tile(i, tm), 0, 0)),
            pl.BlockSpec((1, kdim), lambda i: (0, 0)),
            _resident((kdim, D_MODEL)),
        ],
        out_specs=pl.BlockSpec((tm, D_MODEL), lambda i: (i, 0)),
        out_shape=jax.ShapeDtypeStruct((N_TOK, D_MODEL), F32),
        compiler_params=pltpu.CompilerParams(dimension_semantics=("arbitrary",),
                                             vmem_limit_bytes=VMEM_LIMIT),
        name="mixer_proj_out",
    )(x, o_in, mod_l, ng, w.astype(BF16))


def _final_norm_kernel(x_ref, g_ref, o_ref):
    x = x_ref[...]
    ms = jnp.mean(x * x, axis=-1, keepdims=True)
    o_ref[...] = x * lax.rsqrt(ms + EPS) * g_ref[...]


def _final_norm(x, g):
    tm = 1024
    return pl.pallas_call(
        _final_norm_kernel,
        grid=(N_TOK // tm,),
        in_specs=[pl.BlockSpec((tm, D_MODEL), lambda i: (i, 0)),
                  pl.BlockSpec((1, D_MODEL), lambda i: (0, 0))],
        out_specs=pl.BlockSpec((tm, D_MODEL), lambda i: (i, 0)),
        out_shape=jax.ShapeDtypeStruct((N_TOK, D_MODEL), F32),
        name="final_rmsnorm",
    )(x, g.reshape(1, D_MODEL))


def _split_streams(y):
    return y[:N_PROMPT].reshape(BATCH, SEQ, -1), y[N_PROMPT:].reshape(DEC_BATCH, DEC_SEQ, -1)


def _join_streams(yp, ys):
    return jnp.concatenate([yp.reshape(N_PROMPT, -1), ys.reshape(N_SAMPLE, -1)], axis=0)


def _rmsnorm(x, g):
    return x * lax.rsqrt(jnp.mean(x * x, axis=-1, keepdims=True) + EPS) * g


def _flip_t(t):
    return jnp.flip(t, axis=1)


def _rope_2d(x):
    T, dh = x.shape[1], x.shape[-1]
    half = dh // 2
    t = jnp.arange(T)
    inv = ROPE_THETA ** (-jnp.arange(0, half, 2, dtype=F32) / half)

    def rot(xa, pos):
        ang = pos.astype(F32)[:, None] * inv
        cos = jnp.cos(ang)[None, :, None, :]
        sin = jnp.sin(ang)[None, :, None, :]
        x1, x2 = jnp.split(xa, 2, axis=-1)
        return jnp.concatenate([x1 * cos - x2 * sin, x1 * sin + x2 * cos], axis=-1)

    return jnp.concatenate([rot(x[..., :half], t // GRID_W), rot(x[..., half:], t % GRID_W)], axis=-1)


def _gla_chunked(q, k, v, log_a, s0):
    B_, T, H, _ = q.shape
    dv = v.shape[-1]
    C = GLA_CHUNK
    n = T // C

    def blk(t):
        return t.reshape(B_, n, C, H, -1).transpose(0, 1, 3, 2, 4)

    q, k, v, la = blk(q), blk(k), blk(v), blk(log_a)
    b = jnp.cumsum(la, axis=3)
    tril = jnp.tril(jnp.ones((C, C), dtype=bool))
    diff = b[:, :, :, :, None, :] - b[:, :, :, None, :, :]
    decay = jnp.exp(jnp.where(tril[:, :, None], diff, -jnp.inf))
    attn = jnp.einsum('bnhid,bnhjd,bnhijd->bnhij', q, k, decay)
    o_intra = jnp.einsum('bnhij,bnhjv->bnhiv', attn, v)
    b_last = b[:, :, :, -1:, :]
    q_dec = q * jnp.exp(b)
    u = jnp.einsum('bnhjd,bnhjv->bnhdv', k * jnp.exp(b_last - b), v)
    g = jnp.exp(b_last[:, :, :, 0])

    def step(s, xs):
        qd, un, gn = xs
        o = jnp.einsum('bhid,bhdv->bhiv', qd, s)
        return gn[..., None] * s + un, o

    s_fin, o_inter = lax.scan(step, s0,
                              (jnp.swapaxes(q_dec, 0, 1), jnp.swapaxes(u, 0, 1), jnp.swapaxes(g, 0, 1)))
    o = o_intra + jnp.swapaxes(o_inter, 0, 1)
    return o.transpose(0, 1, 3, 2, 4).reshape(B_, T, H, dv), s_fin


def _gla_core(y, s0, w_a2, b_a, norm_g, use_rope):
    B_, T, _ = y.shape
    nq = GLA_H * GLA_DK
    nv = GLA_H * GLA_DV
    q = y[..., :nq].reshape(B_, T, GLA_H, GLA_DK) * (GLA_DK ** -0.5)
    k = y[..., nq:2 * nq].reshape(B_, T, GLA_H, GLA_DK)
    v = y[..., 2 * nq:2 * nq + nv].reshape(B_, T, GLA_H, GLA_DV)
    r = y[..., 2 * nq + nv:2 * nq + 2 * nv]
    za = y[..., GLA_IN:GLA_IN + 2 * GLA_RANK].reshape(B_, T, 2, GLA_RANK)
    if use_rope:
        q, k = _rope_2d(q), _rope_2d(k)
    z = jnp.einsum('bter,erk->btek', za, w_a2) + b_a
    log_a = (jax.nn.log_sigmoid(z) / GLA_TAU).reshape(B_, T, 2, GLA_H, GLA_DK)
    o_f, s_f = _gla_chunked(q, k, v, log_a[:, :, 0], s0[:, 0])
    o_b, s_b = _gla_chunked(_flip_t(q), _flip_t(k), _flip_t(v), _flip_t(log_a[:, :, 1]), s0[:, 1])
    o = o_f + _flip_t(o_b)
    o = _rmsnorm(o, norm_g.reshape(GLA_H, GLA_DV))
    o = o.reshape(B_, T, nv) * jax.nn.silu(r)
    return o, jnp.stack([s_f, s_b], axis=1)


def _nat_tables(rows):
    wh = min(NAT_WH, rows)
    r = np.arange(rows)
    row_idx = np.clip(r - wh // 2, 0, rows - wh)[:, None] + np.arange(wh)
    ncb = GRID_W // NAT_QB
    col_idx = np.clip(np.arange(ncb) * NAT_QB - (NAT_KB - NAT_QB) // 2, 0,
                      GRID_W - NAT_KB)[:, None] + np.arange(NAT_KB)
    qcol = np.arange(ncb)[:, None] * NAT_QB + np.arange(NAT_QB)
    c_start = np.clip(qcol - NAT_WW // 2, 0, GRID_W - NAT_WW)
    kc = col_idx[:, None, :]
    col_ok = (kc >= c_start[..., None]) & (kc < c_start[..., None] + NAT_WW)
    dc = kc - qcol[..., None]
    dr = row_idx - r[:, None]
    full = (rows, ncb, NAT_QB, wh, NAT_KB)
    flat = (rows, ncb, NAT_QB, wh * NAT_KB)
    dr_i = np.broadcast_to(dr[:, None, None, :, None] + NAT_WH - 1, full).reshape(flat)
    dc_i = np.broadcast_to(np.clip(dc + NAT_WW - 1, 0, 2 * NAT_WW - 2)[None, :, :, None, :], full).reshape(flat)
    ok = np.broadcast_to(col_ok[None, :, :, None, :], full).reshape(flat)
    return row_idx, col_idx, dr_i, dc_i, ok


def _nat_context_core(y):
    B_, S, _ = y.shape
    q, k, v = jnp.split(y, 3, axis=-1)
    q = q.reshape(B_, S, NAT_H, NAT_HD)
    k = k.reshape(B_, S, NAT_H, NAT_HD)
    v = v.reshape(B_, S, NAT_H, NAT_HD)
    s = jnp.einsum('bqhd,bkhd->bhqk', q, k) * (NAT_HD ** -0.5)
    p = jax.nn.softmax(s, axis=-1)
    o = jnp.einsum('bhqk,bkhd->bqhd', p, v).reshape(B_, S, D_MODEL)
    return o, k.transpose(0, 2, 1, 3), v.transpose(0, 2, 1, 3)


def _nat_latent_core(y, ck, cv, rpb):
    B_, T, _ = y.shape
    rows = T // GRID_W
    ncb = GRID_W // NAT_QB
    row_idx, col_idx, dr_i, dc_i, ok = _nat_tables(rows)
    q, k, v = jnp.split(y, 3, axis=-1)
    q = q.reshape(B_, rows, ncb, NAT_QB, NAT_H, NAT_HD)
    k = k.reshape(B_, rows, GRID_W, NAT_H, NAT_HD)
    v = v.reshape(B_, rows, GRID_W, NAT_H, NAT_HD)
    ri = row_idx[:, None, :, None]
    ci = col_idx[None, :, None, :]
    kb = k[:, ri, ci].reshape(B_, rows, ncb, -1, NAT_H, NAT_HD)
    vb = v[:, ri, ci].reshape(B_, rows, ncb, -1, NAT_H, NAT_HD)
    nk = kb.shape[3]
    scale = NAT_HD ** -0.5
    s_lat = jnp.einsum('brnqhd,brnkhd->bhrnqk', q, kb) * scale
    s_lat = s_lat + rpb[:, dr_i, dc_i]
    s_lat = jnp.where(ok, s_lat, NEG_INF)
    s_ctx = jnp.einsum('brnqhd,bhsd->bhrnqs', q, ck) * scale
    p = jax.nn.softmax(jnp.concatenate([s_lat, s_ctx], axis=-1), axis=-1)
    o = (jnp.einsum('bhrnqk,brnkhd->brnqhd', p[..., :nk], vb)
         + jnp.einsum('bhrnqs,bhsd->brnqhd', p[..., nk:], cv))
    return o.reshape(B_, T, D_MODEL)


def _gmlp_core(y, ln_g, ln_b, w_s, b_s):
    B_, T, _ = y.shape
    u, v = jnp.split(y, 2, axis=-1)
    vc = v - jnp.mean(v, axis=-1, keepdims=True)
    v = vc * lax.rsqrt(jnp.mean(vc * vc, axis=-1, keepdims=True) + EPS) * ln_g + ln_b
    v = v.reshape(B_, T // GM_CHUNK, GM_CHUNK, GM_G, GM_CG)
    v = jnp.einsum('gpq,bnqgc->bnpgc', w_s, v) + b_s.T[None, None, :, :, None]
    return u * v.reshape(B_, T, GM_DH)


GLA_BLK = 128
GLA_CPB = GLA_BLK // GLA_CHUNK
GLA_ZCOL = GLA_IN // LANE


def _gla_consts():
    r = np.arange(GLA_BLK)
    same = (r[:, None] // GLA_CHUNK) == (r[None, :] // GLA_CHUNK)
    ri, ci = r[:, None] % GLA_CHUNK, r[None, :] % GLA_CHUNK
    lf = np.concatenate([same & (ci <= ri), same & (ci > ri)], axis=0)
    lb = np.concatenate([same & (ci >= ri), same & (ci < ri)], axis=0)
    rows = np.arange(GLA_CHUNK * GLA_DK)
    sel = (rows[:, None] // GLA_DK) == (np.arange(LANE)[None, :] % GLA_CHUNK)
    return (jnp.asarray(lf, BF16), jnp.asarray(lb, BF16), jnp.asarray(sel, BF16))


def _rope_tables(T):
    half = GLA_DK // 2
    t = np.arange(T)
    inv = ROPE_THETA ** (-np.arange(0, half, 2, dtype=np.float64) / half)
    lane = np.arange(GLA_DK)
    pos = np.where(lane[None, :] < half, (t // GRID_W)[:, None], (t % GRID_W)[:, None])
    ang = pos * inv[lane % (half // 2)][None, :]
    sign = np.where((lane % half) < half // 2, -1.0, 1.0)[None, :]
    return jnp.asarray(np.cos(ang), F32), jnp.asarray(np.sin(ang) * sign, F32)


def _rope_apply(x, cos, sin_signed):
    half = GLA_DK // 2
    lane = lax.broadcasted_iota(jnp.int32, (1, GLA_DK), 1)
    partner = jnp.where((lane % half) < half // 2,
                        pltpu.roll(x, GLA_DK - half // 2, axis=1), pltpu.roll(x, half // 2, axis=1))
    return x * cos + partner * sin_signed


def _split3(x):
    hi = x.astype(BF16)
    r1 = x - hi.astype(F32)
    mid = r1.astype(BF16)
    lo = (r1 - mid.astype(F32)).astype(BF16)
    return hi, mid, lo


def _dot3(m, x):
    hi, mid, lo = _split3(x)
    d = lambda p: jnp.dot(m, p, preferred_element_type=F32)
    return d(hi) + d(mid) + d(lo)


def _gla_kernel(*refs, use_rope, has_s0, want_state):
    it = iter(refs)
    q_ref, k_ref, v_ref, r_ref, za_ref, w2_ref, ba_ref, ng_ref, lf_ref, lb_ref, sel_ref = (next(it) for _ in range(11))
    cos_ref, sin_ref = (next(it), next(it)) if use_rope else (None, None)
    s0_ref = next(it) if has_s0 else None
    if has_s0:
        next(it)
    o_ref = next(it)
    st_ref = next(it) if want_state else None
    (q_s, k_s, bf_s, bb_s, ef_s, eb_s, qdf_s, kdf_s, qdb_s, kdb_s, v_s, w_s, o_s, sf_s, sb_s) = it

    T = q_ref.shape[0]
    nblk, nch = T // GLA_BLK, T // GLA_CHUNK

    for blk in range(nblk):
        rows = slice(blk * GLA_BLK, (blk + 1) * GLA_BLK)
        z = jnp.dot(za_ref[rows, :].astype(BF16), w2_ref[...], preferred_element_type=F32) + ba_ref[...]
        la = (jnp.minimum(z, 0.0) - jnp.log1p(jnp.exp(-jnp.abs(z)))) * (1.0 / GLA_TAU)
        cf = _dot3(lf_ref[...], la[:, :GLA_DK])
        cb = _dot3(lb_ref[...], la[:, GLA_DK:])
        bf, bb = cf[:GLA_BLK], cb[:GLA_BLK]
        ef, eb = jnp.exp(bf), jnp.exp(bb)
        q = q_ref[rows, :] * (GLA_DK ** -0.5)
        k = k_ref[rows, :]
        if use_rope:
            q = _rope_apply(q, cos_ref[rows, :], sin_ref[rows, :])
            k = _rope_apply(k, cos_ref[rows, :], sin_ref[rows, :])
        q_s[rows, :], k_s[rows, :] = q, k
        bf_s[rows, :], bb_s[rows, :] = bf, bb
        ef_s[rows, :], eb_s[rows, :] = ef, eb
        qdf_s[rows, :] = (q * ef).astype(BF16)
        qdb_s[rows, :] = (q * eb).astype(BF16)
        kdf_s[rows, :] = (k * jnp.exp(cf[GLA_BLK:])).astype(BF16)
        kdb_s[rows, :] = (k * jnp.exp(cb[GLA_BLK:])).astype(BF16)
        v_s[rows, :] = v_ref[rows, :].astype(BF16)

    irow = lax.broadcasted_iota(jnp.int32, (GLA_CHUNK, 1), 0)

    def intra_chunk(c, carry):
        rows = pl.ds(pl.multiple_of(c * GLA_CHUNK, GLA_CHUNK), GLA_CHUNK)
        qc, bfc, bbc = q_s[rows, :], bf_s[rows, :], bb_s[rows, :]
        for j in range(GLA_CHUNK):
            row = pl.ds(c * GLA_CHUNK + j, 1)
            e = (jnp.exp(jnp.where(irow >= j, bfc - bf_s[row, :], -jnp.inf))
                 + jnp.exp(jnp.where(irow <= j, bbc - bb_s[row, :], -jnp.inf)))
            w_s[rows, j * GLA_DK:(j + 1) * GLA_DK] = (qc * k_s[row, :] * e).astype(BF16)
        return carry

    lax.fori_loop(0, nch, intra_chunk, 0)

    lane_c = lax.broadcasted_iota(jnp.int32, (GLA_BLK, LANE), 1) // GLA_CHUNK
    row_c = lax.broadcasted_iota(jnp.int32, (GLA_BLK, LANE), 0) // GLA_CHUNK
    for blk in range(nblk):
        rows = slice(blk * GLA_BLK, (blk + 1) * GLA_BLK)
        a = jnp.dot(w_s[rows, :], sel_ref[...], preferred_element_type=F32)
        a = jnp.where(lane_c == row_c, a, 0.0).astype(BF16)
        o_s[rows, :] = jnp.dot(a, v_s[rows, :], preferred_element_type=F32)

    if has_s0:
        sf_s[...] = s0_ref[0].T
        sb_s[...] = s0_ref[1].T
    else:
        sf_s[...] = jnp.zeros_like(sf_s)
        sb_s[...] = jnp.zeros_like(sb_s)

    def chain(rows, g_row, qd_s, kd_s, e_s, st_s):
        s = st_s[...]
        o_s[rows, :] += _dot_nt(qd_s[rows, :], s.astype(BF16))
        u = lax.dot_general(v_s[rows, :], kd_s[rows, :], (((0,), (0,)), ((), ())), preferred_element_type=F32)
        st_s[...] = s * e_s[g_row, :] + u

    def inter_chunk(n, carry):
        cf_ = pl.multiple_of(n * GLA_CHUNK, GLA_CHUNK)
        cb_ = pl.multiple_of((nch - 1 - n) * GLA_CHUNK, GLA_CHUNK)
        chain(pl.ds(cf_, GLA_CHUNK), pl.ds(cf_ + GLA_CHUNK - 1, 1), qdf_s, kdf_s, ef_s, sf_s)
        chain(pl.ds(cb_, GLA_CHUNK), pl.ds(cb_, 1), qdb_s, kdb_s, eb_s, sb_s)
        return carry

    lax.fori_loop(0, nch, inter_chunk, 0, unroll=4)

    o = o_s[...]
    o = o * lax.rsqrt(jnp.mean(o * o, axis=-1, keepdims=True) + EPS) * ng_ref[...]
    r = r_ref[...]
    o_ref[...] = o * (r * jax.nn.sigmoid(r))
    if want_state:
        st_ref[0] = sf_s[...].T
        st_ref[1] = sb_s[...].T


def _gla_call(y, w2, ba, ng, T, n_seq, row0, use_rope, s0, o_prev=None):
    has_s0 = s0 is not None
    want_state = not has_s0
    rb = row0 // T
    nqb = GLA_H
    col = lambda w, off: pl.BlockSpec((T, w), lambda b, h: (rb + b, off + h))
    cst = lambda shape: pl.BlockSpec(shape, lambda b, h: (0,) * len(shape))
    per_head = lambda shape: pl.BlockSpec((None,) + shape, lambda b, h: (h,) + (0,) * len(shape))
    lf, lb, sel = _gla_consts()
    in_specs = [col(GLA_DK, 0), col(GLA_DK, nqb), col(GLA_DV, nqb), col(GLA_DV, nqb + GLA_H),
                pl.BlockSpec((T, LANE), lambda b, h: (rb + b, GLA_ZCOL)),
                per_head((LANE, 2 * GLA_DK)), per_head((1, 2 * GLA_DK)), per_head((1, GLA_DV)),
                cst(lf.shape), cst(lb.shape), cst(sel.shape)]
    args = [y, y, y, y, y, w2, ba, ng, lf, lb, sel]
    if use_rope:
        cos, sin = _rope_tables(T)
        in_specs += [cst(cos.shape), cst(sin.shape)]
        args += [cos, sin]
    if has_s0:
        in_specs += [pl.BlockSpec((None, 2, None, GLA_DK, GLA_DV), lambda b, h: (b, 0, h, 0, 0)),
                     pl.BlockSpec(memory_space=pl.ANY)]
        args += [s0, o_prev]
    aliases = {len(args) - 1: 0} if has_s0 else {}
    out_specs = [pl.BlockSpec((T, GLA_DV), lambda b, h: (rb + b, h))]
    out_shape = [jax.ShapeDtypeStruct((N_TOK, GLA_H * GLA_DV), F32)]
    if want_state:
        out_specs.append(pl.BlockSpec((None, 2, None, GLA_DK, GLA_DV), lambda b, h: (b, 0, h, 0, 0)))
        out_shape.append(jax.ShapeDtypeStruct((n_seq, 2, GLA_H, GLA_DK, GLA_DV), F32))
    f32s = lambda w: pltpu.VMEM((T, w), F32)
    bf16s = lambda w: pltpu.VMEM((T, w), BF16)
    scratch = ([f32s(GLA_DK)] * 6 + [bf16s(GLA_DK)] * 4
               + [bf16s(GLA_DV), bf16s(GLA_CHUNK * GLA_DK), f32s(GLA_DV),
                  pltpu.VMEM((GLA_DV, GLA_DK), F32), pltpu.VMEM((GLA_DV, GLA_DK), F32)])
    outs = pl.pallas_call(
        functools.partial(_gla_kernel, use_rope=use_rope, has_s0=has_s0, want_state=want_state),
        grid=(n_seq, GLA_H),
        in_specs=in_specs, out_specs=out_specs, out_shape=out_shape,
        scratch_shapes=scratch,
        input_output_aliases=aliases,
        compiler_params=pltpu.CompilerParams(dimension_semantics=("arbitrary", "arbitrary"),
                                             vmem_limit_bytes=VMEM_LIMIT),
        name="gla_rope" if use_rope else "gla",
    )(*args)
    return (outs[0], outs[1]) if want_state else (outs[0], None)


def _gla_seq_kernel(*refs, use_rope, has_s0, want_state):
    it = iter(refs)
    q_ref, k_ref, v_ref, r_ref, za_ref, w2_ref, ba_ref, ng_ref, lf_ref, lb_ref, sel_ref = (next(it) for _ in range(11))
    cos_ref, sin_ref = (next(it), next(it)) if use_rope else (None, None)
    s0_ref = next(it) if has_s0 else None
    if has_s0:
        next(it)
    o_ref = next(it)
    st_ref = next(it) if want_state else None
    q_s, k_s, bf_s, bb_s, qdf_s, kdf_s, qdb_s, kdb_s, w_s, sf_s, sb_s = it

    T = q_ref.shape[0]
    nblk, nch = T // GLA_BLK, T // GLA_CHUNK
    hq = lambda h: slice(h * GLA_DK, (h + 1) * GLA_DK)
    hv = lambda h: slice(h * GLA_DV, (h + 1) * GLA_DV)

    for blk in range(nblk):
        rows = slice(blk * GLA_BLK, (blk + 1) * GLA_BLK)
        zab = za_ref[rows, :].astype(BF16)
        for h in range(GLA_H):
            z = jnp.dot(zab, w2_ref[h], preferred_element_type=F32) + ba_ref[h]
            la = (jnp.minimum(z, 0.0) - jnp.log1p(jnp.exp(-jnp.abs(z)))) * (1.0 / GLA_TAU)
            cf = _dot3(lf_ref[...], la[:, :GLA_DK])
            cb = _dot3(lb_ref[...], la[:, GLA_DK:])
            bf, bb = cf[:GLA_BLK], cb[:GLA_BLK]
            q = q_ref[rows, hq(h)] * (GLA_DK ** -0.5)
            k = k_ref[rows, hq(h)]
            if use_rope:
                q = _rope_apply(q, cos_ref[rows, :], sin_ref[rows, :])
                k = _rope_apply(k, cos_ref[rows, :], sin_ref[rows, :])
            q_s[h, rows, :], k_s[h, rows, :] = q, k
            bf_s[h, rows, :], bb_s[h, rows, :] = bf, bb
            qdf_s[h, rows, :] = (q * jnp.exp(bf)).astype(BF16)
            qdb_s[h, rows, :] = (q * jnp.exp(bb)).astype(BF16)
            kdf_s[h, rows, :] = (k * jnp.exp(cf[GLA_BLK:])).astype(BF16)
            kdb_s[h, rows, :] = (k * jnp.exp(cb[GLA_BLK:])).astype(BF16)

    irow = lax.broadcasted_iota(jnp.int32, (GLA_CHUNK, 1), 0)
    lane_c = lax.broadcasted_iota(jnp.int32, (GLA_BLK, LANE), 1) // GLA_CHUNK
    row_c = lax.broadcasted_iota(jnp.int32, (GLA_BLK, LANE), 0) // GLA_CHUNK
    for h in range(GLA_H):
        def intra_chunk(c, carry, h=h):
            rows = pl.ds(pl.multiple_of(c * GLA_CHUNK, GLA_CHUNK), GLA_CHUNK)
            qc, bfc, bbc = q_s[h, rows, :], bf_s[h, rows, :], bb_s[h, rows, :]
            for j in range(GLA_CHUNK):
                row = pl.ds(c * GLA_CHUNK + j, 1)
                e = (jnp.exp(jnp.where(irow >= j, bfc - bf_s[h, row, :], -jnp.inf))
                     + jnp.exp(jnp.where(irow <= j, bbc - bb_s[h, row, :], -jnp.inf)))
                w_s[rows, j * GLA_DK:(j + 1) * GLA_DK] = (qc * k_s[h, row, :] * e).astype(BF16)
            return carry

        lax.fori_loop(0, nch, intra_chunk, 0)
        for blk in range(nblk):
            rows = slice(blk * GLA_BLK, (blk + 1) * GLA_BLK)
            a = jnp.dot(w_s[rows, :], sel_ref[...], preferred_element_type=F32)
            a = jnp.where(lane_c == row_c, a, 0.0).astype(BF16)
            o_ref[rows, hv(h)] = jnp.dot(a, v_ref[rows, hv(h)].astype(BF16), preferred_element_type=F32)

    for h in range(GLA_H):
        if has_s0:
            sf_s[h] = s0_ref[0, h].T
            sb_s[h] = s0_ref[1, h].T
        else:
            sf_s[h] = jnp.zeros((GLA_DV, GLA_DK), F32)
            sb_s[h] = jnp.zeros((GLA_DV, GLA_DK), F32)

    def chain(h, rows, g_row, qd_s, kd_s, b_s, st_s):
        s = st_s[h]
        o_ref[rows, hv(h)] += _dot_nt(qd_s[h, rows, :], s.astype(BF16))
        u = lax.dot_general(v_ref[rows, hv(h)].astype(BF16), kd_s[h, rows, :], (((0,), (0,)), ((), ())),
                            preferred_element_type=F32)
        st_s[h] = s * jnp.exp(b_s[h, g_row, :]) + u

    def inter_chunk(n, carry):
        cf_ = pl.multiple_of(n * GLA_CHUNK, GLA_CHUNK)
        cb_ = pl.multiple_of((nch - 1 - n) * GLA_CHUNK, GLA_CHUNK)
        for h in range(GLA_H):
            chain(h, pl.ds(cf_, GLA_CHUNK), pl.ds(cf_ + GLA_CHUNK - 1, 1), qdf_s, kdf_s, bf_s, sf_s)
            chain(h, pl.ds(cb_, GLA_CHUNK), pl.ds(cb_, 1), qdb_s, kdb_s, bb_s, sb_s)
        return carry

    lax.fori_loop(0, nch, inter_chunk, 0, unroll=2)

    for h in range(GLA_H):
        o = o_ref[:, hv(h)]
        o = o * lax.rsqrt(jnp.mean(o * o, axis=-1, keepdims=True) + EPS) * ng_ref[h]
        r = r_ref[:, hv(h)]
        o_ref[:, hv(h)] = o * (r * jax.nn.sigmoid(r))
        if want_state:
            st_ref[0, h] = sf_s[h].T
            st_ref[1, h] = sb_s[h].T


def _gla_seq_call(y, w2, ba, ng, T, n_seq, row0, use_rope, s0, o_prev=None):
    has_s0 = s0 is not None
    want_state = not has_s0
    rb = row0 // T
    nq, nv = GLA_H * GLA_DK, GLA_H * GLA_DV
    mode = dict(pipeline_mode=pl.Buffered(1)) if n_seq <= 2 else {}
    col = lambda w, j: pl.BlockSpec((T, w), lambda b: (rb + b, j), **mode)
    cst = lambda a: pl.BlockSpec(a.shape, lambda b: (0,) * a.ndim)
    lf, lb, sel = _gla_consts()
    in_specs = [col(nq, 0), col(nq, 1), col(nv, 1), col(nv, 2), col(LANE, GLA_ZCOL),
                cst(w2), cst(ba), cst(ng), cst(lf), cst(lb), cst(sel)]
    args = [y, y, y, y, y, w2, ba, ng, lf, lb, sel]
    if use_rope:
        cos, sin = _rope_tables(T)
        in_specs += [cst(cos), cst(sin)]
        args += [cos, sin]
    st_spec = pl.BlockSpec((None, 2, GLA_H, GLA_DK, GLA_DV), lambda b: (b, 0, 0, 0, 0))
    if has_s0:
        in_specs += [st_spec, pl.BlockSpec(memory_space=pl.ANY)]
        args += [s0, o_prev]
    aliases = {len(args) - 1: 0} if has_s0 else {}
    out_specs = [pl.BlockSpec((T, nv), lambda b: (rb + b, 0))]
    out_shape = [jax.ShapeDtypeStruct((N_TOK, nv), F32)]
    if want_state:
        out_specs.append(st_spec)
        out_shape.append(jax.ShapeDtypeStruct((n_seq, 2, GLA_H, GLA_DK, GLA_DV), F32))
    scratch = ([pltpu.VMEM((GLA_H, T, GLA_DK), F32)] * 4 + [pltpu.VMEM((GLA_H, T, GLA_DK), BF16)] * 4
               + [pltpu.VMEM((T, GLA_CHUNK * GLA_DK), BF16),
                  pltpu.VMEM((GLA_H, GLA_DV, GLA_DK), F32), pltpu.VMEM((GLA_H, GLA_DV, GLA_DK), F32)])
    outs = pl.pallas_call(
        functools.partial(_gla_seq_kernel, use_rope=use_rope, has_s0=has_s0, want_state=want_state),
        grid=(n_seq,),
        in_specs=in_specs, out_specs=out_specs, out_shape=out_shape,
        scratch_shapes=scratch,
        input_output_aliases=aliases,
        compiler_params=pltpu.CompilerParams(dimension_semantics=("arbitrary",),
                                             vmem_limit_bytes=VMEM_LIMIT),
        name="gla_rope" if use_rope else "gla",
    )(*args)
    return (outs[0], outs[1]) if want_state else (outs[0], None)


def _gla_mixer(y, state, w_a2, b_a, norm_g):
    w2 = jnp.zeros((GLA_H, LANE, 2 * GLA_DK), F32)
    for e in range(2):
        we = w_a2[e].reshape(GLA_RANK, GLA_H, GLA_DK).transpose(1, 0, 2)
        w2 = w2.at[:, e * GLA_RANK:(e + 1) * GLA_RANK, e * GLA_DK:(e + 1) * GLA_DK].set(we)
    ba = b_a.reshape(2, GLA_H, GLA_DK).transpose(1, 0, 2).reshape(GLA_H, 1, 2 * GLA_DK)
    ng = norm_g.reshape(GLA_H, 1, GLA_DV)
    op, st = _gla_seq_call(y, w2.astype(BF16), ba, ng, SEQ, BATCH, 0, False, None)
    o, _ = _gla_seq_call(y, w2.astype(BF16), ba, ng, DEC_SEQ, DEC_BATCH, N_PROMPT, True, state, op)
    return o, st


N_HEAD_PAIRS = NAT_H // 2
NAT_ROWS = DEC_SEQ // GRID_W
NAT_WIN = NAT_WH * GRID_W
NAT_CLS = NAT_WH


def _nat_row_window(r):
    rs = min(max(r - NAT_WH // 2, 0), NAT_ROWS - NAT_WH)
    return rs, r - rs


def _nat_bias_table(rpb):
    qc = np.arange(GRID_W)[:, None]
    kc = np.arange(GRID_W)[None, :]
    c_start = np.clip(qc - NAT_WW // 2, 0, GRID_W - NAT_WW)
    ok = (kc >= c_start) & (kc < c_start + NAT_WW)
    dc = np.clip(kc - qc + NAT_WW - 1, 0, 2 * NAT_WW - 2)
    cls = np.arange(NAT_CLS)[:, None]
    w = np.arange(NAT_WH)[None, :]
    dr = w - cls + NAT_WH - 1
    pick = jnp.asarray(dc[None] == np.arange(2 * NAT_WW - 1)[:, None, None], F32)
    t = jnp.einsum('hcwd,dqk->hcqwk', rpb[:, dr], pick, precision=lax.Precision.HIGHEST)
    t = jnp.where(ok[None, None, :, None, :], t, NEG_INF)
    return t.reshape(NAT_H, NAT_CLS, GRID_W, NAT_WIN)


def _head_mask(hh):
    lane = lax.broadcasted_iota(jnp.int32, (1, LANE), 1)
    return (lane < NAT_HD) if hh == 0 else (lane >= NAT_HD)


def _dot_nt(a, b):
    return lax.dot_general(a, b, (((1,), (1,)), ((), ())), preferred_element_type=F32)


def _nat_ctx_kernel(q_ref, k_ref, v_ref, o_ref, kc_ref, vc_ref):
    for hp in range(N_HEAD_PAIRS):
        cols = slice(hp * LANE, (hp + 1) * LANE)
        q = q_ref[:, cols] * (NAT_HD ** -0.5)
        k, v = k_ref[:, cols], v_ref[:, cols]
        kb, vb = k.astype(BF16), v.astype(BF16)
        outs = []
        for hh in range(2):
            qm = jnp.where(_head_mask(hh), q, 0.0).astype(BF16)
            s = _dot_nt(qm, kb)
            p = jnp.exp(s - jnp.max(s, axis=-1, keepdims=True))
            l = jnp.sum(p, axis=-1, keepdims=True)
            outs.append(jnp.dot(p.astype(BF16), vb, preferred_element_type=F32) / l)
            kc_ref[2 * hp + hh] = k[:, hh * NAT_HD:(hh + 1) * NAT_HD]
            vc_ref[2 * hp + hh] = v[:, hh * NAT_HD:(hh + 1) * NAT_HD]
        o_ref[:, cols] = jnp.where(_head_mask(0), outs[0], outs[1])


def _nat_context(y):
    blk = lambda j: pl.BlockSpec((SEQ, D_MODEL), lambda b: (b, j))
    cache = pl.BlockSpec((None, NAT_H, SEQ, NAT_HD), lambda b: (b, 0, 0, 0))
    cache_shape = jax.ShapeDtypeStruct((BATCH, NAT_H, SEQ, NAT_HD), F32)
    return pl.pallas_call(
        _nat_ctx_kernel,
        grid=(BATCH,),
        in_specs=[blk(0), blk(1), blk(2)],
        out_specs=[pl.BlockSpec((SEQ, D_MODEL), lambda b: (b, 0)), cache, cache],
        out_shape=[jax.ShapeDtypeStruct((N_TOK, D_MODEL), F32), cache_shape, cache_shape],
        compiler_params=pltpu.CompilerParams(dimension_semantics=("arbitrary",)),
        name="nat_context",
    )(y, y, y)


def _nat_lat_kernel(q_ref, k_ref, v_ref, ck_ref, cv_ref, tab_ref, o_in_ref, o_ref):
    del o_in_ref
    q = q_ref[...] * (NAT_HD ** -0.5)
    qm = [jnp.where(_head_mask(hh), q, 0.0).astype(BF16) for hh in range(2)]
    ckb = ck_ref[...].astype(BF16)
    cvb = cv_ref[...].astype(BF16)
    for r in range(NAT_ROWS):
        rs, cls = _nat_row_window(r)
        kw = k_ref[rs * GRID_W:rs * GRID_W + NAT_WIN, :].astype(BF16)
        vw = v_ref[rs * GRID_W:rs * GRID_W + NAT_WIN, :].astype(BF16)
        outs = []
        for hh in range(2):
            qr = qm[hh][r * GRID_W:(r + 1) * GRID_W]
            s_lat = _dot_nt(qr, kw) + tab_ref[hh, cls]
            s_ctx = _dot_nt(qr, ckb)
            m = jnp.maximum(jnp.max(s_lat, axis=-1, keepdims=True), jnp.max(s_ctx, axis=-1, keepdims=True))
            p_lat = jnp.exp(s_lat - m)
            p_ctx = jnp.exp(s_ctx - m)
            l = jnp.sum(p_lat, axis=-1, keepdims=True) + jnp.sum(p_ctx, axis=-1, keepdims=True)
            o = (jnp.dot(p_lat.astype(BF16), vw, preferred_element_type=F32)
                 + jnp.dot(p_ctx.astype(BF16), cvb, preferred_element_type=F32))
            outs.append(o / l)
        o_ref[r * GRID_W:(r + 1) * GRID_W, :] = jnp.where(_head_mask(0), outs[0], outs[1])


def _nat_latent(y, o_ctx, ck, cv, rpb):
    row0 = N_PROMPT // DEC_SEQ
    blk = lambda off: pl.BlockSpec((DEC_SEQ, LANE), lambda b, hp: (row0 + b, off + hp))
    ctx = pl.BlockSpec((None, ck.shape[1], LANE), lambda b, hp: (b, 0, hp))
    return pl.pallas_call(
        _nat_lat_kernel,
        grid=(DEC_BATCH, N_HEAD_PAIRS),
        in_specs=[blk(0), blk(N_HEAD_PAIRS), blk(2 * N_HEAD_PAIRS), ctx, ctx,
                  pl.BlockSpec((2, NAT_CLS, GRID_W, NAT_WIN), lambda b, hp: (hp, 0, 0, 0)),
                  pl.BlockSpec(memory_space=pl.ANY)],
        out_specs=pl.BlockSpec((DEC_SEQ, LANE), lambda b, hp: (row0 + b, hp)),
        out_shape=jax.ShapeDtypeStruct((N_TOK, D_MODEL), F32),
        input_output_aliases={6: 0},
        compiler_params=pltpu.CompilerParams(dimension_semantics=("arbitrary", "arbitrary")),
        name="nat_latent",
    )(y, y, y, ck, cv, _nat_bias_table(rpb), o_ctx)


def _heads_last(t):
    b, h, s, d = t.shape
    return t.transpose(0, 2, 1, 3).reshape(b, s, h * d)


def _heads_first(t):
    b, s, _ = t.shape
    return t.reshape(b, s, NAT_H, NAT_HD).transpose(0, 2, 1, 3)


def _gmlp_kernel(x_ref, y_ref, mod_ref, lng_ref, lnb_ref, ws_ref, bs_ref, w_ref, o_ref, t_ref):
    tm = x_ref.shape[0]
    v = y_ref[:, GM_DH:]
    vc = v - jnp.mean(v, axis=-1, keepdims=True)
    vn = vc * lax.rsqrt(jnp.mean(vc * vc, axis=-1, keepdims=True) + EPS) * lng_ref[...] + lnb_ref[...]
    vn = vn.astype(BF16)
    for n in range(tm // GM_CHUNK):
        rows = slice(n * GM_CHUNK, (n + 1) * GM_CHUNK)
        for g in range(GM_G):
            cols = slice(g * GM_CG, (g + 1) * GM_CG)
            sp = jnp.dot(ws_ref[g], vn[rows, cols], preferred_element_type=F32) + bs_ref[:, cols]
            t_ref[rows, cols] = (y_ref[rows, cols] * sp).astype(BF16)
    gate = mod_ref[5:6, :]
    o_ref[...] = x_ref[...] + gate * jnp.dot(t_ref[...], w_ref[...], preferred_element_type=F32)


def _gmlp_out(x, y, mod_l, ln_g, ln_b, w_s, b_s, w_out):
    tm = TM_PROJ
    bias = jnp.repeat(b_s.T, GM_CG, axis=1)
    return pl.pallas_call(
        _gmlp_kernel,
        grid=(N_TOK // tm,),
        in_specs=[
            pl.BlockSpec((tm, D_MODEL), lambda i: (i, 0)),
            pl.BlockSpec((tm, 2 * GM_DH), lambda i: (i, 0)),
            pl.BlockSpec((None, N_MOD, D_MODEL), lambda i: (_group_of_tile(i, tm), 0, 0)),
            pl.BlockSpec((1, GM_DH), lambda i: (0, 0)),
            pl.BlockSpec((1, GM_DH), lambda i: (0, 0)),
            _resident((GM_G, GM_CHUNK, GM_CHUNK)),
            _resident((GM_CHUNK, GM_DH)),
            _resident((GM_DH, D_MODEL)),
        ],
        out_specs=pl.BlockSpec((tm, D_MODEL), lambda i: (i, 0)),
        out_shape=jax.ShapeDtypeStruct((N_TOK, D_MODEL), F32),
        scratch_shapes=[pltpu.VMEM((tm, GM_DH), BF16)],
        compiler_params=pltpu.CompilerParams(dimension_semantics=("arbitrary",),
                                             vmem_limit_bytes=VMEM_LIMIT),
        name="gmlp_gate_out",
    )(x, y, mod_l, ln_g.reshape(1, GM_DH), ln_b.reshape(1, GM_DH), w_s.astype(BF16), bias,
      w_out.astype(BF16))


def _dwconv_centred(x, w, b):
    K = w.shape[0]
    T = x.shape[1]
    pad = K // 2
    xp = jnp.pad(x, ((0, 0), (pad, pad), (0, 0)))
    out = xp[:, 0:T] * w[0]
    for i in range(1, K):
        out = out + xp[:, i:i + T] * w[i]
    return out + b


def _ssd_chunked(x, dt, a, bm, cm, s0):
    B_, T = x.shape[:2]
    L = SSD_CHUNK
    n = T // L
    E = SSD_H // SSD_G
    x = x.reshape(B_, n, L, SSD_G, E, SSD_P)
    dt = dt.reshape(B_, n, L, SSD_G, E)
    bm = bm.reshape(B_, n, L, SSD_G, SSD_N)
    cm = cm.reshape(B_, n, L, SSD_G, SSD_N)
    cum = jnp.cumsum(dt * a.reshape(SSD_G, E), axis=2)
    tril = jnp.tril(jnp.ones((L, L), dtype=bool))
    seg = cum[:, :, :, None] - cum[:, :, None, :]
    decay = jnp.exp(jnp.where(tril[:, :, None, None], seg, -jnp.inf))
    dtx = x * dt[..., None]
    cb = jnp.einsum('bnigs,bnjgs->bnijg', cm, bm)
    y_diag = jnp.einsum('bnijg,bnijge,bnjgep->bnigep', cb, decay, dtx)
    u = jnp.einsum('bnjgs,bnjge,bnjgep->bngeps', bm, jnp.exp(cum[:, :, -1:] - cum), dtx)
    chunk_decay = jnp.exp(cum[:, :, -1])
    q_decay = jnp.exp(cum)

    def step(s, xs):
        c_n, qd_n, u_n, g_n = xs
        y = jnp.einsum('bigs,bige,bgeps->bigep', c_n, qd_n, s)
        return g_n[..., None, None] * s + u_n, y

    sw = lambda t: jnp.swapaxes(t, 0, 1)
    s_fin, y_off = lax.scan(step, s0.reshape(B_, SSD_G, E, SSD_P, SSD_N),
                            (sw(cm), sw(q_decay), sw(u), sw(chunk_decay)))
    y = y_diag + sw(y_off)
    return y.reshape(B_, T, SSD_H, SSD_P), s_fin.reshape(B_, SSD_H, SSD_P, SSD_N)


def _ssd_core(y, s0, conv_w, conv_b, dt_bias, a_log, d_skip, norm_g):
    B_, T, _ = y.shape
    z = y[..., :SSD_DI]
    xbc = y[..., SSD_DI:SSD_DI + SSD_XBC]
    dt = y[..., SSD_DI + SSD_XBC:SSD_IN]
    xbc = jax.nn.silu(_dwconv_centred(xbc, conv_w, conv_b))
    x = xbc[..., :SSD_DI].reshape(B_, T, SSD_H, SSD_P)
    bm = xbc[..., SSD_DI:SSD_DI + SSD_G * SSD_N].reshape(B_, T, SSD_G, SSD_N)
    cm = xbc[..., SSD_DI + SSD_G * SSD_N:].reshape(B_, T, SSD_G, SSD_N)
    dt = jax.nn.softplus(dt.reshape(B_, T, 2, SSD_H) + dt_bias)
    a = -jnp.exp(a_log)
    y_f, s_f = _ssd_chunked(x, dt[:, :, 0], a[0], bm, cm, s0[:, 0])
    y_b, s_b = _ssd_chunked(_flip_t(x), _flip_t(dt[:, :, 1]), a[1], _flip_t(bm), _flip_t(cm), s0[:, 1])
    yy = y_f + _flip_t(y_b) + d_skip[:, None] * x
    yy = _rmsnorm(yy.reshape(B_, T, SSD_DI) * jax.nn.silu(z), norm_g)
    return yy, jnp.stack([s_f, s_b], axis=1)


SSD_E = SSD_H // SSD_G
SSD_GP = SSD_E * SSD_P
SSD_BLK = 2 * SSD_CHUNK
SSD_COL_X = SSD_DI
SSD_COL_B = 2 * SSD_DI
SSD_COL_C = SSD_COL_B + SSD_G * SSD_N
SSD_COL_DT = SSD_COL_C + SSD_G * SSD_N
SSD_PROJ = SSD_COL_DT + SSD_G * LANE


def _ssd_w_in(w_in):
    base = SSD_DI + SSD_XBC
    w_dt = jnp.zeros((D_MODEL, SSD_G, LANE), F32)
    for g in range(SSD_G):
        cols = jnp.concatenate([w_in[:, base + g * SSD_E:base + (g + 1) * SSD_E],
                                w_in[:, base + SSD_H + g * SSD_E:base + SSD_H + (g + 1) * SSD_E]], axis=1)
        w_dt = w_dt.at[:, g, :2 * SSD_E].set(cols).at[:, g, 2 * SSD_E:4 * SSD_E].set(cols)
    return jnp.concatenate([w_in[:, :base], w_dt.reshape(D_MODEL, SSD_G * LANE)], axis=1)


def _ssd_consts():
    r = np.arange(SSD_BLK)
    same = (r[:, None] // SSD_CHUNK) == (r[None, :] // SSD_CHUNK)
    cum = np.concatenate([same & (r[None, :] <= r[:, None]), same & (r[None, :] >= r[:, None])], axis=0)
    lane = np.arange(LANE)[:, None]
    col = np.arange(SSD_GP)[None, :] // SSD_P
    exp_f = lane == col
    exp_b = lane == col + SSD_E
    return jnp.asarray(cum, BF16), jnp.asarray(exp_f, BF16), jnp.asarray(exp_b, BF16)


def _dot2_r(x, m):
    hi = x.astype(BF16)
    lo = (x - hi.astype(F32)).astype(BF16)
    return jnp.dot(hi, m, preferred_element_type=F32) + jnp.dot(lo, m, preferred_element_type=F32)


def _softplus(x):
    return jnp.maximum(x, 0.0) + jnp.log1p(jnp.exp(-jnp.abs(x)))


def _ssd_kernel(*refs, has_s0, want_state):
    it = iter(refs)
    (z_ref, x_ref, b_ref, c_ref, dt_ref, cwx_ref, cwb_ref, cwc_ref, cbx_ref, cbb_ref, cbc_ref,
     dtb_ref, alog_ref, dsk_ref, cum_ref, ef_ref, eb_ref) = (next(it) for _ in range(17))
    s0_ref = next(it) if has_s0 else None
    if has_s0:
        next(it)
    o_ref = next(it)
    st_ref = next(it) if want_state else None
    xs_s, xb_s, bm_s, cm_s, cu_s, dt_s, y_s, sf_s, sb_s = it

    T = x_ref.shape[0]
    nch = T // SSD_CHUNK
    L = SSD_CHUNK

    trow = lax.broadcasted_iota(jnp.int32, (T, 1), 0)

    def conv_silu(v_ref, w_ref, bias_ref):
        v = v_ref[...]
        prev = jnp.where(trow == 0, 0.0, pltpu.roll(v, 1, axis=0))
        nxt = jnp.where(trow == T - 1, 0.0, pltpu.roll(v, T - 1, axis=0))
        y = prev * w_ref[0:1, :] + v * w_ref[1:2, :] + nxt * w_ref[2:3, :] + bias_ref[...]
        return y * jax.nn.sigmoid(y)

    xs = conv_silu(x_ref, cwx_ref, cbx_ref)
    xs_s[...] = xs
    xb_s[...] = xs.astype(BF16)
    bm_s[...] = conv_silu(b_ref, cwb_ref, cbb_ref).astype(BF16)
    cm_s[...] = conv_silu(c_ref, cwc_ref, cbc_ref).astype(BF16)

    lane1 = lax.broadcasted_iota(jnp.int32, (1, LANE), 1)
    a_row = jnp.where(lane1 < 2 * SSD_E, -jnp.exp(alog_ref[...]), 0.0)
    for blk in range(T // SSD_BLK):
        rows = slice(blk * SSD_BLK, (blk + 1) * SSD_BLK)
        dt = _softplus(dt_ref[rows, :] + dtb_ref[...])
        c2 = _dot3(cum_ref[...], dt * a_row)
        cu_s[rows, :] = jnp.where(lane1 < SSD_E, c2[:SSD_BLK], c2[SSD_BLK:])
        dt_s[rows, :] = dt

    ii = lax.broadcasted_iota(jnp.int32, (L, LANE), 0)
    jj = lax.broadcasted_iota(jnp.int32, (L, LANE), 1)
    fwd_half = jj < L
    fwd_half1 = lane1 < L
    tri = (fwd_half & (ii >= jj)) | ((jj >= L) & (ii <= jj - L))
    left = lane1 < SSD_P
    for c in range(nch):
        rows = slice(c * L, (c + 1) * L)
        cum_c, dt_c = cu_s[rows, :], dt_s[rows, :]
        bm_c, cm_c = bm_s[rows, :], cm_s[rows, :]
        cb2 = _dot_nt(cm_c, jnp.concatenate([bm_c, bm_c], axis=0))
        arr = jnp.where(lane1 < 2 * SSD_E, cum_c, dt_c)
        arr_t = jnp.concatenate([arr, arr], axis=0).T
        gs = []
        for e in range(SSD_E):
            row_c = jnp.where(fwd_half1, arr_t[e:e + 1, :], arr_t[SSD_E + e:SSD_E + e + 1, :])
            row_dt = jnp.where(fwd_half1, arr_t[2 * SSD_E + e:2 * SSD_E + e + 1, :],
                               arr_t[3 * SSD_E + e:3 * SSD_E + e + 1, :])
            col_c = jnp.where(fwd_half, jnp.broadcast_to(cum_c[:, e:e + 1], (L, LANE)),
                              jnp.broadcast_to(cum_c[:, SSD_E + e:SSD_E + e + 1], (L, LANE)))
            dec = jnp.exp(jnp.where(tri, col_c - row_c, -jnp.inf))
            gs.append((cb2 * dec * row_dt).astype(BF16))
        for pr in range(SSD_E // 2):
            cols = slice(pr * LANE, (pr + 1) * LANE)
            xp = xb_s[rows, cols]
            xl = jnp.where(left, xp, jnp.zeros_like(xp))
            xr = jnp.where(left, jnp.zeros_like(xp), xp)
            lhs = jnp.concatenate([gs[2 * pr], gs[2 * pr + 1]], axis=1)
            rhs = jnp.concatenate([xl, xl, xr, xr], axis=0)
            y_s[rows, cols] = (jnp.dot(lhs, rhs, preferred_element_type=F32)
                               + dsk_ref[:, cols] * xs_s[rows, cols])

    if has_s0:
        sf_s[...] = s0_ref[0].T
        sb_s[...] = s0_ref[1].T
    else:
        sf_s[...] = jnp.zeros_like(sf_s)
        sb_s[...] = jnp.zeros_like(sb_s)

    def chain(c, last, lane0, exp_ref, st_s):
        rows = slice(c * L, (c + 1) * L)
        cum_c = jnp.where((lane1 >= lane0) & (lane1 < lane0 + SSD_E), cu_s[rows, :], 0.0)
        tot = cum_c[last:last + 1, :]
        qw = jnp.concatenate([jnp.exp(cum_c), jnp.exp(tot - cum_c) * dt_s[rows, :]], axis=0)
        qw = _dot2_r(qw, exp_ref[...])
        qd, w = qw[:L], qw[L:]
        s = st_s[...]
        y_s[rows, :] += jnp.dot(cm_s[rows, :], s.astype(BF16), preferred_element_type=F32) * qd
        xw = (xs_s[rows, :] * w).astype(BF16)
        u = lax.dot_general(bm_s[rows, :], xw, (((0,), (0,)), ((), ())), preferred_element_type=F32)
        st_s[...] = s * qd[last:last + 1, :] + u

    for n in range(nch):
        chain(n, L - 1, 0, ef_ref, sf_s)
        chain(nch - 1 - n, 0, SSD_E, eb_ref, sb_s)

    z = z_ref[...]
    o_ref[...] = y_s[...] * (z * jax.nn.sigmoid(z))
    if want_state:
        st_ref[0] = sf_s[...].T
        st_ref[1] = sb_s[...].T


def _ssd_call(y, prm, T, n_seq, row0, s0, o_prev=None):
    has_s0 = s0 is not None
    want_state = not has_s0
    rb = row0 // T
    col = lambda w, off: pl.BlockSpec((T, w), lambda b, g: (rb + b, off // w + g))
    cst = lambda shape: pl.BlockSpec(shape, lambda b, g: (0,) * len(shape))
    wcol = lambda rows, w, off: pl.BlockSpec((rows, w), lambda b, g: (0, off // w + g))
    per_g = lambda w: pl.BlockSpec((None, 1, w), lambda b, g: (g, 0, 0))
    cum, exp_f, exp_b = _ssd_consts()
    xoff, boff, coff = 0, SSD_DI, SSD_DI + SSD_G * SSD_N
    in_specs = [col(SSD_GP, 0), col(SSD_GP, SSD_COL_X), col(SSD_N, SSD_COL_B), col(SSD_N, SSD_COL_C),
                col(LANE, SSD_COL_DT),
                wcol(SSD_CONV, SSD_GP, xoff), wcol(SSD_CONV, SSD_N, boff), wcol(SSD_CONV, SSD_N, coff),
                wcol(1, SSD_GP, xoff), wcol(1, SSD_N, boff), wcol(1, SSD_N, coff),
                per_g(LANE), per_g(LANE), per_g(SSD_GP),
                cst(cum.shape), cst(exp_f.shape), cst(exp_b.shape)]
    args = [y, y, y, y, y, prm["conv_w"], prm["conv_w"], prm["conv_w"], prm["conv_b"], prm["conv_b"], prm["conv_b"],
            prm["dt_bias"], prm["a_log"], prm["d_skip"], cum, exp_f, exp_b]
    st_spec = pl.BlockSpec((None, 2, None, SSD_GP, SSD_N), lambda b, g: (b, 0, g, 0, 0))
    if has_s0:
        in_specs += [st_spec, pl.BlockSpec(memory_space=pl.ANY)]
        args += [s0, o_prev]
    aliases = {len(args) - 1: 0} if has_s0 else {}
    out_specs = [pl.BlockSpec((T, SSD_GP), lambda b, g: (rb + b, g))]
    out_shape = [jax.ShapeDtypeStruct((N_TOK, SSD_DI), F32)]
    if want_state:
        out_specs.append(st_spec)
        out_shape.append(jax.ShapeDtypeStruct((n_seq, 2, SSD_G, SSD_GP, SSD_N), F32))
    scratch = [pltpu.VMEM((T, SSD_GP), F32), pltpu.VMEM((T, SSD_GP), BF16),
               pltpu.VMEM((T, SSD_N), BF16), pltpu.VMEM((T, SSD_N), BF16),
               pltpu.VMEM((T, LANE), F32), pltpu.VMEM((T, LANE), F32), pltpu.VMEM((T, SSD_GP), F32),
               pltpu.VMEM((SSD_N, SSD_GP), F32), pltpu.VMEM((SSD_N, SSD_GP), F32)]
    outs = pl.pallas_call(
        functools.partial(_ssd_kernel, has_s0=has_s0, want_state=want_state),
        grid=(n_seq, SSD_G),
        in_specs=in_specs, out_specs=out_specs, out_shape=out_shape,
        scratch_shapes=scratch,
        input_output_aliases=aliases,
        compiler_params=pltpu.CompilerParams(dimension_semantics=("arbitrary", "arbitrary"),
                                             vmem_limit_bytes=VMEM_LIMIT),
        name="ssd_state" if has_s0 else "ssd",
    )(*args)
    return (outs[0], outs[1]) if want_state else (outs[0], None)


def _ssd_mixer(y, state, conv_w, conv_b, dt_bias, a_log, d_skip):
    def lanes(p):
        pg = p.reshape(2, SSD_G, SSD_E).transpose(1, 0, 2).reshape(SSD_G, 2 * SSD_E)
        return jnp.pad(jnp.concatenate([pg, pg], axis=1), ((0, 0), (0, LANE - 4 * SSD_E))).reshape(SSD_G, 1, LANE)
    prm = dict(conv_w=conv_w, conv_b=conv_b.reshape(1, SSD_XBC), dt_bias=lanes(dt_bias), a_log=lanes(a_log),
               d_skip=jnp.repeat(d_skip, SSD_P).reshape(SSD_G, 1, SSD_GP))
    op, st = _ssd_call(y, prm, SEQ, BATCH, 0, None)
    s0 = state.reshape(DEC_BATCH, 2, SSD_G, SSD_GP, SSD_N)
    o, _ = _ssd_call(y, prm, DEC_SEQ, DEC_BATCH, N_PROMPT, s0, op)
    return o, st.reshape(BATCH, 2, SSD_H, SSD_P, SSD_N)


def _pad_cols(w, mult):
    n = w.shape[1]
    n_pad = -n % mult
    return jnp.pad(w, ((0, 0), (0, n_pad))) if n_pad else w


def kernel(x_prompt, x_sample, state_gla, cache_nat_k, cache_nat_v, state_ssd, c,
           c_ctx, norm_g, w_ada, b_ada, w_ffn_in, w_ffn_out,
           gla_w_in, gla_w_a1, gla_w_a2, gla_b_a, gla_norm_g, gla_w_out,
           nat_w_qkv, nat_rpb, nat_w_out,
           gm_w_in, gm_ln_g, gm_ln_b, gm_w_s, gm_b_s, gm_w_out,
           ssd_w_in, ssd_conv_w, ssd_conv_b, ssd_dt_bias, ssd_a_log, ssd_d, ssd_norm_g, ssd_w_out,
           final_g):
    x = _join_streams(x_prompt, x_sample)
    mod = _modulation_all(c, c_ctx, w_ada, b_ada)
    new_gla, new_k, new_v, new_ssd = [], [], [], []
    for l in range(DEPTH):
        kind, j = l % N_MIXERS, l // N_MIXERS
        x = _ffn(x, mod[l], norm_g[l, 0], w_ffn_in[l, 0], w_ffn_out[l, 0], 0)
        if kind == 0:
            w = jnp.concatenate([gla_w_in[j], gla_w_a1[j, 0], gla_w_a1[j, 1]], axis=1)
            y = _proj_in(x, mod[l], norm_g[l, 1], _pad_cols(w, 640), 640)
            o, st = _gla_mixer(y, state_gla[:, j], gla_w_a2[j], gla_b_a[j], gla_norm_g[j])
            new_gla.append(st)
            x = _proj_out(x, o, mod[l], gla_w_out[j])
        elif kind == 1:
            y = _proj_in(x, mod[l], norm_g[l, 1], nat_w_qkv[j], 768)
            o, kc, vc = _nat_context(y)
            o = _nat_latent(y, o, _heads_last(cache_nat_k[:, j]), _heads_last(cache_nat_v[:, j]), nat_rpb[j])
            new_k.append(kc)
            new_v.append(vc)
            x = _proj_out(x, o, mod[l], nat_w_out[j])
        elif kind == 2:
            y = _proj_in(x, mod[l], norm_g[l, 1], gm_w_in[j], 512, act="gelu")
            x = _gmlp_out(x, y, mod[l], gm_ln_g[j], gm_ln_b[j], gm_w_s[j], gm_b_s[j], gm_w_out[j])
        else:
            y = _proj_in(x, mod[l], norm_g[l, 1], _ssd_w_in(ssd_w_in[j]), 512)
            o, st = _ssd_mixer(y, state_ssd[:, j], ssd_conv_w[j], ssd_conv_b[j], ssd_dt_bias[j],
                               ssd_a_log[j], ssd_d[j])
            new_ssd.append(st)
            x = _proj_out(x, o, mod[l], ssd_w_out[j], norm_g=ssd_norm_g[j])
        x = _ffn(x, mod[l], norm_g[l, 2], w_ffn_in[l, 1], w_ffn_out[l, 1], 2)
    yn = _final_norm(x, final_g)
    y_prompt, y_sample = _split_streams(yn)
    return (y_prompt, y_sample, jnp.stack(new_gla, axis=1), jnp.stack(new_k, axis=1),
            jnp.stack(new_v, axis=1), jnp.stack(new_ssd, axis=1))
```

```python
import functools
import math

import jax
import jax.numpy as jnp
import numpy as np
from jax import lax
from jax.experimental import pallas as pl
from jax.experimental.pallas import tpu as pltpu

D_MODEL = 1024
BATCH = 32
SEQ = 256
DEPTH = 4
DEC_BATCH = 2
DEC_SEQ = 1024
N_PROMPT = BATCH * SEQ
N_SAMPLE = DEC_BATCH * DEC_SEQ
N_TOK = N_PROMPT + N_SAMPLE
N_GROUPS = 1 + DEC_BATCH

GRID_W = 64
N_MIXERS = 4
N_SUB = 3
N_MOD = 3 * N_SUB
D_FF = 2816
EPS = 1e-6
NEG_INF = -1e30
ROPE_THETA = 10000.0
GLA_H, GLA_DK, GLA_DV, GLA_RANK, GLA_TAU, GLA_CHUNK = 4, 128, 256, 16, 16.0, 16
GLA_IN = 2 * GLA_H * GLA_DK + 2 * GLA_H * GLA_DV
NAT_H, NAT_HD, NAT_WH, NAT_WW, NAT_QB, NAT_KB = 16, 64, 8, 16, 16, 32
GM_DH, GM_G, GM_CHUNK = 1024, 8, 128
GM_CG = GM_DH // GM_G
SSD_DI = 2 * D_MODEL
SSD_P = 64
SSD_H = SSD_DI // SSD_P
SSD_N, SSD_G, SSD_CONV, SSD_CHUNK = 128, 4, 3, 64
SSD_XBC = SSD_DI + 2 * SSD_G * SSD_N
SSD_IN = SSD_DI + SSD_XBC + 2 * SSD_H

LANE = 128
VMEM_LIMIT = 56 * 1024 * 1024
BF16 = jnp.bfloat16
F32 = jnp.float32

FF_CHUNK = 256
N_FF_CHUNKS = D_FF // FF_CHUNK
TM_FFN = 512
TM_PROJ = 512
ADA_TK = 128


def _group_of_tile(i, tm):
    n_prompt_tiles = N_PROMPT // tm
    return jnp.where(i < n_prompt_tiles, 0, 1 + (i - n_prompt_tiles) // (DEC_SEQ // tm))


def _resident(shape):
    nd = len(shape)
    return pl.BlockSpec(shape, lambda i: (0,) * nd, pipeline_mode=pl.Buffered(1))


def _premod(x, g, mod_ref, k):
    shift = mod_ref[3 * k:3 * k + 1, :]
    scale = mod_ref[3 * k + 1:3 * k + 2, :]
    gate = mod_ref[3 * k + 2:3 * k + 3, :]
    ms = jnp.mean(x * x, axis=-1, keepdims=True)
    h = x * lax.rsqrt(ms + EPS) * g
    return h * (1.0 + scale) + shift, gate


def _ada_kernel(cond_ref, wa_ref, wb_ref, b_ref, o_ref):
    cnd = cond_ref[...]
    s = (cnd * jax.nn.sigmoid(cnd)).astype(BF16)
    p = jnp.concatenate([jnp.dot(s, w_ref[...].astype(BF16), preferred_element_type=F32)
                         for w_ref in (wa_ref, wb_ref)], axis=1)

    @pl.when(pl.program_id(1) == 0)
    def _():
        o_ref[...] = p + b_ref[...]

    @pl.when(pl.program_id(1) > 0)
    def _():
        o_ref[...] += p


def _modulation_all(c, c_ctx, w_ada, b_ada):
    rows = 8
    nk = D_MODEL // ADA_TK
    cond = jnp.concatenate([c_ctx[None], c, jnp.zeros((rows - N_GROUPS, D_MODEL), F32)], axis=0)
    cond = cond.reshape(rows, nk, ADA_TK).transpose(1, 0, 2)
    n_out = N_MOD * D_MODEL
    out = pl.pallas_call(
        _ada_kernel,
        grid=(DEPTH, nk),
        in_specs=[
            pl.BlockSpec((None, rows, ADA_TK), lambda l, k: (k, 0, 0)),
            pl.BlockSpec((None, ADA_TK, n_out // 2), lambda l, k: (l, k, 0)),
            pl.BlockSpec((None, ADA_TK, n_out // 2), lambda l, k: (l, k, 1)),
            pl.BlockSpec((None, 1, n_out), lambda l, k: (l, 0, 0)),
        ],
        out_specs=pl.BlockSpec((None, rows, n_out), lambda l, k: (l, 0, 0)),
        out_shape=jax.ShapeDtypeStruct((DEPTH, rows, n_out), F32),
        compiler_params=pltpu.CompilerParams(dimension_semantics=("arbitrary", "arbitrary")),
        name="ada_modulation",
    )(cond, w_ada, w_ada, b_ada.reshape(DEPTH, 1, n_out))
    return out[:, :N_GROUPS].reshape(DEPTH, N_GROUPS, N_MOD, D_MODEL)


def _ffn_kernel(*refs, k, split_in, mix, mix_norm, final):
    it = iter(refs)
    x_refs = (next(it), next(it)) if split_in else (next(it),)
    mod_ref, g_ref = next(it), next(it)
    o_in_ref, ng_ref, wmix_ref = (next(it), next(it), next(it)) if mix else (None, None, None)
    win_ref, wout_ref = next(it), next(it)
    fg_ref = next(it) if final else None
    out_refs = (next(it), next(it)) if final else (next(it),)
    acc_ref = next(it)
    i = pl.program_id(0)
    n_prompt_tiles = N_PROMPT // x_refs[0].shape[0]

    x = jnp.where(i < n_prompt_tiles, x_refs[0][...], x_refs[1][...]) if split_in else x_refs[0][...]
    if mix:
        o_in = o_in_ref[...]
        if mix_norm:
            o_in = o_in * lax.rsqrt(jnp.mean(o_in * o_in, axis=-1, keepdims=True) + EPS) * ng_ref[...]
        x = x + mod_ref[5:6, :] * jnp.dot(o_in.astype(BF16), wmix_ref[...], preferred_element_type=F32)
    h, gate = _premod(x, g_ref[...], mod_ref, k)
    hb = h.astype(BF16)
    for j in range(N_FF_CHUNKS):
        a = jnp.dot(hb, win_ref[:, j * FF_CHUNK:(j + 1) * FF_CHUNK], preferred_element_type=F32)
        u = jnp.dot(hb, win_ref[:, D_FF + j * FF_CHUNK:D_FF + (j + 1) * FF_CHUNK], preferred_element_type=F32)
        t = (a * jax.nn.sigmoid(a) * u).astype(BF16)
        p = jnp.dot(t, wout_ref[j * FF_CHUNK:(j + 1) * FF_CHUNK, :], preferred_element_type=F32)
        if j == 0:
            acc_ref[...] = p
        else:
            acc_ref[...] += p
    y = x + 0.5 * gate * acc_ref[...]
    if not final:
        out_refs[0][...] = y
    else:
        y = y * lax.rsqrt(jnp.mean(y * y, axis=-1, keepdims=True) + EPS) * fg_ref[...]

        @pl.when(i < n_prompt_tiles)
        def _():
            out_refs[0][...] = y

        @pl.when(i >= n_prompt_tiles)
        def _():
            out_refs[1][...] = y


def _ffn(x, mod_l, g, w_in, w_out, k, mix=None, final_g=None):
    tm = TM_FFN
    split_in = isinstance(x, tuple)
    npt = N_PROMPT // tm
    row = pl.BlockSpec((tm, D_MODEL), lambda i: (i, 0))
    prompt_rows = pl.BlockSpec((tm, D_MODEL), lambda i: (jnp.minimum(i, npt - 1), 0))
    latent_rows = pl.BlockSpec((tm, D_MODEL), lambda i: (jnp.maximum(i - npt, 0), 0))
    vec = lambda n: pl.BlockSpec((1, n), lambda i: (0, 0))
    in_specs = [prompt_rows, latent_rows] if split_in else [row]
    args = list(x) if split_in else [x]
    in_specs += [pl.BlockSpec((None, N_MOD, D_MODEL), lambda i: (_group_of_tile(i, tm), 0, 0)), vec(D_MODEL)]
    args += [mod_l, g.reshape(1, D_MODEL)]
    if mix is not None:
        o_in, w_mix, norm_g = mix
        kdim = w_mix.shape[0]
        ng = jnp.ones((1, kdim), F32) if norm_g is None else norm_g.reshape(1, kdim)
        in_specs += [pl.BlockSpec((tm, kdim), lambda i: (i, 0)), vec(kdim), _resident((kdim, D_MODEL))]
        args += [o_in, ng, w_mix.astype(BF16)]
    in_specs += [_resident((D_MODEL, 2 * D_FF)), _resident((D_FF, D_MODEL))]
    args += [w_in.astype(BF16), w_out.astype(BF16)]
    if final_g is not None:
        in_specs.append(vec(D_MODEL))
        args.append(final_g.reshape(1, D_MODEL))
        out_specs = [prompt_rows, latent_rows]
        out_shape = [jax.ShapeDtypeStruct((N_PROMPT, D_MODEL), F32), jax.ShapeDtypeStruct((N_SAMPLE, D_MODEL), F32)]
    else:
        out_specs, out_shape = row, jax.ShapeDtypeStruct((N_TOK, D_MODEL), F32)
    return pl.pallas_call(
        functools.partial(_ffn_kernel, k=k, split_in=split_in, mix=mix is not None,
                          mix_norm=mix is not None and mix[2] is not None, final=final_g is not None),
        grid=(N_TOK // tm,),
        in_specs=in_specs, out_specs=out_specs, out_shape=out_shape,
        scratch_shapes=[pltpu.VMEM((tm, D_MODEL), F32)],
        compiler_params=pltpu.CompilerParams(dimension_semantics=("arbitrary",),
                                             vmem_limit_bytes=VMEM_LIMIT),
        name="ffn_swiglu",
    )(*args)


def _proj_in_kernel(x_ref, mod_ref, g_ref, w_ref, o_ref, *, tn, act):
    h, _ = _premod(x_ref[...], g_ref[...], mod_ref, 1)
    hb = h.astype(BF16)
    for j in range(w_ref.shape[1] // tn):
        y = jnp.dot(hb, w_ref[:, j * tn:(j + 1) * tn], preferred_element_type=F32)
        if act == "gelu":
            y = jax.nn.gelu(y)
        o_ref[:, j * tn:(j + 1) * tn] = y


def _proj_in(x, mod_l, g, w, tn, act=None):
    tm = TM_PROJ
    n_out = w.shape[1]
    return pl.pallas_call(
        functools.partial(_proj_in_kernel, tn=tn, act=act),
        grid=(N_TOK // tm,),
        in_specs=[
            pl.BlockSpec((tm, D_MODEL), lambda i: (i, 0)),
            pl.BlockSpec((None, N_MOD, D_MODEL), lambda i: (_group_of_tile(i, tm), 0, 0)),
            pl.BlockSpec((1, D_MODEL), lambda i: (0, 0)),
            _resident((D_MODEL, n_out)),
        ],
        out_specs=pl.BlockSpec((tm, n_out), lambda i: (i, 0)),
        out_shape=jax.ShapeDtypeStruct((N_TOK, n_out), F32),
        compiler_params=pltpu.CompilerParams(dimension_semantics=("arbitrary",),
                                             vmem_limit_bytes=VMEM_LIMIT),
        name="mixer_proj_in",
    )(x, mod_l, g.reshape(1, D_MODEL), w.astype(BF16))


def _proj_out_kernel(x_ref, o_in_ref, mod_ref, ng_ref, w_ref, o_ref, *, norm):
    gate = mod_ref[5:6, :]
    o_in = o_in_ref[...]
    if norm:
        o_in = o_in * lax.rsqrt(jnp.mean(o_in * o_in, axis=-1, keepdims=True) + EPS) * ng_ref[...]
    y = jnp.dot(o_in.astype(BF16), w_ref[...], preferred_element_type=F32)
    o_ref[...] = x_ref[...] + gate * y


def _proj_out(x, o_in, mod_l, w, norm_g=None):
    tm = TM_PROJ
    kdim = w.shape[0]
    ng = jnp.ones((1, kdim), F32) if norm_g is None else norm_g.reshape(1, kdim)
    return pl.pallas_call(
        functools.partial(_proj_out_kernel, norm=norm_g is not None),
        grid=(N_TOK // tm,),
        in_specs=[
            pl.BlockSpec((tm, D_MODEL), lambda i: (i, 0)),
            pl.BlockSpec((tm, kdim), lambda i: (i, 0)),
            pl.BlockSpec((None, N_MOD, D_MODEL), lambda i: (_group_of_tile(i, tm), 0, 0)),
            pl.BlockSpec((1, kdim), lambda i: (0, 0)),
            _resident((kdim, D_MODEL)),
        ],
        out_specs=pl.BlockSpec((tm, D_MODEL), lambda i: (i, 0)),
        out_shape=jax.ShapeDtypeStruct((N_TOK, D_MODEL), F32),
        compiler_params=pltpu.CompilerParams(dimension_semantics=("arbitrary",),
                                             vmem_limit_bytes=VMEM_LIMIT),
        name="mixer_proj_out",
    )(x, o_in, mod_l, ng, w.astype(BF16))


def _final_norm_kernel(x_ref, g_ref, o_ref):
    x = x_ref[...]
    ms = jnp.mean(x * x, axis=-1, keepdims=True)
    o_ref[...] = x * lax.rsqrt(ms + EPS) * g_ref[...]


def _final_norm(x, g):
    tm = 1024
    return pl.pallas_call(
        _final_norm_kernel,
        grid=(N_TOK // tm,),
        in_specs=[pl.BlockSpec((tm, D_MODEL), lambda i: (i, 0)),
                  pl.BlockSpec((1, D_MODEL), lambda i: (0, 0))],
        out_specs=pl.BlockSpec((tm, D_MODEL), lambda i: (i, 0)),
        out_shape=jax.ShapeDtypeStruct((N_TOK, D_MODEL), F32),
        name="final_rmsnorm",
    )(x, g.reshape(1, D_MODEL))


def _split_streams(y):
    return y[:N_PROMPT].reshape(BATCH, SEQ, -1), y[N_PROMPT:].reshape(DEC_BATCH, DEC_SEQ, -1)


def _join_streams(yp, ys):
    return jnp.concatenate([yp.reshape(N_PROMPT, -1), ys.reshape(N_SAMPLE, -1)], axis=0)


def _rmsnorm(x, g):
    return x * lax.rsqrt(jnp.mean(x * x, axis=-1, keepdims=True) + EPS) * g


def _flip_t(t):
    return jnp.flip(t, axis=1)


def _rope_2d(x):
    T, dh = x.shape[1], x.shape[-1]
    half = dh // 2
    t = jnp.arange(T)
    inv = ROPE_THETA ** (-jnp.arange(0, half, 2, dtype=F32) / half)

    def rot(xa, pos):
        ang = pos.astype(F32)[:, None] * inv
        cos = jnp.cos(ang)[None, :, None, :]
        sin = jnp.sin(ang)[None, :, None, :]
        x1, x2 = jnp.split(xa, 2, axis=-1)
        return jnp.concatenate([x1 * cos - x2 * sin, x1 * sin + x2 * cos], axis=-1)

    return jnp.concatenate([rot(x[..., :half], t // GRID_W), rot(x[..., half:], t % GRID_W)], axis=-1)


def _gla_chunked(q, k, v, log_a, s0):
    B_, T, H, _ = q.shape
    dv = v.shape[-1]
    C = GLA_CHUNK
    n = T // C

    def blk(t):
        return t.reshape(B_, n, C, H, -1).transpose(0, 1, 3, 2, 4)

    q, k, v, la = blk(q), blk(k), blk(v), blk(log_a)
    b = jnp.cumsum(la, axis=3)
    tril = jnp.tril(jnp.ones((C, C), dtype=bool))
    diff = b[:, :, :, :, None, :] - b[:, :, :, None, :, :]
    decay = jnp.exp(jnp.where(tril[:, :, None], diff, -jnp.inf))
    attn = jnp.einsum('bnhid,bnhjd,bnhijd->bnhij', q, k, decay)
    o_intra = jnp.einsum('bnhij,bnhjv->bnhiv', attn, v)
    b_last = b[:, :, :, -1:, :]
    q_dec = q * jnp.exp(b)
    u = jnp.einsum('bnhjd,bnhjv->bnhdv', k * jnp.exp(b_last - b), v)
    g = jnp.exp(b_last[:, :, :, 0])

    def step(s, xs):
        qd, un, gn = xs
        o = jnp.einsum('bhid,bhdv->bhiv', qd, s)
        return gn[..., None] * s + un, o

    s_fin, o_inter = lax.scan(step, s0,
                              (jnp.swapaxes(q_dec, 0, 1), jnp.swapaxes(u, 0, 1), jnp.swapaxes(g, 0, 1)))
    o = o_intra + jnp.swapaxes(o_inter, 0, 1)
    return o.transpose(0, 1, 3, 2, 4).reshape(B_, T, H, dv), s_fin


def _gla_core(y, s0, w_a2, b_a, norm_g, use_rope):
    B_, T, _ = y.shape
    nq = GLA_H * GLA_DK
    nv = GLA_H * GLA_DV
    q = y[..., :nq].reshape(B_, T, GLA_H, GLA_DK) * (GLA_DK ** -0.5)
    k = y[..., nq:2 * nq].reshape(B_, T, GLA_H, GLA_DK)
    v = y[..., 2 * nq:2 * nq + nv].reshape(B_, T, GLA_H, GLA_DV)
    r = y[..., 2 * nq + nv:2 * nq + 2 * nv]
    za = y[..., GLA_IN:GLA_IN + 2 * GLA_RANK].reshape(B_, T, 2, GLA_RANK)
    if use_rope:
        q, k = _rope_2d(q), _rope_2d(k)
    z = jnp.einsum('bter,erk->btek', za, w_a2) + b_a
    log_a = (jax.nn.log_sigmoid(z) / GLA_TAU).reshape(B_, T, 2, GLA_H, GLA_DK)
    o_f, s_f = _gla_chunked(q, k, v, log_a[:, :, 0], s0[:, 0])
    o_b, s_b = _gla_chunked(_flip_t(q), _flip_t(k), _flip_t(v), _flip_t(log_a[:, :, 1]), s0[:, 1])
    o = o_f + _flip_t(o_b)
    o = _rmsnorm(o, norm_g.reshape(GLA_H, GLA_DV))
    o = o.reshape(B_, T, nv) * jax.nn.silu(r)
    return o, jnp.stack([s_f, s_b], axis=1)


def _nat_tables(rows):
    wh = min(NAT_WH, rows)
    r = np.arange(rows)
    row_idx = np.clip(r - wh // 2, 0, rows - wh)[:, None] + np.arange(wh)
    ncb = GRID_W // NAT_QB
    col_idx = np.clip(np.arange(ncb) * NAT_QB - (NAT_KB - NAT_QB) // 2, 0,
                      GRID_W - NAT_KB)[:, None] + np.arange(NAT_KB)
    qcol = np.arange(ncb)[:, None] * NAT_QB + np.arange(NAT_QB)
    c_start = np.clip(qcol - NAT_WW // 2, 0, GRID_W - NAT_WW)
    kc = col_idx[:, None, :]
    col_ok = (kc >= c_start[..., None]) & (kc < c_start[..., None] + NAT_WW)
    dc = kc - qcol[..., None]
    dr = row_idx - r[:, None]
    full = (rows, ncb, NAT_QB, wh, NAT_KB)
    flat = (rows, ncb, NAT_QB, wh * NAT_KB)
    dr_i = np.broadcast_to(dr[:, None, None, :, None] + NAT_WH - 1, full).reshape(flat)
    dc_i = np.broadcast_to(np.clip(dc + NAT_WW - 1, 0, 2 * NAT_WW - 2)[None, :, :, None, :], full).reshape(flat)
    ok = np.broadcast_to(col_ok[None, :, :, None, :], full).reshape(flat)
    return row_idx, col_idx, dr_i, dc_i, ok


def _nat_context_core(y):
    B_, S, _ = y.shape
    q, k, v = jnp.split(y, 3, axis=-1)
    q = q.reshape(B_, S, NAT_H, NAT_HD)
    k = k.reshape(B_, S, NAT_H, NAT_HD)
    v = v.reshape(B_, S, NAT_H, NAT_HD)
    s = jnp.einsum('bqhd,bkhd->bhqk', q, k) * (NAT_HD ** -0.5)
    p = jax.nn.softmax(s, axis=-1)
    o = jnp.einsum('bhqk,bkhd->bqhd', p, v).reshape(B_, S, D_MODEL)
    return o, k.transpose(0, 2, 1, 3), v.transpose(0, 2, 1, 3)


def _nat_latent_core(y, ck, cv, rpb):
    B_, T, _ = y.shape
    rows = T // GRID_W
    ncb = GRID_W // NAT_QB
    row_idx, col_idx, dr_i, dc_i, ok = _nat_tables(rows)
    q, k, v = jnp.split(y, 3, axis=-1)
    q = q.reshape(B_, rows, ncb, NAT_QB, NAT_H, NAT_HD)
    k = k.reshape(B_, rows, GRID_W, NAT_H, NAT_HD)
    v = v.reshape(B_, rows, GRID_W, NAT_H, NAT_HD)
    ri = row_idx[:, None, :, None]
    ci = col_idx[None, :, None, :]
    kb = k[:, ri, ci].reshape(B_, rows, ncb, -1, NAT_H, NAT_HD)
    vb = v[:, ri, ci].reshape(B_, rows, ncb, -1, NAT_H, NAT_HD)
    nk = kb.shape[3]
    scale = NAT_HD ** -0.5
    s_lat = jnp.einsum('brnqhd,brnkhd->bhrnqk', q, kb) * scale
    s_lat = s_lat + rpb[:, dr_i, dc_i]
    s_lat = jnp.where(ok, s_lat, NEG_INF)
    s_ctx = jnp.einsum('brnqhd,bhsd->bhrnqs', q, ck) * scale
    p = jax.nn.softmax(jnp.concatenate([s_lat, s_ctx], axis=-1), axis=-1)
    o = (jnp.einsum('bhrnqk,brnkhd->brnqhd', p[..., :nk], vb)
         + jnp.einsum('bhrnqs,bhsd->brnqhd', p[..., nk:], cv))
    return o.reshape(B_, T, D_MODEL)


def _gmlp_core(y, ln_g, ln_b, w_s, b_s):
    B_, T, _ = y.shape
    u, v = jnp.split(y, 2, axis=-1)
    vc = v - jnp.mean(v, axis=-1, keepdims=True)
    v = vc * lax.rsqrt(jnp.mean(vc * vc, axis=-1, keepdims=True) + EPS) * ln_g + ln_b
    v = v.reshape(B_, T // GM_CHUNK, GM_CHUNK, GM_G, GM_CG)
    v = jnp.einsum('gpq,bnqgc->bnpgc', w_s, v) + b_s.T[None, None, :, :, None]
    return u * v.reshape(B_, T, GM_DH)


GLA_BLK = 128
GLA_CPB = GLA_BLK // GLA_CHUNK
GLA_ZCOL = GLA_IN // LANE


def _gla_consts():
    r = np.arange(GLA_BLK)
    same = (r[:, None] // GLA_CHUNK) == (r[None, :] // GLA_CHUNK)
    ri, ci = r[:, None] % GLA_CHUNK, r[None, :] % GLA_CHUNK
    lf = np.concatenate([same & (ci <= ri), same & (ci > ri)], axis=0)
    lb = np.concatenate([same & (ci >= ri), same & (ci < ri)], axis=0)
    rows = np.arange(GLA_CHUNK * GLA_DK)
    sel = (rows[:, None] // GLA_DK) == (np.arange(LANE)[None, :] % GLA_CHUNK)
    return (jnp.asarray(lf, BF16), jnp.asarray(lb, BF16), jnp.asarray(sel, BF16))


def _rope_tables(T):
    half = GLA_DK // 2
    t = np.arange(T)
    inv = ROPE_THETA ** (-np.arange(0, half, 2, dtype=np.float64) / half)
    lane = np.arange(GLA_DK)
    pos = np.where(lane[None, :] < half, (t // GRID_W)[:, None], (t % GRID_W)[:, None])
    ang = pos * inv[lane % (half // 2)][None, :]
    sign = np.where((lane % half) < half // 2, -1.0, 1.0)[None, :]
    return jnp.asarray(np.cos(ang), F32), jnp.asarray(np.sin(ang) * sign, F32)


def _rope_apply(x, cos, sin_signed):
    half = GLA_DK // 2
    lane = lax.broadcasted_iota(jnp.int32, (1, GLA_DK), 1)
    partner = jnp.where((lane % half) < half // 2,
                        pltpu.roll(x, GLA_DK - half // 2, axis=1), pltpu.roll(x, half // 2, axis=1))
    return x * cos + partner * sin_signed


def _split3(x):
    hi = x.astype(BF16)
    r1 = x - hi.astype(F32)
    mid = r1.astype(BF16)
    lo = (r1 - mid.astype(F32)).astype(BF16)
    return hi, mid, lo


def _dot3(m, x):
    hi, mid, lo = _split3(x)
    d = lambda p: jnp.dot(m, p, preferred_element_type=F32)
    return d(hi) + d(mid) + d(lo)


def _gla_kernel(*refs, use_rope, has_s0, want_state):
    it = iter(refs)
    q_ref, k_ref, v_ref, r_ref, za_ref, w2_ref, ba_ref, ng_ref, lf_ref, lb_ref, sel_ref = (next(it) for _ in range(11))
    cos_ref, sin_ref = (next(it), next(it)) if use_rope else (None, None)
    s0_ref = next(it) if has_s0 else None
    if has_s0:
        next(it)
    o_ref = next(it)
    st_ref = next(it) if want_state else None
    (q_s, k_s, bf_s, bb_s, ef_s, eb_s, qdf_s, kdf_s, qdb_s, kdb_s, v_s, w_s, o_s, sf_s, sb_s) = it

    T = q_ref.shape[0]
    nblk, nch = T // GLA_BLK, T // GLA_CHUNK

    for blk in range(nblk):
        rows = slice(blk * GLA_BLK, (blk + 1) * GLA_BLK)
        z = jnp.dot(za_ref[rows, :].astype(BF16), w2_ref[...], preferred_element_type=F32) + ba_ref[...]
        la = (jnp.minimum(z, 0.0) - jnp.log1p(jnp.exp(-jnp.abs(z)))) * (1.0 / GLA_TAU)
        cf = _dot3(lf_ref[...], la[:, :GLA_DK])
        cb = _dot3(lb_ref[...], la[:, GLA_DK:])
        bf, bb = cf[:GLA_BLK], cb[:GLA_BLK]
        ef, eb = jnp.exp(bf), jnp.exp(bb)
        q = q_ref[rows, :] * (GLA_DK ** -0.5)
        k = k_ref[rows, :]
        if use_rope:
            q = _rope_apply(q, cos_ref[rows, :], sin_ref[rows, :])
            k = _rope_apply(k, cos_ref[rows, :], sin_ref[rows, :])
        q_s[rows, :], k_s[rows, :] = q, k
        bf_s[rows, :], bb_s[rows, :] = bf, bb
        ef_s[rows, :], eb_s[rows, :] = ef, eb
        qdf_s[rows, :] = (q * ef).astype(BF16)
        qdb_s[rows, :] = (q * eb).astype(BF16)
        kdf_s[rows, :] = (k * jnp.exp(cf[GLA_BLK:])).astype(BF16)
        kdb_s[rows, :] = (k * jnp.exp(cb[GLA_BLK:])).astype(BF16)
        v_s[rows, :] = v_ref[rows, :].astype(BF16)

    irow = lax.broadcasted_iota(jnp.int32, (GLA_CHUNK, 1), 0)

    def intra_chunk(c, carry):
        rows = pl.ds(pl.multiple_of(c * GLA_CHUNK, GLA_CHUNK), GLA_CHUNK)
        qc, bfc, bbc = q_s[rows, :], bf_s[rows, :], bb_s[rows, :]
        for j in range(GLA_CHUNK):
            row = pl.ds(c * GLA_CHUNK + j, 1)
            e = (jnp.exp(jnp.where(irow >= j, bfc - bf_s[row, :], -jnp.inf))
                 + jnp.exp(jnp.where(irow <= j, bbc - bb_s[row, :], -jnp.inf)))
            w_s[rows, j * GLA_DK:(j + 1) * GLA_DK] = (qc * k_s[row, :] * e).astype(BF16)
        return carry

    lax.fori_loop(0, nch, intra_chunk, 0)

    lane_c = lax.broadcasted_iota(jnp.int32, (GLA_BLK, LANE), 1) // GLA_CHUNK
    row_c = lax.broadcasted_iota(jnp.int32, (GLA_BLK, LANE), 0) // GLA_CHUNK
    for blk in range(nblk):
        rows = slice(blk * GLA_BLK, (blk + 1) * GLA_BLK)
        a = jnp.dot(w_s[rows, :], sel_ref[...], preferred_element_type=F32)
        a = jnp.where(lane_c == row_c, a, 0.0).astype(BF16)
        o_s[rows, :] = jnp.dot(a, v_s[rows, :], preferred_element_type=F32)

    if has_s0:
        sf_s[...] = s0_ref[0].T
        sb_s[...] = s0_ref[1].T
    else:
        sf_s[...] = jnp.zeros_like(sf_s)
        sb_s[...] = jnp.zeros_like(sb_s)

    def chain(rows, g_row, qd_s, kd_s, e_s, st_s):
        s = st_s[...]
        o_s[rows, :] += _dot_nt(qd_s[rows, :], s.astype(BF16))
        u = lax.dot_general(v_s[rows, :], kd_s[rows, :], (((0,), (0,)), ((), ())), preferred_element_type=F32)
        st_s[...] = s * e_s[g_row, :] + u

    def inter_chunk(n, carry):
        cf_ = pl.multiple_of(n * GLA_CHUNK, GLA_CHUNK)
        cb_ = pl.multiple_of((nch - 1 - n) * GLA_CHUNK, GLA_CHUNK)
        chain(pl.ds(cf_, GLA_CHUNK), pl.ds(cf_ + GLA_CHUNK - 1, 1), qdf_s, kdf_s, ef_s, sf_s)
        chain(pl.ds(cb_, GLA_CHUNK), pl.ds(cb_, 1), qdb_s, kdb_s, eb_s, sb_s)
        return carry

    lax.fori_loop(0, nch, inter_chunk, 0, unroll=4)

    o = o_s[...]
    o = o * lax.rsqrt(jnp.mean(o * o, axis=-1, keepdims=True) + EPS) * ng_ref[...]
    r = r_ref[...]
    o_ref[...] = o * (r * jax.nn.sigmoid(r))
    if want_state:
        st_ref[0] = sf_s[...].T
        st_ref[1] = sb_s[...].T


def _gla_call(y, w2, ba, ng, T, n_seq, row0, use_rope, s0, o_prev=None):
    has_s0 = s0 is not None
    want_state = not has_s0
    rb = row0 // T
    nqb = GLA_H
    col = lambda w, off: pl.BlockSpec((T, w), lambda b, h: (rb + b, off + h))
    cst = lambda shape: pl.BlockSpec(shape, lambda b, h: (0,) * len(shape))
    per_head = lambda shape: pl.BlockSpec((None,) + shape, lambda b, h: (h,) + (0,) * len(shape))
    lf, lb, sel = _gla_consts()
    in_specs = [col(GLA_DK, 0), col(GLA_DK, nqb), col(GLA_DV, nqb), col(GLA_DV, nqb + GLA_H),
                pl.BlockSpec((T, LANE), lambda b, h: (rb + b, GLA_ZCOL)),
                per_head((LANE, 2 * GLA_DK)), per_head((1, 2 * GLA_DK)), per_head((1, GLA_DV)),
                cst(lf.shape), cst(lb.shape), cst(sel.shape)]
    args = [y, y, y, y, y, w2, ba, ng, lf, lb, sel]
    if use_rope:
        cos, sin = _rope_tables(T)
        in_specs += [cst(cos.shape), cst(sin.shape)]
        args += [cos, sin]
    if has_s0:
        in_specs += [pl.BlockSpec((None, 2, None, GLA_DK, GLA_DV), lambda b, h: (b, 0, h, 0, 0)),
                     pl.BlockSpec(memory_space=pl.ANY)]
        args += [s0, o_prev]
    aliases = {len(args) - 1: 0} if has_s0 else {}
    out_specs = [pl.BlockSpec((T, GLA_DV), lambda b, h: (rb + b, h))]
    out_shape = [jax.ShapeDtypeStruct((N_TOK, GLA_H * GLA_DV), F32)]
    if want_state:
        out_specs.append(pl.BlockSpec((None, 2, None, GLA_DK, GLA_DV), lambda b, h: (b, 0, h, 0, 0)))
        out_shape.append(jax.ShapeDtypeStruct((n_seq, 2, GLA_H, GLA_DK, GLA_DV), F32))
    f32s = lambda w: pltpu.VMEM((T, w), F32)
    bf16s = lambda w: pltpu.VMEM((T, w), BF16)
    scratch = ([f32s(GLA_DK)] * 6 + [bf16s(GLA_DK)] * 4
               + [bf16s(GLA_DV), bf16s(GLA_CHUNK * GLA_DK), f32s(GLA_DV),
                  pltpu.VMEM((GLA_DV, GLA_DK), F32), pltpu.VMEM((GLA_DV, GLA_DK), F32)])
    outs = pl.pallas_call(
        functools.partial(_gla_kernel, use_rope=use_rope, has_s0=has_s0, want_state=want_state),
        grid=(n_seq, GLA_H),
        in_specs=in_specs, out_specs=out_specs, out_shape=out_shape,
        scratch_shapes=scratch,
        input_output_aliases=aliases,
        compiler_params=pltpu.CompilerParams(dimension_semantics=("arbitrary", "arbitrary"),
                                             vmem_limit_bytes=VMEM_LIMIT),
        name="gla_rope" if use_rope else "gla",
    )(*args)
    return (outs[0], outs[1]) if want_state else (outs[0], None)


def _gla_seq_kernel(*refs, use_rope, has_s0, want_state):
    it = iter(refs)
    q_ref, k_ref, v_ref, r_ref, za_ref, w2_ref, ba_ref, ng_ref, lf_ref, lb_ref, sel_ref = (next(it) for _ in range(11))
    cos_ref, sin_ref = (next(it), next(it)) if use_rope else (None, None)
    s0_ref = next(it) if has_s0 else None
    if has_s0:
        next(it)
    o_ref = next(it)
    st_ref = next(it) if want_state else None
    q_s, k_s, bf_s, bb_s, qdf_s, kdf_s, qdb_s, kdb_s, w_s, sf_s, sb_s = it

    T = q_ref.shape[0]
    nblk, nch = T // GLA_BLK, T // GLA_CHUNK
    hq = lambda h: slice(h * GLA_DK, (h + 1) * GLA_DK)
    hv = lambda h: slice(h * GLA_DV, (h + 1) * GLA_DV)

    for blk in range(nblk):
        rows = slice(blk * GLA_BLK, (blk + 1) * GLA_BLK)
        zab = za_ref[rows, :].astype(BF16)
        for h in range(GLA_H):
            z = jnp.dot(zab, w2_ref[h], preferred_element_type=F32) + ba_ref[h]
            la = (jnp.minimum(z, 0.0) - jnp.log1p(jnp.exp(-jnp.abs(z)))) * (1.0 / GLA_TAU)
            cf = _dot3(lf_ref[...], la[:, :GLA_DK])
            cb = _dot3(lb_ref[...], la[:, GLA_DK:])
            bf, bb = cf[:GLA_BLK], cb[:GLA_BLK]
            q = q_ref[rows, hq(h)] * (GLA_DK ** -0.5)
            k = k_ref[rows, hq(h)]
            if use_rope:
                q = _rope_apply(q, cos_ref[rows, :], sin_ref[rows, :])
                k = _rope_apply(k, cos_ref[rows, :], sin_ref[rows, :])
            q_s[h, rows, :], k_s[h, rows, :] = q, k
            bf_s[h, rows, :], bb_s[h, rows, :] = bf, bb
            qdf_s[h, rows, :] = (q * jnp.exp(bf)).astype(BF16)
            qdb_s[h, rows, :] = (q * jnp.exp(bb)).astype(BF16)
            kdf_s[h, rows, :] = (k * jnp.exp(cf[GLA_BLK:])).astype(BF16)
            kdb_s[h, rows, :] = (k * jnp.exp(cb[GLA_BLK:])).astype(BF16)

    irow = lax.broadcasted_iota(jnp.int32, (GLA_CHUNK, 1), 0)
    lane_c = lax.broadcasted_iota(jnp.int32, (GLA_BLK, LANE), 1) // GLA_CHUNK
    row_c = lax.broadcasted_iota(jnp.int32, (GLA_BLK, LANE), 0) // GLA_CHUNK
    for h in range(GLA_H):
        def intra_chunk(c, carry, h=h):
            rows = pl.ds(pl.multiple_of(c * GLA_CHUNK, GLA_CHUNK), GLA_CHUNK)
            qc, bfc, bbc = q_s[h, rows, :], bf_s[h, rows, :], bb_s[h, rows, :]
            for j in range(GLA_CHUNK):
                row = pl.ds(c * GLA_CHUNK + j, 1)
                e = (jnp.exp(jnp.where(irow >= j, bfc - bf_s[h, row, :], -jnp.inf))
                     + jnp.exp(jnp.where(irow <= j, bbc - bb_s[h, row, :], -jnp.inf)))
                w_s[rows, j * GLA_DK:(j + 1) * GLA_DK] = (qc * k_s[h, row, :] * e).astype(BF16)
            return carry

        lax.fori_loop(0, nch, intra_chunk, 0)
        for blk in range(nblk):
            rows = slice(blk * GLA_BLK, (blk + 1) * GLA_BLK)
            a = jnp.dot(w_s[rows, :], sel_ref[...], preferred_element_type=F32)
            a = jnp.where(lane_c == row_c, a, 0.0).astype(BF16)
            o_ref[rows, hv(h)] = jnp.dot(a, v_ref[rows, hv(h)].astype(BF16), preferred_element_type=F32)

    for h in range(GLA_H):
        if has_s0:
            sf_s[h] = s0_ref[0, h].T
            sb_s[h] = s0_ref[1, h].T
        else:
            sf_s[h] = jnp.zeros((GLA_DV, GLA_DK), F32)
            sb_s[h] = jnp.zeros((GLA_DV, GLA_DK), F32)

    def chain(h, rows, g_row, qd_s, kd_s, b_s, st_s):
        s = st_s[h]
        o_ref[rows, hv(h)] += _dot_nt(qd_s[h, rows, :], s.astype(BF16))
        u = lax.dot_general(v_ref[rows, hv(h)].astype(BF16), kd_s[h, rows, :], (((0,), (0,)), ((), ())),
                            preferred_element_type=F32)
        st_s[h] = s * jnp.exp(b_s[h, g_row, :]) + u

    def inter_chunk(n, carry):
        cf_ = pl.multiple_of(n * GLA_CHUNK, GLA_CHUNK)
        cb_ = pl.multiple_of((nch - 1 - n) * GLA_CHUNK, GLA_CHUNK)
        for h in range(GLA_H):
            chain(h, pl.ds(cf_, GLA_CHUNK), pl.ds(cf_ + GLA_CHUNK - 1, 1), qdf_s, kdf_s, bf_s, sf_s)
            chain(h, pl.ds(cb_, GLA_CHUNK), pl.ds(cb_, 1), qdb_s, kdb_s, bb_s, sb_s)
        return carry

    lax.fori_loop(0, nch, inter_chunk, 0, unroll=2)

    for h in range(GLA_H):
        o = o_ref[:, hv(h)]
        o = o * lax.rsqrt(jnp.mean(o * o, axis=-1, keepdims=True) + EPS) * ng_ref[h]
        r = r_ref[:, hv(h)]
        o_ref[:, hv(h)] = o * (r * jax.nn.sigmoid(r))
        if want_state:
            st_ref[0, h] = sf_s[h].T
            st_ref[1, h] = sb_s[h].T


def _gla_seq_call(y, w2, ba, ng, T, n_seq, row0, use_rope, s0, o_prev=None):
    has_s0 = s0 is not None
    want_state = not has_s0
    rb = row0 // T
    nq, nv = GLA_H * GLA_DK, GLA_H * GLA_DV
    mode = dict(pipeline_mode=pl.Buffered(1)) if n_seq <= 2 else {}
    col = lambda w, j: pl.BlockSpec((T, w), lambda b: (rb + b, j), **mode)
    cst = lambda a: pl.BlockSpec(a.shape, lambda b: (0,) * a.ndim)
    lf, lb, sel = _gla_consts()
    in_specs = [col(nq, 0), col(nq, 1), col(nv, 1), col(nv, 2), col(LANE, GLA_ZCOL),
                cst(w2), cst(ba), cst(ng), cst(lf), cst(lb), cst(sel)]
    args = [y, y, y, y, y, w2, ba, ng, lf, lb, sel]
    if use_rope:
        cos, sin = _rope_tables(T)
        in_specs += [cst(cos), cst(sin)]
        args += [cos, sin]
    st_spec = pl.BlockSpec((None, 2, GLA_H, GLA_DK, GLA_DV), lambda b: (b, 0, 0, 0, 0))
    if has_s0:
        in_specs += [st_spec, pl.BlockSpec(memory_space=pl.ANY)]
        args += [s0, o_prev]
    aliases = {len(args) - 1: 0} if has_s0 else {}
    out_specs = [pl.BlockSpec((T, nv), lambda b: (rb + b, 0))]
    out_shape = [jax.ShapeDtypeStruct((N_TOK, nv), F32)]
    if want_state:
        out_specs.append(st_spec)
        out_shape.append(jax.ShapeDtypeStruct((n_seq, 2, GLA_H, GLA_DK, GLA_DV), F32))
    scratch = ([pltpu.VMEM((GLA_H, T, GLA_DK), F32)] * 4 + [pltpu.VMEM((GLA_H, T, GLA_DK), BF16)] * 4
               + [pltpu.VMEM((T, GLA_CHUNK * GLA_DK), BF16),
                  pltpu.VMEM((GLA_H, GLA_DV, GLA_DK), F32), pltpu.VMEM((GLA_H, GLA_DV, GLA_DK), F32)])
    outs = pl.pallas_call(
        functools.partial(_gla_seq_kernel, use_rope=use_rope, has_s0=has_s0, want_state=want_state),
        grid=(n_seq,),
        in_specs=in_specs, out_specs=out_specs, out_shape=out_shape,
        scratch_shapes=scratch,
        input_output_aliases=aliases,
        compiler_params=pltpu.CompilerParams(dimension_semantics=("arbitrary",),
                                             vmem_limit_bytes=VMEM_LIMIT),
        name="gla_rope" if use_rope else "gla",
    )(*args)
    return (outs[0], outs[1]) if want_state else (outs[0], None)


def _gla_mixer(y, state, w_a2, b_a, norm_g):
    w2 = jnp.zeros((GLA_H, LANE, 2 * GLA_DK), F32)
    for e in range(2):
        we = w_a2[e].reshape(GLA_RANK, GLA_H, GLA_DK).transpose(1, 0, 2)
        w2 = w2.at[:, e * GLA_RANK:(e + 1) * GLA_RANK, e * GLA_DK:(e + 1) * GLA_DK].set(we)
    ba = b_a.reshape(2, GLA_H, GLA_DK).transpose(1, 0, 2).reshape(GLA_H, 1, 2 * GLA_DK)
    ng = norm_g.reshape(GLA_H, 1, GLA_DV)
    op, st = _gla_seq_call(y, w2.astype(BF16), ba, ng, SEQ, BATCH, 0, False, None)
    o, _ = _gla_seq_call(y, w2.astype(BF16), ba, ng, DEC_SEQ, DEC_BATCH, N_PROMPT, True, state, op)
    return o, st


N_HEAD_PAIRS = NAT_H // 2
NAT_ROWS = DEC_SEQ // GRID_W
NAT_WIN = NAT_WH * GRID_W
NAT_CLS = NAT_WH


def _nat_row_window(r):
    rs = min(max(r - NAT_WH // 2, 0), NAT_ROWS - NAT_WH)
    return rs, r - rs


def _nat_bias_table(rpb):
    qc = np.arange(GRID_W)[:, None]
    kc = np.arange(GRID_W)[None, :]
    c_start = np.clip(qc - NAT_WW // 2, 0, GRID_W - NAT_WW)
    ok = (kc >= c_start) & (kc < c_start + NAT_WW)
    dc = np.clip(kc - qc + NAT_WW - 1, 0, 2 * NAT_WW - 2)
    cls = np.arange(NAT_CLS)[:, None]
    w = np.arange(NAT_WH)[None, :]
    dr = w - cls + NAT_WH - 1
    pick = jnp.asarray(dc[None] == np.arange(2 * NAT_WW - 1)[:, None, None], F32)
    t = jnp.einsum('hcwd,dqk->hcqwk', rpb[:, dr], pick, precision=lax.Precision.HIGHEST)
    t = jnp.where(ok[None, None, :, None, :], t, NEG_INF)
    return t.reshape(NAT_H, NAT_CLS, GRID_W, NAT_WIN)


def _head_mask(hh):
    lane = lax.broadcasted_iota(jnp.int32, (1, LANE), 1)
    return (lane < NAT_HD) if hh == 0 else (lane >= NAT_HD)


def _dot_nt(a, b):
    return lax.dot_general(a, b, (((1,), (1,)), ((), ())), preferred_element_type=F32)


def _nat_ctx_kernel(q_ref, k_ref, v_ref, o_ref, kc_ref, vc_ref):
    for hp in range(N_HEAD_PAIRS):
        cols = slice(hp * LANE, (hp + 1) * LANE)
        q = q_ref[:, cols] * (NAT_HD ** -0.5)
        k, v = k_ref[:, cols], v_ref[:, cols]
        kb, vb = k.astype(BF16), v.astype(BF16)
        q2 = jnp.concatenate([jnp.where(_head_mask(hh), q, 0.0) for hh in range(2)], axis=0).astype(BF16)
        s = _dot_nt(q2, kb)
        p = jnp.exp(s - jnp.max(s, axis=-1, keepdims=True))
        l = jnp.sum(p, axis=-1, keepdims=True)
        o2 = jnp.dot(p.astype(BF16), vb, preferred_element_type=F32) / l
        o_ref[:, cols] = jnp.where(_head_mask(0), o2[:SEQ], o2[SEQ:])
        for hh in range(2):
            kc_ref[2 * hp + hh] = k[:, hh * NAT_HD:(hh + 1) * NAT_HD]
            vc_ref[2 * hp + hh] = v[:, hh * NAT_HD:(hh + 1) * NAT_HD]


def _nat_context(y):
    blk = lambda j: pl.BlockSpec((SEQ, D_MODEL), lambda b: (b, j))
    cache = pl.BlockSpec((None, NAT_H, SEQ, NAT_HD), lambda b: (b, 0, 0, 0))
    cache_shape = jax.ShapeDtypeStruct((BATCH, NAT_H, SEQ, NAT_HD), F32)
    return pl.pallas_call(
        _nat_ctx_kernel,
        grid=(BATCH,),
        in_specs=[blk(0), blk(1), blk(2)],
        out_specs=[pl.BlockSpec((SEQ, D_MODEL), lambda b: (b, 0)), cache, cache],
        out_shape=[jax.ShapeDtypeStruct((N_TOK, D_MODEL), F32), cache_shape, cache_shape],
        compiler_params=pltpu.CompilerParams(dimension_semantics=("arbitrary",)),
        name="nat_context",
    )(y, y, y)


def _nat_lat_kernel(q_ref, k_ref, v_ref, ck_ref, cv_ref, tab_ref, o_in_ref, o_ref):
    del o_in_ref
    q = q_ref[...] * (NAT_HD ** -0.5)
    qm = [jnp.where(_head_mask(hh), q, 0.0).astype(BF16) for hh in range(2)]
    ckb = ck_ref[...].astype(BF16)
    cvb = cv_ref[...].astype(BF16)
    for r in range(NAT_ROWS):
        rs, cls = _nat_row_window(r)
        kw = k_ref[rs * GRID_W:rs * GRID_W + NAT_WIN, :].astype(BF16)
        vw = v_ref[rs * GRID_W:rs * GRID_W + NAT_WIN, :].astype(BF16)
        qr = jnp.concatenate([qm[hh][r * GRID_W:(r + 1) * GRID_W] for hh in range(2)], axis=0)
        s_lat = _dot_nt(qr, kw) + jnp.concatenate([tab_ref[0, cls], tab_ref[1, cls]], axis=0)
        s_ctx = _dot_nt(qr, ckb)
        m = jnp.maximum(jnp.max(s_lat, axis=-1, keepdims=True), jnp.max(s_ctx, axis=-1, keepdims=True))
        p_lat = jnp.exp(s_lat - m)
        p_ctx = jnp.exp(s_ctx - m)
        l = jnp.sum(p_lat, axis=-1, keepdims=True) + jnp.sum(p_ctx, axis=-1, keepdims=True)
        o2 = (jnp.dot(p_lat.astype(BF16), vw, preferred_element_type=F32)
              + jnp.dot(p_ctx.astype(BF16), cvb, preferred_element_type=F32)) / l
        o_ref[r * GRID_W:(r + 1) * GRID_W, :] = jnp.where(_head_mask(0), o2[:GRID_W], o2[GRID_W:])


def _nat_latent(y, o_ctx, ck, cv, rpb):
    row0 = N_PROMPT // DEC_SEQ
    blk = lambda off: pl.BlockSpec((DEC_SEQ, LANE), lambda b, hp: (row0 + b, off + hp))
    ctx = pl.BlockSpec((None, ck.shape[1], LANE), lambda b, hp: (b, 0, hp))
    return pl.pallas_call(
        _nat_lat_kernel,
        grid=(DEC_BATCH, N_HEAD_PAIRS),
        in_specs=[blk(0), blk(N_HEAD_PAIRS), blk(2 * N_HEAD_PAIRS), ctx, ctx,
                  pl.BlockSpec((2, NAT_CLS, GRID_W, NAT_WIN), lambda b, hp: (hp, 0, 0, 0)),
                  pl.BlockSpec(memory_space=pl.ANY)],
        out_specs=pl.BlockSpec((DEC_SEQ, LANE), lambda b, hp: (row0 + b, hp)),
        out_shape=jax.ShapeDtypeStruct((N_TOK, D_MODEL), F32),
        input_output_aliases={6: 0},
        compiler_params=pltpu.CompilerParams(dimension_semantics=("arbitrary", "arbitrary")),
        name="nat_latent",
    )(y, y, y, ck, cv, _nat_bias_table(rpb), o_ctx)


def _heads_last(t):
    b, h, s, d = t.shape
    return t.transpose(0, 2, 1, 3).reshape(b, s, h * d)


def _heads_first(t):
    b, s, _ = t.shape
    return t.reshape(b, s, NAT_H, NAT_HD).transpose(0, 2, 1, 3)


def _gmlp_kernel(x_ref, y_ref, mod_ref, lng_ref, lnb_ref, ws_ref, bs_ref, w_ref, o_ref, t_ref):
    tm = x_ref.shape[0]
    v = y_ref[:, GM_DH:]
    vc = v - jnp.mean(v, axis=-1, keepdims=True)
    vn = vc * lax.rsqrt(jnp.mean(vc * vc, axis=-1, keepdims=True) + EPS) * lng_ref[...] + lnb_ref[...]
    vn = vn.astype(BF16)
    for n in range(tm // GM_CHUNK):
        rows = slice(n * GM_CHUNK, (n + 1) * GM_CHUNK)
        for g in range(GM_G):
            cols = slice(g * GM_CG, (g + 1) * GM_CG)
            sp = jnp.dot(ws_ref[g], vn[rows, cols], preferred_element_type=F32) + bs_ref[:, cols]
            t_ref[rows, cols] = (y_ref[rows, cols] * sp).astype(BF16)
    gate = mod_ref[5:6, :]
    o_ref[...] = x_ref[...] + gate * jnp.dot(t_ref[...], w_ref[...], preferred_element_type=F32)


def _gmlp_out(x, y, mod_l, ln_g, ln_b, w_s, b_s, w_out):
    tm = TM_PROJ
    bias = jnp.repeat(b_s.T, GM_CG, axis=1)
    return pl.pallas_call(
        _gmlp_kernel,
        grid=(N_TOK // tm,),
        in_specs=[
            pl.BlockSpec((tm, D_MODEL), lambda i: (i, 0)),
            pl.BlockSpec((tm, 2 * GM_DH), lambda i: (i, 0)),
            pl.BlockSpec((None, N_MOD, D_MODEL), lambda i: (_group_of_tile(i, tm), 0, 0)),
            pl.BlockSpec((1, GM_DH), lambda i: (0, 0)),
            pl.BlockSpec((1, GM_DH), lambda i: (0, 0)),
            _resident((GM_G, GM_CHUNK, GM_CHUNK)),
            _resident((GM_CHUNK, GM_DH)),
            _resident((GM_DH, D_MODEL)),
        ],
        out_specs=pl.BlockSpec((tm, D_MODEL), lambda i: (i, 0)),
        out_shape=jax.ShapeDtypeStruct((N_TOK, D_MODEL), F32),
        scratch_shapes=[pltpu.VMEM((tm, GM_DH), BF16)],
        compiler_params=pltpu.CompilerParams(dimension_semantics=("arbitrary",),
                                             vmem_limit_bytes=VMEM_LIMIT),
        name="gmlp_gate_out",
    )(x, y, mod_l, ln_g.reshape(1, GM_DH), ln_b.reshape(1, GM_DH), w_s.astype(BF16), bias,
      w_out.astype(BF16))


def _dwconv_centred(x, w, b):
    K = w.shape[0]
    T = x.shape[1]
    pad = K // 2
    xp = jnp.pad(x, ((0, 0), (pad, pad), (0, 0)))
    out = xp[:, 0:T] * w[0]
    for i in range(1, K):
        out = out + xp[:, i:i + T] * w[i]
    return out + b


def _ssd_chunked(x, dt, a, bm, cm, s0):
    B_, T = x.shape[:2]
    L = SSD_CHUNK
    n = T // L
    E = SSD_H // SSD_G
    x = x.reshape(B_, n, L, SSD_G, E, SSD_P)
    dt = dt.reshape(B_, n, L, SSD_G, E)
    bm = bm.reshape(B_, n, L, SSD_G, SSD_N)
    cm = cm.reshape(B_, n, L, SSD_G, SSD_N)
    cum = jnp.cumsum(dt * a.reshape(SSD_G, E), axis=2)
    tril = jnp.tril(jnp.ones((L, L), dtype=bool))
    seg = cum[:, :, :, None] - cum[:, :, None, :]
    decay = jnp.exp(jnp.where(tril[:, :, None, None], seg, -jnp.inf))
    dtx = x * dt[..., None]
    cb = jnp.einsum('bnigs,bnjgs->bnijg', cm, bm)
    y_diag = jnp.einsum('bnijg,bnijge,bnjgep->bnigep', cb, decay, dtx)
    u = jnp.einsum('bnjgs,bnjge,bnjgep->bngeps', bm, jnp.exp(cum[:, :, -1:] - cum), dtx)
    chunk_decay = jnp.exp(cum[:, :, -1])
    q_decay = jnp.exp(cum)

    def step(s, xs):
        c_n, qd_n, u_n, g_n = xs
        y = jnp.einsum('bigs,bige,bgeps->bigep', c_n, qd_n, s)
        return g_n[..., None, None] * s + u_n, y

    sw = lambda t: jnp.swapaxes(t, 0, 1)
    s_fin, y_off = lax.scan(step, s0.reshape(B_, SSD_G, E, SSD_P, SSD_N),
                            (sw(cm), sw(q_decay), sw(u), sw(chunk_decay)))
    y = y_diag + sw(y_off)
    return y.reshape(B_, T, SSD_H, SSD_P), s_fin.reshape(B_, SSD_H, SSD_P, SSD_N)


def _ssd_core(y, s0, conv_w, conv_b, dt_bias, a_log, d_skip, norm_g):
    B_, T, _ = y.shape
    z = y[..., :SSD_DI]
    xbc = y[..., SSD_DI:SSD_DI + SSD_XBC]
    dt = y[..., SSD_DI + SSD_XBC:SSD_IN]
    xbc = jax.nn.silu(_dwconv_centred(xbc, conv_w, conv_b))
    x = xbc[..., :SSD_DI].reshape(B_, T, SSD_H, SSD_P)
    bm = xbc[..., SSD_DI:SSD_DI + SSD_G * SSD_N].reshape(B_, T, SSD_G, SSD_N)
    cm = xbc[..., SSD_DI + SSD_G * SSD_N:].reshape(B_, T, SSD_G, SSD_N)
    dt = jax.nn.softplus(dt.reshape(B_, T, 2, SSD_H) + dt_bias)
    a = -jnp.exp(a_log)
    y_f, s_f = _ssd_chunked(x, dt[:, :, 0], a[0], bm, cm, s0[:, 0])
    y_b, s_b = _ssd_chunked(_flip_t(x), _flip_t(dt[:, :, 1]), a[1], _flip_t(bm), _flip_t(cm), s0[:, 1])
    yy = y_f + _flip_t(y_b) + d_skip[:, None] * x
    yy = _rmsnorm(yy.reshape(B_, T, SSD_DI) * jax.nn.silu(z), norm_g)
    return yy, jnp.stack([s_f, s_b], axis=1)


SSD_E = SSD_H // SSD_G
SSD_GP = SSD_E * SSD_P
SSD_BLK = 2 * SSD_CHUNK
SSD_COL_X = SSD_DI
SSD_COL_B = 2 * SSD_DI
SSD_COL_C = SSD_COL_B + SSD_G * SSD_N
SSD_COL_DT = SSD_COL_C + SSD_G * SSD_N
SSD_PROJ = SSD_COL_DT + SSD_G * LANE


def _ssd_w_in(w_in):
    base = SSD_DI + SSD_XBC
    w_dt = jnp.zeros((D_MODEL, SSD_G, LANE), F32)
    for g in range(SSD_G):
        cols = jnp.concatenate([w_in[:, base + g * SSD_E:base + (g + 1) * SSD_E],
                                w_in[:, base + SSD_H + g * SSD_E:base + SSD_H + (g + 1) * SSD_E]], axis=1)
        w_dt = w_dt.at[:, g, :2 * SSD_E].set(cols).at[:, g, 2 * SSD_E:4 * SSD_E].set(cols)
    return jnp.concatenate([w_in[:, :base], w_dt.reshape(D_MODEL, SSD_G * LANE)], axis=1)


def _ssd_consts():
    r = np.arange(SSD_BLK)
    same = (r[:, None] // SSD_CHUNK) == (r[None, :] // SSD_CHUNK)
    cum = np.concatenate([same & (r[None, :] <= r[:, None]), same & (r[None, :] >= r[:, None])], axis=0)
    lane = np.arange(LANE)[:, None]
    col = np.arange(SSD_GP)[None, :] // SSD_P
    exp_f = lane == col
    exp_b = lane == col + SSD_E
    return jnp.asarray(cum, BF16), jnp.asarray(exp_f, BF16), jnp.asarray(exp_b, BF16)


def _dot2_r(x, m):
    hi = x.astype(BF16)
    lo = (x - hi.astype(F32)).astype(BF16)
    return jnp.dot(hi, m, preferred_element_type=F32) + jnp.dot(lo, m, preferred_element_type=F32)


def _softplus(x):
    return jnp.maximum(x, 0.0) + jnp.log1p(jnp.exp(-jnp.abs(x)))


def _ssd_kernel(*refs, has_s0, want_state):
    it = iter(refs)
    (z_ref, x_ref, b_ref, c_ref, dt_ref, cwx_ref, cwb_ref, cwc_ref, cbx_ref, cbb_ref, cbc_ref,
     dtb_ref, alog_ref, dsk_ref, cum_ref, ef_ref, eb_ref) = (next(it) for _ in range(17))
    s0_ref = next(it) if has_s0 else None
    if has_s0:
        next(it)
    o_ref = next(it)
    st_ref = next(it) if want_state else None
    xs_s, xb_s, bm_s, cm_s, cu_s, dt_s, y_s, sf_s, sb_s = it

    T = x_ref.shape[0]
    nch = T // SSD_CHUNK
    L = SSD_CHUNK

    trow = lax.broadcasted_iota(jnp.int32, (T, 1), 0)

    def conv_silu(v_ref, w_ref, bias_ref):
        v = v_ref[...]
        prev = jnp.where(trow == 0, 0.0, pltpu.roll(v, 1, axis=0))
        nxt = jnp.where(trow == T - 1, 0.0, pltpu.roll(v, T - 1, axis=0))
        y = prev * w_ref[0:1, :] + v * w_ref[1:2, :] + nxt * w_ref[2:3, :] + bias_ref[...]
        return y * jax.nn.sigmoid(y)

    xs = conv_silu(x_ref, cwx_ref, cbx_ref)
    xs_s[...] = xs
    xb_s[...] = xs.astype(BF16)
    bm_s[...] = conv_silu(b_ref, cwb_ref, cbb_ref).astype(BF16)
    cm_s[...] = conv_silu(c_ref, cwc_ref, cbc_ref).astype(BF16)

    lane1 = lax.broadcasted_iota(jnp.int32, (1, LANE), 1)
    a_row = jnp.where(lane1 < 2 * SSD_E, -jnp.exp(alog_ref[...]), 0.0)
    for blk in range(T // SSD_BLK):
        rows = slice(blk * SSD_BLK, (blk + 1) * SSD_BLK)
        dt = _softplus(dt_ref[rows, :] + dtb_ref[...])
        c2 = _dot3(cum_ref[...], dt * a_row)
        cu_s[rows, :] = jnp.where(lane1 < SSD_E, c2[:SSD_BLK], c2[SSD_BLK:])
        dt_s[rows, :] = dt

    ii = lax.broadcasted_iota(jnp.int32, (L, LANE), 0)
    jj = lax.broadcasted_iota(jnp.int32, (L, LANE), 1)
    fwd_half = jj < L
    fwd_half1 = lane1 < L
    tri = (fwd_half & (ii >= jj)) | ((jj >= L) & (ii <= jj - L))
    left = lane1 < SSD_P
    for c in range(nch):
        rows = slice(c * L, (c + 1) * L)
        cum_c, dt_c = cu_s[rows, :], dt_s[rows, :]
        bm_c, cm_c = bm_s[rows, :], cm_s[rows, :]
        cb2 = _dot_nt(cm_c, jnp.concatenate([bm_c, bm_c], axis=0))
        arr = jnp.where(lane1 < 2 * SSD_E, cum_c, dt_c)
        arr_t = jnp.concatenate([arr, arr], axis=0).T
        gs = []
        for e in range(SSD_E):
            row_c = jnp.where(fwd_half1, arr_t[e:e + 1, :], arr_t[SSD_E + e:SSD_E + e + 1, :])
            row_dt = jnp.where(fwd_half1, arr_t[2 * SSD_E + e:2 * SSD_E + e + 1, :],
                               arr_t[3 * SSD_E + e:3 * SSD_E + e + 1, :])
            col_c = jnp.where(fwd_half, jnp.broadcast_to(cum_c[:, e:e + 1], (L, LANE)),
                              jnp.broadcast_to(cum_c[:, SSD_E + e:SSD_E + e + 1], (L, LANE)))
            dec = jnp.exp(jnp.where(tri, col_c - row_c, -jnp.inf))
            gs.append((cb2 * dec * row_dt).astype(BF16))
        for pr in range(SSD_E // 2):
            cols = slice(pr * LANE, (pr + 1) * LANE)
            xp = xb_s[rows, cols]
            xl = jnp.where(left, xp, jnp.zeros_like(xp))
            xr = jnp.where(left, jnp.zeros_like(xp), xp)
            lhs = jnp.concatenate([gs[2 * pr], gs[2 * pr + 1]], axis=1)
            rhs = jnp.concatenate([xl, xl, xr, xr], axis=0)
            y_s[rows, cols] = (jnp.dot(lhs, rhs, preferred_element_type=F32)
                               + dsk_ref[:, cols] * xs_s[rows, cols])

    if has_s0:
        sf_s[...] = s0_ref[0].T
        sb_s[...] = s0_ref[1].T
    else:
        sf_s[...] = jnp.zeros_like(sf_s)
        sb_s[...] = jnp.zeros_like(sb_s)

    def chain(c, last, lane0, exp_ref, st_s):
        rows = slice(c * L, (c + 1) * L)
        cum_c = jnp.where((lane1 >= lane0) & (lane1 < lane0 + SSD_E), cu_s[rows, :], 0.0)
        tot = cum_c[last:last + 1, :]
        qw = jnp.concatenate([jnp.exp(cum_c), jnp.exp(tot - cum_c) * dt_s[rows, :]], axis=0)
        qw = _dot2_r(qw, exp_ref[...])
        qd, w = qw[:L], qw[L:]
        s = st_s[...]
        y_s[rows, :] += jnp.dot(cm_s[rows, :], s.astype(BF16), preferred_element_type=F32) * qd
        xw = (xs_s[rows, :] * w).astype(BF16)
        u = lax.dot_general(bm_s[rows, :], xw, (((0,), (0,)), ((), ())), preferred_element_type=F32)
        st_s[...] = s * qd[last:last + 1, :] + u

    for n in range(nch):
        chain(n, L - 1, 0, ef_ref, sf_s)
        chain(nch - 1 - n, 0, SSD_E, eb_ref, sb_s)

    z = z_ref[...]
    o_ref[...] = y_s[...] * (z * jax.nn.sigmoid(z))
    if want_state:
        st_ref[0] = sf_s[...].T
        st_ref[1] = sb_s[...].T


def _ssd_call(y, prm, T, n_seq, row0, s0, o_prev=None):
    has_s0 = s0 is not None
    want_state = not has_s0
    rb = row0 // T
    col = lambda w, off: pl.BlockSpec((T, w), lambda b, g: (rb + b, off // w + g))
    cst = lambda shape: pl.BlockSpec(shape, lambda b, g: (0,) * len(shape))
    wcol = lambda rows, w, off: pl.BlockSpec((rows, w), lambda b, g: (0, off // w + g))
    per_g = lambda w: pl.BlockSpec((None, 1, w), lambda b, g: (g, 0, 0))
    cum, exp_f, exp_b = _ssd_consts()
    xoff, boff, coff = 0, SSD_DI, SSD_DI + SSD_G * SSD_N
    in_specs = [col(SSD_GP, 0), col(SSD_GP, SSD_COL_X), col(SSD_N, SSD_COL_B), col(SSD_N, SSD_COL_C),
                col(LANE, SSD_COL_DT),
                wcol(SSD_CONV, SSD_GP, xoff), wcol(SSD_CONV, SSD_N, boff), wcol(SSD_CONV, SSD_N, coff),
                wcol(1, SSD_GP, xoff), wcol(1, SSD_N, boff), wcol(1, SSD_N, coff),
                per_g(LANE), per_g(LANE), per_g(SSD_GP),
                cst(cum.shape), cst(exp_f.shape), cst(exp_b.shape)]
    args = [y, y, y, y, y, prm["conv_w"], prm["conv_w"], prm["conv_w"], prm["conv_b"], prm["conv_b"], prm["conv_b"],
            prm["dt_bias"], prm["a_log"], prm["d_skip"], cum, exp_f, exp_b]
    st_spec = pl.BlockSpec((None, 2, None, SSD_GP, SSD_N), lambda b, g: (b, 0, g, 0, 0))
    if has_s0:
        in_specs += [st_spec, pl.BlockSpec(memory_space=pl.ANY)]
        args += [s0, o_prev]
    aliases = {len(args) - 1: 0} if has_s0 else {}
    out_specs = [pl.BlockSpec((T, SSD_GP), lambda b, g: (rb + b, g))]
    out_shape = [jax.ShapeDtypeStruct((N_TOK, SSD_DI), F32)]
    if want_state:
        out_specs.append(st_spec)
        out_shape.append(jax.ShapeDtypeStruct((n_seq, 2, SSD_G, SSD_GP, SSD_N), F32))
    scratch = [pltpu.VMEM((T, SSD_GP), F32), pltpu.VMEM((T, SSD_GP), BF16),
               pltpu.VMEM((T, SSD_N), BF16), pltpu.VMEM((T, SSD_N), BF16),
               pltpu.VMEM((T, LANE), F32), pltpu.VMEM((T, LANE), F32), pltpu.VMEM((T, SSD_GP), F32),
               pltpu.VMEM((SSD_N, SSD_GP), F32), pltpu.VMEM((SSD_N, SSD_GP), F32)]
    outs = pl.pallas_call(
        functools.partial(_ssd_kernel, has_s0=has_s0, want_state=want_state),
        grid=(n_seq, SSD_G),
        in_specs=in_specs, out_specs=out_specs, out_shape=out_shape,
        scratch_shapes=scratch,
        input_output_aliases=aliases,
        compiler_params=pltpu.CompilerParams(dimension_semantics=("arbitrary", "arbitrary"),
                                             vmem_limit_bytes=VMEM_LIMIT),
        name="ssd_state" if has_s0 else "ssd",
    )(*args)
    return (outs[0], outs[1]) if want_state else (outs[0], None)


def _ssd_mixer(y, state, conv_w, conv_b, dt_bias, a_log, d_skip):
    def lanes(p):
        pg = p.reshape(2, SSD_G, SSD_E).transpose(1, 0, 2).reshape(SSD_G, 2 * SSD_E)
        return jnp.pad(jnp.concatenate([pg, pg], axis=1), ((0, 0), (0, LANE - 4 * SSD_E))).reshape(SSD_G, 1, LANE)
    prm = dict(conv_w=conv_w, conv_b=conv_b.reshape(1, SSD_XBC), dt_bias=lanes(dt_bias), a_log=lanes(a_log),
               d_skip=jnp.repeat(d_skip, SSD_P).reshape(SSD_G, 1, SSD_GP))
    op, st = _ssd_call(y, prm, SEQ, BATCH, 0, None)
    s0 = state.reshape(DEC_BATCH, 2, SSD_G, SSD_GP, SSD_N)
    o, _ = _ssd_call(y, prm, DEC_SEQ, DEC_BATCH, N_PROMPT, s0, op)
    return o, st.reshape(BATCH, 2, SSD_H, SSD_P, SSD_N)


def _pad_cols(w, mult):
    n = w.shape[1]
    n_pad = -n % mult
    return jnp.pad(w, ((0, 0), (0, n_pad))) if n_pad else w


def kernel(x_prompt, x_sample, state_gla, cache_nat_k, cache_nat_v, state_ssd, c,
           c_ctx, norm_g, w_ada, b_ada, w_ffn_in, w_ffn_out,
           gla_w_in, gla_w_a1, gla_w_a2, gla_b_a, gla_norm_g, gla_w_out,
           nat_w_qkv, nat_rpb, nat_w_out,
           gm_w_in, gm_ln_g, gm_ln_b, gm_w_s, gm_b_s, gm_w_out,
           ssd_w_in, ssd_conv_w, ssd_conv_b, ssd_dt_bias, ssd_a_log, ssd_d, ssd_norm_g, ssd_w_out,
           final_g):
    x = (x_prompt.reshape(N_PROMPT, D_MODEL), x_sample.reshape(N_SAMPLE, D_MODEL))
    mod = _modulation_all(c, c_ctx, w_ada, b_ada)
    new_gla, new_k, new_v, new_ssd = [], [], [], []
    for l in range(DEPTH):
        kind, j = l % N_MIXERS, l // N_MIXERS
        x = _ffn(x, mod[l], norm_g[l, 0], w_ffn_in[l, 0], w_ffn_out[l, 0], 0)
        mix = None
        if kind == 0:
            w = jnp.concatenate([gla_w_in[j], gla_w_a1[j, 0], gla_w_a1[j, 1]], axis=1)
            y = _proj_in(x, mod[l], norm_g[l, 1], _pad_cols(w, 640), 640)
            o, st = _gla_mixer(y, state_gla[:, j], gla_w_a2[j], gla_b_a[j], gla_norm_g[j])
            new_gla.append(st)
            mix = (o, gla_w_out[j], None)
        elif kind == 1:
            y = _proj_in(x, mod[l], norm_g[l, 1], nat_w_qkv[j], 768)
            o, kc, vc = _nat_context(y)
            o = _nat_latent(y, o, _heads_last(cache_nat_k[:, j]), _heads_last(cache_nat_v[:, j]), nat_rpb[j])
            new_k.append(kc)
            new_v.append(vc)
            mix = (o, nat_w_out[j], None)
        elif kind == 2:
            y = _proj_in(x, mod[l], norm_g[l, 1], gm_w_in[j], 512, act="gelu")
            x = _gmlp_out(x, y, mod[l], gm_ln_g[j], gm_ln_b[j], gm_w_s[j], gm_b_s[j], gm_w_out[j])
        else:
            y = _proj_in(x, mod[l], norm_g[l, 1], _ssd_w_in(ssd_w_in[j]), 512)
            o, st = _ssd_mixer(y, state_ssd[:, j], ssd_conv_w[j], ssd_conv_b[j], ssd_dt_bias[j],
                               ssd_a_log[j], ssd_d[j])
            new_ssd.append(st)
            mix = (o, ssd_w_out[j], ssd_norm_g[j])
        x = _ffn(x, mod[l], norm_g[l, 2], w_ffn_in[l, 1], w_ffn_out[l, 1], 2, mix=mix,
                 final_g=final_g if l == DEPTH - 1 else None)
    y_prompt = x[0].reshape(BATCH, SEQ, D_MODEL)
    y_sample = x[1].reshape(DEC_BATCH, DEC_SEQ, D_MODEL)
    return (y_prompt, y_sample, jnp.stack(new_gla, axis=1), jnp.stack(new_k, axis=1),
            jnp.stack(new_v, axis=1), jnp.stack(new_ssd, axis=1))
```

```python
import functools
import math

import jax
import jax.numpy as jnp
import numpy as np
from jax import lax
from jax.experimental import pallas as pl
from jax.experimental.pallas import tpu as pltpu

D_MODEL = 1024
BATCH = 32
SEQ = 256
DEPTH = 4
DEC_BATCH = 2
DEC_SEQ = 1024
N_PROMPT = BATCH * SEQ
N_SAMPLE = DEC_BATCH * DEC_SEQ
N_TOK = N_PROMPT + N_SAMPLE
N_GROUPS = 1 + DEC_BATCH

GRID_W = 64
N_MIXERS = 4
N_SUB = 3
N_MOD = 3 * N_SUB
D_FF = 2816
EPS = 1e-6
NEG_INF = -1e30
ROPE_THETA = 10000.0
GLA_H, GLA_DK, GLA_DV, GLA_RANK, GLA_TAU, GLA_CHUNK = 4, 128, 256, 16, 16.0, 16
GLA_IN = 2 * GLA_H * GLA_DK + 2 * GLA_H * GLA_DV
NAT_H, NAT_HD, NAT_WH, NAT_WW, NAT_QB, NAT_KB = 16, 64, 8, 16, 16, 32
GM_DH, GM_G, GM_CHUNK = 1024, 8, 128
GM_CG = GM_DH // GM_G
SSD_DI = 2 * D_MODEL
SSD_P = 64
SSD_H = SSD_DI // SSD_P
SSD_N, SSD_G, SSD_CONV, SSD_CHUNK = 128, 4, 3, 64
SSD_XBC = SSD_DI + 2 * SSD_G * SSD_N
SSD_IN = SSD_DI + SSD_XBC + 2 * SSD_H

LANE = 128
VMEM_LIMIT = 56 * 1024 * 1024
BF16 = jnp.bfloat16
F32 = jnp.float32

FF_CHUNK = 256
N_FF_CHUNKS = D_FF // FF_CHUNK
TM_FFN = 512
TM_PROJ = 512
ADA_TK = 128


def _group_of_tile(i, tm):
    n_prompt_tiles = N_PROMPT // tm
    return jnp.where(i < n_prompt_tiles, 0, 1 + (i - n_prompt_tiles) // (DEC_SEQ // tm))


def _resident(shape):
    nd = len(shape)
    return pl.BlockSpec(shape, lambda i: (0,) * nd, pipeline_mode=pl.Buffered(1))


def _premod(x, g, mod_ref, k):
    shift = mod_ref[3 * k:3 * k + 1, :]
    scale = mod_ref[3 * k + 1:3 * k + 2, :]
    gate = mod_ref[3 * k + 2:3 * k + 3, :]
    ms = jnp.mean(x * x, axis=-1, keepdims=True)
    h = x * lax.rsqrt(ms + EPS) * g
    return h * (1.0 + scale) + shift, gate


def _ada_kernel(cond_ref, wa_ref, wb_ref, b_ref, o_ref):
    cnd = cond_ref[...]
    s = (cnd * jax.nn.sigmoid(cnd)).astype(BF16)
    p = jnp.concatenate([jnp.dot(s, w_ref[...].astype(BF16), preferred_element_type=F32)
                         for w_ref in (wa_ref, wb_ref)], axis=1)

    @pl.when(pl.program_id(1) == 0)
    def _():
        o_ref[...] = p + b_ref[...]

    @pl.when(pl.program_id(1) > 0)
    def _():
        o_ref[...] += p


def _modulation_all(c, c_ctx, w_ada, b_ada):
    rows = 8
    nk = D_MODEL // ADA_TK
    cond = jnp.concatenate([c_ctx[None], c, jnp.zeros((rows - N_GROUPS, D_MODEL), F32)], axis=0)
    cond = cond.reshape(rows, nk, ADA_TK).transpose(1, 0, 2)
    n_out = N_MOD * D_MODEL
    out = pl.pallas_call(
        _ada_kernel,
        grid=(DEPTH, nk),
        in_specs=[
            pl.BlockSpec((None, rows, ADA_TK), lambda l, k: (k, 0, 0)),
            pl.BlockSpec((None, ADA_TK, n_out // 2), lambda l, k: (l, k, 0)),
            pl.BlockSpec((None, ADA_TK, n_out // 2), lambda l, k: (l, k, 1)),
            pl.BlockSpec((None, 1, n_out), lambda l, k: (l, 0, 0)),
        ],
        out_specs=pl.BlockSpec((None, rows, n_out), lambda l, k: (l, 0, 0)),
        out_shape=jax.ShapeDtypeStruct((DEPTH, rows, n_out), F32),
        compiler_params=pltpu.CompilerParams(dimension_semantics=("arbitrary", "arbitrary")),
        name="ada_modulation",
    )(cond, w_ada, w_ada, b_ada.reshape(DEPTH, 1, n_out))
    return out[:, :N_GROUPS].reshape(DEPTH, N_GROUPS, N_MOD, D_MODEL)


def _ffn_kernel(*refs, k, split_in, mix, mix_norm, final):
    it = iter(refs)
    x_refs = (next(it), next(it)) if split_in else (next(it),)
    mod_ref, g_ref = next(it), next(it)
    o_in_ref, ng_ref, wmix_ref = (next(it), next(it), next(it)) if mix else (None, None, None)
    win_ref, wout_ref = next(it), next(it)
    fg_ref = next(it) if final else None
    out_refs = (next(it), next(it)) if final else (next(it),)
    acc_ref = next(it)
    i = pl.program_id(0)
    n_prompt_tiles = N_PROMPT // x_refs[0].shape[0]

    x = jnp.where(i < n_prompt_tiles, x_refs[0][...], x_refs[1][...]) if split_in else x_refs[0][...]
    if mix:
        o_in = o_in_ref[...]
        if mix_norm:
            o_in = o_in * lax.rsqrt(jnp.mean(o_in * o_in, axis=-1, keepdims=True) + EPS) * ng_ref[...]
        x = x + mod_ref[5:6, :] * jnp.dot(o_in.astype(BF16), wmix_ref[...], preferred_element_type=F32)
    h, gate = _premod(x, g_ref[...], mod_ref, k)
    hb = h.astype(BF16)
    for j in range(N_FF_CHUNKS):
        a = jnp.dot(hb, win_ref[:, j * FF_CHUNK:(j + 1) * FF_CHUNK], preferred_element_type=F32)
        u = jnp.dot(hb, win_ref[:, D_FF + j * FF_CHUNK:D_FF + (j + 1) * FF_CHUNK], preferred_element_type=F32)
        t = (a * jax.nn.sigmoid(a) * u).astype(BF16)
        p = jnp.dot(t, wout_ref[j * FF_CHUNK:(j + 1) * FF_CHUNK, :], preferred_element_type=F32)
        if j == 0:
            acc_ref[...] = p
        else:
            acc_ref[...] += p
    y = x + 0.5 * gate * acc_ref[...]
    if not final:
        out_refs[0][...] = y
    else:
        y = y * lax.rsqrt(jnp.mean(y * y, axis=-1, keepdims=True) + EPS) * fg_ref[...]

        @pl.when(i < n_prompt_tiles)
        def _():
            out_refs[0][...] = y

        @pl.when(i >= n_prompt_tiles)
        def _():
            out_refs[1][...] = y


def _ffn(x, mod_l, g, w_in_all, w_out_all, l, half, k, mix=None, final_g=None):
    tm = TM_FFN
    split_in = isinstance(x, tuple)
    npt = N_PROMPT // tm
    row = pl.BlockSpec((tm, D_MODEL), lambda i: (i, 0))
    prompt_rows = pl.BlockSpec((tm, D_MODEL), lambda i: (jnp.minimum(i, npt - 1), 0))
    latent_rows = pl.BlockSpec((tm, D_MODEL), lambda i: (jnp.maximum(i - npt, 0), 0))
    vec = lambda n: pl.BlockSpec((1, n), lambda i: (0, 0))
    in_specs = [prompt_rows, latent_rows] if split_in else [row]
    args = list(x) if split_in else [x]
    in_specs += [pl.BlockSpec((None, N_MOD, D_MODEL), lambda i: (_group_of_tile(i, tm), 0, 0)), vec(D_MODEL)]
    args += [mod_l, g.reshape(1, D_MODEL)]
    if mix is not None:
        o_in, w_mix, norm_g = mix
        kdim = w_mix.shape[0]
        ng = jnp.ones((1, kdim), F32) if norm_g is None else norm_g.reshape(1, kdim)
        in_specs += [pl.BlockSpec((tm, kdim), lambda i: (i, 0)), vec(kdim), _resident((kdim, D_MODEL))]
        args += [o_in, ng, w_mix.astype(BF16)]
    layer_w = lambda r, c_: pl.BlockSpec((None, None, r, c_), lambda i: (l, half, 0, 0), pipeline_mode=pl.Buffered(1))
    in_specs += [layer_w(D_MODEL, 2 * D_FF), layer_w(D_FF, D_MODEL)]
    args += [w_in_all, w_out_all]
    if final_g is not None:
        in_specs.append(vec(D_MODEL))
        args.append(final_g.reshape(1, D_MODEL))
        out_specs = [prompt_rows, latent_rows]
        out_shape = [jax.ShapeDtypeStruct((N_PROMPT, D_MODEL), F32), jax.ShapeDtypeStruct((N_SAMPLE, D_MODEL), F32)]
    else:
        out_specs, out_shape = row, jax.ShapeDtypeStruct((N_TOK, D_MODEL), F32)
    return pl.pallas_call(
        functools.partial(_ffn_kernel, k=k, split_in=split_in, mix=mix is not None,
                          mix_norm=mix is not None and mix[2] is not None, final=final_g is not None),
        grid=(N_TOK // tm,),
        in_specs=in_specs, out_specs=out_specs, out_shape=out_shape,
        scratch_shapes=[pltpu.VMEM((tm, D_MODEL), F32)],
        compiler_params=pltpu.CompilerParams(dimension_semantics=("arbitrary",),
                                             vmem_limit_bytes=VMEM_LIMIT),
        name="ffn_swiglu",
    )(*args)


def _proj_in_kernel(x_ref, mod_ref, g_ref, w_ref, o_ref, *, tn, act):
    h, _ = _premod(x_ref[...], g_ref[...], mod_ref, 1)
    hb = h.astype(BF16)
    for j in range(w_ref.shape[1] // tn):
        y = jnp.dot(hb, w_ref[:, j * tn:(j + 1) * tn], preferred_element_type=F32)
        if act == "gelu":
            y = jax.nn.gelu(y)
        o_ref[:, j * tn:(j + 1) * tn] = y


def _proj_in(x, mod_l, g, w, tn, act=None):
    tm = TM_PROJ
    n_out = w.shape[1]
    return pl.pallas_call(
        functools.partial(_proj_in_kernel, tn=tn, act=act),
        grid=(N_TOK // tm,),
        in_specs=[
            pl.BlockSpec((tm, D_MODEL), lambda i: (i, 0)),
            pl.BlockSpec((None, N_MOD, D_MODEL), lambda i: (_group_of_tile(i, tm), 0, 0)),
            pl.BlockSpec((1, D_MODEL), lambda i: (0, 0)),
            _resident((D_MODEL, n_out)),
        ],
        out_specs=pl.BlockSpec((tm, n_out), lambda i: (i, 0)),
        out_shape=jax.ShapeDtypeStruct((N_TOK, n_out), F32),
        compiler_params=pltpu.CompilerParams(dimension_semantics=("arbitrary",),
                                             vmem_limit_bytes=VMEM_LIMIT),
        name="mixer_proj_in",
    )(x, mod_l, g.reshape(1, D_MODEL), w.astype(BF16))


def _proj_out_kernel(x_ref, o_in_ref, mod_ref, ng_ref, w_ref, o_ref, *, norm):
    gate = mod_ref[5:6, :]
    o_in = o_in_ref[...]
    if norm:
        o_in = o_in * lax.rsqrt(jnp.mean(o_in * o_in, axis=-1, keepdims=True) + EPS) * ng_ref[...]
    y = jnp.dot(o_in.astype(BF16), w_ref[...], preferred_element_type=F32)
    o_ref[...] = x_ref[...] + gate * y


def _proj_out(x, o_in, mod_l, w, norm_g=None):
    tm = TM_PROJ
    kdim = w.shape[0]
    ng = jnp.ones((1, kdim), F32) if norm_g is None else norm_g.reshape(1, kdim)
    return pl.pallas_call(
        functools.partial(_proj_out_kernel, norm=norm_g is not None),
        grid=(N_TOK // tm,),
        in_specs=[
            pl.BlockSpec((tm, D_MODEL), lambda i: (i, 0)),
            pl.BlockSpec((tm, kdim), lambda i: (i, 0)),
            pl.BlockSpec((None, N_MOD, D_MODEL), lambda i: (_group_of_tile(i, tm), 0, 0)),
            pl.BlockSpec((1, kdim), lambda i: (0, 0)),
            _resident((kdim, D_MODEL)),
        ],
        out_specs=pl.BlockSpec((tm, D_MODEL), lambda i: (i, 0)),
        out_shape=jax.ShapeDtypeStruct((N_TOK, D_MODEL), F32),
        compiler_params=pltpu.CompilerParams(dimension_semantics=("arbitrary",),
                                             vmem_limit_bytes=VMEM_LIMIT),
        name="mixer_proj_out",
    )(x, o_in, mod_l, ng, w.astype(BF16))


def _final_norm_kernel(x_ref, g_ref, o_ref):
    x = x_ref[...]
    ms = jnp.mean(x * x, axis=-1, keepdims=True)
    o_ref[...] = x * lax.rsqrt(ms + EPS) * g_ref[...]


def _final_norm(x, g):
    tm = 1024
    return pl.pallas_call(
        _final_norm_kernel,
        grid=(N_TOK // tm,),
        in_specs=[pl.BlockSpec((tm, D_MODEL), lambda i: (i, 0)),
                  pl.BlockSpec((1, D_MODEL), lambda i: (0, 0))],
        out_specs=pl.BlockSpec((tm, D_MODEL), lambda i: (i, 0)),
        out_shape=jax.ShapeDtypeStruct((N_TOK, D_MODEL), F32),
        name="final_rmsnorm",
    )(x, g.reshape(1, D_MODEL))


def _split_streams(y):
    return y[:N_PROMPT].reshape(BATCH, SEQ, -1), y[N_PROMPT:].reshape(DEC_BATCH, DEC_SEQ, -1)


def _join_streams(yp, ys):
    return jnp.concatenate([yp.reshape(N_PROMPT, -1), ys.reshape(N_SAMPLE, -1)], axis=0)


def _rmsnorm(x, g):
    return x * lax.rsqrt(jnp.mean(x * x, axis=-1, keepdims=True) + EPS) * g


def _flip_t(t):
    return jnp.flip(t, axis=1)


def _rope_2d(x):
    T, dh = x.shape[1], x.shape[-1]
    half = dh // 2
    t = jnp.arange(T)
    inv = ROPE_THETA ** (-jnp.arange(0, half, 2, dtype=F32) / half)

    def rot(xa, pos):
        ang = pos.astype(F32)[:, None] * inv
        cos = jnp.cos(ang)[None, :, None, :]
        sin = jnp.sin(ang)[None, :, None, :]
        x1, x2 = jnp.split(xa, 2, axis=-1)
        return jnp.concatenate([x1 * cos - x2 * sin, x1 * sin + x2 * cos], axis=-1)

    return jnp.concatenate([rot(x[..., :half], t // GRID_W), rot(x[..., half:], t % GRID_W)], axis=-1)


def _gla_chunked(q, k, v, log_a, s0):
    B_, T, H, _ = q.shape
    dv = v.shape[-1]
    C = GLA_CHUNK
    n = T // C

    def blk(t):
        return t.reshape(B_, n, C, H, -1).transpose(0, 1, 3, 2, 4)

    q, k, v, la = blk(q), blk(k), blk(v), blk(log_a)
    b = jnp.cumsum(la, axis=3)
    tril = jnp.tril(jnp.ones((C, C), dtype=bool))
    diff = b[:, :, :, :, None, :] - b[:, :, :, None, :, :]
    decay = jnp.exp(jnp.where(tril[:, :, None], diff, -jnp.inf))
    attn = jnp.einsum('bnhid,bnhjd,bnhijd->bnhij', q, k, decay)
    o_intra = jnp.einsum('bnhij,bnhjv->bnhiv', attn, v)
    b_last = b[:, :, :, -1:, :]
    q_dec = q * jnp.exp(b)
    u = jnp.einsum('bnhjd,bnhjv->bnhdv', k * jnp.exp(b_last - b), v)
    g = jnp.exp(b_last[:, :, :, 0])

    def step(s, xs):
        qd, un, gn = xs
        o = jnp.einsum('bhid,bhdv->bhiv', qd, s)
        return gn[..., None] * s + un, o

    s_fin, o_inter = lax.scan(step, s0,
                              (jnp.swapaxes(q_dec, 0, 1), jnp.swapaxes(u, 0, 1), jnp.swapaxes(g, 0, 1)))
    o = o_intra + jnp.swapaxes(o_inter, 0, 1)
    return o.transpose(0, 1, 3, 2, 4).reshape(B_, T, H, dv), s_fin


def _gla_core(y, s0, w_a2, b_a, norm_g, use_rope):
    B_, T, _ = y.shape
    nq = GLA_H * GLA_DK
    nv = GLA_H * GLA_DV
    q = y[..., :nq].reshape(B_, T, GLA_H, GLA_DK) * (GLA_DK ** -0.5)
    k = y[..., nq:2 * nq].reshape(B_, T, GLA_H, GLA_DK)
    v = y[..., 2 * nq:2 * nq + nv].reshape(B_, T, GLA_H, GLA_DV)
    r = y[..., 2 * nq + nv:2 * nq + 2 * nv]
    za = y[..., GLA_IN:GLA_IN + 2 * GLA_RANK].reshape(B_, T, 2, GLA_RANK)
    if use_rope:
        q, k = _rope_2d(q), _rope_2d(k)
    z = jnp.einsum('bter,erk->btek', za, w_a2) + b_a
    log_a = (jax.nn.log_sigmoid(z) / GLA_TAU).reshape(B_, T, 2, GLA_H, GLA_DK)
    o_f, s_f = _gla_chunked(q, k, v, log_a[:, :, 0], s0[:, 0])
    o_b, s_b = _gla_chunked(_flip_t(q), _flip_t(k), _flip_t(v), _flip_t(log_a[:, :, 1]), s0[:, 1])
    o = o_f + _flip_t(o_b)
    o = _rmsnorm(o, norm_g.reshape(GLA_H, GLA_DV))
    o = o.reshape(B_, T, nv) * jax.nn.silu(r)
    return o, jnp.stack([s_f, s_b], axis=1)


def _nat_tables(rows):
    wh = min(NAT_WH, rows)
    r = np.arange(rows)
    row_idx = np.clip(r - wh // 2, 0, rows - wh)[:, None] + np.arange(wh)
    ncb = GRID_W // NAT_QB
    col_idx = np.clip(np.arange(ncb) * NAT_QB - (NAT_KB - NAT_QB) // 2, 0,
                      GRID_W - NAT_KB)[:, None] + np.arange(NAT_KB)
    qcol = np.arange(ncb)[:, None] * NAT_QB + np.arange(NAT_QB)
    c_start = np.clip(qcol - NAT_WW // 2, 0, GRID_W - NAT_WW)
    kc = col_idx[:, None, :]
    col_ok = (kc >= c_start[..., None]) & (kc < c_start[..., None] + NAT_WW)
    dc = kc - qcol[..., None]
    dr = row_idx - r[:, None]
    full = (rows, ncb, NAT_QB, wh, NAT_KB)
    flat = (rows, ncb, NAT_QB, wh * NAT_KB)
    dr_i = np.broadcast_to(dr[:, None, None, :, None] + NAT_WH - 1, full).reshape(flat)
    dc_i = np.broadcast_to(np.clip(dc + NAT_WW - 1, 0, 2 * NAT_WW - 2)[None, :, :, None, :], full).reshape(flat)
    ok = np.broadcast_to(col_ok[None, :, :, None, :], full).reshape(flat)
    return row_idx, col_idx, dr_i, dc_i, ok


def _nat_context_core(y):
    B_, S, _ = y.shape
    q, k, v = jnp.split(y, 3, axis=-1)
    q = q.reshape(B_, S, NAT_H, NAT_HD)
    k = k.reshape(B_, S, NAT_H, NAT_HD)
    v = v.reshape(B_, S, NAT_H, NAT_HD)
    s = jnp.einsum('bqhd,bkhd->bhqk', q, k) * (NAT_HD ** -0.5)
    p = jax.nn.softmax(s, axis=-1)
    o = jnp.einsum('bhqk,bkhd->bqhd', p, v).reshape(B_, S, D_MODEL)
    return o, k.transpose(0, 2, 1, 3), v.transpose(0, 2, 1, 3)


def _nat_latent_core(y, ck, cv, rpb):
    B_, T, _ = y.shape
    rows = T // GRID_W
    ncb = GRID_W // NAT_QB
    row_idx, col_idx, dr_i, dc_i, ok = _nat_tables(rows)
    q, k, v = jnp.split(y, 3, axis=-1)
    q = q.reshape(B_, rows, ncb, NAT_QB, NAT_H, NAT_HD)
    k = k.reshape(B_, rows, GRID_W, NAT_H, NAT_HD)
    v = v.reshape(B_, rows, GRID_W, NAT_H, NAT_HD)
    ri = row_idx[:, None, :, None]
    ci = col_idx[None, :, None, :]
    kb = k[:, ri, ci].reshape(B_, rows, ncb, -1, NAT_H, NAT_HD)
    vb = v[:, ri, ci].reshape(B_, rows, ncb, -1, NAT_H, NAT_HD)
    nk = kb.shape[3]
    scale = NAT_HD ** -0.5
    s_lat = jnp.einsum('brnqhd,brnkhd->bhrnqk', q, kb) * scale
    s_lat = s_lat + rpb[:, dr_i, dc_i]
    s_lat = jnp.where(ok, s_lat, NEG_INF)
    s_ctx = jnp.einsum('brnqhd,bhsd->bhrnqs', q, ck) * scale
    p = jax.nn.softmax(jnp.concatenate([s_lat, s_ctx], axis=-1), axis=-1)
    o = (jnp.einsum('bhrnqk,brnkhd->brnqhd', p[..., :nk], vb)
         + jnp.einsum('bhrnqs,bhsd->brnqhd', p[..., nk:], cv))
    return o.reshape(B_, T, D_MODEL)


def _gmlp_core(y, ln_g, ln_b, w_s, b_s):
    B_, T, _ = y.shape
    u, v = jnp.split(y, 2, axis=-1)
    vc = v - jnp.mean(v, axis=-1, keepdims=True)
    v = vc * lax.rsqrt(jnp.mean(vc * vc, axis=-1, keepdims=True) + EPS) * ln_g + ln_b
    v = v.reshape(B_, T // GM_CHUNK, GM_CHUNK, GM_G, GM_CG)
    v = jnp.einsum('gpq,bnqgc->bnpgc', w_s, v) + b_s.T[None, None, :, :, None]
    return u * v.reshape(B_, T, GM_DH)


GLA_BLK = 128
GLA_CPB = GLA_BLK // GLA_CHUNK
GLA_ZCOL = GLA_IN // LANE


def _gla_consts():
    r = np.arange(GLA_BLK)
    same = (r[:, None] // GLA_CHUNK) == (r[None, :] // GLA_CHUNK)
    ri, ci = r[:, None] % GLA_CHUNK, r[None, :] % GLA_CHUNK
    lf = np.concatenate([same & (ci <= ri), same & (ci > ri)], axis=0)
    lb = np.concatenate([same & (ci >= ri), same & (ci < ri)], axis=0)
    rows = np.arange(GLA_CHUNK * GLA_DK)
    sel = (rows[:, None] // GLA_DK) == (np.arange(LANE)[None, :] % GLA_CHUNK)
    return (jnp.asarray(lf, BF16), jnp.asarray(lb, BF16), jnp.asarray(sel, BF16))


def _rope_tables(T):
    half = GLA_DK // 2
    t = np.arange(T)
    inv = ROPE_THETA ** (-np.arange(0, half, 2, dtype=np.float64) / half)
    lane = np.arange(GLA_DK)
    pos = np.where(lane[None, :] < half, (t // GRID_W)[:, None], (t % GRID_W)[:, None])
    ang = pos * inv[lane % (half // 2)][None, :]
    sign = np.where((lane % half) < half // 2, -1.0, 1.0)[None, :]
    return jnp.asarray(np.cos(ang), F32), jnp.asarray(np.sin(ang) * sign, F32)


def _rope_apply(x, cos, sin_signed):
    half = GLA_DK // 2
    lane = lax.broadcasted_iota(jnp.int32, (1, GLA_DK), 1)
    partner = jnp.where((lane % half) < half // 2,
                        pltpu.roll(x, GLA_DK - half // 2, axis=1), pltpu.roll(x, half // 2, axis=1))
    return x * cos + partner * sin_signed


def _split3(x):
    hi = x.astype(BF16)
    r1 = x - hi.astype(F32)
    mid = r1.astype(BF16)
    lo = (r1 - mid.astype(F32)).astype(BF16)
    return hi, mid, lo


def _dot3(m, x):
    hi, mid, lo = _split3(x)
    d = lambda p: jnp.dot(m, p, preferred_element_type=F32)
    return d(hi) + d(mid) + d(lo)


def _gla_kernel(*refs, use_rope, has_s0, want_state):
    it = iter(refs)
    q_ref, k_ref, v_ref, r_ref, za_ref, w2_ref, ba_ref, ng_ref, lf_ref, lb_ref, sel_ref = (next(it) for _ in range(11))
    cos_ref, sin_ref = (next(it), next(it)) if use_rope else (None, None)
    s0_ref = next(it) if has_s0 else None
    if has_s0:
        next(it)
    o_ref = next(it)
    st_ref = next(it) if want_state else None
    (q_s, k_s, bf_s, bb_s, ef_s, eb_s, qdf_s, kdf_s, qdb_s, kdb_s, v_s, w_s, o_s, sf_s, sb_s) = it

    T = q_ref.shape[0]
    nblk, nch = T // GLA_BLK, T // GLA_CHUNK

    for blk in range(nblk):
        rows = slice(blk * GLA_BLK, (blk + 1) * GLA_BLK)
        z = jnp.dot(za_ref[rows, :].astype(BF16), w2_ref[...], preferred_element_type=F32) + ba_ref[...]
        la = (jnp.minimum(z, 0.0) - jnp.log1p(jnp.exp(-jnp.abs(z)))) * (1.0 / GLA_TAU)
        cf = _dot3(lf_ref[...], la[:, :GLA_DK])
        cb = _dot3(lb_ref[...], la[:, GLA_DK:])
        bf, bb = cf[:GLA_BLK], cb[:GLA_BLK]
        ef, eb = jnp.exp(bf), jnp.exp(bb)
        q = q_ref[rows, :] * (GLA_DK ** -0.5)
        k = k_ref[rows, :]
        if use_rope:
            q = _rope_apply(q, cos_ref[rows, :], sin_ref[rows, :])
            k = _rope_apply(k, cos_ref[rows, :], sin_ref[rows, :])
        q_s[rows, :], k_s[rows, :] = q, k
        bf_s[rows, :], bb_s[rows, :] = bf, bb
        ef_s[rows, :], eb_s[rows, :] = ef, eb
        qdf_s[rows, :] = (q * ef).astype(BF16)
        qdb_s[rows, :] = (q * eb).astype(BF16)
        kdf_s[rows, :] = (k * jnp.exp(cf[GLA_BLK:])).astype(BF16)
        kdb_s[rows, :] = (k * jnp.exp(cb[GLA_BLK:])).astype(BF16)
        v_s[rows, :] = v_ref[rows, :].astype(BF16)

    irow = lax.broadcasted_iota(jnp.int32, (GLA_CHUNK, 1), 0)

    def intra_chunk(c, carry):
        rows = pl.ds(pl.multiple_of(c * GLA_CHUNK, GLA_CHUNK), GLA_CHUNK)
        qc, bfc, bbc = q_s[rows, :], bf_s[rows, :], bb_s[rows, :]
        for j in range(GLA_CHUNK):
            row = pl.ds(c * GLA_CHUNK + j, 1)
            e = (jnp.exp(jnp.where(irow >= j, bfc - bf_s[row, :], -jnp.inf))
                 + jnp.exp(jnp.where(irow <= j, bbc - bb_s[row, :], -jnp.inf)))
            w_s[rows, j * GLA_DK:(j + 1) * GLA_DK] = (qc * k_s[row, :] * e).astype(BF16)
        return carry

    lax.fori_loop(0, nch, intra_chunk, 0)

    lane_c = lax.broadcasted_iota(jnp.int32, (GLA_BLK, LANE), 1) // GLA_CHUNK
    row_c = lax.broadcasted_iota(jnp.int32, (GLA_BLK, LANE), 0) // GLA_CHUNK
    for blk in range(nblk):
        rows = slice(blk * GLA_BLK, (blk + 1) * GLA_BLK)
        a = jnp.dot(w_s[rows, :], sel_ref[...], preferred_element_type=F32)
        a = jnp.where(lane_c == row_c, a, 0.0).astype(BF16)
        o_s[rows, :] = jnp.dot(a, v_s[rows, :], preferred_element_type=F32)

    if has_s0:
        sf_s[...] = s0_ref[0].T
        sb_s[...] = s0_ref[1].T
    else:
        sf_s[...] = jnp.zeros_like(sf_s)
        sb_s[...] = jnp.zeros_like(sb_s)

    def chain(rows, g_row, qd_s, kd_s, e_s, st_s):
        s = st_s[...]
        o_s[rows, :] += _dot_nt(qd_s[rows, :], s.astype(BF16))
        u = lax.dot_general(v_s[rows, :], kd_s[rows, :], (((0,), (0,)), ((), ())), preferred_element_type=F32)
        st_s[...] = s * e_s[g_row, :] + u

    def inter_chunk(n, carry):
        cf_ = pl.multiple_of(n * GLA_CHUNK, GLA_CHUNK)
        cb_ = pl.multiple_of((nch - 1 - n) * GLA_CHUNK, GLA_CHUNK)
        chain(pl.ds(cf_, GLA_CHUNK), pl.ds(cf_ + GLA_CHUNK - 1, 1), qdf_s, kdf_s, ef_s, sf_s)
        chain(pl.ds(cb_, GLA_CHUNK), pl.ds(cb_, 1), qdb_s, kdb_s, eb_s, sb_s)
        return carry

    lax.fori_loop(0, nch, inter_chunk, 0, unroll=4)

    o = o_s[...]
    o = o * lax.rsqrt(jnp.mean(o * o, axis=-1, keepdims=True) + EPS) * ng_ref[...]
    r = r_ref[...]
    o_ref[...] = o * (r * jax.nn.sigmoid(r))
    if want_state:
        st_ref[0] = sf_s[...].T
        st_ref[1] = sb_s[...].T


def _gla_call(y, w2, ba, ng, T, n_seq, row0, use_rope, s0, o_prev=None):
    has_s0 = s0 is not None
    want_state = not has_s0
    rb = row0 // T
    nqb = GLA_H
    col = lambda w, off: pl.BlockSpec((T, w), lambda b, h: (rb + b, off + h))
    cst = lambda shape: pl.BlockSpec(shape, lambda b, h: (0,) * len(shape))
    per_head = lambda shape: pl.BlockSpec((None,) + shape, lambda b, h: (h,) + (0,) * len(shape))
    lf, lb, sel = _gla_consts()
    in_specs = [col(GLA_DK, 0), col(GLA_DK, nqb), col(GLA_DV, nqb), col(GLA_DV, nqb + GLA_H),
                pl.BlockSpec((T, LANE), lambda b, h: (rb + b, GLA_ZCOL)),
                per_head((LANE, 2 * GLA_DK)), per_head((1, 2 * GLA_DK)), per_head((1, GLA_DV)),
                cst(lf.shape), cst(lb.shape), cst(sel.shape)]
    args = [y, y, y, y, y, w2, ba, ng, lf, lb, sel]
    if use_rope:
        cos, sin = _rope_tables(T)
        in_specs += [cst(cos.shape), cst(sin.shape)]
        args += [cos, sin]
    if has_s0:
        in_specs += [pl.BlockSpec((None, 2, None, GLA_DK, GLA_DV), lambda b, h: (b, 0, h, 0, 0)),
                     pl.BlockSpec(memory_space=pl.ANY)]
        args += [s0, o_prev]
    aliases = {len(args) - 1: 0} if has_s0 else {}
    out_specs = [pl.BlockSpec((T, GLA_DV), lambda b, h: (rb + b, h))]
    out_shape = [jax.ShapeDtypeStruct((N_TOK, GLA_H * GLA_DV), F32)]
    if want_state:
        out_specs.append(pl.BlockSpec((None, 2, None, GLA_DK, GLA_DV), lambda b, h: (b, 0, h, 0, 0)))
        out_shape.append(jax.ShapeDtypeStruct((n_seq, 2, GLA_H, GLA_DK, GLA_DV), F32))
    f32s = lambda w: pltpu.VMEM((T, w), F32)
    bf16s = lambda w: pltpu.VMEM((T, w), BF16)
    scratch = ([f32s(GLA_DK)] * 6 + [bf16s(GLA_DK)] * 4
               + [bf16s(GLA_DV), bf16s(GLA_CHUNK * GLA_DK), f32s(GLA_DV),
                  pltpu.VMEM((GLA_DV, GLA_DK), F32), pltpu.VMEM((GLA_DV, GLA_DK), F32)])
    outs = pl.pallas_call(
        functools.partial(_gla_kernel, use_rope=use_rope, has_s0=has_s0, want_state=want_state),
        grid=(n_seq, GLA_H),
        in_specs=in_specs, out_specs=out_specs, out_shape=out_shape,
        scratch_shapes=scratch,
        input_output_aliases=aliases,
        compiler_params=pltpu.CompilerParams(dimension_semantics=("arbitrary", "arbitrary"),
                                             vmem_limit_bytes=VMEM_LIMIT),
        name="gla_rope" if use_rope else "gla",
    )(*args)
    return (outs[0], outs[1]) if want_state else (outs[0], None)


def _gla_seq_kernel(*refs, use_rope, has_s0, want_state):
    it = iter(refs)
    q_ref, k_ref, v_ref, r_ref, za_ref, w2_ref, ba_ref, ng_ref, lf_ref, lb_ref, sel_ref = (next(it) for _ in range(11))
    cos_ref, sin_ref = (next(it), next(it)) if use_rope else (None, None)
    s0_ref = next(it) if has_s0 else None
    if has_s0:
        next(it)
    o_ref = next(it)
    st_ref = next(it) if want_state else None
    q_s, k_s, bf_s, bb_s, qdf_s, kdf_s, qdb_s, kdb_s, w_s, sf_s, sb_s = it

    T = q_ref.shape[0]
    nblk, nch = T // GLA_BLK, T // GLA_CHUNK
    hq = lambda h: slice(h * GLA_DK, (h + 1) * GLA_DK)
    hv = lambda h: slice(h * GLA_DV, (h + 1) * GLA_DV)

    for blk in range(nblk):
        rows = slice(blk * GLA_BLK, (blk + 1) * GLA_BLK)
        zab = za_ref[rows, :].astype(BF16)
        for h in range(GLA_H):
            z = jnp.dot(zab, w2_ref[h], preferred_element_type=F32) + ba_ref[h]
            la = (jnp.minimum(z, 0.0) - jnp.log1p(jnp.exp(-jnp.abs(z)))) * (1.0 / GLA_TAU)
            cf = _dot3(lf_ref[...], la[:, :GLA_DK])
            cb = _dot3(lb_ref[...], la[:, GLA_DK:])
            bf, bb = cf[:GLA_BLK], cb[:GLA_BLK]
            q = q_ref[rows, hq(h)] * (GLA_DK ** -0.5)
            k = k_ref[rows, hq(h)]
            if use_rope:
                q = _rope_apply(q, cos_ref[rows, :], sin_ref[rows, :])
                k = _rope_apply(k, cos_ref[rows, :], sin_ref[rows, :])
            q_s[h, rows, :], k_s[h, rows, :] = q, k
            bf_s[h, rows, :], bb_s[h, rows, :] = bf, bb
            qdf_s[h, rows, :] = (q * jnp.exp(bf)).astype(BF16)
            qdb_s[h, rows, :] = (q * jnp.exp(bb)).astype(BF16)
            kdf_s[h, rows, :] = (k * jnp.exp(cf[GLA_BLK:])).astype(BF16)
            kdb_s[h, rows, :] = (k * jnp.exp(cb[GLA_BLK:])).astype(BF16)

    irow = lax.broadcasted_iota(jnp.int32, (GLA_CHUNK, 1), 0)
    lane_c = lax.broadcasted_iota(jnp.int32, (GLA_BLK, LANE), 1) // GLA_CHUNK
    row_c = lax.broadcasted_iota(jnp.int32, (GLA_BLK, LANE), 0) // GLA_CHUNK
    for h in range(GLA_H):
        def intra_chunk(c, carry, h=h):
            rows = pl.ds(pl.multiple_of(c * GLA_CHUNK, GLA_CHUNK), GLA_CHUNK)
            qc, bfc, bbc = q_s[h, rows, :], bf_s[h, rows, :], bb_s[h, rows, :]
            for j in range(GLA_CHUNK):
                row = pl.ds(c * GLA_CHUNK + j, 1)
                e = (jnp.exp(jnp.where(irow >= j, bfc - bf_s[h, row, :], bbc - bb_s[h, row, :]))
                     + jnp.where(irow == j, 1.0, 0.0))
                w_s[rows, j * GLA_DK:(j + 1) * GLA_DK] = (qc * k_s[h, row, :] * e).astype(BF16)
            return carry

        lax.fori_loop(0, nch, intra_chunk, 0)
        a_all = jnp.dot(w_s[...], sel_ref[...], preferred_element_type=F32)
        for blk in range(nblk):
            rows = slice(blk * GLA_BLK, (blk + 1) * GLA_BLK)
            a = jnp.where(lane_c == row_c, a_all[rows, :], 0.0).astype(BF16)
            o_ref[rows, hv(h)] = jnp.dot(a, v_ref[rows, hv(h)].astype(BF16), preferred_element_type=F32)

    for h in range(GLA_H):
        if has_s0:
            sf_s[h] = s0_ref[0, h].T
            sb_s[h] = s0_ref[1, h].T
        else:
            sf_s[h] = jnp.zeros((GLA_DV, GLA_DK), F32)
            sb_s[h] = jnp.zeros((GLA_DV, GLA_DK), F32)

    def chain(h, rows, g_row, qd_s, kd_s, b_s, st_s):
        s = st_s[h]
        o_ref[rows, hv(h)] += _dot_nt(qd_s[h, rows, :], s.astype(BF16))
        u = lax.dot_general(v_ref[rows, hv(h)].astype(BF16), kd_s[h, rows, :], (((0,), (0,)), ((), ())),
                            preferred_element_type=F32)
        st_s[h] = s * jnp.exp(b_s[h, g_row, :]) + u

    def inter_chunk(n, carry):
        cf_ = pl.multiple_of(n * GLA_CHUNK, GLA_CHUNK)
        cb_ = pl.multiple_of((nch - 1 - n) * GLA_CHUNK, GLA_CHUNK)
        for h in range(GLA_H):
            chain(h, pl.ds(cf_, GLA_CHUNK), pl.ds(cf_ + GLA_CHUNK - 1, 1), qdf_s, kdf_s, bf_s, sf_s)
            chain(h, pl.ds(cb_, GLA_CHUNK), pl.ds(cb_, 1), qdb_s, kdb_s, bb_s, sb_s)
        return carry

    lax.fori_loop(0, nch, inter_chunk, 0, unroll=2)

    for h in range(GLA_H):
        o = o_ref[:, hv(h)]
        o = o * lax.rsqrt(jnp.mean(o * o, axis=-1, keepdims=True) + EPS) * ng_ref[h]
        r = r_ref[:, hv(h)]
        o_ref[:, hv(h)] = o * (r * jax.nn.sigmoid(r))
        if want_state:
            st_ref[0, h] = sf_s[h].T
            st_ref[1, h] = sb_s[h].T


def _gla_seq_call(y, w2, ba, ng, T, n_seq, row0, use_rope, s0, o_prev=None):
    has_s0 = s0 is not None
    want_state = not has_s0
    rb = row0 // T
    nq, nv = GLA_H * GLA_DK, GLA_H * GLA_DV
    mode = dict(pipeline_mode=pl.Buffered(1)) if n_seq <= 2 else {}
    col = lambda w, j: pl.BlockSpec((T, w), lambda b: (rb + b, j), **mode)
    cst = lambda a: pl.BlockSpec(a.shape, lambda b: (0,) * a.ndim)
    lf, lb, sel = _gla_consts()
    in_specs = [col(nq, 0), col(nq, 1), col(nv, 1), col(nv, 2), col(LANE, GLA_ZCOL),
                cst(w2), cst(ba), cst(ng), cst(lf), cst(lb), cst(sel)]
    args = [y, y, y, y, y, w2, ba, ng, lf, lb, sel]
    if use_rope:
        cos, sin = _rope_tables(T)
        in_specs += [cst(cos), cst(sin)]
        args += [cos, sin]
    st_spec = pl.BlockSpec((None, 2, GLA_H, GLA_DK, GLA_DV), lambda b: (b, 0, 0, 0, 0))
    if has_s0:
        in_specs += [st_spec, pl.BlockSpec(memory_space=pl.ANY)]
        args += [s0, o_prev]
    aliases = {len(args) - 1: 0} if has_s0 else {}
    out_specs = [pl.BlockSpec((T, nv), lambda b: (rb + b, 0))]
    out_shape = [jax.ShapeDtypeStruct((N_TOK, nv), F32)]
    if want_state:
        out_specs.append(st_spec)
        out_shape.append(jax.ShapeDtypeStruct((n_seq, 2, GLA_H, GLA_DK, GLA_DV), F32))
    scratch = ([pltpu.VMEM((GLA_H, T, GLA_DK), F32)] * 4 + [pltpu.VMEM((GLA_H, T, GLA_DK), BF16)] * 4
               + [pltpu.VMEM((T, GLA_CHUNK * GLA_DK), BF16),
                  pltpu.VMEM((GLA_H, GLA_DV, GLA_DK), F32), pltpu.VMEM((GLA_H, GLA_DV, GLA_DK), F32)])
    outs = pl.pallas_call(
        functools.partial(_gla_seq_kernel, use_rope=use_rope, has_s0=has_s0, want_state=want_state),
        grid=(n_seq,),
        in_specs=in_specs, out_specs=out_specs, out_shape=out_shape,
        scratch_shapes=scratch,
        input_output_aliases=aliases,
        compiler_params=pltpu.CompilerParams(dimension_semantics=("arbitrary",),
                                             vmem_limit_bytes=VMEM_LIMIT),
        name="gla_rope" if use_rope else "gla",
    )(*args)
    return (outs[0], outs[1]) if want_state else (outs[0], None)


def _gla_mixer(y, state, w_a2, b_a, norm_g):
    w2 = jnp.zeros((GLA_H, LANE, 2 * GLA_DK), F32)
    for e in range(2):
        we = w_a2[e].reshape(GLA_RANK, GLA_H, GLA_DK).transpose(1, 0, 2)
        w2 = w2.at[:, e * GLA_RANK:(e + 1) * GLA_RANK, e * GLA_DK:(e + 1) * GLA_DK].set(we)
    ba = b_a.reshape(2, GLA_H, GLA_DK).transpose(1, 0, 2).reshape(GLA_H, 1, 2 * GLA_DK)
    ng = norm_g.reshape(GLA_H, 1, GLA_DV)
    op, st = _gla_seq_call(y, w2.astype(BF16), ba, ng, SEQ, BATCH, 0, False, None)
    o, _ = _gla_seq_call(y, w2.astype(BF16), ba, ng, DEC_SEQ, DEC_BATCH, N_PROMPT, True, state, op)
    return o, st


N_HEAD_PAIRS = NAT_H // 2
NAT_ROWS = DEC_SEQ // GRID_W
NAT_WIN = NAT_WH * GRID_W
NAT_CLS = NAT_WH


def _nat_row_window(r):
    rs = min(max(r - NAT_WH // 2, 0), NAT_ROWS - NAT_WH)
    return rs, r - rs


NAT_NDR = 2 * NAT_WH - 1
NAT_NDC = 2 * NAT_WW - 1


def _nat_bias_table(rpb):
    qc = np.arange(GRID_W)[:, None]
    kc = np.arange(GRID_W)[None, :]
    c_start = np.clip(qc - NAT_WW // 2, 0, GRID_W - NAT_WW)
    ok = (kc >= c_start) & (kc < c_start + NAT_WW)
    dc = np.clip(kc - qc + NAT_WW - 1, 0, NAT_NDC - 1)
    pick = jnp.asarray(dc[None] == np.arange(NAT_NDC)[:, None, None], F32)
    rows2 = jnp.stack([rpb[:, :NAT_NDR - 1], rpb[:, 1:]], axis=2)
    t = jnp.einsum('hdsx,xqk->hdqsk', rows2, pick, precision=lax.Precision.HIGHEST)
    t = jnp.where(ok[None, None, :, None, :], t, NEG_INF)
    return t.reshape(NAT_H, NAT_NDR - 1, GRID_W, 2 * GRID_W)


def _head_mask(hh):
    lane = lax.broadcasted_iota(jnp.int32, (1, LANE), 1)
    return (lane < NAT_HD) if hh == 0 else (lane >= NAT_HD)


def _dot_nt(a, b):
    return lax.dot_general(a, b, (((1,), (1,)), ((), ())), preferred_element_type=F32)


def _nat_ctx_kernel(q_ref, k_ref, v_ref, o_ref, kc_ref, vc_ref):
    for hp in range(N_HEAD_PAIRS):
        cols = slice(hp * LANE, (hp + 1) * LANE)
        q = q_ref[:, cols] * (NAT_HD ** -0.5)
        k, v = k_ref[:, cols], v_ref[:, cols]
        kb, vb = k.astype(BF16), v.astype(BF16)
        q2 = jnp.concatenate([jnp.where(_head_mask(hh), q, 0.0) for hh in range(2)], axis=0).astype(BF16)
        s = _dot_nt(q2, kb)
        p = jnp.exp(s - jnp.max(s, axis=-1, keepdims=True))
        l = jnp.sum(p, axis=-1, keepdims=True)
        o2 = jnp.dot(p.astype(BF16), vb, preferred_element_type=F32) / l
        o_ref[:, cols] = jnp.where(_head_mask(0), o2[:SEQ], o2[SEQ:])
        for hh in range(2):
            kc_ref[2 * hp + hh] = k[:, hh * NAT_HD:(hh + 1) * NAT_HD]
            vc_ref[2 * hp + hh] = v[:, hh * NAT_HD:(hh + 1) * NAT_HD]


def _nat_context(y):
    blk = lambda j: pl.BlockSpec((SEQ, D_MODEL), lambda b: (b, j))
    cache = pl.BlockSpec((None, NAT_H, SEQ, NAT_HD), lambda b: (b, 0, 0, 0))
    cache_shape = jax.ShapeDtypeStruct((BATCH, NAT_H, SEQ, NAT_HD), F32)
    return pl.pallas_call(
        _nat_ctx_kernel,
        grid=(BATCH,),
        in_specs=[blk(0), blk(1), blk(2)],
        out_specs=[pl.BlockSpec((SEQ, D_MODEL), lambda b: (b, 0)), cache, cache],
        out_shape=[jax.ShapeDtypeStruct((N_TOK, D_MODEL), F32), cache_shape, cache_shape],
        compiler_params=pltpu.CompilerParams(dimension_semantics=("arbitrary",)),
        name="nat_context",
    )(y, y, y)


def _nat_lat_kernel(q_ref, k_ref, v_ref, ck_ref, cv_ref, tab_ref, o_in_ref, o_ref):
    del o_in_ref
    q = q_ref[...] * (NAT_HD ** -0.5)
    qm = [jnp.where(_head_mask(hh), q, 0.0).astype(BF16) for hh in range(2)]
    ckb = ck_ref[...].astype(BF16)
    cvb = cv_ref[...].astype(BF16)
    for r in range(NAT_ROWS):
        rs, cls = _nat_row_window(r)
        kw = k_ref[rs * GRID_W:rs * GRID_W + NAT_WIN, :].astype(BF16)
        vw = v_ref[rs * GRID_W:rs * GRID_W + NAT_WIN, :].astype(BF16)
        qr = jnp.concatenate([qm[hh][r * GRID_W:(r + 1) * GRID_W] for hh in range(2)], axis=0)
        bias = jnp.concatenate(
            [jnp.concatenate([tab_ref[hh, w - cls + NAT_WH - 1] for w in range(0, NAT_WH, 2)], axis=1)
             for hh in range(2)], axis=0)
        s_lat = _dot_nt(qr, kw) + bias
        s_ctx = _dot_nt(qr, ckb)
        m = jnp.maximum(jnp.max(s_lat, axis=-1, keepdims=True), jnp.max(s_ctx, axis=-1, keepdims=True))
        p_lat = jnp.exp(s_lat - m)
        p_ctx = jnp.exp(s_ctx - m)
        l = jnp.sum(p_lat, axis=-1, keepdims=True) + jnp.sum(p_ctx, axis=-1, keepdims=True)
        o2 = (jnp.dot(p_lat.astype(BF16), vw, preferred_element_type=F32)
              + jnp.dot(p_ctx.astype(BF16), cvb, preferred_element_type=F32)) / l
        o_ref[r * GRID_W:(r + 1) * GRID_W, :] = jnp.where(_head_mask(0), o2[:GRID_W], o2[GRID_W:])


def _nat_latent(y, o_ctx, ck, cv, rpb):
    row0 = N_PROMPT // DEC_SEQ
    blk = lambda off: pl.BlockSpec((DEC_SEQ, LANE), lambda b, hp: (row0 + b, off + hp))
    ctx = pl.BlockSpec((None, ck.shape[1], LANE), lambda b, hp: (b, 0, hp))
    return pl.pallas_call(
        _nat_lat_kernel,
        grid=(DEC_BATCH, N_HEAD_PAIRS),
        in_specs=[blk(0), blk(N_HEAD_PAIRS), blk(2 * N_HEAD_PAIRS), ctx, ctx,
                  pl.BlockSpec((2, NAT_NDR - 1, GRID_W, 2 * GRID_W), lambda b, hp: (hp, 0, 0, 0)),
                  pl.BlockSpec(memory_space=pl.ANY)],
        out_specs=pl.BlockSpec((DEC_SEQ, LANE), lambda b, hp: (row0 + b, hp)),
        out_shape=jax.ShapeDtypeStruct((N_TOK, D_MODEL), F32),
        input_output_aliases={6: 0},
        compiler_params=pltpu.CompilerParams(dimension_semantics=("arbitrary", "arbitrary")),
        name="nat_latent",
    )(y, y, y, ck, cv, _nat_bias_table(rpb), o_ctx)


def _heads_last(t):
    b, h, s, d = t.shape
    return t.transpose(0, 2, 1, 3).reshape(b, s, h * d)


def _heads_first(t):
    b, s, _ = t.shape
    return t.reshape(b, s, NAT_H, NAT_HD).transpose(0, 2, 1, 3)


def _gmlp_kernel(x_ref, y_ref, mod_ref, lng_ref, lnb_ref, ws_ref, bs_ref, w_ref, o_ref, t_ref):
    tm = x_ref.shape[0]
    v = y_ref[:, GM_DH:]
    vc = v - jnp.mean(v, axis=-1, keepdims=True)
    vn = vc * lax.rsqrt(jnp.mean(vc * vc, axis=-1, keepdims=True) + EPS) * lng_ref[...] + lnb_ref[...]
    vn = vn.astype(BF16)
    for n in range(tm // GM_CHUNK):
        rows = slice(n * GM_CHUNK, (n + 1) * GM_CHUNK)
        for g in range(GM_G):
            cols = slice(g * GM_CG, (g + 1) * GM_CG)
            sp = jnp.dot(ws_ref[g], vn[rows, cols], preferred_element_type=F32) + bs_ref[:, cols]
            t_ref[rows, cols] = (y_ref[rows, cols] * sp).astype(BF16)
    gate = mod_ref[5:6, :]
    o_ref[...] = x_ref[...] + gate * jnp.dot(t_ref[...], w_ref[...], preferred_element_type=F32)


def _gmlp_out(x, y, mod_l, ln_g, ln_b, w_s, b_s, w_out):
    tm = TM_PROJ
    bias = jnp.repeat(b_s.T, GM_CG, axis=1)
    return pl.pallas_call(
        _gmlp_kernel,
        grid=(N_TOK // tm,),
        in_specs=[
            pl.BlockSpec((tm, D_MODEL), lambda i: (i, 0)),
            pl.BlockSpec((tm, 2 * GM_DH), lambda i: (i, 0)),
            pl.BlockSpec((None, N_MOD, D_MODEL), lambda i: (_group_of_tile(i, tm), 0, 0)),
            pl.BlockSpec((1, GM_DH), lambda i: (0, 0)),
            pl.BlockSpec((1, GM_DH), lambda i: (0, 0)),
            _resident((GM_G, GM_CHUNK, GM_CHUNK)),
            _resident((GM_CHUNK, GM_DH)),
            _resident((GM_DH, D_MODEL)),
        ],
        out_specs=pl.BlockSpec((tm, D_MODEL), lambda i: (i, 0)),
        out_shape=jax.ShapeDtypeStruct((N_TOK, D_MODEL), F32),
        scratch_shapes=[pltpu.VMEM((tm, GM_DH), BF16)],
        compiler_params=pltpu.CompilerParams(dimension_semantics=("arbitrary",),
                                             vmem_limit_bytes=VMEM_LIMIT),
        name="gmlp_gate_out",
    )(x, y, mod_l, ln_g.reshape(1, GM_DH), ln_b.reshape(1, GM_DH), w_s.astype(BF16), bias,
      w_out.astype(BF16))


def _dwconv_centred(x, w, b):
    K = w.shape[0]
    T = x.shape[1]
    pad = K // 2
    xp = jnp.pad(x, ((0, 0), (pad, pad), (0, 0)))
    out = xp[:, 0:T] * w[0]
    for i in range(1, K):
        out = out + xp[:, i:i + T] * w[i]
    return out + b


def _ssd_chunked(x, dt, a, bm, cm, s0):
    B_, T = x.shape[:2]
    L = SSD_CHUNK
    n = T // L
    E = SSD_H // SSD_G
    x = x.reshape(B_, n, L, SSD_G, E, SSD_P)
    dt = dt.reshape(B_, n, L, SSD_G, E)
    bm = bm.reshape(B_, n, L, SSD_G, SSD_N)
    cm = cm.reshape(B_, n, L, SSD_G, SSD_N)
    cum = jnp.cumsum(dt * a.reshape(SSD_G, E), axis=2)
    tril = jnp.tril(jnp.ones((L, L), dtype=bool))
    seg = cum[:, :, :, None] - cum[:, :, None, :]
    decay = jnp.exp(jnp.where(tril[:, :, None, None], seg, -jnp.inf))
    dtx = x * dt[..., None]
    cb = jnp.einsum('bnigs,bnjgs->bnijg', cm, bm)
    y_diag = jnp.einsum('bnijg,bnijge,bnjgep->bnigep', cb, decay, dtx)
    u = jnp.einsum('bnjgs,bnjge,bnjgep->bngeps', bm, jnp.exp(cum[:, :, -1:] - cum), dtx)
    chunk_decay = jnp.exp(cum[:, :, -1])
    q_decay = jnp.exp(cum)

    def step(s, xs):
        c_n, qd_n, u_n, g_n = xs
        y = jnp.einsum('bigs,bige,bgeps->bigep', c_n, qd_n, s)
        return g_n[..., None, None] * s + u_n, y

    sw = lambda t: jnp.swapaxes(t, 0, 1)
    s_fin, y_off = lax.scan(step, s0.reshape(B_, SSD_G, E, SSD_P, SSD_N),
                            (sw(cm), sw(q_decay), sw(u), sw(chunk_decay)))
    y = y_diag + sw(y_off)
    return y.reshape(B_, T, SSD_H, SSD_P), s_fin.reshape(B_, SSD_H, SSD_P, SSD_N)


def _ssd_core(y, s0, conv_w, conv_b, dt_bias, a_log, d_skip, norm_g):
    B_, T, _ = y.shape
    z = y[..., :SSD_DI]
    xbc = y[..., SSD_DI:SSD_DI + SSD_XBC]
    dt = y[..., SSD_DI + SSD_XBC:SSD_IN]
    xbc = jax.nn.silu(_dwconv_centred(xbc, conv_w, conv_b))
    x = xbc[..., :SSD_DI].reshape(B_, T, SSD_H, SSD_P)
    bm = xbc[..., SSD_DI:SSD_DI + SSD_G * SSD_N].reshape(B_, T, SSD_G, SSD_N)
    cm = xbc[..., SSD_DI + SSD_G * SSD_N:].reshape(B_, T, SSD_G, SSD_N)
    dt = jax.nn.softplus(dt.reshape(B_, T, 2, SSD_H) + dt_bias)
    a = -jnp.exp(a_log)
    y_f, s_f = _ssd_chunked(x, dt[:, :, 0], a[0], bm, cm, s0[:, 0])
    y_b, s_b = _ssd_chunked(_flip_t(x), _flip_t(dt[:, :, 1]), a[1], _flip_t(bm), _flip_t(cm), s0[:, 1])
    yy = y_f + _flip_t(y_b) + d_skip[:, None] * x
    yy = _rmsnorm(yy.reshape(B_, T, SSD_DI) * jax.nn.silu(z), norm_g)
    return yy, jnp.stack([s_f, s_b], axis=1)


SSD_E = SSD_H // SSD_G
SSD_GP = SSD_E * SSD_P
SSD_BLK = 2 * SSD_CHUNK
SSD_COL_X = SSD_DI
SSD_COL_B = 2 * SSD_DI
SSD_COL_C = SSD_COL_B + SSD_G * SSD_N
SSD_COL_DT = SSD_COL_C + SSD_G * SSD_N
SSD_PROJ = SSD_COL_DT + SSD_G * LANE


def _ssd_w_in(w_in):
    base = SSD_DI + SSD_XBC
    w_dt = jnp.zeros((D_MODEL, SSD_G, LANE), F32)
    for g in range(SSD_G):
        cols = jnp.concatenate([w_in[:, base + g * SSD_E:base + (g + 1) * SSD_E],
                                w_in[:, base + SSD_H + g * SSD_E:base + SSD_H + (g + 1) * SSD_E]], axis=1)
        w_dt = w_dt.at[:, g, :2 * SSD_E].set(cols).at[:, g, 2 * SSD_E:4 * SSD_E].set(cols)
    return jnp.concatenate([w_in[:, :base], w_dt.reshape(D_MODEL, SSD_G * LANE)], axis=1)


def _ssd_consts():
    r = np.arange(SSD_BLK)
    same = (r[:, None] // SSD_CHUNK) == (r[None, :] // SSD_CHUNK)
    cum = np.concatenate([same & (r[None, :] <= r[:, None]), same & (r[None, :] >= r[:, None])], axis=0)
    return jnp.asarray(cum, BF16)


def _softplus(x):
    return jnp.maximum(x, 0.0) + jnp.log1p(jnp.exp(-jnp.abs(x)))


def _ssd_kernel(*refs, has_s0, want_state):
    it = iter(refs)
    (z_ref, x_ref, b_ref, c_ref, dt_ref, cwx_ref, cwb_ref, cwc_ref, cbx_ref, cbb_ref, cbc_ref,
     dtb_ref, alog_ref, dsk_ref, cum_ref) = (next(it) for _ in range(15))
    s0_ref = next(it) if has_s0 else None
    if has_s0:
        next(it)
    o_ref = next(it)
    st_ref = next(it) if want_state else None
    xs_s, xb_s, bm_s, cm_s, cu_s, dt_s, y_s, sf_s, sb_s = it

    T = x_ref.shape[0]
    nch = T // SSD_CHUNK
    L = SSD_CHUNK

    trow = lax.broadcasted_iota(jnp.int32, (T, 1), 0)

    def conv_silu(v_ref, w_ref, bias_ref):
        v = v_ref[...]
        prev = jnp.where(trow == 0, 0.0, pltpu.roll(v, 1, axis=0))
        nxt = jnp.where(trow == T - 1, 0.0, pltpu.roll(v, T - 1, axis=0))
        y = prev * w_ref[0:1, :] + v * w_ref[1:2, :] + nxt * w_ref[2:3, :] + bias_ref[...]
        return y * jax.nn.sigmoid(y)

    xs = conv_silu(x_ref, cwx_ref, cbx_ref)
    xs_s[...] = xs
    xb_s[...] = xs.astype(BF16)
    bm_s[...] = conv_silu(b_ref, cwb_ref, cbb_ref).astype(BF16)
    cm_s[...] = conv_silu(c_ref, cwc_ref, cbc_ref).astype(BF16)

    lane1 = lax.broadcasted_iota(jnp.int32, (1, LANE), 1)
    a_row = jnp.where(lane1 < 2 * SSD_E, -jnp.exp(alog_ref[...]), 0.0)
    for blk in range(T // SSD_BLK):
        rows = slice(blk * SSD_BLK, (blk + 1) * SSD_BLK)
        dt = _softplus(dt_ref[rows, :] + dtb_ref[...])
        c2 = _dot3(cum_ref[...], dt * a_row)
        cu_s[rows, :] = jnp.where(lane1 < SSD_E, c2[:SSD_BLK], c2[SSD_BLK:])
        dt_s[rows, :] = dt

    ii = lax.broadcasted_iota(jnp.int32, (L, LANE), 0)
    jj = lax.broadcasted_iota(jnp.int32, (L, LANE), 1)
    fwd_half = jj < L
    fwd_half1 = lane1 < L
    tri = (fwd_half & (ii >= jj)) | ((jj >= L) & (ii <= jj - L))
    left = lane1 < SSD_P
    for c in range(nch):
        rows = slice(c * L, (c + 1) * L)
        cum_c, dt_c = cu_s[rows, :], dt_s[rows, :]
        bm_c, cm_c = bm_s[rows, :], cm_s[rows, :]
        cb2 = _dot_nt(cm_c, jnp.concatenate([bm_c, bm_c], axis=0))
        arr = jnp.where(lane1 < 2 * SSD_E, cum_c, dt_c)
        arr_t = jnp.concatenate([arr, arr], axis=0).T
        gs = []
        for e in range(SSD_E):
            row_c = jnp.where(fwd_half1, arr_t[e:e + 1, :], arr_t[SSD_E + e:SSD_E + e + 1, :])
            row_dt = jnp.where(fwd_half1, arr_t[2 * SSD_E + e:2 * SSD_E + e + 1, :],
                               arr_t[3 * SSD_E + e:3 * SSD_E + e + 1, :])
            col_c = jnp.where(fwd_half, jnp.broadcast_to(cum_c[:, e:e + 1], (L, LANE)),
                              jnp.broadcast_to(cum_c[:, SSD_E + e:SSD_E + e + 1], (L, LANE)))
            dec = jnp.exp(jnp.where(tri, col_c - row_c, -jnp.inf))
            gs.append((cb2 * dec * row_dt).astype(BF16))
        for pr in range(SSD_E // 2):
            cols = slice(pr * LANE, (pr + 1) * LANE)
            xp = xb_s[rows, cols]
            xl = jnp.where(left, xp, jnp.zeros_like(xp))
            xr = jnp.where(left, jnp.zeros_like(xp), xp)
            lhs = jnp.concatenate([gs[2 * pr], gs[2 * pr + 1]], axis=1)
            rhs = jnp.concatenate([xl, xl, xr, xr], axis=0)
            y_s[rows, cols] = (jnp.dot(lhs, rhs, preferred_element_type=F32)
                               + dsk_ref[:, cols] * xs_s[rows, cols])

    if has_s0:
        sf_s[...] = s0_ref[0].T
        sb_s[...] = s0_ref[1].T
    else:
        sf_s[...] = jnp.zeros_like(sf_s)
        sb_s[...] = jnp.zeros_like(sb_s)

    def per_head_cols(v, lane0):
        tiles = []
        for t in range(SSD_E // 2):
            a, b = lane0 + 2 * t, lane0 + 2 * t + 1
            tiles.append(jnp.where(left, jnp.broadcast_to(v[:, a:a + 1], (L, LANE)),
                                   jnp.broadcast_to(v[:, b:b + 1], (L, LANE))))
        return jnp.concatenate(tiles, axis=1)

    def chain(c, last, lane0, st_s):
        rows = slice(c * L, (c + 1) * L)
        cum_c = jnp.where((lane1 >= lane0) & (lane1 < lane0 + SSD_E), cu_s[rows, :], 0.0)
        tot = cum_c[last:last + 1, :]
        qd = per_head_cols(jnp.exp(cum_c), lane0)
        w = per_head_cols(jnp.exp(tot - cum_c) * dt_s[rows, :], lane0)
        s = st_s[...]
        y_s[rows, :] += jnp.dot(cm_s[rows, :], s.astype(BF16), preferred_element_type=F32) * qd
        xw = (xs_s[rows, :] * w).astype(BF16)
        u = lax.dot_general(bm_s[rows, :], xw, (((0,), (0,)), ((), ())), preferred_element_type=F32)
        st_s[...] = s * qd[last:last + 1, :] + u

    for n in range(nch):
        chain(n, L - 1, 0, sf_s)
        chain(nch - 1 - n, 0, SSD_E, sb_s)

    z = z_ref[...]
    o_ref[...] = y_s[...] * (z * jax.nn.sigmoid(z))
    if want_state:
        st_ref[0] = sf_s[...].T
        st_ref[1] = sb_s[...].T


def _ssd_call(y, prm, T, n_seq, row0, s0, o_prev=None):
    has_s0 = s0 is not None
    want_state = not has_s0
    rb = row0 // T
    col = lambda w, off: pl.BlockSpec((T, w), lambda b, g: (rb + b, off // w + g))
    cst = lambda shape: pl.BlockSpec(shape, lambda b, g: (0,) * len(shape))
    wcol = lambda rows, w, off: pl.BlockSpec((rows, w), lambda b, g: (0, off // w + g))
    per_g = lambda w: pl.BlockSpec((None, 1, w), lambda b, g: (g, 0, 0))
    cum = _ssd_consts()
    xoff, boff, coff = 0, SSD_DI, SSD_DI + SSD_G * SSD_N
    in_specs = [col(SSD_GP, 0), col(SSD_GP, SSD_COL_X), col(SSD_N, SSD_COL_B), col(SSD_N, SSD_COL_C),
                col(LANE, SSD_COL_DT),
                wcol(SSD_CONV, SSD_GP, xoff), wcol(SSD_CONV, SSD_N, boff), wcol(SSD_CONV, SSD_N, coff),
                wcol(1, SSD_GP, xoff), wcol(1, SSD_N, boff), wcol(1, SSD_N, coff),
                per_g(LANE), per_g(LANE), per_g(SSD_GP),
                cst(cum.shape)]
    args = [y, y, y, y, y, prm["conv_w"], prm["conv_w"], prm["conv_w"], prm["conv_b"], prm["conv_b"], prm["conv_b"],
            prm["dt_bias"], prm["a_log"], prm["d_skip"], cum]
    st_spec = pl.BlockSpec((None, 2, None, SSD_GP, SSD_N), lambda b, g: (b, 0, g, 0, 0))
    if has_s0:
        in_specs += [st_spec, pl.BlockSpec(memory_space=pl.ANY)]
        args += [s0, o_prev]
    aliases = {len(args) - 1: 0} if has_s0 else {}
    out_specs = [pl.BlockSpec((T, SSD_GP), lambda b, g: (rb + b, g))]
    out_shape = [jax.ShapeDtypeStruct((N_TOK, SSD_DI), F32)]
    if want_state:
        out_specs.append(st_spec)
        out_shape.append(jax.ShapeDtypeStruct((n_seq, 2, SSD_G, SSD_GP, SSD_N), F32))
    scratch = [pltpu.VMEM((T, SSD_GP), F32), pltpu.VMEM((T, SSD_GP), BF16),
               pltpu.VMEM((T, SSD_N), BF16), pltpu.VMEM((T, SSD_N), BF16),
               pltpu.VMEM((T, LANE), F32), pltpu.VMEM((T, LANE), F32), pltpu.VMEM((T, SSD_GP), F32),
               pltpu.VMEM((SSD_N, SSD_GP), F32), pltpu.VMEM((SSD_N, SSD_GP), F32)]
    outs = pl.pallas_call(
        functools.partial(_ssd_kernel, has_s0=has_s0, want_state=want_state),
        grid=(n_seq, SSD_G),
        in_specs=in_specs, out_specs=out_specs, out_shape=out_shape,
        scratch_shapes=scratch,
        input_output_aliases=aliases,
        compiler_params=pltpu.CompilerParams(dimension_semantics=("arbitrary", "arbitrary"),
                                             vmem_limit_bytes=VMEM_LIMIT),
        name="ssd_state" if has_s0 else "ssd",
    )(*args)
    return (outs[0], outs[1]) if want_state else (outs[0], None)


def _ssd_mixer(y, state, conv_w, conv_b, dt_bias, a_log, d_skip):
    def lanes(p):
        pg = p.reshape(2, SSD_G, SSD_E).transpose(1, 0, 2).reshape(SSD_G, 2 * SSD_E)
        return jnp.pad(jnp.concatenate([pg, pg], axis=1), ((0, 0), (0, LANE - 4 * SSD_E))).reshape(SSD_G, 1, LANE)
    prm = dict(conv_w=conv_w, conv_b=conv_b.reshape(1, SSD_XBC), dt_bias=lanes(dt_bias), a_log=lanes(a_log),
               d_skip=jnp.repeat(d_skip, SSD_P).reshape(SSD_G, 1, SSD_GP))
    op, st = _ssd_call(y, prm, SEQ, BATCH, 0, None)
    s0 = state.reshape(DEC_BATCH, 2, SSD_G, SSD_GP, SSD_N)
    o, _ = _ssd_call(y, prm, DEC_SEQ, DEC_BATCH, N_PROMPT, s0, op)
    return o, st.reshape(BATCH, 2, SSD_H, SSD_P, SSD_N)


def _pad_cols(w, mult):
    n = w.shape[1]
    n_pad = -n % mult
    return jnp.pad(w, ((0, 0), (0, n_pad))) if n_pad else w


def kernel(x_prompt, x_sample, state_gla, cache_nat_k, cache_nat_v, state_ssd, c,
           c_ctx, norm_g, w_ada, b_ada, w_ffn_in, w_ffn_out,
           gla_w_in, gla_w_a1, gla_w_a2, gla_b_a, gla_norm_g, gla_w_out,
           nat_w_qkv, nat_rpb, nat_w_out,
           gm_w_in, gm_ln_g, gm_ln_b, gm_w_s, gm_b_s, gm_w_out,
           ssd_w_in, ssd_conv_w, ssd_conv_b, ssd_dt_bias, ssd_a_log, ssd_d, ssd_norm_g, ssd_w_out,
           final_g):
    x = (x_prompt.reshape(N_PROMPT, D_MODEL), x_sample.reshape(N_SAMPLE, D_MODEL))
    mod = _modulation_all(c, c_ctx, w_ada, b_ada)
    w_in_all, w_out_all = w_ffn_in.astype(BF16), w_ffn_out.astype(BF16)
    new_gla, new_k, new_v, new_ssd = [], [], [], []
    for l in range(DEPTH):
        kind, j = l % N_MIXERS, l // N_MIXERS
        x = _ffn(x, mod[l], norm_g[l, 0], w_in_all, w_out_all, l, 0, 0)
        mix = None
        if kind == 0:
            w = jnp.concatenate([gla_w_in[j], gla_w_a1[j, 0], gla_w_a1[j, 1]], axis=1)
            y = _proj_in(x, mod[l], norm_g[l, 1], _pad_cols(w, 640), 640)
            o, st = _gla_mixer(y, state_gla[:, j], gla_w_a2[j], gla_b_a[j], gla_norm_g[j])
            new_gla.append(st)
            mix = (o, gla_w_out[j], None)
        elif kind == 1:
            y = _proj_in(x, mod[l], norm_g[l, 1], nat_w_qkv[j], 768)
            o, kc, vc = _nat_context(y)
            o = _nat_latent(y, o, _heads_last(cache_nat_k[:, j]), _heads_last(cache_nat_v[:, j]), nat_rpb[j])
            new_k.append(kc)
            new_v.append(vc)
            mix = (o, nat_w_out[j], None)
        elif kind == 2:
            y = _proj_in(x, mod[l], norm_g[l, 1], gm_w_in[j], 512, act="gelu")
            x = _gmlp_out(x, y, mod[l], gm_ln_g[j], gm_ln_b[j], gm_w_s[j], gm_b_s[j], gm_w_out[j])
        else:
            y = _proj_in(x, mod[l], norm_g[l, 1], _ssd_w_in(ssd_w_in[j]), 512)
            o, st = _ssd_mixer(y, state_ssd[:, j], ssd_conv_w[j], ssd_conv_b[j], ssd_dt_bias[j],
                               ssd_a_log[j], ssd_d[j])
            new_ssd.append(st)
            mix = (o, ssd_w_out[j], ssd_norm_g[j])
        x = _ffn(x, mod[l], norm_g[l, 2], w_in_all, w_out_all, l, 1, 2, mix=mix,
                 final_g=final_g if l == DEPTH - 1 else None)
    y_prompt = x[0].reshape(BATCH, SEQ, D_MODEL)
    y_sample = x[1].reshape(DEC_BATCH, DEC_SEQ, D_MODEL)
    return (y_prompt, y_sample, jnp.stack(new_gla, axis=1), jnp.stack(new_k, axis=1),
            jnp.stack(new_v, axis=1), jnp.stack(new_ssd, axis=1))
```

```python
import functools

import jax
import jax.numpy as jnp
import numpy as np
from jax import lax
from jax.experimental import pallas as pl
from jax.experimental.pallas import tpu as pltpu

D_MODEL = 1024
BATCH = 32
SEQ = 256
DEPTH = 4
DEC_BATCH = 2
DEC_SEQ = 1024
N_PROMPT = BATCH * SEQ
N_SAMPLE = DEC_BATCH * DEC_SEQ
N_TOK = N_PROMPT + N_SAMPLE
N_GROUPS = 1 + DEC_BATCH

GRID_W = 64
N_MIXERS = 4
N_SUB = 3
N_MOD = 3 * N_SUB
D_FF = 2816
EPS = 1e-6
NEG_INF = -1e30
ROPE_THETA = 10000.0
GLA_H, GLA_DK, GLA_DV, GLA_RANK, GLA_TAU, GLA_CHUNK = 4, 128, 256, 16, 16.0, 16
GLA_IN = 2 * GLA_H * GLA_DK + 2 * GLA_H * GLA_DV
NAT_H, NAT_HD, NAT_WH, NAT_WW = 16, 64, 8, 16
GM_DH, GM_G, GM_CHUNK = 1024, 8, 128
GM_CG = GM_DH // GM_G
SSD_DI = 2 * D_MODEL
SSD_P = 64
SSD_H = SSD_DI // SSD_P
SSD_N, SSD_G, SSD_CONV, SSD_CHUNK = 128, 4, 3, 64
SSD_XBC = SSD_DI + 2 * SSD_G * SSD_N

LANE = 128
VMEM_LIMIT = 56 * 1024 * 1024
BF16 = jnp.bfloat16
F32 = jnp.float32

FF_CHUNK = 256
N_FF_CHUNKS = D_FF // FF_CHUNK
TM_FFN = 512
TM_PROJ = 512
ADA_TK = 128


def _group_of_tile(i, tm):
    n_prompt_tiles = N_PROMPT // tm
    return jnp.where(i < n_prompt_tiles, 0, 1 + (i - n_prompt_tiles) // (DEC_SEQ // tm))


def _resident(shape):
    nd = len(shape)
    return pl.BlockSpec(shape, lambda i: (0,) * nd, pipeline_mode=pl.Buffered(1))


def _stream_rows(tm, width, single_buffer_latent=False):
    npt = N_PROMPT // tm
    mode = dict(pipeline_mode=pl.Buffered(1)) if single_buffer_latent else {}
    return (pl.BlockSpec((tm, width), lambda i: (jnp.minimum(i, npt - 1), 0)),
            pl.BlockSpec((tm, width), lambda i: (jnp.maximum(i - npt, 0), 0), **mode))


def _premod(x, g, mod_ref, k):
    shift = mod_ref[3 * k:3 * k + 1, :]
    scale = mod_ref[3 * k + 1:3 * k + 2, :]
    gate = mod_ref[3 * k + 2:3 * k + 3, :]
    ms = jnp.mean(x * x, axis=-1, keepdims=True)
    h = x * lax.rsqrt(ms + EPS) * g
    return h * (1.0 + scale) + shift, gate


def _dot_nt(a, b):
    return lax.dot_general(a, b, (((1,), (1,)), ((), ())), preferred_element_type=F32)


def _dot_tn(a, b):
    return lax.dot_general(a, b, (((0,), (0,)), ((), ())), preferred_element_type=F32)


def _split3(x):
    hi = x.astype(BF16)
    r1 = x - hi.astype(F32)
    mid = r1.astype(BF16)
    lo = (r1 - mid.astype(F32)).astype(BF16)
    return hi, mid, lo


def _dot3(m, x):
    hi, mid, lo = _split3(x)
    d = lambda p: jnp.dot(m, p, preferred_element_type=F32)
    return d(hi) + d(mid) + d(lo)


def _ada_kernel(cond_ref, wa_ref, wb_ref, b_ref, o_ref):
    cnd = cond_ref[...]
    s = (cnd * jax.nn.sigmoid(cnd)).astype(BF16)
    p = jnp.concatenate([jnp.dot(s, w_ref[...].astype(BF16), preferred_element_type=F32)
                         for w_ref in (wa_ref, wb_ref)], axis=1)

    @pl.when(pl.program_id(1) == 0)
    def _():
        o_ref[...] = p + b_ref[...]

    @pl.when(pl.program_id(1) > 0)
    def _():
        o_ref[...] += p


def _modulation_all(c, c_ctx, w_ada, b_ada):
    rows = 8
    nk = D_MODEL // ADA_TK
    cond = jnp.concatenate([c_ctx[None], c, jnp.zeros((rows - N_GROUPS, D_MODEL), F32)], axis=0)
    cond = cond.reshape(rows, nk, ADA_TK).transpose(1, 0, 2)
    n_out = N_MOD * D_MODEL
    out = pl.pallas_call(
        _ada_kernel,
        grid=(DEPTH, nk),
        in_specs=[
            pl.BlockSpec((None, rows, ADA_TK), lambda l, k: (k, 0, 0)),
            pl.BlockSpec((None, ADA_TK, n_out // 2), lambda l, k: (l, k, 0)),
            pl.BlockSpec((None, ADA_TK, n_out // 2), lambda l, k: (l, k, 1)),
            pl.BlockSpec((None, 1, n_out), lambda l, k: (l, 0, 0)),
        ],
        out_specs=pl.BlockSpec((None, rows, n_out), lambda l, k: (l, 0, 0)),
        out_shape=jax.ShapeDtypeStruct((DEPTH, rows, n_out), F32),
        compiler_params=pltpu.CompilerParams(dimension_semantics=("arbitrary", "arbitrary")),
        name="ada_modulation",
    )(cond, w_ada, w_ada, b_ada.reshape(DEPTH, 1, n_out))
    return out[:, :N_GROUPS].reshape(DEPTH, N_GROUPS, N_MOD, D_MODEL)


def _ffn_weight_copies(win_hbm, wout_hbm, st_in, st_out, sem, l, half, j, slot):
    cols = lambda off: pl.ds(off + j * FF_CHUNK, FF_CHUNK)
    return (pltpu.make_async_copy(win_hbm.at[l, half, :, cols(0)], st_in.at[slot, 0], sem.at[slot, 0]),
            pltpu.make_async_copy(win_hbm.at[l, half, :, cols(D_FF)], st_in.at[slot, 1], sem.at[slot, 1]),
            pltpu.make_async_copy(wout_hbm.at[l, half, cols(0), :], st_out.at[slot], sem.at[slot, 2]))


def _ffn_load_weights(win_hbm, wout_hbm, win_ref, wout_ref, st_in, st_out, sem, l, half):
    copies = functools.partial(_ffn_weight_copies, win_hbm, wout_hbm, st_in, st_out, sem, l, half)
    for c in copies(0, 0):
        c.start()
    for j in range(N_FF_CHUNKS):
        slot = j % 2
        if j + 1 < N_FF_CHUNKS:
            for c in copies(j + 1, 1 - slot):
                c.start()
        for c in copies(j, slot):
            c.wait()
        win_ref[:, j * FF_CHUNK:(j + 1) * FF_CHUNK] = st_in[slot, 0].astype(BF16)
        win_ref[:, D_FF + j * FF_CHUNK:D_FF + (j + 1) * FF_CHUNK] = st_in[slot, 1].astype(BF16)
        wout_ref[j * FF_CHUNK:(j + 1) * FF_CHUNK, :] = st_out[slot].astype(BF16)


def _ffn_kernel(*refs, l, half, k, split_in, mix, mix_norm, final):
    it = iter(refs)
    x_refs = (next(it), next(it)) if split_in else (next(it),)
    mod_ref, g_ref = next(it), next(it)
    o_in_refs, ng_ref, wmix_ref = ((next(it), next(it)), next(it), next(it)) if mix else (None, None, None)
    win_hbm, wout_hbm = next(it), next(it)
    fg_ref = next(it) if final else None
    out_refs = (next(it), next(it)) if final else (next(it),)
    acc_ref, win_ref, wout_ref, st_in, st_out, sem = it
    i = pl.program_id(0)
    is_prompt = i < N_PROMPT // x_refs[0].shape[0]

    @pl.when(i == 0)
    def _():
        _ffn_load_weights(win_hbm, wout_hbm, win_ref, wout_ref, st_in, st_out, sem, l, half)

    x = jnp.where(is_prompt, x_refs[0][...], x_refs[1][...]) if split_in else x_refs[0][...]
    if mix:
        o_in = jnp.where(is_prompt, o_in_refs[0][...], o_in_refs[1][...])
        if mix_norm:
            o_in = o_in * lax.rsqrt(jnp.mean(o_in * o_in, axis=-1, keepdims=True) + EPS) * ng_ref[...]
        x = x + mod_ref[5:6, :] * jnp.dot(o_in.astype(BF16), wmix_ref[...], preferred_element_type=F32)
    h, gate = _premod(x, g_ref[...], mod_ref, k)
    hb = h.astype(BF16)
    for j in range(N_FF_CHUNKS):
        a = jnp.dot(hb, win_ref[:, j * FF_CHUNK:(j + 1) * FF_CHUNK], preferred_element_type=F32)
        u = jnp.dot(hb, win_ref[:, D_FF + j * FF_CHUNK:D_FF + (j + 1) * FF_CHUNK], preferred_element_type=F32)
        t = (a * jax.nn.sigmoid(a) * u).astype(BF16)
        p = jnp.dot(t, wout_ref[j * FF_CHUNK:(j + 1) * FF_CHUNK, :], preferred_element_type=F32)
        if j == 0:
            acc_ref[...] = p
        else:
            acc_ref[...] += p
    y = x + 0.5 * gate * acc_ref[...]
    if not final:
        out_refs[0][...] = y
    else:
        y = y * lax.rsqrt(jnp.mean(y * y, axis=-1, keepdims=True) + EPS) * fg_ref[...]

        @pl.when(is_prompt)
        def _():
            out_refs[0][...] = y

        @pl.when(jnp.logical_not(is_prompt))
        def _():
            out_refs[1][...] = y


def _ffn(x, mod_l, g, w_in_all, w_out_all, l, half, k, mix=None, final_g=None):
    tm = TM_FFN
    split_in = isinstance(x, tuple)
    row = pl.BlockSpec((tm, D_MODEL), lambda i: (i, 0))
    vec = lambda n: pl.BlockSpec((1, n), lambda i: (0, 0))
    in_specs = list(_stream_rows(tm, D_MODEL)) if split_in else [row]
    args = list(x) if split_in else [x]
    in_specs += [pl.BlockSpec((None, N_MOD, D_MODEL), lambda i: (_group_of_tile(i, tm), 0, 0)), vec(D_MODEL)]
    args += [mod_l, g.reshape(1, D_MODEL)]
    if mix is not None:
        o_in, w_mix, norm_g = mix
        kdim = w_mix.shape[0]
        ng = jnp.ones((1, kdim), F32) if norm_g is None else norm_g.reshape(1, kdim)
        in_specs += [*_stream_rows(tm, kdim, single_buffer_latent=kdim > D_MODEL), vec(kdim),
                     _resident((kdim, D_MODEL))]
        args += [*o_in, ng, w_mix.astype(BF16)]
    in_specs += [pl.BlockSpec(memory_space=pl.ANY), pl.BlockSpec(memory_space=pl.ANY)]
    args += [w_in_all, w_out_all]
    if final_g is not None:
        in_specs.append(vec(D_MODEL))
        args.append(final_g.reshape(1, D_MODEL))
        out_specs = list(_stream_rows(tm, D_MODEL))
        out_shape = [jax.ShapeDtypeStruct((N_PROMPT, D_MODEL), F32), jax.ShapeDtypeStruct((N_SAMPLE, D_MODEL), F32)]
    else:
        out_specs, out_shape = row, jax.ShapeDtypeStruct((N_TOK, D_MODEL), F32)
    return pl.pallas_call(
        functools.partial(_ffn_kernel, l=l, half=half, k=k, split_in=split_in, mix=mix is not None,
                          mix_norm=mix is not None and mix[2] is not None, final=final_g is not None),
        grid=(N_TOK // tm,),
        in_specs=in_specs, out_specs=out_specs, out_shape=out_shape,
        scratch_shapes=[pltpu.VMEM((tm, D_MODEL), F32),
                        pltpu.VMEM((D_MODEL, 2 * D_FF), BF16), pltpu.VMEM((D_FF, D_MODEL), BF16),
                        pltpu.VMEM((2, 2, D_MODEL, FF_CHUNK), F32), pltpu.VMEM((2, FF_CHUNK, D_MODEL), F32),
                        pltpu.SemaphoreType.DMA((2, 3))],
        compiler_params=pltpu.CompilerParams(dimension_semantics=("arbitrary",),
                                             vmem_limit_bytes=VMEM_LIMIT),
        name="ffn_swiglu",
    )(*args)


def _proj_in_kernel(x_ref, mod_ref, g_ref, w_ref, o_ref, *, tn, act):
    h, _ = _premod(x_ref[...], g_ref[...], mod_ref, 1)
    hb = h.astype(BF16)
    for j in range(w_ref.shape[1] // tn):
        y = jnp.dot(hb, w_ref[:, j * tn:(j + 1) * tn], preferred_element_type=F32)
        if act == "gelu":
            y = jax.nn.gelu(y)
        o_ref[:, j * tn:(j + 1) * tn] = y


def _proj_in(x, mod_l, g, w, tn, act=None):
    tm = TM_PROJ
    n_out = w.shape[1]
    return pl.pallas_call(
        functools.partial(_proj_in_kernel, tn=tn, act=act),
        grid=(N_TOK // tm,),
        in_specs=[
            pl.BlockSpec((tm, D_MODEL), lambda i: (i, 0)),
            pl.BlockSpec((None, N_MOD, D_MODEL), lambda i: (_group_of_tile(i, tm), 0, 0)),
            pl.BlockSpec((1, D_MODEL), lambda i: (0, 0)),
            _resident((D_MODEL, n_out)),
        ],
        out_specs=pl.BlockSpec((tm, n_out), lambda i: (i, 0)),
        out_shape=jax.ShapeDtypeStruct((N_TOK, n_out), F32),
        compiler_params=pltpu.CompilerParams(dimension_semantics=("arbitrary",),
                                             vmem_limit_bytes=VMEM_LIMIT),
        name="mixer_proj_in",
    )(x, mod_l, g.reshape(1, D_MODEL), w.astype(BF16))


def _pad_cols(w, mult):
    n_pad = -w.shape[1] % mult
    return jnp.pad(w, ((0, 0), (0, n_pad))) if n_pad else w


GLA_BLK = 128
GLA_ZCOL = GLA_IN // LANE


def _gla_consts():
    r = np.arange(GLA_BLK)
    same = (r[:, None] // GLA_CHUNK) == (r[None, :] // GLA_CHUNK)
    ri, ci = r[:, None] % GLA_CHUNK, r[None, :] % GLA_CHUNK
    lf = np.concatenate([same & (ci <= ri), same & (ci > ri)], axis=0)
    lb = np.concatenate([same & (ci >= ri), same & (ci < ri)], axis=0)
    rows = np.arange(GLA_CHUNK * GLA_DK)
    sel = (rows[:, None] // GLA_DK) == (np.arange(LANE)[None, :] % GLA_CHUNK)
    return (jnp.asarray(lf, BF16), jnp.asarray(lb, BF16), jnp.asarray(sel, BF16))


def _rope_tables(T):
    half = GLA_DK // 2
    t = np.arange(T)
    inv = ROPE_THETA ** (-np.arange(0, half, 2, dtype=np.float64) / half)
    lane = np.arange(GLA_DK)
    pos = np.where(lane[None, :] < half, (t // GRID_W)[:, None], (t % GRID_W)[:, None])
    ang = pos * inv[lane % (half // 2)][None, :]
    sign = np.where((lane % half) < half // 2, -1.0, 1.0)[None, :]
    return jnp.asarray(np.cos(ang), F32), jnp.asarray(np.sin(ang) * sign, F32)


def _rope_apply(x, cos, sin_signed):
    half = GLA_DK // 2
    lane = lax.broadcasted_iota(jnp.int32, (1, GLA_DK), 1)
    partner = jnp.where((lane % half) < half // 2,
                        pltpu.roll(x, GLA_DK - half // 2, axis=1), pltpu.roll(x, half // 2, axis=1))
    return x * cos + partner * sin_signed


def _gla_kernel(*refs, use_rope, has_s0):
    it = iter(refs)
    q_ref, k_ref, v_ref, r_ref, za_ref, w2_ref, ba_ref, ng_ref, lf_ref, lb_ref, sel_ref = (next(it) for _ in range(11))
    cos_ref, sin_ref = (next(it), next(it)) if use_rope else (None, None)
    s0_ref = next(it) if has_s0 else None
    o_ref = next(it)
    st_ref = None if has_s0 else next(it)
    q_s, k_s, bf_s, bb_s, qdf_s, kdf_s, qdb_s, kdb_s, w_s, sf_s, sb_s = it

    T = q_ref.shape[0]
    nblk, nch = T // GLA_BLK, T // GLA_CHUNK
    hq = lambda h: slice(h * GLA_DK, (h + 1) * GLA_DK)
    hv = lambda h: slice(h * GLA_DV, (h + 1) * GLA_DV)

    for blk in range(nblk):
        rows = slice(blk * GLA_BLK, (blk + 1) * GLA_BLK)
        zab = za_ref[rows, :].astype(BF16)
        for h in range(GLA_H):
            z = jnp.dot(zab, w2_ref[h], preferred_element_type=F32) + ba_ref[h]
            la = (jnp.minimum(z, 0.0) - jnp.log1p(jnp.exp(-jnp.abs(z)))) * (1.0 / GLA_TAU)
            cf = _dot3(lf_ref[...], la[:, :GLA_DK])
            cb = _dot3(lb_ref[...], la[:, GLA_DK:])
            bf, bb = cf[:GLA_BLK], cb[:GLA_BLK]
            q = q_ref[rows, hq(h)] * (GLA_DK ** -0.5)
            k = k_ref[rows, hq(h)]
            if use_rope:
                q = _rope_apply(q, cos_ref[rows, :], sin_ref[rows, :])
                k = _rope_apply(k, cos_ref[rows, :], sin_ref[rows, :])
            q_s[h, rows, :], k_s[h, rows, :] = q, k
            bf_s[h, rows, :], bb_s[h, rows, :] = bf, bb
            qdf_s[h, rows, :] = (q * jnp.exp(bf)).astype(BF16)
            qdb_s[h, rows, :] = (q * jnp.exp(bb)).astype(BF16)
            kdf_s[h, rows, :] = (k * jnp.exp(cf[GLA_BLK:])).astype(BF16)
            kdb_s[h, rows, :] = (k * jnp.exp(cb[GLA_BLK:])).astype(BF16)

    irow = lax.broadcasted_iota(jnp.int32, (GLA_CHUNK, 1), 0)
    lane_c = lax.broadcasted_iota(jnp.int32, (GLA_BLK, LANE), 1) // GLA_CHUNK
    row_c = lax.broadcasted_iota(jnp.int32, (GLA_BLK, LANE), 0) // GLA_CHUNK
    for h in range(GLA_H):
        def intra_chunk(c, carry, h=h):
            rows = pl.ds(pl.multiple_of(c * GLA_CHUNK, GLA_CHUNK), GLA_CHUNK)
            qc, bfc, bbc = q_s[h, rows, :], bf_s[h, rows, :], bb_s[h, rows, :]
            for j in range(GLA_CHUNK):
                row = pl.ds(c * GLA_CHUNK + j, 1)
                e = (jnp.exp(jnp.where(irow >= j, bfc - bf_s[h, row, :], bbc - bb_s[h, row, :]))
                     + jnp.where(irow == j, 1.0, 0.0))
                w_s[rows, j * GLA_DK:(j + 1) * GLA_DK] = (qc * k_s[h, row, :] * e).astype(BF16)
            return carry

        lax.fori_loop(0, nch, intra_chunk, 0)
        a_all = jnp.dot(w_s[...], sel_ref[...], preferred_element_type=F32)
        for blk in range(nblk):
            rows = slice(blk * GLA_BLK, (blk + 1) * GLA_BLK)
            a = jnp.where(lane_c == row_c, a_all[rows, :], 0.0).astype(BF16)
            o_ref[rows, hv(h)] = jnp.dot(a, v_ref[rows, hv(h)].astype(BF16), preferred_element_type=F32)

    for h in range(GLA_H):
        if has_s0:
            sf_s[h] = s0_ref[0, h].T
            sb_s[h] = s0_ref[1, h].T
        else:
            sf_s[h] = jnp.zeros((GLA_DV, GLA_DK), F32)
            sb_s[h] = jnp.zeros((GLA_DV, GLA_DK), F32)

    def chain(h, rows, g_row, qd_s, kd_s, b_s, st_s):
        s = st_s[h]
        o_ref[rows, hv(h)] += _dot_nt(qd_s[h, rows, :], s.astype(BF16))
        u = _dot_tn(v_ref[rows, hv(h)].astype(BF16), kd_s[h, rows, :])
        st_s[h] = s * jnp.exp(b_s[h, g_row, :]) + u

    def inter_chunk(n, carry):
        cf_ = pl.multiple_of(n * GLA_CHUNK, GLA_CHUNK)
        cb_ = pl.multiple_of((nch - 1 - n) * GLA_CHUNK, GLA_CHUNK)
        for h in range(GLA_H):
            chain(h, pl.ds(cf_, GLA_CHUNK), pl.ds(cf_ + GLA_CHUNK - 1, 1), qdf_s, kdf_s, bf_s, sf_s)
            chain(h, pl.ds(cb_, GLA_CHUNK), pl.ds(cb_, 1), qdb_s, kdb_s, bb_s, sb_s)
        return carry

    lax.fori_loop(0, nch, inter_chunk, 0, unroll=2)

    for h in range(GLA_H):
        o = o_ref[:, hv(h)]
        o = o * lax.rsqrt(jnp.mean(o * o, axis=-1, keepdims=True) + EPS) * ng_ref[h]
        r = r_ref[:, hv(h)]
        o_ref[:, hv(h)] = o * (r * jax.nn.sigmoid(r))
        if not has_s0:
            st_ref[0, h] = sf_s[h].T
            st_ref[1, h] = sb_s[h].T


def _gla_call(y, w2, ba, ng, T, n_seq, row0, use_rope, s0):
    has_s0 = s0 is not None
    rb = row0 // T
    nq, nv = GLA_H * GLA_DK, GLA_H * GLA_DV
    mode = dict(pipeline_mode=pl.Buffered(1)) if n_seq <= 2 else {}
    col = lambda w, j: pl.BlockSpec((T, w), lambda b: (rb + b, j), **mode)
    cst = lambda a: pl.BlockSpec(a.shape, lambda b: (0,) * a.ndim)
    lf, lb, sel = _gla_consts()
    in_specs = [col(nq, 0), col(nq, 1), col(nv, 1), col(nv, 2), col(LANE, GLA_ZCOL),
                cst(w2), cst(ba), cst(ng), cst(lf), cst(lb), cst(sel)]
    args = [y, y, y, y, y, w2, ba, ng, lf, lb, sel]
    if use_rope:
        cos, sin = _rope_tables(T)
        in_specs += [cst(cos), cst(sin)]
        args += [cos, sin]
    st_spec = pl.BlockSpec((None, 2, GLA_H, GLA_DK, GLA_DV), lambda b: (b, 0, 0, 0, 0))
    out_specs = [pl.BlockSpec((T, nv), lambda b: (b, 0))]
    out_shape = [jax.ShapeDtypeStruct((n_seq * T, nv), F32)]
    if has_s0:
        in_specs.append(st_spec)
        args.append(s0)
    else:
        out_specs.append(st_spec)
        out_shape.append(jax.ShapeDtypeStruct((n_seq, 2, GLA_H, GLA_DK, GLA_DV), F32))
    scratch = ([pltpu.VMEM((GLA_H, T, GLA_DK), F32)] * 4 + [pltpu.VMEM((GLA_H, T, GLA_DK), BF16)] * 4
               + [pltpu.VMEM((T, GLA_CHUNK * GLA_DK), BF16),
                  pltpu.VMEM((GLA_H, GLA_DV, GLA_DK), F32), pltpu.VMEM((GLA_H, GLA_DV, GLA_DK), F32)])
    outs = pl.pallas_call(
        functools.partial(_gla_kernel, use_rope=use_rope, has_s0=has_s0),
        grid=(n_seq,),
        in_specs=in_specs, out_specs=out_specs, out_shape=out_shape,
        scratch_shapes=scratch,
        compiler_params=pltpu.CompilerParams(dimension_semantics=("arbitrary",),
                                             vmem_limit_bytes=VMEM_LIMIT),
        name="gla_rope" if use_rope else "gla",
    )(*args)
    return outs[0], (None if has_s0 else outs[1])


def _gla_mixer(y, state, w_a2, b_a, norm_g):
    w2 = jnp.zeros((GLA_H, LANE, 2 * GLA_DK), F32)
    for e in range(2):
        we = w_a2[e].reshape(GLA_RANK, GLA_H, GLA_DK).transpose(1, 0, 2)
        w2 = w2.at[:, e * GLA_RANK:(e + 1) * GLA_RANK, e * GLA_DK:(e + 1) * GLA_DK].set(we)
    w2 = w2.astype(BF16)
    ba = b_a.reshape(2, GLA_H, GLA_DK).transpose(1, 0, 2).reshape(GLA_H, 1, 2 * GLA_DK)
    ng = norm_g.reshape(GLA_H, 1, GLA_DV)
    op, st = _gla_call(y, w2, ba, ng, SEQ, BATCH, 0, False, None)
    os_, _ = _gla_call(y, w2, ba, ng, DEC_SEQ, DEC_BATCH, N_PROMPT, True, state)
    return (op, os_), st


N_HEAD_PAIRS = NAT_H // 2
NAT_ROWS = DEC_SEQ // GRID_W
NAT_WIN = NAT_WH * GRID_W
NAT_NDR = 2 * NAT_WH - 1
NAT_NDC = 2 * NAT_WW - 1


def _nat_row_window(r):
    rs = min(max(r - NAT_WH // 2, 0), NAT_ROWS - NAT_WH)
    return rs, r - rs


def _nat_bias_table(rpb):
    qc = np.arange(GRID_W)[:, None]
    kc = np.arange(GRID_W)[None, :]
    c_start = np.clip(qc - NAT_WW // 2, 0, GRID_W - NAT_WW)
    ok = (kc >= c_start) & (kc < c_start + NAT_WW)
    dc = np.clip(kc - qc + NAT_WW - 1, 0, NAT_NDC - 1)
    pick = jnp.asarray(dc[None] == np.arange(NAT_NDC)[:, None, None], F32)
    rows2 = jnp.stack([rpb[:, :NAT_NDR - 1], rpb[:, 1:]], axis=2)
    t = jnp.einsum('hdsx,xqk->hdqsk', rows2, pick, precision=lax.Precision.HIGHEST)
    t = jnp.where(ok[None, None, :, None, :], t, NEG_INF)
    return t.reshape(NAT_H, NAT_NDR - 1, GRID_W, 2 * GRID_W)


def _head_mask(hh):
    lane = lax.broadcasted_iota(jnp.int32, (1, LANE), 1)
    return (lane < NAT_HD) if hh == 0 else (lane >= NAT_HD)


def _nat_ctx_kernel(q_ref, k_ref, v_ref, o_ref, kc_ref, vc_ref):
    for hp in range(N_HEAD_PAIRS):
        cols = slice(hp * LANE, (hp + 1) * LANE)
        q = q_ref[:, cols] * (NAT_HD ** -0.5)
        k, v = k_ref[:, cols], v_ref[:, cols]
        kb, vb = k.astype(BF16), v.astype(BF16)
        q2 = jnp.concatenate([jnp.where(_head_mask(hh), q, 0.0) for hh in range(2)], axis=0).astype(BF16)
        s = _dot_nt(q2, kb)
        p = jnp.exp(s - jnp.max(s, axis=-1, keepdims=True))
        l = jnp.sum(p, axis=-1, keepdims=True)
        o2 = jnp.dot(p.astype(BF16), vb, preferred_element_type=F32) / l
        o_ref[:, cols] = jnp.where(_head_mask(0), o2[:SEQ], o2[SEQ:])
        for hh in range(2):
            kc_ref[2 * hp + hh] = k[:, hh * NAT_HD:(hh + 1) * NAT_HD]
            vc_ref[2 * hp + hh] = v[:, hh * NAT_HD:(hh + 1) * NAT_HD]


def _nat_context(y):
    blk = lambda j: pl.BlockSpec((SEQ, D_MODEL), lambda b: (b, j))
    cache = pl.BlockSpec((None, NAT_H, SEQ, NAT_HD), lambda b: (b, 0, 0, 0))
    cache_shape = jax.ShapeDtypeStruct((BATCH, NAT_H, SEQ, NAT_HD), F32)
    return pl.pallas_call(
        _nat_ctx_kernel,
        grid=(BATCH,),
        in_specs=[blk(0), blk(1), blk(2)],
        out_specs=[pl.BlockSpec((SEQ, D_MODEL), lambda b: (b, 0)), cache, cache],
        out_shape=[jax.ShapeDtypeStruct((N_PROMPT, D_MODEL), F32), cache_shape, cache_shape],
        compiler_params=pltpu.CompilerParams(dimension_semantics=("arbitrary",)),
        name="nat_context",
    )(y, y, y)


def _nat_lat_kernel(q_ref, k_ref, v_ref, ck_ref, cv_ref, tab_ref, o_ref):
    q = q_ref[...] * (NAT_HD ** -0.5)
    qm = [jnp.where(_head_mask(hh), q, 0.0).astype(BF16) for hh in range(2)]
    ckb = ck_ref[...].astype(BF16)
    cvb = cv_ref[...].astype(BF16)
    for r in range(NAT_ROWS):
        rs, off = _nat_row_window(r)
        kw = k_ref[rs * GRID_W:rs * GRID_W + NAT_WIN, :].astype(BF16)
        vw = v_ref[rs * GRID_W:rs * GRID_W + NAT_WIN, :].astype(BF16)
        qr = jnp.concatenate([qm[hh][r * GRID_W:(r + 1) * GRID_W] for hh in range(2)], axis=0)
        bias = jnp.concatenate(
            [jnp.concatenate([tab_ref[hh, w - off + NAT_WH - 1] for w in range(0, NAT_WH, 2)], axis=1)
             for hh in range(2)], axis=0)
        s_lat = _dot_nt(qr, kw) + bias
        s_ctx = _dot_nt(qr, ckb)
        m = jnp.maximum(jnp.max(s_lat, axis=-1, keepdims=True), jnp.max(s_ctx, axis=-1, keepdims=True))
        p_lat = jnp.exp(s_lat - m)
        p_ctx = jnp.exp(s_ctx - m)
        l = jnp.sum(p_lat, axis=-1, keepdims=True) + jnp.sum(p_ctx, axis=-1, keepdims=True)
        o2 = (jnp.dot(p_lat.astype(BF16), vw, preferred_element_type=F32)
              + jnp.dot(p_ctx.astype(BF16), cvb, preferred_element_type=F32)) / l
        o_ref[r * GRID_W:(r + 1) * GRID_W, :] = jnp.where(_head_mask(0), o2[:GRID_W], o2[GRID_W:])


def _nat_latent(y, ck, cv, rpb):
    row0 = N_PROMPT // DEC_SEQ
    blk = lambda off: pl.BlockSpec((DEC_SEQ, LANE), lambda b, hp: (row0 + b, off + hp))
    ctx = pl.BlockSpec((None, ck.shape[1], LANE), lambda b, hp: (b, 0, hp))
    return pl.pallas_call(
        _nat_lat_kernel,
        grid=(DEC_BATCH, N_HEAD_PAIRS),
        in_specs=[blk(0), blk(N_HEAD_PAIRS), blk(2 * N_HEAD_PAIRS), ctx, ctx,
                  pl.BlockSpec((2, NAT_NDR - 1, GRID_W, 2 * GRID_W), lambda b, hp: (hp, 0, 0, 0))],
        out_specs=pl.BlockSpec((DEC_SEQ, LANE), lambda b, hp: (b, hp)),
        out_shape=jax.ShapeDtypeStruct((N_SAMPLE, D_MODEL), F32),
        compiler_params=pltpu.CompilerParams(dimension_semantics=("arbitrary", "arbitrary")),
        name="nat_latent",
    )(y, y, y, ck, cv, _nat_bias_table(rpb))


def _heads_last(t):
    b, h, s, d = t.shape
    return t.transpose(0, 2, 1, 3).reshape(b, s, h * d)


def _gmlp_kernel(x_ref, y_ref, mod_ref, lng_ref, lnb_ref, ws_ref, bs_ref, w_ref, o_ref, t_ref):
    tm = x_ref.shape[0]
    v = y_ref[:, GM_DH:]
    vc = v - jnp.mean(v, axis=-1, keepdims=True)
    vn = vc * lax.rsqrt(jnp.mean(vc * vc, axis=-1, keepdims=True) + EPS) * lng_ref[...] + lnb_ref[...]
    vn = vn.astype(BF16)
    for n in range(tm // GM_CHUNK):
        rows = slice(n * GM_CHUNK, (n + 1) * GM_CHUNK)
        for g in range(GM_G):
            cols = slice(g * GM_CG, (g + 1) * GM_CG)
            sp = jnp.dot(ws_ref[g], vn[rows, cols], preferred_element_type=F32) + bs_ref[:, cols]
            t_ref[rows, cols] = (y_ref[rows, cols] * sp).astype(BF16)
    gate = mod_ref[5:6, :]
    o_ref[...] = x_ref[...] + gate * jnp.dot(t_ref[...], w_ref[...], preferred_element_type=F32)


def _gmlp_out(x, y, mod_l, ln_g, ln_b, w_s, b_s, w_out):
    tm = TM_PROJ
    bias = jnp.repeat(b_s.T, GM_CG, axis=1)
    return pl.pallas_call(
        _gmlp_kernel,
        grid=(N_TOK // tm,),
        in_specs=[
            pl.BlockSpec((tm, D_MODEL), lambda i: (i, 0)),
            pl.BlockSpec((tm, 2 * GM_DH), lambda i: (i, 0)),
            pl.BlockSpec((None, N_MOD, D_MODEL), lambda i: (_group_of_tile(i, tm), 0, 0)),
            pl.BlockSpec((1, GM_DH), lambda i: (0, 0)),
            pl.BlockSpec((1, GM_DH), lambda i: (0, 0)),
            _resident((GM_G, GM_CHUNK, GM_CHUNK)),
            _resident((GM_CHUNK, GM_DH)),
            _resident((GM_DH, D_MODEL)),
        ],
        out_specs=pl.BlockSpec((tm, D_MODEL), lambda i: (i, 0)),
        out_shape=jax.ShapeDtypeStruct((N_TOK, D_MODEL), F32),
        scratch_shapes=[pltpu.VMEM((tm, GM_DH), BF16)],
        compiler_params=pltpu.CompilerParams(dimension_semantics=("arbitrary",),
                                             vmem_limit_bytes=VMEM_LIMIT),
        name="gmlp_gate_out",
    )(x, y, mod_l, ln_g.reshape(1, GM_DH), ln_b.reshape(1, GM_DH), w_s.astype(BF16), bias,
      w_out.astype(BF16))


SSD_E = SSD_H // SSD_G
SSD_GP = SSD_E * SSD_P
SSD_BLK = 2 * SSD_CHUNK
SSD_COL_X = SSD_DI
SSD_COL_B = 2 * SSD_DI
SSD_COL_C = SSD_COL_B + SSD_G * SSD_N
SSD_COL_DT = SSD_COL_C + SSD_G * SSD_N


def _ssd_w_in(w_in):
    base = SSD_DI + SSD_XBC
    w_dt = jnp.zeros((D_MODEL, SSD_G, LANE), F32)
    for g in range(SSD_G):
        cols = jnp.concatenate([w_in[:, base + g * SSD_E:base + (g + 1) * SSD_E],
                                w_in[:, base + SSD_H + g * SSD_E:base + SSD_H + (g + 1) * SSD_E]], axis=1)
        w_dt = w_dt.at[:, g, :2 * SSD_E].set(cols).at[:, g, 2 * SSD_E:4 * SSD_E].set(cols)
    return jnp.concatenate([w_in[:, :base], w_dt.reshape(D_MODEL, SSD_G * LANE)], axis=1)


def _ssd_cum_matrix():
    r = np.arange(SSD_BLK)
    same = (r[:, None] // SSD_CHUNK) == (r[None, :] // SSD_CHUNK)
    cum = np.concatenate([same & (r[None, :] <= r[:, None]), same & (r[None, :] >= r[:, None])], axis=0)
    return jnp.asarray(cum, BF16)


def _softplus(x):
    return jnp.maximum(x, 0.0) + jnp.log1p(jnp.exp(-jnp.abs(x)))


def _ssd_kernel(*refs, has_s0):
    it = iter(refs)
    (z_ref, x_ref, b_ref, c_ref, dt_ref, cwx_ref, cwb_ref, cwc_ref, cbx_ref, cbb_ref, cbc_ref,
     dtb_ref, alog_ref, dsk_ref, cum_ref) = (next(it) for _ in range(15))
    s0_ref = next(it) if has_s0 else None
    o_ref = next(it)
    st_ref = None if has_s0 else next(it)
    xs_s, xb_s, bm_s, cm_s, cu_s, dt_s, y_s, sf_s, sb_s = it

    T = x_ref.shape[0]
    nch = T // SSD_CHUNK
    L = SSD_CHUNK

    trow = lax.broadcasted_iota(jnp.int32, (T, 1), 0)

    def conv_silu(v_ref, w_ref, bias_ref):
        v = v_ref[...]
        prev = jnp.where(trow == 0, 0.0, pltpu.roll(v, 1, axis=0))
        nxt = jnp.where(trow == T - 1, 0.0, pltpu.roll(v, T - 1, axis=0))
        y = prev * w_ref[0:1, :] + v * w_ref[1:2, :] + nxt * w_ref[2:3, :] + bias_ref[...]
        return y * jax.nn.sigmoid(y)

    xs = conv_silu(x_ref, cwx_ref, cbx_ref)
    xs_s[...] = xs
    xb_s[...] = xs.astype(BF16)
    bm_s[...] = conv_silu(b_ref, cwb_ref, cbb_ref).astype(BF16)
    cm_s[...] = conv_silu(c_ref, cwc_ref, cbc_ref).astype(BF16)

    lane1 = lax.broadcasted_iota(jnp.int32, (1, LANE), 1)
    a_row = jnp.where(lane1 < 2 * SSD_E, -jnp.exp(alog_ref[...]), 0.0)
    for blk in range(T // SSD_BLK):
        rows = slice(blk * SSD_BLK, (blk + 1) * SSD_BLK)
        dt = _softplus(dt_ref[rows, :] + dtb_ref[...])
        c2 = _dot3(cum_ref[...], dt * a_row)
        cu_s[rows, :] = jnp.where(lane1 < SSD_E, c2[:SSD_BLK], c2[SSD_BLK:])
        dt_s[rows, :] = dt

    ii = lax.broadcasted_iota(jnp.int32, (L, LANE), 0)
    jj = lax.broadcasted_iota(jnp.int32, (L, LANE), 1)
    fwd_half = jj < L
    fwd_half1 = lane1 < L
    tri = (fwd_half & (ii >= jj)) | ((jj >= L) & (ii <= jj - L))
    left = lane1 < SSD_P
    for c in range(nch):
        rows = slice(c * L, (c + 1) * L)
        cum_c, dt_c = cu_s[rows, :], dt_s[rows, :]
        bm_c, cm_c = bm_s[rows, :], cm_s[rows, :]
        cb2 = _dot_nt(cm_c, jnp.concatenate([bm_c, bm_c], axis=0))
        arr = jnp.where(lane1 < 2 * SSD_E, cum_c, dt_c)
        arr_t = jnp.concatenate([arr, arr], axis=0).T
        gs = []
        for e in range(SSD_E):
            row_c = jnp.where(fwd_half1, arr_t[e:e + 1, :], arr_t[SSD_E + e:SSD_E + e + 1, :])
            row_dt = jnp.where(fwd_half1, arr_t[2 * SSD_E + e:2 * SSD_E + e + 1, :],
                               arr_t[3 * SSD_E + e:3 * SSD_E + e + 1, :])
            col_c = jnp.where(fwd_half, jnp.broadcast_to(cum_c[:, e:e + 1], (L, LANE)),
                              jnp.broadcast_to(cum_c[:, SSD_E + e:SSD_E + e + 1], (L, LANE)))
            dec = jnp.exp(jnp.where(tri, col_c - row_c, -jnp.inf))
            gs.append((cb2 * dec * row_dt).astype(BF16))
        for pr in range(SSD_E // 2):
            cols = slice(pr * LANE, (pr + 1) * LANE)
            xp = xb_s[rows, cols]
            xl = jnp.where(left, xp, jnp.zeros_like(xp))
            xr = jnp.where(left, jnp.zeros_like(xp), xp)
            lhs = jnp.concatenate([gs[2 * pr], gs[2 * pr + 1]], axis=1)
            rhs = jnp.concatenate([xl, xl, xr, xr], axis=0)
            y_s[rows, cols] = (jnp.dot(lhs, rhs, preferred_element_type=F32)
                               + dsk_ref[:, cols] * xs_s[rows, cols])

    if has_s0:
        sf_s[...] = s0_ref[0].T
        sb_s[...] = s0_ref[1].T
    else:
        sf_s[...] = jnp.zeros_like(sf_s)
        sb_s[...] = jnp.zeros_like(sb_s)

    def per_head_cols(v, lane0):
        tiles = []
        for t in range(SSD_E // 2):
            a, b = lane0 + 2 * t, lane0 + 2 * t + 1
            tiles.append(jnp.where(left, jnp.broadcast_to(v[:, a:a + 1], (L, LANE)),
                                   jnp.broadcast_to(v[:, b:b + 1], (L, LANE))))
        return jnp.concatenate(tiles, axis=1)

    def chain(c, last, lane0, st_s):
        rows = slice(c * L, (c + 1) * L)
        cum_c = jnp.where((lane1 >= lane0) & (lane1 < lane0 + SSD_E), cu_s[rows, :], 0.0)
        tot = cum_c[last:last + 1, :]
        qd = per_head_cols(jnp.exp(cum_c), lane0)
        w = per_head_cols(jnp.exp(tot - cum_c) * dt_s[rows, :], lane0)
        s = st_s[...]
        y_s[rows, :] += jnp.dot(cm_s[rows, :], s.astype(BF16), preferred_element_type=F32) * qd
        xw = (xs_s[rows, :] * w).astype(BF16)
        st_s[...] = s * qd[last:last + 1, :] + _dot_tn(bm_s[rows, :], xw)

    for n in range(nch):
        chain(n, L - 1, 0, sf_s)
        chain(nch - 1 - n, 0, SSD_E, sb_s)

    z = z_ref[...]
    o_ref[...] = y_s[...] * (z * jax.nn.sigmoid(z))
    if not has_s0:
        st_ref[0] = sf_s[...].T
        st_ref[1] = sb_s[...].T


def _ssd_call(y, prm, T, n_seq, row0, s0):
    has_s0 = s0 is not None
    rb = row0 // T
    col = lambda w, off: pl.BlockSpec((T, w), lambda b, g: (rb + b, off // w + g))
    wcol = lambda rows, w, off: pl.BlockSpec((rows, w), lambda b, g: (0, off // w + g))
    per_g = lambda w: pl.BlockSpec((None, 1, w), lambda b, g: (g, 0, 0))
    cum = _ssd_cum_matrix()
    xoff, boff, coff = 0, SSD_DI, SSD_DI + SSD_G * SSD_N
    in_specs = [col(SSD_GP, 0), col(SSD_GP, SSD_COL_X), col(SSD_N, SSD_COL_B), col(SSD_N, SSD_COL_C),
                col(LANE, SSD_COL_DT),
                wcol(SSD_CONV, SSD_GP, xoff), wcol(SSD_CONV, SSD_N, boff), wcol(SSD_CONV, SSD_N, coff),
                wcol(1, SSD_GP, xoff), wcol(1, SSD_N, boff), wcol(1, SSD_N, coff),
                per_g(LANE), per_g(LANE), per_g(SSD_GP),
                pl.BlockSpec(cum.shape, lambda b, g: (0, 0))]
    args = [y, y, y, y, y, prm["conv_w"], prm["conv_w"], prm["conv_w"], prm["conv_b"], prm["conv_b"], prm["conv_b"],
            prm["dt_bias"], prm["a_log"], prm["d_skip"], cum]
    st_spec = pl.BlockSpec((None, 2, None, SSD_GP, SSD_N), lambda b, g: (b, 0, g, 0, 0))
    out_specs = [pl.BlockSpec((T, SSD_GP), lambda b, g: (b, g))]
    out_shape = [jax.ShapeDtypeStruct((n_seq * T, SSD_DI), F32)]
    if has_s0:
        in_specs.append(st_spec)
        args.append(s0)
    else:
        out_specs.append(st_spec)
        out_shape.append(jax.ShapeDtypeStruct((n_seq, 2, SSD_G, SSD_GP, SSD_N), F32))
    scratch = [pltpu.VMEM((T, SSD_GP), F32), pltpu.VMEM((T, SSD_GP), BF16),
               pltpu.VMEM((T, SSD_N), BF16), pltpu.VMEM((T, SSD_N), BF16),
               pltpu.VMEM((T, LANE), F32), pltpu.VMEM((T, LANE), F32), pltpu.VMEM((T, SSD_GP), F32),
               pltpu.VMEM((SSD_N, SSD_GP), F32), pltpu.VMEM((SSD_N, SSD_GP), F32)]
    outs = pl.pallas_call(
        functools.partial(_ssd_kernel, has_s0=has_s0),
        grid=(n_seq, SSD_G),
        in_specs=in_specs, out_specs=out_specs, out_shape=out_shape,
        scratch_shapes=scratch,
        compiler_params=pltpu.CompilerParams(dimension_semantics=("arbitrary", "arbitrary"),
                                             vmem_limit_bytes=VMEM_LIMIT),
        name="ssd_state" if has_s0 else "ssd",
    )(*args)
    return outs[0], (None if has_s0 else outs[1])


def _ssd_mixer(y, state, conv_w, conv_b, dt_bias, a_log, d_skip):
    def lanes(p):
        pg = p.reshape(2, SSD_G, SSD_E).transpose(1, 0, 2).reshape(SSD_G, 2 * SSD_E)
        return jnp.pad(jnp.concatenate([pg, pg], axis=1), ((0, 0), (0, LANE - 4 * SSD_E))).reshape(SSD_G, 1, LANE)
    prm = dict(conv_w=conv_w, conv_b=conv_b.reshape(1, SSD_XBC), dt_bias=lanes(dt_bias), a_log=lanes(a_log),
               d_skip=jnp.repeat(d_skip, SSD_P).reshape(SSD_G, 1, SSD_GP))
    op, st = _ssd_call(y, prm, SEQ, BATCH, 0, None)
    s0 = state.reshape(DEC_BATCH, 2, SSD_G, SSD_GP, SSD_N)
    os_, _ = _ssd_call(y, prm, DEC_SEQ, DEC_BATCH, N_PROMPT, s0)
    return (op, os_), st.reshape(BATCH, 2, SSD_H, SSD_P, SSD_N)


def kernel(x_prompt, x_sample, state_gla, cache_nat_k, cache_nat_v, state_ssd, c,
           c_ctx, norm_g, w_ada, b_ada, w_ffn_in, w_ffn_out,
           gla_w_in, gla_w_a1, gla_w_a2, gla_b_a, gla_norm_g, gla_w_out,
           nat_w_qkv, nat_rpb, nat_w_out,
           gm_w_in, gm_ln_g, gm_ln_b, gm_w_s, gm_b_s, gm_w_out,
           ssd_w_in, ssd_conv_w, ssd_conv_b, ssd_dt_bias, ssd_a_log, ssd_d, ssd_norm_g, ssd_w_out,
           final_g):
    x = (x_prompt.reshape(N_PROMPT, D_MODEL), x_sample.reshape(N_SAMPLE, D_MODEL))
    mod = _modulation_all(c, c_ctx, w_ada, b_ada)
    new_gla, new_k, new_v, new_ssd = [], [], [], []
    for l in range(DEPTH):
        kind, j = l % N_MIXERS, l // N_MIXERS
        x = _ffn(x, mod[l], norm_g[l, 0], w_ffn_in, w_ffn_out, l, 0, 0)
        mix = None
        if kind == 0:
            w = jnp.concatenate([gla_w_in[j], gla_w_a1[j, 0], gla_w_a1[j, 1]], axis=1)
            y = _proj_in(x, mod[l], norm_g[l, 1], _pad_cols(w, 640), 640)
            o, st = _gla_mixer(y, state_gla[:, j], gla_w_a2[j], gla_b_a[j], gla_norm_g[j])
            new_gla.append(st)
            mix = (o, gla_w_out[j], None)
        elif kind == 1:
            y = _proj_in(x, mod[l], norm_g[l, 1], nat_w_qkv[j], 768)
            op, kc, vc = _nat_context(y)
            os_ = _nat_latent(y, _heads_last(cache_nat_k[:, j]), _heads_last(cache_nat_v[:, j]), nat_rpb[j])
            new_k.append(kc)
            new_v.append(vc)
            mix = ((op, os_), nat_w_out[j], None)
        elif kind == 2:
            y = _proj_in(x, mod[l], norm_g[l, 1], gm_w_in[j], 512, act="gelu")
            x = _gmlp_out(x, y, mod[l], gm_ln_g[j], gm_ln_b[j], gm_w_s[j], gm_b_s[j], gm_w_out[j])
        else:
            y = _proj_in(x, mod[l], norm_g[l, 1], _ssd_w_in(ssd_w_in[j]), 512)
            o, st = _ssd_mixer(y, state_ssd[:, j], ssd_conv_w[j], ssd_conv_b[j], ssd_dt_bias[j],
                               ssd_a_log[j], ssd_d[j])
            new_ssd.append(st)
            mix = (o, ssd_w_out[j], ssd_norm_g[j])
        x = _ffn(x, mod[l], norm_g[l, 2], w_ffn_in, w_ffn_out, l, 1, 2, mix=mix,
                 final_g=final_g if l == DEPTH - 1 else None)
    y_prompt = x[0].reshape(BATCH, SEQ, D_MODEL)
    y_sample = x[1].reshape(DEC_BATCH, DEC_SEQ, D_MODEL)
    return (y_prompt, y_sample, jnp.stack(new_gla, axis=1), jnp.stack(new_k, axis=1),
            jnp.stack(new_v, axis=1), jnp.stack(new_ssd, axis=1))
```

```python
import functools

import jax
import jax.numpy as jnp
import numpy as np
from jax import lax
from jax.experimental import pallas as pl
from jax.experimental.pallas import tpu as pltpu

D_MODEL = 1024
BATCH = 32
SEQ = 256
DEPTH = 4
DEC_BATCH = 2
DEC_SEQ = 1024
N_PROMPT = BATCH * SEQ
N_SAMPLE = DEC_BATCH * DEC_SEQ
N_TOK = N_PROMPT + N_SAMPLE
N_GROUPS = 1 + DEC_BATCH

GRID_W = 64
N_MIXERS = 4
N_SUB = 3
N_MOD = 3 * N_SUB
D_FF = 2816
EPS = 1e-6
NEG_INF = -1e30
ROPE_THETA = 10000.0
GLA_H, GLA_DK, GLA_DV, GLA_RANK, GLA_TAU, GLA_CHUNK = 4, 128, 256, 16, 16.0, 16
GLA_IN = 2 * GLA_H * GLA_DK + 2 * GLA_H * GLA_DV
NAT_H, NAT_HD, NAT_WH, NAT_WW = 16, 64, 8, 16
GM_DH, GM_G, GM_CHUNK = 1024, 8, 128
GM_CG = GM_DH // GM_G
SSD_DI = 2 * D_MODEL
SSD_P = 64
SSD_H = SSD_DI // SSD_P
SSD_N, SSD_G, SSD_CONV, SSD_CHUNK = 128, 4, 3, 64
SSD_XBC = SSD_DI + 2 * SSD_G * SSD_N

LANE = 128
VMEM_LIMIT = 56 * 1024 * 1024
BF16 = jnp.bfloat16
F32 = jnp.float32

FF_CHUNK = 256
N_FF_CHUNKS = D_FF // FF_CHUNK
TM_FFN = 512
TM_PROJ = 512
ADA_TK = 128


def _group_of_tile(i, tm):
    n_prompt_tiles = N_PROMPT // tm
    return jnp.where(i < n_prompt_tiles, 0, 1 + (i - n_prompt_tiles) // (DEC_SEQ // tm))


def _resident(shape):
    nd = len(shape)
    return pl.BlockSpec(shape, lambda i: (0,) * nd, pipeline_mode=pl.Buffered(1))


def _stream_rows(tm, width, single_buffer_latent=False):
    npt = N_PROMPT // tm
    mode = dict(pipeline_mode=pl.Buffered(1)) if single_buffer_latent else {}
    return (pl.BlockSpec((tm, width), lambda i: (jnp.minimum(i, npt - 1), 0)),
            pl.BlockSpec((tm, width), lambda i: (jnp.maximum(i - npt, 0), 0), **mode))


def _premod(x, g, mod_ref, k):
    shift = mod_ref[3 * k:3 * k + 1, :]
    scale = mod_ref[3 * k + 1:3 * k + 2, :]
    gate = mod_ref[3 * k + 2:3 * k + 3, :]
    ms = jnp.mean(x * x, axis=-1, keepdims=True)
    h = x * lax.rsqrt(ms + EPS) * g
    return h * (1.0 + scale) + shift, gate


def _dot_nt(a, b):
    return lax.dot_general(a, b, (((1,), (1,)), ((), ())), preferred_element_type=F32)


def _dot_tn(a, b):
    return lax.dot_general(a, b, (((0,), (0,)), ((), ())), preferred_element_type=F32)


def _split3(x):
    hi = x.astype(BF16)
    r1 = x - hi.astype(F32)
    mid = r1.astype(BF16)
    lo = (r1 - mid.astype(F32)).astype(BF16)
    return hi, mid, lo


def _dot3(m, x):
    hi, mid, lo = _split3(x)
    d = lambda p: jnp.dot(m, p, preferred_element_type=F32)
    return d(hi) + d(mid) + d(lo)


def _ada_kernel(cond_ref, wa_ref, wb_ref, b_ref, o_ref):
    cnd = cond_ref[...]
    s = (cnd * jax.nn.sigmoid(cnd)).astype(BF16)
    p = jnp.concatenate([jnp.dot(s, w_ref[...].astype(BF16), preferred_element_type=F32)
                         for w_ref in (wa_ref, wb_ref)], axis=1)

    @pl.when(pl.program_id(1) == 0)
    def _():
        o_ref[...] = p + b_ref[...]

    @pl.when(pl.program_id(1) > 0)
    def _():
        o_ref[...] += p


def _modulation_all(c, c_ctx, w_ada, b_ada):
    rows = 8
    nk = D_MODEL // ADA_TK
    cond = jnp.concatenate([c_ctx[None], c, jnp.zeros((rows - N_GROUPS, D_MODEL), F32)], axis=0)
    cond = cond.reshape(rows, nk, ADA_TK).transpose(1, 0, 2)
    n_out = N_MOD * D_MODEL
    out = pl.pallas_call(
        _ada_kernel,
        grid=(DEPTH, nk),
        in_specs=[
            pl.BlockSpec((None, rows, ADA_TK), lambda l, k: (k, 0, 0)),
            pl.BlockSpec((None, ADA_TK, n_out // 2), lambda l, k: (l, k, 0)),
            pl.BlockSpec((None, ADA_TK, n_out // 2), lambda l, k: (l, k, 1)),
            pl.BlockSpec((None, 1, n_out), lambda l, k: (l, 0, 0)),
        ],
        out_specs=pl.BlockSpec((None, rows, n_out), lambda l, k: (l, 0, 0)),
        out_shape=jax.ShapeDtypeStruct((DEPTH, rows, n_out), F32),
        compiler_params=pltpu.CompilerParams(dimension_semantics=("arbitrary", "arbitrary")),
        name="ada_modulation",
    )(cond, w_ada, w_ada, b_ada.reshape(DEPTH, 1, n_out))
    return out[:, :N_GROUPS].reshape(DEPTH, N_GROUPS, N_MOD, D_MODEL)


def _ffn_weight_copies(win_hbm, wout_hbm, st_in, st_out, sem, l, half, j, slot):
    cols = lambda off: pl.ds(off + j * FF_CHUNK, FF_CHUNK)
    return (pltpu.make_async_copy(win_hbm.at[l, half, :, cols(0)], st_in.at[slot, 0], sem.at[slot, 0]),
            pltpu.make_async_copy(win_hbm.at[l, half, :, cols(D_FF)], st_in.at[slot, 1], sem.at[slot, 1]),
            pltpu.make_async_copy(wout_hbm.at[l, half, cols(0), :], st_out.at[slot], sem.at[slot, 2]))


def _ffn_kernel(*refs, l, half, k, split_in, mix, mix_norm, final):
    it = iter(refs)
    x_refs = (next(it), next(it)) if split_in else (next(it),)
    mod_ref, g_ref = next(it), next(it)
    o_in_refs, ng_ref, wmix_ref = ((next(it), next(it)), next(it), next(it)) if mix else (None, None, None)
    win_hbm, wout_hbm = next(it), next(it)
    fg_ref = next(it) if final else None
    out_refs = (next(it), next(it)) if final else (next(it),)
    acc_ref, win_ref, wout_ref, st_in, st_out, sem = it
    i = pl.program_id(0)
    is_prompt = i < N_PROMPT // x_refs[0].shape[0]
    copies = functools.partial(_ffn_weight_copies, win_hbm, wout_hbm, st_in, st_out, sem, l, half)

    def row_tile(fetch_weights):
        if fetch_weights:
            for c in copies(0, 0):
                c.start()
        x = jnp.where(is_prompt, x_refs[0][...], x_refs[1][...]) if split_in else x_refs[0][...]
        if mix:
            o_in = jnp.where(is_prompt, o_in_refs[0][...], o_in_refs[1][...])
            if mix_norm:
                o_in = o_in * lax.rsqrt(jnp.mean(o_in * o_in, axis=-1, keepdims=True) + EPS) * ng_ref[...]
            x = x + mod_ref[5:6, :] * jnp.dot(o_in.astype(BF16), wmix_ref[...], preferred_element_type=F32)
        h, gate = _premod(x, g_ref[...], mod_ref, k)
        hb = h.astype(BF16)
        for j in range(N_FF_CHUNKS):
            gate_cols = slice(j * FF_CHUNK, (j + 1) * FF_CHUNK)
            up_cols = slice(D_FF + j * FF_CHUNK, D_FF + (j + 1) * FF_CHUNK)
            if fetch_weights:
                slot = j % 2
                if j + 1 < N_FF_CHUNKS:
                    for c in copies(j + 1, 1 - slot):
                        c.start()
                for c in copies(j, slot):
                    c.wait()
                store_chunk(j, slot)
            a = jnp.dot(hb, win_ref[:, gate_cols], preferred_element_type=F32)
            u = jnp.dot(hb, win_ref[:, up_cols], preferred_element_type=F32)
            t = (a * jax.nn.sigmoid(a) * u).astype(BF16)
            p = jnp.dot(t, wout_ref[gate_cols, :], preferred_element_type=F32)
            if j == 0:
                acc_ref[...] = p
            else:
                acc_ref[...] += p
        y = x + 0.5 * gate * acc_ref[...]
        if not final:
            out_refs[0][...] = y
        else:
            y = y * lax.rsqrt(jnp.mean(y * y, axis=-1, keepdims=True) + EPS) * fg_ref[...]

            @pl.when(is_prompt)
            def _():
                out_refs[0][...] = y

            @pl.when(jnp.logical_not(is_prompt))
            def _():
                out_refs[1][...] = y

    def store_chunk(j, slot):
        win_ref[:, j * FF_CHUNK:(j + 1) * FF_CHUNK] = st_in[slot, 0].astype(BF16)
        win_ref[:, D_FF + j * FF_CHUNK:D_FF + (j + 1) * FF_CHUNK] = st_in[slot, 1].astype(BF16)
        wout_ref[j * FF_CHUNK:(j + 1) * FF_CHUNK, :] = st_out[slot].astype(BF16)

    def fetch_all_weights():
        for c in copies(0, 0):
            c.start()
        for j in range(N_FF_CHUNKS):
            slot = j % 2
            if j + 1 < N_FF_CHUNKS:
                for c in copies(j + 1, 1 - slot):
                    c.start()
            for c in copies(j, slot):
                c.wait()
            store_chunk(j, slot)

    if mix_norm:
        pl.when(i == 0)(fetch_all_weights)
        row_tile(False)
    else:
        pl.when(i == 0)(functools.partial(row_tile, True))
        pl.when(i > 0)(functools.partial(row_tile, False))


def _ffn(x, mod_l, g, w_in_all, w_out_all, l, half, k, mix=None, final_g=None):
    tm = TM_FFN
    split_in = isinstance(x, tuple)
    row = pl.BlockSpec((tm, D_MODEL), lambda i: (i, 0))
    vec = lambda n: pl.BlockSpec((1, n), lambda i: (0, 0))
    in_specs = list(_stream_rows(tm, D_MODEL)) if split_in else [row]
    args = list(x) if split_in else [x]
    in_specs += [pl.BlockSpec((None, N_MOD, D_MODEL), lambda i: (_group_of_tile(i, tm), 0, 0)), vec(D_MODEL)]
    args += [mod_l, g.reshape(1, D_MODEL)]
    if mix is not None:
        o_in, w_mix, norm_g = mix
        kdim = w_mix.shape[0]
        ng = jnp.ones((1, kdim), F32) if norm_g is None else norm_g.reshape(1, kdim)
        in_specs += [*_stream_rows(tm, kdim, single_buffer_latent=kdim > D_MODEL), vec(kdim),
                     _resident((kdim, D_MODEL))]
        args += [*o_in, ng, w_mix.astype(BF16)]
    in_specs += [pl.BlockSpec(memory_space=pl.ANY), pl.BlockSpec(memory_space=pl.ANY)]
    args += [w_in_all, w_out_all]
    if final_g is not None:
        in_specs.append(vec(D_MODEL))
        args.append(final_g.reshape(1, D_MODEL))
        out_specs = list(_stream_rows(tm, D_MODEL))
        out_shape = [jax.ShapeDtypeStruct((N_PROMPT, D_MODEL), F32), jax.ShapeDtypeStruct((N_SAMPLE, D_MODEL), F32)]
    else:
        out_specs, out_shape = row, jax.ShapeDtypeStruct((N_TOK, D_MODEL), F32)
    return pl.pallas_call(
        functools.partial(_ffn_kernel, l=l, half=half, k=k, split_in=split_in, mix=mix is not None,
                          mix_norm=mix is not None and mix[2] is not None, final=final_g is not None),
        grid=(N_TOK // tm,),
        in_specs=in_specs, out_specs=out_specs, out_shape=out_shape,
        scratch_shapes=[pltpu.VMEM((tm, D_MODEL), F32),
                        pltpu.VMEM((D_MODEL, 2 * D_FF), BF16), pltpu.VMEM((D_FF, D_MODEL), BF16),
                        pltpu.VMEM((2, 2, D_MODEL, FF_CHUNK), F32), pltpu.VMEM((2, FF_CHUNK, D_MODEL), F32),
                        pltpu.SemaphoreType.DMA((2, 3))],
        compiler_params=pltpu.CompilerParams(dimension_semantics=("arbitrary",),
                                             vmem_limit_bytes=VMEM_LIMIT),
        name="ffn_swiglu",
    )(*args)


def _proj_in_kernel(x_ref, mod_ref, g_ref, w_ref, o_ref, *, tn, act):
    h, _ = _premod(x_ref[...], g_ref[...], mod_ref, 1)
    hb = h.astype(BF16)
    for j in range(w_ref.shape[1] // tn):
        y = jnp.dot(hb, w_ref[:, j * tn:(j + 1) * tn], preferred_element_type=F32)
        if act == "gelu":
            y = jax.nn.gelu(y)
        o_ref[:, j * tn:(j + 1) * tn] = y


def _proj_in(x, mod_l, g, w, tn, act=None):
    tm = TM_PROJ
    n_out = w.shape[1]
    return pl.pallas_call(
        functools.partial(_proj_in_kernel, tn=tn, act=act),
        grid=(N_TOK // tm,),
        in_specs=[
            pl.BlockSpec((tm, D_MODEL), lambda i: (i, 0)),
            pl.BlockSpec((None, N_MOD, D_MODEL), lambda i: (_group_of_tile(i, tm), 0, 0)),
            pl.BlockSpec((1, D_MODEL), lambda i: (0, 0)),
            _resident((D_MODEL, n_out)),
        ],
        out_specs=pl.BlockSpec((tm, n_out), lambda i: (i, 0)),
        out_shape=jax.ShapeDtypeStruct((N_TOK, n_out), F32),
        compiler_params=pltpu.CompilerParams(dimension_semantics=("arbitrary",),
                                             vmem_limit_bytes=VMEM_LIMIT),
        name="mixer_proj_in",
    )(x, mod_l, g.reshape(1, D_MODEL), w.astype(BF16))


def _pad_cols(w, mult):
    n_pad = -w.shape[1] % mult
    return jnp.pad(w, ((0, 0), (0, n_pad))) if n_pad else w


GLA_BLK = 128
GLA_ZCOL = GLA_IN // LANE


def _gla_consts():
    r = np.arange(GLA_BLK)
    same = (r[:, None] // GLA_CHUNK) == (r[None, :] // GLA_CHUNK)
    ri, ci = r[:, None] % GLA_CHUNK, r[None, :] % GLA_CHUNK
    lf = np.concatenate([same & (ci <= ri), same & (ci > ri)], axis=0)
    lb = np.concatenate([same & (ci >= ri), same & (ci < ri)], axis=0)
    rows = np.arange(GLA_CHUNK * GLA_DK)
    sel = (rows[:, None] // GLA_DK) == (np.arange(LANE)[None, :] % GLA_CHUNK)
    return (jnp.asarray(lf, BF16), jnp.asarray(lb, BF16), jnp.asarray(sel, BF16))


def _rope_tables(T):
    half = GLA_DK // 2
    t = np.arange(T)
    inv = ROPE_THETA ** (-np.arange(0, half, 2, dtype=np.float64) / half)
    lane = np.arange(GLA_DK)
    pos = np.where(lane[None, :] < half, (t // GRID_W)[:, None], (t % GRID_W)[:, None])
    ang = pos * inv[lane % (half // 2)][None, :]
    sign = np.where((lane % half) < half // 2, -1.0, 1.0)[None, :]
    return jnp.asarray(np.cos(ang), F32), jnp.asarray(np.sin(ang) * sign, F32)


def _rope_apply(x, cos, sin_signed):
    half = GLA_DK // 2
    lane = lax.broadcasted_iota(jnp.int32, (1, GLA_DK), 1)
    partner = jnp.where((lane % half) < half // 2,
                        pltpu.roll(x, GLA_DK - half // 2, axis=1), pltpu.roll(x, half // 2, axis=1))
    return x * cos + partner * sin_signed


def _gla_kernel(*refs, use_rope, has_s0):
    it = iter(refs)
    q_ref, k_ref, v_ref, r_ref, za_ref, w2_ref, ba_ref, ng_ref, lf_ref, lb_ref, sel_ref = (next(it) for _ in range(11))
    cos_ref, sin_ref = (next(it), next(it)) if use_rope else (None, None)
    s0_ref = next(it) if has_s0 else None
    o_ref = next(it)
    st_ref = None if has_s0 else next(it)
    q_s, k_s, bf_s, bb_s, qdf_s, kdf_s, qdb_s, kdb_s, w_s, sf_s, sb_s = it

    T = q_ref.shape[0]
    nblk, nch = T // GLA_BLK, T // GLA_CHUNK
    hq = lambda h: slice(h * GLA_DK, (h + 1) * GLA_DK)
    hv = lambda h: slice(h * GLA_DV, (h + 1) * GLA_DV)

    for blk in range(nblk):
        rows = slice(blk * GLA_BLK, (blk + 1) * GLA_BLK)
        zab = za_ref[rows, :].astype(BF16)
        for h in range(GLA_H):
            z = jnp.dot(zab, w2_ref[h], preferred_element_type=F32) + ba_ref[h]
            la = (jnp.minimum(z, 0.0) - jnp.log1p(jnp.exp(-jnp.abs(z)))) * (1.0 / GLA_TAU)
            cf = _dot3(lf_ref[...], la[:, :GLA_DK])
            cb = _dot3(lb_ref[...], la[:, GLA_DK:])
            bf, bb = cf[:GLA_BLK], cb[:GLA_BLK]
            q = q_ref[rows, hq(h)] * (GLA_DK ** -0.5)
            k = k_ref[rows, hq(h)]
            if use_rope:
                q = _rope_apply(q, cos_ref[rows, :], sin_ref[rows, :])
                k = _rope_apply(k, cos_ref[rows, :], sin_ref[rows, :])
            q_s[h, rows, :], k_s[h, rows, :] = q, k
            bf_s[h, rows, :], bb_s[h, rows, :] = bf, bb
            qdf_s[h, rows, :] = (q * jnp.exp(bf)).astype(BF16)
            qdb_s[h, rows, :] = (q * jnp.exp(bb)).astype(BF16)
            kdf_s[h, rows, :] = (k * jnp.exp(cf[GLA_BLK:])).astype(BF16)
            kdb_s[h, rows, :] = (k * jnp.exp(cb[GLA_BLK:])).astype(BF16)

    irow = lax.broadcasted_iota(jnp.int32, (GLA_CHUNK, 1), 0)
    lane_c = lax.broadcasted_iota(jnp.int32, (GLA_BLK, LANE), 1) // GLA_CHUNK
    row_c = lax.broadcasted_iota(jnp.int32, (GLA_BLK, LANE), 0) // GLA_CHUNK
    for h in range(GLA_H):
        def intra_chunk(c, carry, h=h):
            rows = pl.ds(pl.multiple_of(c * GLA_CHUNK, GLA_CHUNK), GLA_CHUNK)
            qc, bfc, bbc = q_s[h, rows, :], bf_s[h, rows, :], bb_s[h, rows, :]
            for j in range(GLA_CHUNK):
                row = pl.ds(c * GLA_CHUNK + j, 1)
                e = (jnp.exp(jnp.where(irow >= j, bfc - bf_s[h, row, :], bbc - bb_s[h, row, :]))
                     + jnp.where(irow == j, 1.0, 0.0))
                w_s[rows, j * GLA_DK:(j + 1) * GLA_DK] = (qc * k_s[h, row, :] * e).astype(BF16)
            return carry

        lax.fori_loop(0, nch, intra_chunk, 0)
        a_all = jnp.dot(w_s[...], sel_ref[...], preferred_element_type=F32)
        for blk in range(nblk):
            rows = slice(blk * GLA_BLK, (blk + 1) * GLA_BLK)
            a = jnp.where(lane_c == row_c, a_all[rows, :], 0.0).astype(BF16)
            o_ref[rows, hv(h)] = jnp.dot(a, v_ref[rows, hv(h)].astype(BF16), preferred_element_type=F32)

    for h in range(GLA_H):
        if has_s0:
            sf_s[h] = s0_ref[0, h].T
            sb_s[h] = s0_ref[1, h].T
        else:
            sf_s[h] = jnp.zeros((GLA_DV, GLA_DK), F32)
            sb_s[h] = jnp.zeros((GLA_DV, GLA_DK), F32)

    def chain(h, rows, g_row, qd_s, kd_s, b_s, st_s):
        s = st_s[h]
        o_ref[rows, hv(h)] += _dot_nt(qd_s[h, rows, :], s.astype(BF16))
        u = _dot_tn(v_ref[rows, hv(h)].astype(BF16), kd_s[h, rows, :])
        st_s[h] = s * jnp.exp(b_s[h, g_row, :]) + u

    def inter_chunk(n, carry):
        cf_ = pl.multiple_of(n * GLA_CHUNK, GLA_CHUNK)
        cb_ = pl.multiple_of((nch - 1 - n) * GLA_CHUNK, GLA_CHUNK)
        for h in range(GLA_H):
            chain(h, pl.ds(cf_, GLA_CHUNK), pl.ds(cf_ + GLA_CHUNK - 1, 1), qdf_s, kdf_s, bf_s, sf_s)
            chain(h, pl.ds(cb_, GLA_CHUNK), pl.ds(cb_, 1), qdb_s, kdb_s, bb_s, sb_s)
        return carry

    lax.fori_loop(0, nch, inter_chunk, 0, unroll=4)

    for h in range(GLA_H):
        o = o_ref[:, hv(h)]
        o = o * lax.rsqrt(jnp.mean(o * o, axis=-1, keepdims=True) + EPS) * ng_ref[h]
        r = r_ref[:, hv(h)]
        o_ref[:, hv(h)] = o * (r * jax.nn.sigmoid(r))
        if not has_s0:
            st_ref[0, h] = sf_s[h].T
            st_ref[1, h] = sb_s[h].T


def _gla_call(y, w2, ba, ng, T, n_seq, row0, use_rope, s0):
    has_s0 = s0 is not None
    rb = row0 // T
    nq, nv = GLA_H * GLA_DK, GLA_H * GLA_DV
    mode = dict(pipeline_mode=pl.Buffered(1)) if n_seq <= 2 else {}
    col = lambda w, j: pl.BlockSpec((T, w), lambda b: (rb + b, j), **mode)
    cst = lambda a: pl.BlockSpec(a.shape, lambda b: (0,) * a.ndim)
    lf, lb, sel = _gla_consts()
    in_specs = [col(nq, 0), col(nq, 1), col(nv, 1), col(nv, 2), col(LANE, GLA_ZCOL),
                cst(w2), cst(ba), cst(ng), cst(lf), cst(lb), cst(sel)]
    args = [y, y, y, y, y, w2, ba, ng, lf, lb, sel]
    if use_rope:
        cos, sin = _rope_tables(T)
        in_specs += [cst(cos), cst(sin)]
        args += [cos, sin]
    st_spec = pl.BlockSpec((None, 2, GLA_H, GLA_DK, GLA_DV), lambda b: (b, 0, 0, 0, 0))
    out_specs = [pl.BlockSpec((T, nv), lambda b: (b, 0))]
    out_shape = [jax.ShapeDtypeStruct((n_seq * T, nv), F32)]
    if has_s0:
        in_specs.append(st_spec)
        args.append(s0)
    else:
        out_specs.append(st_spec)
        out_shape.append(jax.ShapeDtypeStruct((n_seq, 2, GLA_H, GLA_DK, GLA_DV), F32))
    scratch = ([pltpu.VMEM((GLA_H, T, GLA_DK), F32)] * 4 + [pltpu.VMEM((GLA_H, T, GLA_DK), BF16)] * 4
               + [pltpu.VMEM((T, GLA_CHUNK * GLA_DK), BF16),
                  pltpu.VMEM((GLA_H, GLA_DV, GLA_DK), F32), pltpu.VMEM((GLA_H, GLA_DV, GLA_DK), F32)])
    outs = pl.pallas_call(
        functools.partial(_gla_kernel, use_rope=use_rope, has_s0=has_s0),
        grid=(n_seq,),
        in_specs=in_specs, out_specs=out_specs, out_shape=out_shape,
        scratch_shapes=scratch,
        compiler_params=pltpu.CompilerParams(dimension_semantics=("arbitrary",),
                                             vmem_limit_bytes=VMEM_LIMIT),
        name="gla_rope" if use_rope else "gla",
    )(*args)
    return outs[0], (None if has_s0 else outs[1])


def _gla_mixer(y, state, w_a2, b_a, norm_g):
    w2 = jnp.zeros((GLA_H, LANE, 2 * GLA_DK), F32)
    for e in range(2):
        we = w_a2[e].reshape(GLA_RANK, GLA_H, GLA_DK).transpose(1, 0, 2)
        w2 = w2.at[:, e * GLA_RANK:(e + 1) * GLA_RANK, e * GLA_DK:(e + 1) * GLA_DK].set(we)
    w2 = w2.astype(BF16)
    ba = b_a.reshape(2, GLA_H, GLA_DK).transpose(1, 0, 2).reshape(GLA_H, 1, 2 * GLA_DK)
    ng = norm_g.reshape(GLA_H, 1, GLA_DV)
    op, st = _gla_call(y, w2, ba, ng, SEQ, BATCH, 0, False, None)
    os_, _ = _gla_call(y, w2, ba, ng, DEC_SEQ, DEC_BATCH, N_PROMPT, True, state)
    return (op, os_), st


N_HEAD_PAIRS = NAT_H // 2
NAT_ROWS = DEC_SEQ // GRID_W
NAT_WIN = NAT_WH * GRID_W
NAT_NDR = 2 * NAT_WH - 1
NAT_NDC = 2 * NAT_WW - 1


def _nat_row_window(r):
    rs = min(max(r - NAT_WH // 2, 0), NAT_ROWS - NAT_WH)
    return rs, r - rs


def _nat_bias_table(rpb):
    qc = np.arange(GRID_W)[:, None]
    kc = np.arange(GRID_W)[None, :]
    c_start = np.clip(qc - NAT_WW // 2, 0, GRID_W - NAT_WW)
    ok = (kc >= c_start) & (kc < c_start + NAT_WW)
    dc = np.clip(kc - qc + NAT_WW - 1, 0, NAT_NDC - 1)
    pick = jnp.asarray(dc[None] == np.arange(NAT_NDC)[:, None, None], F32)
    rows2 = jnp.stack([rpb[:, :NAT_NDR - 1], rpb[:, 1:]], axis=2)
    t = jnp.einsum('hdsx,xqk->hdqsk', rows2, pick, precision=lax.Precision.HIGHEST)
    t = jnp.where(ok[None, None, :, None, :], t, NEG_INF)
    return t.reshape(NAT_H, NAT_NDR - 1, GRID_W, 2 * GRID_W)


def _head_mask(hh):
    lane = lax.broadcasted_iota(jnp.int32, (1, LANE), 1)
    return (lane < NAT_HD) if hh == 0 else (lane >= NAT_HD)


def _nat_ctx_kernel(q_ref, k_ref, v_ref, o_ref, kc_ref, vc_ref):
    for hp in range(N_HEAD_PAIRS):
        cols = slice(hp * LANE, (hp + 1) * LANE)
        q = q_ref[:, cols] * (NAT_HD ** -0.5)
        k, v = k_ref[:, cols], v_ref[:, cols]
        kb, vb = k.astype(BF16), v.astype(BF16)
        q2 = jnp.concatenate([jnp.where(_head_mask(hh), q, 0.0) for hh in range(2)], axis=0).astype(BF16)
        s = _dot_nt(q2, kb)
        p = jnp.exp(s - jnp.max(s, axis=-1, keepdims=True))
        l = jnp.sum(p, axis=-1, keepdims=True)
        o2 = jnp.dot(p.astype(BF16), vb, preferred_element_type=F32) / l
        o_ref[:, cols] = jnp.where(_head_mask(0), o2[:SEQ], o2[SEQ:])
        for hh in range(2):
            kc_ref[2 * hp + hh] = k[:, hh * NAT_HD:(hh + 1) * NAT_HD]
            vc_ref[2 * hp + hh] = v[:, hh * NAT_HD:(hh + 1) * NAT_HD]


def _nat_context(y):
    blk = lambda j: pl.BlockSpec((SEQ, D_MODEL), lambda b: (b, j))
    cache = pl.BlockSpec((None, NAT_H, SEQ, NAT_HD), lambda b: (b, 0, 0, 0))
    cache_shape = jax.ShapeDtypeStruct((BATCH, NAT_H, SEQ, NAT_HD), F32)
    return pl.pallas_call(
        _nat_ctx_kernel,
        grid=(BATCH,),
        in_specs=[blk(0), blk(1), blk(2)],
        out_specs=[pl.BlockSpec((SEQ, D_MODEL), lambda b: (b, 0)), cache, cache],
        out_shape=[jax.ShapeDtypeStruct((N_PROMPT, D_MODEL), F32), cache_shape, cache_shape],
        compiler_params=pltpu.CompilerParams(dimension_semantics=("arbitrary",)),
        name="nat_context",
    )(y, y, y)


def _nat_lat_kernel(q_ref, k_ref, v_ref, ck_ref, cv_ref, tab_ref, o_ref):
    q = q_ref[...] * (NAT_HD ** -0.5)
    qm = [jnp.where(_head_mask(hh), q, 0.0).astype(BF16) for hh in range(2)]
    ckb = ck_ref[...].astype(BF16)
    cvb = cv_ref[...].astype(BF16)
    for r in range(NAT_ROWS):
        rs, off = _nat_row_window(r)
        kw = k_ref[rs * GRID_W:rs * GRID_W + NAT_WIN, :].astype(BF16)
        vw = v_ref[rs * GRID_W:rs * GRID_W + NAT_WIN, :].astype(BF16)
        qr = jnp.concatenate([qm[hh][r * GRID_W:(r + 1) * GRID_W] for hh in range(2)], axis=0)
        bias = jnp.concatenate(
            [jnp.concatenate([tab_ref[hh, w - off + NAT_WH - 1] for w in range(0, NAT_WH, 2)], axis=1)
             for hh in range(2)], axis=0)
        s_lat = _dot_nt(qr, kw) + bias
        s_ctx = _dot_nt(qr, ckb)
        m = jnp.maximum(jnp.max(s_lat, axis=-1, keepdims=True), jnp.max(s_ctx, axis=-1, keepdims=True))
        p_lat = jnp.exp(s_lat - m)
        p_ctx = jnp.exp(s_ctx - m)
        l = jnp.sum(p_lat, axis=-1, keepdims=True) + jnp.sum(p_ctx, axis=-1, keepdims=True)
        o2 = (jnp.dot(p_lat.astype(BF16), vw, preferred_element_type=F32)
              + jnp.dot(p_ctx.astype(BF16), cvb, preferred_element_type=F32)) / l
        o_ref[r * GRID_W:(r + 1) * GRID_W, :] = jnp.where(_head_mask(0), o2[:GRID_W], o2[GRID_W:])


def _nat_latent(y, ck, cv, rpb):
    row0 = N_PROMPT // DEC_SEQ
    blk = lambda off: pl.BlockSpec((DEC_SEQ, LANE), lambda b, hp: (row0 + b, off + hp))
    ctx = pl.BlockSpec((None, ck.shape[1], LANE), lambda b, hp: (b, 0, hp))
    return pl.pallas_call(
        _nat_lat_kernel,
        grid=(DEC_BATCH, N_HEAD_PAIRS),
        in_specs=[blk(0), blk(N_HEAD_PAIRS), blk(2 * N_HEAD_PAIRS), ctx, ctx,
                  pl.BlockSpec((2, NAT_NDR - 1, GRID_W, 2 * GRID_W), lambda b, hp: (hp, 0, 0, 0))],
        out_specs=pl.BlockSpec((DEC_SEQ, LANE), lambda b, hp: (b, hp)),
        out_shape=jax.ShapeDtypeStruct((N_SAMPLE, D_MODEL), F32),
        compiler_params=pltpu.CompilerParams(dimension_semantics=("arbitrary", "arbitrary")),
        name="nat_latent",
    )(y, y, y, ck, cv, _nat_bias_table(rpb))


def _heads_last(t):
    b, h, s, d = t.shape
    return t.transpose(0, 2, 1, 3).reshape(b, s, h * d)


def _gmlp_kernel(x_ref, y_ref, mod_ref, lng_ref, lnb_ref, ws_ref, bs_ref, w_ref, o_ref, t_ref):
    tm = x_ref.shape[0]
    v = y_ref[:, GM_DH:]
    vc = v - jnp.mean(v, axis=-1, keepdims=True)
    vn = vc * lax.rsqrt(jnp.mean(vc * vc, axis=-1, keepdims=True) + EPS) * lng_ref[...] + lnb_ref[...]
    vn = vn.astype(BF16)
    for n in range(tm // GM_CHUNK):
        rows = slice(n * GM_CHUNK, (n + 1) * GM_CHUNK)
        for g in range(GM_G):
            cols = slice(g * GM_CG, (g + 1) * GM_CG)
            sp = jnp.dot(ws_ref[g], vn[rows, cols], preferred_element_type=F32) + bs_ref[:, cols]
            t_ref[rows, cols] = (y_ref[rows, cols] * sp).astype(BF16)
    gate = mod_ref[5:6, :]
    o_ref[...] = x_ref[...] + gate * jnp.dot(t_ref[...], w_ref[...], preferred_element_type=F32)


def _gmlp_out(x, y, mod_l, ln_g, ln_b, w_s, b_s, w_out):
    tm = TM_PROJ
    bias = jnp.repeat(b_s.T, GM_CG, axis=1)
    return pl.pallas_call(
        _gmlp_kernel,
        grid=(N_TOK // tm,),
        in_specs=[
            pl.BlockSpec((tm, D_MODEL), lambda i: (i, 0)),
            pl.BlockSpec((tm, 2 * GM_DH), lambda i: (i, 0)),
            pl.BlockSpec((None, N_MOD, D_MODEL), lambda i: (_group_of_tile(i, tm), 0, 0)),
            pl.BlockSpec((1, GM_DH), lambda i: (0, 0)),
            pl.BlockSpec((1, GM_DH), lambda i: (0, 0)),
            _resident((GM_G, GM_CHUNK, GM_CHUNK)),
            _resident((GM_CHUNK, GM_DH)),
            _resident((GM_DH, D_MODEL)),
        ],
        out_specs=pl.BlockSpec((tm, D_MODEL), lambda i: (i, 0)),
        out_shape=jax.ShapeDtypeStruct((N_TOK, D_MODEL), F32),
        scratch_shapes=[pltpu.VMEM((tm, GM_DH), BF16)],
        compiler_params=pltpu.CompilerParams(dimension_semantics=("arbitrary",),
                                             vmem_limit_bytes=VMEM_LIMIT),
        name="gmlp_gate_out",
    )(x, y, mod_l, ln_g.reshape(1, GM_DH), ln_b.reshape(1, GM_DH), w_s.astype(BF16), bias,
      w_out.astype(BF16))


SSD_E = SSD_H // SSD_G
SSD_GP = SSD_E * SSD_P
SSD_BLK = 2 * SSD_CHUNK
SSD_COL_X = SSD_DI
SSD_COL_B = 2 * SSD_DI
SSD_COL_C = SSD_COL_B + SSD_G * SSD_N
SSD_COL_DT = SSD_COL_C + SSD_G * SSD_N


def _ssd_w_in(w_in):
    base = SSD_DI + SSD_XBC
    w_dt = jnp.zeros((D_MODEL, SSD_G, LANE), F32)
    for g in range(SSD_G):
        cols = jnp.concatenate([w_in[:, base + g * SSD_E:base + (g + 1) * SSD_E],
                                w_in[:, base + SSD_H + g * SSD_E:base + SSD_H + (g + 1) * SSD_E]], axis=1)
        w_dt = w_dt.at[:, g, :2 * SSD_E].set(cols).at[:, g, 2 * SSD_E:4 * SSD_E].set(cols)
    return jnp.concatenate([w_in[:, :base], w_dt.reshape(D_MODEL, SSD_G * LANE)], axis=1)


def _ssd_cum_matrix():
    r = np.arange(SSD_BLK)
    same = (r[:, None] // SSD_CHUNK) == (r[None, :] // SSD_CHUNK)
    cum = np.concatenate([same & (r[None, :] <= r[:, None]), same & (r[None, :] >= r[:, None])], axis=0)
    return jnp.asarray(cum, BF16)


def _softplus(x):
    return jnp.maximum(x, 0.0) + jnp.log1p(jnp.exp(-jnp.abs(x)))


def _ssd_kernel(*refs, has_s0):
    it = iter(refs)
    (z_ref, x_ref, b_ref, c_ref, dt_ref, cwx_ref, cwb_ref, cwc_ref, cbx_ref, cbb_ref, cbc_ref,
     dtb_ref, alog_ref, dsk_ref, cum_ref) = (next(it) for _ in range(15))
    s0_ref = next(it) if has_s0 else None
    o_ref = next(it)
    st_ref = None if has_s0 else next(it)
    xs_s, xb_s, bm_s, cm_s, cu_s, dt_s, y_s, sf_s, sb_s = it

    T = x_ref.shape[0]
    nch = T // SSD_CHUNK
    L = SSD_CHUNK

    trow = lax.broadcasted_iota(jnp.int32, (T, 1), 0)

    def conv_silu(v_ref, w_ref, bias_ref):
        v = v_ref[...]
        prev = jnp.where(trow == 0, 0.0, pltpu.roll(v, 1, axis=0))
        nxt = jnp.where(trow == T - 1, 0.0, pltpu.roll(v, T - 1, axis=0))
        y = prev * w_ref[0:1, :] + v * w_ref[1:2, :] + nxt * w_ref[2:3, :] + bias_ref[...]
        return y * jax.nn.sigmoid(y)

    xs = conv_silu(x_ref, cwx_ref, cbx_ref)
    xs_s[...] = xs
    xb_s[...] = xs.astype(BF16)
    bm_s[...] = conv_silu(b_ref, cwb_ref, cbb_ref).astype(BF16)
    cm_s[...] = conv_silu(c_ref, cwc_ref, cbc_ref).astype(BF16)

    lane1 = lax.broadcasted_iota(jnp.int32, (1, LANE), 1)
    a_row = jnp.where(lane1 < 2 * SSD_E, -jnp.exp(alog_ref[...]), 0.0)
    for blk in range(T // SSD_BLK):
        rows = slice(blk * SSD_BLK, (blk + 1) * SSD_BLK)
        dt = _softplus(dt_ref[rows, :] + dtb_ref[...])
        c2 = _dot3(cum_ref[...], dt * a_row)
        cu_s[rows, :] = jnp.where(lane1 < SSD_E, c2[:SSD_BLK], c2[SSD_BLK:])
        dt_s[rows, :] = dt

    ii = lax.broadcasted_iota(jnp.int32, (L, LANE), 0)
    jj = lax.broadcasted_iota(jnp.int32, (L, LANE), 1)
    fwd_half = jj < L
    fwd_half1 = lane1 < L
    tri = (fwd_half & (ii >= jj)) | ((jj >= L) & (ii <= jj - L))
    left = lane1 < SSD_P
    for c in range(nch):
        rows = slice(c * L, (c + 1) * L)
        cum_c, dt_c = cu_s[rows, :], dt_s[rows, :]
        bm_c, cm_c = bm_s[rows, :], cm_s[rows, :]
        cb2 = _dot_nt(cm_c, jnp.concatenate([bm_c, bm_c], axis=0))
        arr = jnp.where(lane1 < 2 * SSD_E, cum_c, dt_c)
        arr_t = jnp.concatenate([arr, arr], axis=0).T
        gs = []
        for e in range(SSD_E):
            row_c = jnp.where(fwd_half1, arr_t[e:e + 1, :], arr_t[SSD_E + e:SSD_E + e + 1, :])
            row_dt = jnp.where(fwd_half1, arr_t[2 * SSD_E + e:2 * SSD_E + e + 1, :],
                               arr_t[3 * SSD_E + e:3 * SSD_E + e + 1, :])
            col_c = jnp.where(fwd_half, jnp.broadcast_to(cum_c[:, e:e + 1], (L, LANE)),
                              jnp.broadcast_to(cum_c[:, SSD_E + e:SSD_E + e + 1], (L, LANE)))
            dec = jnp.exp(jnp.where(tri, col_c - row_c, -jnp.inf))
            gs.append((cb2 * dec * row_dt).astype(BF16))
        for pr in range(SSD_E // 2):
            cols = slice(pr * LANE, (pr + 1) * LANE)
            xp = xb_s[rows, cols]
            xl = jnp.where(left, xp, jnp.zeros_like(xp))
            xr = jnp.where(left, jnp.zeros_like(xp), xp)
            lhs = jnp.concatenate([gs[2 * pr], gs[2 * pr + 1]], axis=1)
            rhs = jnp.concatenate([xl, xl, xr, xr], axis=0)
            y_s[rows, cols] = (jnp.dot(lhs, rhs, preferred_element_type=F32)
                               + dsk_ref[:, cols] * xs_s[rows, cols])

    if has_s0:
        sf_s[...] = s0_ref[0].T
        sb_s[...] = s0_ref[1].T
    else:
        sf_s[...] = jnp.zeros_like(sf_s)
        sb_s[...] = jnp.zeros_like(sb_s)

    def per_head_cols(v, lane0):
        tiles = []
        for t in range(SSD_E // 2):
            a, b = lane0 + 2 * t, lane0 + 2 * t + 1
            tiles.append(jnp.where(left, jnp.broadcast_to(v[:, a:a + 1], (L, LANE)),
                                   jnp.broadcast_to(v[:, b:b + 1], (L, LANE))))
        return jnp.concatenate(tiles, axis=1)

    def chain(c, last, lane0, st_s):
        rows = slice(c * L, (c + 1) * L)
        cum_c = jnp.where((lane1 >= lane0) & (lane1 < lane0 + SSD_E), cu_s[rows, :], 0.0)
        tot = cum_c[last:last + 1, :]
        qd = per_head_cols(jnp.exp(cum_c), lane0)
        w = per_head_cols(jnp.exp(tot - cum_c) * dt_s[rows, :], lane0)
        s = st_s[...]
        y_s[rows, :] += jnp.dot(cm_s[rows, :], s.astype(BF16), preferred_element_type=F32) * qd
        xw = (xs_s[rows, :] * w).astype(BF16)
        st_s[...] = s * qd[last:last + 1, :] + _dot_tn(bm_s[rows, :], xw)

    for n in range(nch):
        chain(n, L - 1, 0, sf_s)
        chain(nch - 1 - n, 0, SSD_E, sb_s)

    z = z_ref[...]
    o_ref[...] = y_s[...] * (z * jax.nn.sigmoid(z))
    if not has_s0:
        st_ref[0] = sf_s[...].T
        st_ref[1] = sb_s[...].T


def _ssd_call(y, prm, T, n_seq, row0, s0):
    has_s0 = s0 is not None
    rb = row0 // T
    col = lambda w, off: pl.BlockSpec((T, w), lambda b, g: (rb + b, off // w + g))
    wcol = lambda rows, w, off: pl.BlockSpec((rows, w), lambda b, g: (0, off // w + g))
    per_g = lambda w: pl.BlockSpec((None, 1, w), lambda b, g: (g, 0, 0))
    cum = _ssd_cum_matrix()
    xoff, boff, coff = 0, SSD_DI, SSD_DI + SSD_G * SSD_N
    in_specs = [col(SSD_GP, 0), col(SSD_GP, SSD_COL_X), col(SSD_N, SSD_COL_B), col(SSD_N, SSD_COL_C),
                col(LANE, SSD_COL_DT),
                wcol(SSD_CONV, SSD_GP, xoff), wcol(SSD_CONV, SSD_N, boff), wcol(SSD_CONV, SSD_N, coff),
                wcol(1, SSD_GP, xoff), wcol(1, SSD_N, boff), wcol(1, SSD_N, coff),
                per_g(LANE), per_g(LANE), per_g(SSD_GP),
                pl.BlockSpec(cum.shape, lambda b, g: (0, 0))]
    args = [y, y, y, y, y, prm["conv_w"], prm["conv_w"], prm["conv_w"], prm["conv_b"], prm["conv_b"], prm["conv_b"],
            prm["dt_bias"], prm["a_log"], prm["d_skip"], cum]
    st_spec = pl.BlockSpec((None, 2, None, SSD_GP, SSD_N), lambda b, g: (b, 0, g, 0, 0))
    out_specs = [pl.BlockSpec((T, SSD_GP), lambda b, g: (b, g))]
    out_shape = [jax.ShapeDtypeStruct((n_seq * T, SSD_DI), F32)]
    if has_s0:
        in_specs.append(st_spec)
        args.append(s0)
    else:
        out_specs.append(st_spec)
        out_shape.append(jax.ShapeDtypeStruct((n_seq, 2, SSD_G, SSD_GP, SSD_N), F32))
    scratch = [pltpu.VMEM((T, SSD_GP), F32), pltpu.VMEM((T, SSD_GP), BF16),
               pltpu.VMEM((T, SSD_N), BF16), pltpu.VMEM((T, SSD_N), BF16),
               pltpu.VMEM((T, LANE), F32), pltpu.VMEM((T, LANE), F32), pltpu.VMEM((T, SSD_GP), F32),
               pltpu.VMEM((SSD_N, SSD_GP), F32), pltpu.VMEM((SSD_N, SSD_GP), F32)]
    outs = pl.pallas_call(
        functools.partial(_ssd_kernel, has_s0=has_s0),
        grid=(n_seq, SSD_G),
        in_specs=in_specs, out_specs=out_specs, out_shape=out_shape,
        scratch_shapes=scratch,
        compiler_params=pltpu.CompilerParams(dimension_semantics=("arbitrary", "arbitrary"),
                                             vmem_limit_bytes=VMEM_LIMIT),
        name="ssd_state" if has_s0 else "ssd",
    )(*args)
    return outs[0], (None if has_s0 else outs[1])


def _ssd_mixer(y, state, conv_w, conv_b, dt_bias, a_log, d_skip):
    def lanes(p):
        pg = p.reshape(2, SSD_G, SSD_E).transpose(1, 0, 2).reshape(SSD_G, 2 * SSD_E)
        return jnp.pad(jnp.concatenate([pg, pg], axis=1), ((0, 0), (0, LANE - 4 * SSD_E))).reshape(SSD_G, 1, LANE)
    prm = dict(conv_w=conv_w, conv_b=conv_b.reshape(1, SSD_XBC), dt_bias=lanes(dt_bias), a_log=lanes(a_log),
               d_skip=jnp.repeat(d_skip, SSD_P).reshape(SSD_G, 1, SSD_GP))
    op, st = _ssd_call(y, prm, SEQ, BATCH, 0, None)
    s0 = state.reshape(DEC_BATCH, 2, SSD_G, SSD_GP, SSD_N)
    os_, _ = _ssd_call(y, prm, DEC_SEQ, DEC_BATCH, N_PROMPT, s0)
    return (op, os_), st.reshape(BATCH, 2, SSD_H, SSD_P, SSD_N)


def kernel(x_prompt, x_sample, state_gla, cache_nat_k, cache_nat_v, state_ssd, c,
           c_ctx, norm_g, w_ada, b_ada, w_ffn_in, w_ffn_out,
           gla_w_in, gla_w_a1, gla_w_a2, gla_b_a, gla_norm_g, gla_w_out,
           nat_w_qkv, nat_rpb, nat_w_out,
           gm_w_in, gm_ln_g, gm_ln_b, gm_w_s, gm_b_s, gm_w_out,
           ssd_w_in, ssd_conv_w, ssd_conv_b, ssd_dt_bias, ssd_a_log, ssd_d, ssd_norm_g, ssd_w_out,
           final_g):
    x = (x_prompt.reshape(N_PROMPT, D_MODEL), x_sample.reshape(N_SAMPLE, D_MODEL))
    mod = _modulation_all(c, c_ctx, w_ada, b_ada)
    new_gla, new_k, new_v, new_ssd = [], [], [], []
    for l in range(DEPTH):
        kind, j = l % N_MIXERS, l // N_MIXERS
        x = _ffn(x, mod[l], norm_g[l, 0], w_ffn_in, w_ffn_out, l, 0, 0)
        mix = None
        if kind == 0:
            w = jnp.concatenate([gla_w_in[j], gla_w_a1[j, 0], gla_w_a1[j, 1]], axis=1)
            y = _proj_in(x, mod[l], norm_g[l, 1], _pad_cols(w, 640), 640)
            o, st = _gla_mixer(y, state_gla[:, j], gla_w_a2[j], gla_b_a[j], gla_norm_g[j])
            new_gla.append(st)
            mix = (o, gla_w_out[j], None)
        elif kind == 1:
            y = _proj_in(x, mod[l], norm_g[l, 1], nat_w_qkv[j], 768)
            op, kc, vc = _nat_context(y)
            os_ = _nat_latent(y, _heads_last(cache_nat_k[:, j]), _heads_last(cache_nat_v[:, j]), nat_rpb[j])
            new_k.append(kc)
            new_v.append(vc)
            mix = ((op, os_), nat_w_out[j], None)
        elif kind == 2:
            y = _proj_in(x, mod[l], norm_g[l, 1], gm_w_in[j], 512, act="gelu")
            x = _gmlp_out(x, y, mod[l], gm_ln_g[j], gm_ln_b[j], gm_w_s[j], gm_b_s[j], gm_w_out[j])
        else:
            y = _proj_in(x, mod[l], norm_g[l, 1], _ssd_w_in(ssd_w_in[j]), 512)
            o, st = _ssd_mixer(y, state_ssd[:, j], ssd_conv_w[j], ssd_conv_b[j], ssd_dt_bias[j],
                               ssd_a_log[j], ssd_d[j])
            new_ssd.append(st)
            mix = (o, ssd_w_out[j], ssd_norm_g[j])
        x = _ffn(x, mod[l], norm_g[l, 2], w_ffn_in, w_ffn_out, l, 1, 2, mix=mix,
                 final_g=final_g if l == DEPTH - 1 else None)
    y_prompt = x[0].reshape(BATCH, SEQ, D_MODEL)
    y_sample = x[1].reshape(DEC_BATCH, DEC_SEQ, D_MODEL)
    return (y_prompt, y_sample, jnp.stack(new_gla, axis=1), jnp.stack(new_k, axis=1),
            jnp.stack(new_v, axis=1), jnp.stack(new_ssd, axis=1))
```

```python
import functools

import jax
import jax.numpy as jnp
import numpy as np
from jax import lax
from jax.experimental import pallas as pl
from jax.experimental.pallas import tpu as pltpu

D_MODEL = 1024
BATCH = 32
SEQ = 256
DEPTH = 4
DEC_BATCH = 2
DEC_SEQ = 1024
N_PROMPT = BATCH * SEQ
N_SAMPLE = DEC_BATCH * DEC_SEQ
N_TOK = N_PROMPT + N_SAMPLE
N_GROUPS = 1 + DEC_BATCH

GRID_W = 64
N_MIXERS = 4
N_SUB = 3
N_MOD = 3 * N_SUB
D_FF = 2816
EPS = 1e-6
NEG_INF = -1e30
ROPE_THETA = 10000.0
GLA_H, GLA_DK, GLA_DV, GLA_RANK, GLA_TAU, GLA_CHUNK = 4, 128, 256, 16, 16.0, 16
GLA_IN = 2 * GLA_H * GLA_DK + 2 * GLA_H * GLA_DV
NAT_H, NAT_HD, NAT_WH, NAT_WW = 16, 64, 8, 16
GM_DH, GM_G, GM_CHUNK = 1024, 8, 128
GM_CG = GM_DH // GM_G
SSD_DI = 2 * D_MODEL
SSD_P = 64
SSD_H = SSD_DI // SSD_P
SSD_N, SSD_G, SSD_CONV, SSD_CHUNK = 128, 4, 3, 64
SSD_XBC = SSD_DI + 2 * SSD_G * SSD_N

LANE = 128
VMEM_LIMIT = 56 * 1024 * 1024
BF16 = jnp.bfloat16
F32 = jnp.float32

FF_CHUNK = 256
N_FF_CHUNKS = D_FF // FF_CHUNK
TM_FFN = 512
TM_PROJ = 512
ADA_TK = 128


def _group_of_tile(i, tm):
    n_prompt_tiles = N_PROMPT // tm
    return jnp.where(i < n_prompt_tiles, 0, 1 + (i - n_prompt_tiles) // (DEC_SEQ // tm))


def _resident(shape):
    nd = len(shape)
    return pl.BlockSpec(shape, lambda i: (0,) * nd, pipeline_mode=pl.Buffered(1))


def _stream_rows(tm, width, single_buffer_latent=False):
    npt = N_PROMPT // tm
    mode = dict(pipeline_mode=pl.Buffered(1)) if single_buffer_latent else {}
    return (pl.BlockSpec((tm, width), lambda i: (jnp.minimum(i, npt - 1), 0)),
            pl.BlockSpec((tm, width), lambda i: (jnp.maximum(i - npt, 0), 0), **mode))


def _premod(x, g, mod_ref, k):
    shift = mod_ref[3 * k:3 * k + 1, :]
    scale = mod_ref[3 * k + 1:3 * k + 2, :]
    gate = mod_ref[3 * k + 2:3 * k + 3, :]
    ms = jnp.mean(x * x, axis=-1, keepdims=True)
    h = x * lax.rsqrt(ms + EPS) * g
    return h * (1.0 + scale) + shift, gate


def _dot_nt(a, b):
    return lax.dot_general(a, b, (((1,), (1,)), ((), ())), preferred_element_type=F32)


def _dot_tn(a, b):
    return lax.dot_general(a, b, (((0,), (0,)), ((), ())), preferred_element_type=F32)


def _split3(x):
    hi = x.astype(BF16)
    r1 = x - hi.astype(F32)
    mid = r1.astype(BF16)
    lo = (r1 - mid.astype(F32)).astype(BF16)
    return hi, mid, lo


def _dot3(m, x):
    hi, mid, lo = _split3(x)
    d = lambda p: jnp.dot(m, p, preferred_element_type=F32)
    return d(hi) + d(mid) + d(lo)


def _ada_kernel(cond_ref, wa_ref, wb_ref, b_ref, o_ref):
    cnd = cond_ref[...]
    s = (cnd * jax.nn.sigmoid(cnd)).astype(BF16)
    p = jnp.concatenate([jnp.dot(s, w_ref[...].astype(BF16), preferred_element_type=F32)
                         for w_ref in (wa_ref, wb_ref)], axis=1)

    @pl.when(pl.program_id(1) == 0)
    def _():
        o_ref[...] = p + b_ref[...]

    @pl.when(pl.program_id(1) > 0)
    def _():
        o_ref[...] += p


def _modulation_all(c, c_ctx, w_ada, b_ada):
    rows = 8
    nk = D_MODEL // ADA_TK
    cond = jnp.concatenate([c_ctx[None], c, jnp.zeros((rows - N_GROUPS, D_MODEL), F32)], axis=0)
    cond = cond.reshape(rows, nk, ADA_TK).transpose(1, 0, 2)
    n_out = N_MOD * D_MODEL
    out = pl.pallas_call(
        _ada_kernel,
        grid=(DEPTH, nk),
        in_specs=[
            pl.BlockSpec((None, rows, ADA_TK), lambda l, k: (k, 0, 0)),
            pl.BlockSpec((None, ADA_TK, n_out // 2), lambda l, k: (l, k, 0)),
            pl.BlockSpec((None, ADA_TK, n_out // 2), lambda l, k: (l, k, 1)),
            pl.BlockSpec((None, 1, n_out), lambda l, k: (l, 0, 0)),
        ],
        out_specs=pl.BlockSpec((None, rows, n_out), lambda l, k: (l, 0, 0)),
        out_shape=jax.ShapeDtypeStruct((DEPTH, rows, n_out), F32),
        compiler_params=pltpu.CompilerParams(dimension_semantics=("arbitrary", "arbitrary")),
        name="ada_modulation",
    )(cond, w_ada, w_ada, b_ada.reshape(DEPTH, 1, n_out))
    return out[:, :N_GROUPS].reshape(DEPTH, N_GROUPS, N_MOD, D_MODEL)


def _ffn_weight_copies(win_hbm, wout_hbm, st_in, st_out, sem, l, half, j, slot):
    cols = lambda off: pl.ds(off + j * FF_CHUNK, FF_CHUNK)
    return (pltpu.make_async_copy(win_hbm.at[l, half, :, cols(0)], st_in.at[slot, 0], sem.at[slot, 0]),
            pltpu.make_async_copy(win_hbm.at[l, half, :, cols(D_FF)], st_in.at[slot, 1], sem.at[slot, 1]),
            pltpu.make_async_copy(wout_hbm.at[l, half, cols(0), :], st_out.at[slot], sem.at[slot, 2]))


def _ffn_kernel(*refs, l, half, k, split_in, mix, mix_norm, final):
    it = iter(refs)
    x_refs = (next(it), next(it)) if split_in else (next(it),)
    mod_ref, g_ref = next(it), next(it)
    o_in_refs, ng_ref, wmix_ref = ((next(it), next(it)), next(it), next(it)) if mix else (None, None, None)
    win_hbm, wout_hbm = next(it), next(it)
    fg_ref = next(it) if final else None
    out_refs = (next(it), next(it)) if final else (next(it),)
    acc_ref, win_ref, wout_ref, st_in, st_out, sem = it
    i = pl.program_id(0)
    is_prompt = i < N_PROMPT // x_refs[0].shape[0]
    copies = functools.partial(_ffn_weight_copies, win_hbm, wout_hbm, st_in, st_out, sem, l, half)

    def row_tile(fetch_weights):
        if fetch_weights:
            for c in copies(0, 0):
                c.start()
        x = jnp.where(is_prompt, x_refs[0][...], x_refs[1][...]) if split_in else x_refs[0][...]
        if mix:
            o_in = jnp.where(is_prompt, o_in_refs[0][...], o_in_refs[1][...])
            if mix_norm:
                o_in = o_in * lax.rsqrt(jnp.mean(o_in * o_in, axis=-1, keepdims=True) + EPS) * ng_ref[...]
            x = x + mod_ref[5:6, :] * jnp.dot(o_in.astype(BF16), wmix_ref[...], preferred_element_type=F32)
        h, gate = _premod(x, g_ref[...], mod_ref, k)
        hb = h.astype(BF16)
        for j in range(N_FF_CHUNKS):
            gate_cols = slice(j * FF_CHUNK, (j + 1) * FF_CHUNK)
            up_cols = slice(D_FF + j * FF_CHUNK, D_FF + (j + 1) * FF_CHUNK)
            if fetch_weights:
                slot = j % 2
                if j + 1 < N_FF_CHUNKS:
                    for c in copies(j + 1, 1 - slot):
                        c.start()
                for c in copies(j, slot):
                    c.wait()
                store_chunk(j, slot)
            a = jnp.dot(hb, win_ref[:, gate_cols], preferred_element_type=F32)
            u = jnp.dot(hb, win_ref[:, up_cols], preferred_element_type=F32)
            t = (a * jax.nn.sigmoid(a) * u).astype(BF16)
            p = jnp.dot(t, wout_ref[gate_cols, :], preferred_element_type=F32)
            if j == 0:
                acc_ref[...] = p
            else:
                acc_ref[...] += p
        y = x + 0.5 * gate * acc_ref[...]
        if not final:
            out_refs[0][...] = y
        else:
            y = y * lax.rsqrt(jnp.mean(y * y, axis=-1, keepdims=True) + EPS) * fg_ref[...]

            @pl.when(is_prompt)
            def _():
                out_refs[0][...] = y

            @pl.when(jnp.logical_not(is_prompt))
            def _():
                out_refs[1][...] = y

    def store_chunk(j, slot):
        win_ref[:, j * FF_CHUNK:(j + 1) * FF_CHUNK] = st_in[slot, 0].astype(BF16)
        win_ref[:, D_FF + j * FF_CHUNK:D_FF + (j + 1) * FF_CHUNK] = st_in[slot, 1].astype(BF16)
        wout_ref[j * FF_CHUNK:(j + 1) * FF_CHUNK, :] = st_out[slot].astype(BF16)

    def fetch_all_weights():
        for c in copies(0, 0):
            c.start()
        for j in range(N_FF_CHUNKS):
            slot = j % 2
            if j + 1 < N_FF_CHUNKS:
                for c in copies(j + 1, 1 - slot):
                    c.start()
            for c in copies(j, slot):
                c.wait()
            store_chunk(j, slot)

    if mix_norm:
        pl.when(i == 0)(fetch_all_weights)
        row_tile(False)
    else:
        pl.when(i == 0)(functools.partial(row_tile, True))
        pl.when(i > 0)(functools.partial(row_tile, False))


def _ffn(x, mod_l, g, w_in_all, w_out_all, l, half, k, mix=None, final_g=None):
    tm = TM_FFN
    split_in = isinstance(x, tuple)
    row = pl.BlockSpec((tm, D_MODEL), lambda i: (i, 0))
    vec = lambda n: pl.BlockSpec((1, n), lambda i: (0, 0))
    in_specs = list(_stream_rows(tm, D_MODEL)) if split_in else [row]
    args = list(x) if split_in else [x]
    in_specs += [pl.BlockSpec((None, N_MOD, D_MODEL), lambda i: (_group_of_tile(i, tm), 0, 0)), vec(D_MODEL)]
    args += [mod_l, g.reshape(1, D_MODEL)]
    if mix is not None:
        o_in, w_mix, norm_g = mix
        kdim = w_mix.shape[0]
        ng = jnp.ones((1, kdim), F32) if norm_g is None else norm_g.reshape(1, kdim)
        in_specs += [*_stream_rows(tm, kdim, single_buffer_latent=kdim > D_MODEL), vec(kdim),
                     _resident((kdim, D_MODEL))]
        args += [*o_in, ng, w_mix.astype(BF16)]
    in_specs += [pl.BlockSpec(memory_space=pl.ANY), pl.BlockSpec(memory_space=pl.ANY)]
    args += [w_in_all, w_out_all]
    if final_g is not None:
        in_specs.append(vec(D_MODEL))
        args.append(final_g.reshape(1, D_MODEL))
        out_specs = list(_stream_rows(tm, D_MODEL))
        out_shape = [jax.ShapeDtypeStruct((N_PROMPT, D_MODEL), F32), jax.ShapeDtypeStruct((N_SAMPLE, D_MODEL), F32)]
    else:
        out_specs, out_shape = row, jax.ShapeDtypeStruct((N_TOK, D_MODEL), F32)
    return pl.pallas_call(
        functools.partial(_ffn_kernel, l=l, half=half, k=k, split_in=split_in, mix=mix is not None,
                          mix_norm=mix is not None and mix[2] is not None, final=final_g is not None),
        grid=(N_TOK // tm,),
        in_specs=in_specs, out_specs=out_specs, out_shape=out_shape,
        scratch_shapes=[pltpu.VMEM((tm, D_MODEL), F32),
                        pltpu.VMEM((D_MODEL, 2 * D_FF), BF16), pltpu.VMEM((D_FF, D_MODEL), BF16),
                        pltpu.VMEM((2, 2, D_MODEL, FF_CHUNK), F32), pltpu.VMEM((2, FF_CHUNK, D_MODEL), F32),
                        pltpu.SemaphoreType.DMA((2, 3))],
        compiler_params=pltpu.CompilerParams(dimension_semantics=("arbitrary",),
                                             vmem_limit_bytes=VMEM_LIMIT),
        name="ffn_swiglu",
    )(*args)


def _proj_in_kernel(x_ref, mod_ref, g_ref, *refs, widths, tn, act):
    w_refs, o_ref = refs[:-1], refs[-1]
    h, _ = _premod(x_ref[...], g_ref[...], mod_ref, 1)
    hb = h.astype(BF16)
    off = 0
    for w_ref, n_use in zip(w_refs, widths):
        for j0 in range(0, n_use, tn):
            wd = min(tn, n_use - j0)
            y = jnp.dot(hb, w_ref[:, j0:j0 + wd], preferred_element_type=F32)
            if act == "gelu":
                y = jax.nn.gelu(y)
            o_ref[:, off + j0:off + j0 + wd] = y
        off += n_use


def _proj_in(x, mod_l, g, pieces, tn, act=None):
    tm = TM_PROJ
    widths = tuple(n for _, n in pieces)
    n_out = sum(widths)
    return pl.pallas_call(
        functools.partial(_proj_in_kernel, widths=widths, tn=tn, act=act),
        grid=(N_TOK // tm,),
        in_specs=[
            pl.BlockSpec((tm, D_MODEL), lambda i: (i, 0)),
            pl.BlockSpec((None, N_MOD, D_MODEL), lambda i: (_group_of_tile(i, tm), 0, 0)),
            pl.BlockSpec((1, D_MODEL), lambda i: (0, 0)),
            *[_resident(w.shape) for w, _ in pieces],
        ],
        out_specs=pl.BlockSpec((tm, n_out), lambda i: (i, 0)),
        out_shape=jax.ShapeDtypeStruct((N_TOK, n_out), F32),
        compiler_params=pltpu.CompilerParams(dimension_semantics=("arbitrary",),
                                             vmem_limit_bytes=VMEM_LIMIT),
        name="mixer_proj_in",
    )(x, mod_l, g.reshape(1, D_MODEL), *[w.astype(BF16) for w, _ in pieces])


GLA_BLK = 128
GLA_ZCOL = GLA_IN // LANE


def _gla_consts():
    r = np.arange(GLA_BLK)
    same = (r[:, None] // GLA_CHUNK) == (r[None, :] // GLA_CHUNK)
    ri, ci = r[:, None] % GLA_CHUNK, r[None, :] % GLA_CHUNK
    lf = np.concatenate([same & (ci <= ri), same & (ci > ri)], axis=0)
    lb = np.concatenate([same & (ci >= ri), same & (ci < ri)], axis=0)
    rows = np.arange(GLA_CHUNK * GLA_DK)
    sel = (rows[:, None] // GLA_DK) == (np.arange(LANE)[None, :] % GLA_CHUNK)
    return (jnp.asarray(lf, BF16), jnp.asarray(lb, BF16), jnp.asarray(sel, BF16))


def _rope_tables(T):
    half = GLA_DK // 2
    t = np.arange(T)
    inv = ROPE_THETA ** (-np.arange(0, half, 2, dtype=np.float64) / half)
    lane = np.arange(GLA_DK)
    pos = np.where(lane[None, :] < half, (t // GRID_W)[:, None], (t % GRID_W)[:, None])
    ang = pos * inv[lane % (half // 2)][None, :]
    sign = np.where((lane % half) < half // 2, -1.0, 1.0)[None, :]
    return jnp.asarray(np.cos(ang), F32), jnp.asarray(np.sin(ang) * sign, F32)


def _rope_apply(x, cos, sin_signed):
    half = GLA_DK // 2
    lane = lax.broadcasted_iota(jnp.int32, (1, GLA_DK), 1)
    partner = jnp.where((lane % half) < half // 2,
                        pltpu.roll(x, GLA_DK - half // 2, axis=1), pltpu.roll(x, half // 2, axis=1))
    return x * cos + partner * sin_signed


def _gla_kernel(*refs, use_rope, has_s0):
    it = iter(refs)
    q_ref, k_ref, v_ref, r_ref, za_ref, w2_ref, ba_ref, ng_ref, lf_ref, lb_ref, sel_ref = (next(it) for _ in range(11))
    cos_ref, sin_ref = (next(it), next(it)) if use_rope else (None, None)
    s0_ref = next(it) if has_s0 else None
    o_ref = next(it)
    st_ref = None if has_s0 else next(it)
    q_s, k_s, bf_s, bb_s, qdf_s, kdf_s, qdb_s, kdb_s, w_s, sf_s, sb_s = it

    T = q_ref.shape[0]
    nblk, nch = T // GLA_BLK, T // GLA_CHUNK
    hq = lambda h: slice(h * GLA_DK, (h + 1) * GLA_DK)
    hv = lambda h: slice(h * GLA_DV, (h + 1) * GLA_DV)

    for blk in range(nblk):
        rows = slice(blk * GLA_BLK, (blk + 1) * GLA_BLK)
        zab = za_ref[rows, :].astype(BF16)
        for h in range(GLA_H):
            z = jnp.dot(zab, w2_ref[h], preferred_element_type=F32) + ba_ref[h]
            la = (jnp.minimum(z, 0.0) - jnp.log1p(jnp.exp(-jnp.abs(z)))) * (1.0 / GLA_TAU)
            cf = _dot3(lf_ref[...], la[:, :GLA_DK])
            cb = _dot3(lb_ref[...], la[:, GLA_DK:])
            bf, bb = cf[:GLA_BLK], cb[:GLA_BLK]
            q = q_ref[rows, hq(h)] * (GLA_DK ** -0.5)
            k = k_ref[rows, hq(h)]
            if use_rope:
                q = _rope_apply(q, cos_ref[rows, :], sin_ref[rows, :])
                k = _rope_apply(k, cos_ref[rows, :], sin_ref[rows, :])
            q_s[h, rows, :], k_s[h, rows, :] = q, k
            bf_s[h, rows, :], bb_s[h, rows, :] = bf, bb
            qdf_s[h, rows, :] = (q * jnp.exp(bf)).astype(BF16)
            qdb_s[h, rows, :] = (q * jnp.exp(bb)).astype(BF16)
            kdf_s[h, rows, :] = (k * jnp.exp(cf[GLA_BLK:])).astype(BF16)
            kdb_s[h, rows, :] = (k * jnp.exp(cb[GLA_BLK:])).astype(BF16)

    irow = lax.broadcasted_iota(jnp.int32, (GLA_CHUNK, 1), 0)
    lane_c = lax.broadcasted_iota(jnp.int32, (GLA_BLK, LANE), 1) // GLA_CHUNK
    row_c = lax.broadcasted_iota(jnp.int32, (GLA_BLK, LANE), 0) // GLA_CHUNK
    for h in range(GLA_H):
        def intra_chunk(c, carry, h=h):
            rows = pl.ds(pl.multiple_of(c * GLA_CHUNK, GLA_CHUNK), GLA_CHUNK)
            qc, bfc, bbc = q_s[h, rows, :], bf_s[h, rows, :], bb_s[h, rows, :]
            for j in range(GLA_CHUNK):
                row = pl.ds(c * GLA_CHUNK + j, 1)
                e = (jnp.exp(jnp.where(irow >= j, bfc - bf_s[h, row, :], bbc - bb_s[h, row, :]))
                     + jnp.where(irow == j, 1.0, 0.0))
                w_s[rows, j * GLA_DK:(j + 1) * GLA_DK] = (qc * k_s[h, row, :] * e).astype(BF16)
            return carry

        lax.fori_loop(0, nch, intra_chunk, 0)
        a_all = jnp.dot(w_s[...], sel_ref[...], preferred_element_type=F32)
        for blk in range(nblk):
            rows = slice(blk * GLA_BLK, (blk + 1) * GLA_BLK)
            a = jnp.where(lane_c == row_c, a_all[rows, :], 0.0).astype(BF16)
            o_ref[rows, hv(h)] = jnp.dot(a, v_ref[rows, hv(h)].astype(BF16), preferred_element_type=F32)

    for h in range(GLA_H):
        if has_s0:
            sf_s[h] = s0_ref[0, h].T
            sb_s[h] = s0_ref[1, h].T
        else:
            sf_s[h] = jnp.zeros((GLA_DV, GLA_DK), F32)
            sb_s[h] = jnp.zeros((GLA_DV, GLA_DK), F32)

    def chain(h, rows, g_row, qd_s, kd_s, b_s, st_s):
        s = st_s[h]
        o_ref[rows, hv(h)] += _dot_nt(qd_s[h, rows, :], s.astype(BF16))
        u = _dot_tn(v_ref[rows, hv(h)].astype(BF16), kd_s[h, rows, :])
        st_s[h] = s * jnp.exp(b_s[h, g_row, :]) + u

    def inter_chunk(n, carry):
        cf_ = pl.multiple_of(n * GLA_CHUNK, GLA_CHUNK)
        cb_ = pl.multiple_of((nch - 1 - n) * GLA_CHUNK, GLA_CHUNK)
        for h in range(GLA_H):
            chain(h, pl.ds(cf_, GLA_CHUNK), pl.ds(cf_ + GLA_CHUNK - 1, 1), qdf_s, kdf_s, bf_s, sf_s)
            chain(h, pl.ds(cb_, GLA_CHUNK), pl.ds(cb_, 1), qdb_s, kdb_s, bb_s, sb_s)
        return carry

    lax.fori_loop(0, nch, inter_chunk, 0, unroll=8)

    for h in range(GLA_H):
        o = o_ref[:, hv(h)]
        o = o * lax.rsqrt(jnp.mean(o * o, axis=-1, keepdims=True) + EPS) * ng_ref[h]
        r = r_ref[:, hv(h)]
        o_ref[:, hv(h)] = o * (r * jax.nn.sigmoid(r))
        if not has_s0:
            st_ref[0, h] = sf_s[h].T
            st_ref[1, h] = sb_s[h].T


def _gla_call(y, w2, ba, ng, T, n_seq, row0, use_rope, s0):
    has_s0 = s0 is not None
    rb = row0 // T
    nq, nv = GLA_H * GLA_DK, GLA_H * GLA_DV
    mode = dict(pipeline_mode=pl.Buffered(1)) if n_seq <= 2 else {}
    col = lambda w, j: pl.BlockSpec((T, w), lambda b: (rb + b, j), **mode)
    cst = lambda a: pl.BlockSpec(a.shape, lambda b: (0,) * a.ndim)
    lf, lb, sel = _gla_consts()
    in_specs = [col(nq, 0), col(nq, 1), col(nv, 1), col(nv, 2), col(LANE, GLA_ZCOL),
                cst(w2), cst(ba), cst(ng), cst(lf), cst(lb), cst(sel)]
    args = [y, y, y, y, y, w2, ba, ng, lf, lb, sel]
    if use_rope:
        cos, sin = _rope_tables(T)
        in_specs += [cst(cos), cst(sin)]
        args += [cos, sin]
    st_spec = pl.BlockSpec((None, 2, GLA_H, GLA_DK, GLA_DV), lambda b: (b, 0, 0, 0, 0))
    out_specs = [pl.BlockSpec((T, nv), lambda b: (b, 0))]
    out_shape = [jax.ShapeDtypeStruct((n_seq * T, nv), F32)]
    if has_s0:
        in_specs.append(st_spec)
        args.append(s0)
    else:
        out_specs.append(st_spec)
        out_shape.append(jax.ShapeDtypeStruct((n_seq, 2, GLA_H, GLA_DK, GLA_DV), F32))
    scratch = ([pltpu.VMEM((GLA_H, T, GLA_DK), F32)] * 4 + [pltpu.VMEM((GLA_H, T, GLA_DK), BF16)] * 4
               + [pltpu.VMEM((T, GLA_CHUNK * GLA_DK), BF16),
                  pltpu.VMEM((GLA_H, GLA_DV, GLA_DK), F32), pltpu.VMEM((GLA_H, GLA_DV, GLA_DK), F32)])
    outs = pl.pallas_call(
        functools.partial(_gla_kernel, use_rope=use_rope, has_s0=has_s0),
        grid=(n_seq,),
        in_specs=in_specs, out_specs=out_specs, out_shape=out_shape,
        scratch_shapes=scratch,
        compiler_params=pltpu.CompilerParams(dimension_semantics=("arbitrary",),
                                             vmem_limit_bytes=VMEM_LIMIT),
        name="gla_rope" if use_rope else "gla",
    )(*args)
    return outs[0], (None if has_s0 else outs[1])


def _gla_mixer(y, state, w_a2, b_a, norm_g):
    w2 = jnp.zeros((GLA_H, LANE, 2 * GLA_DK), F32)
    for e in range(2):
        we = w_a2[e].reshape(GLA_RANK, GLA_H, GLA_DK).transpose(1, 0, 2)
        w2 = w2.at[:, e * GLA_RANK:(e + 1) * GLA_RANK, e * GLA_DK:(e + 1) * GLA_DK].set(we)
    w2 = w2.astype(BF16)
    ba = b_a.reshape(2, GLA_H, GLA_DK).transpose(1, 0, 2).reshape(GLA_H, 1, 2 * GLA_DK)
    ng = norm_g.reshape(GLA_H, 1, GLA_DV)
    op, st = _gla_call(y, w2, ba, ng, SEQ, BATCH, 0, False, None)
    os_, _ = _gla_call(y, w2, ba, ng, DEC_SEQ, DEC_BATCH, N_PROMPT, True, state)
    return (op, os_), st


N_HEAD_PAIRS = NAT_H // 2
NAT_ROWS = DEC_SEQ // GRID_W
NAT_WIN = NAT_WH * GRID_W
NAT_NDR = 2 * NAT_WH - 1
NAT_NDC = 2 * NAT_WW - 1


def _nat_row_window(r):
    rs = min(max(r - NAT_WH // 2, 0), NAT_ROWS - NAT_WH)
    return rs, r - rs


def _nat_bias_table(rpb):
    qc = np.arange(GRID_W)[:, None]
    kc = np.arange(GRID_W)[None, :]
    c_start = np.clip(qc - NAT_WW // 2, 0, GRID_W - NAT_WW)
    ok = (kc >= c_start) & (kc < c_start + NAT_WW)
    dc = np.clip(kc - qc + NAT_WW - 1, 0, NAT_NDC - 1)
    pick = dc[None] == np.arange(NAT_NDC)[:, None, None]
    pick2 = np.zeros((2, NAT_NDC, GRID_W, 2, GRID_W), np.float32)
    for s in range(2):
        pick2[s, :, :, s, :] = pick
    pick2 = jnp.asarray(pick2.reshape(2 * NAT_NDC, GRID_W, 2 * GRID_W))
    rows2 = jnp.concatenate([rpb[:, :NAT_NDR - 1], rpb[:, 1:]], axis=2)
    t = jnp.einsum('hdy,yql->hdql', rows2, pick2, precision=lax.Precision.HIGHEST)
    return jnp.where(np.tile(ok, (1, 2))[None, None], t, NEG_INF)


def _head_mask(hh):
    lane = lax.broadcasted_iota(jnp.int32, (1, LANE), 1)
    return (lane < NAT_HD) if hh == 0 else (lane >= NAT_HD)


def _nat_ctx_kernel(q_ref, k_ref, v_ref, o_ref, kc_ref, vc_ref):
    for hp in range(N_HEAD_PAIRS):
        cols = slice(hp * LANE, (hp + 1) * LANE)
        q = q_ref[:, cols] * (NAT_HD ** -0.5)
        k, v = k_ref[:, cols], v_ref[:, cols]
        kb, vb = k.astype(BF16), v.astype(BF16)
        q2 = jnp.concatenate([jnp.where(_head_mask(hh), q, 0.0) for hh in range(2)], axis=0).astype(BF16)
        s = _dot_nt(q2, kb)
        p = jnp.exp(s - jnp.max(s, axis=-1, keepdims=True))
        l = jnp.sum(p, axis=-1, keepdims=True)
        o2 = jnp.dot(p.astype(BF16), vb, preferred_element_type=F32) / l
        o_ref[:, cols] = jnp.where(_head_mask(0), o2[:SEQ], o2[SEQ:])
        for hh in range(2):
            kc_ref[2 * hp + hh] = k[:, hh * NAT_HD:(hh + 1) * NAT_HD]
            vc_ref[2 * hp + hh] = v[:, hh * NAT_HD:(hh + 1) * NAT_HD]


def _nat_context(y):
    blk = lambda j: pl.BlockSpec((SEQ, D_MODEL), lambda b: (b, j))
    cache = pl.BlockSpec((None, NAT_H, SEQ, NAT_HD), lambda b: (b, 0, 0, 0))
    cache_shape = jax.ShapeDtypeStruct((BATCH, NAT_H, SEQ, NAT_HD), F32)
    return pl.pallas_call(
        _nat_ctx_kernel,
        grid=(BATCH,),
        in_specs=[blk(0), blk(1), blk(2)],
        out_specs=[pl.BlockSpec((SEQ, D_MODEL), lambda b: (b, 0)), cache, cache],
        out_shape=[jax.ShapeDtypeStruct((N_PROMPT, D_MODEL), F32), cache_shape, cache_shape],
        compiler_params=pltpu.CompilerParams(dimension_semantics=("arbitrary",)),
        name="nat_context",
    )(y, y, y)


def _nat_lat_kernel(q_ref, k_ref, v_ref, ck_ref, cv_ref, tab_ref, o_ref):
    q = q_ref[...] * (NAT_HD ** -0.5)
    qm = [jnp.where(_head_mask(hh), q, 0.0).astype(BF16) for hh in range(2)]
    ckb = ck_ref[...].astype(BF16)
    cvb = cv_ref[...].astype(BF16)
    for r in range(NAT_ROWS):
        rs, off = _nat_row_window(r)
        kw = k_ref[rs * GRID_W:rs * GRID_W + NAT_WIN, :].astype(BF16)
        vw = v_ref[rs * GRID_W:rs * GRID_W + NAT_WIN, :].astype(BF16)
        qr = jnp.concatenate([qm[hh][r * GRID_W:(r + 1) * GRID_W] for hh in range(2)], axis=0)
        bias = jnp.concatenate(
            [jnp.concatenate([tab_ref[hh, w - off + NAT_WH - 1] for w in range(0, NAT_WH, 2)], axis=1)
             for hh in range(2)], axis=0)
        s_lat = _dot_nt(qr, kw) + bias
        s_ctx = _dot_nt(qr, ckb)
        m = jnp.maximum(jnp.max(s_lat, axis=-1, keepdims=True), jnp.max(s_ctx, axis=-1, keepdims=True))
        p_lat = jnp.exp(s_lat - m)
        p_ctx = jnp.exp(s_ctx - m)
        l = jnp.sum(p_lat, axis=-1, keepdims=True) + jnp.sum(p_ctx, axis=-1, keepdims=True)
        o2 = (jnp.dot(p_lat.astype(BF16), vw, preferred_element_type=F32)
              + jnp.dot(p_ctx.astype(BF16), cvb, preferred_element_type=F32)) / l
        o_ref[r * GRID_W:(r + 1) * GRID_W, :] = jnp.where(_head_mask(0), o2[:GRID_W], o2[GRID_W:])


def _nat_latent(y, ck, cv, rpb):
    row0 = N_PROMPT // DEC_SEQ
    blk = lambda off: pl.BlockSpec((DEC_SEQ, LANE), lambda b, hp: (row0 + b, off + hp))
    ctx = pl.BlockSpec((None, ck.shape[1], LANE), lambda b, hp: (b, 0, hp))
    return pl.pallas_call(
        _nat_lat_kernel,
        grid=(DEC_BATCH, N_HEAD_PAIRS),
        in_specs=[blk(0), blk(N_HEAD_PAIRS), blk(2 * N_HEAD_PAIRS), ctx, ctx,
                  pl.BlockSpec((2, NAT_NDR - 1, GRID_W, 2 * GRID_W), lambda b, hp: (hp, 0, 0, 0))],
        out_specs=pl.BlockSpec((DEC_SEQ, LANE), lambda b, hp: (b, hp)),
        out_shape=jax.ShapeDtypeStruct((N_SAMPLE, D_MODEL), F32),
        compiler_params=pltpu.CompilerParams(dimension_semantics=("arbitrary", "arbitrary")),
        name="nat_latent",
    )(y, y, y, ck, cv, _nat_bias_table(rpb))


def _heads_last(t):
    b, h, s, d = t.shape
    return t.transpose(0, 2, 1, 3).reshape(b, s, h * d)


def _gmlp_kernel(x_ref, y_ref, mod_ref, lng_ref, lnb_ref, ws_ref, bs_ref, w_ref, o_ref, t_ref):
    tm = x_ref.shape[0]
    v = y_ref[:, GM_DH:]
    vc = v - jnp.mean(v, axis=-1, keepdims=True)
    vn = vc * lax.rsqrt(jnp.mean(vc * vc, axis=-1, keepdims=True) + EPS) * lng_ref[...] + lnb_ref[...]
    vn = vn.astype(BF16)
    for n in range(tm // GM_CHUNK):
        rows = slice(n * GM_CHUNK, (n + 1) * GM_CHUNK)
        for g in range(GM_G):
            cols = slice(g * GM_CG, (g + 1) * GM_CG)
            sp = jnp.dot(ws_ref[g], vn[rows, cols], preferred_element_type=F32) + bs_ref[:, cols]
            t_ref[rows, cols] = (y_ref[rows, cols] * sp).astype(BF16)
    gate = mod_ref[5:6, :]
    o_ref[...] = x_ref[...] + gate * jnp.dot(t_ref[...], w_ref[...], preferred_element_type=F32)


def _gmlp_out(x, y, mod_l, ln_g, ln_b, w_s, b_s, w_out):
    tm = TM_PROJ
    bias = jnp.repeat(b_s.T, GM_CG, axis=1)
    return pl.pallas_call(
        _gmlp_kernel,
        grid=(N_TOK // tm,),
        in_specs=[
            pl.BlockSpec((tm, D_MODEL), lambda i: (i, 0)),
            pl.BlockSpec((tm, 2 * GM_DH), lambda i: (i, 0)),
            pl.BlockSpec((None, N_MOD, D_MODEL), lambda i: (_group_of_tile(i, tm), 0, 0)),
            pl.BlockSpec((1, GM_DH), lambda i: (0, 0)),
            pl.BlockSpec((1, GM_DH), lambda i: (0, 0)),
            _resident((GM_G, GM_CHUNK, GM_CHUNK)),
            _resident((GM_CHUNK, GM_DH)),
            _resident((GM_DH, D_MODEL)),
        ],
        out_specs=pl.BlockSpec((tm, D_MODEL), lambda i: (i, 0)),
        out_shape=jax.ShapeDtypeStruct((N_TOK, D_MODEL), F32),
        scratch_shapes=[pltpu.VMEM((tm, GM_DH), BF16)],
        compiler_params=pltpu.CompilerParams(dimension_semantics=("arbitrary",),
                                             vmem_limit_bytes=VMEM_LIMIT),
        name="gmlp_gate_out",
    )(x, y, mod_l, ln_g.reshape(1, GM_DH), ln_b.reshape(1, GM_DH), w_s.astype(BF16), bias,
      w_out.astype(BF16))


SSD_E = SSD_H // SSD_G
SSD_GP = SSD_E * SSD_P
SSD_BLK = 2 * SSD_CHUNK
SSD_COL_X = SSD_DI
SSD_COL_B = 2 * SSD_DI
SSD_COL_C = SSD_COL_B + SSD_G * SSD_N
SSD_COL_DT = SSD_COL_C + SSD_G * SSD_N


def _ssd_w_dt(w_in):
    base = SSD_DI + SSD_XBC
    w_dt = jnp.zeros((D_MODEL, SSD_G, LANE), F32)
    for g in range(SSD_G):
        cols = jnp.concatenate([w_in[:, base + g * SSD_E:base + (g + 1) * SSD_E],
                                w_in[:, base + SSD_H + g * SSD_E:base + SSD_H + (g + 1) * SSD_E]], axis=1)
        w_dt = w_dt.at[:, g, :2 * SSD_E].set(cols).at[:, g, 2 * SSD_E:4 * SSD_E].set(cols)
    return w_dt.reshape(D_MODEL, SSD_G * LANE)


def _ssd_cum_matrix():
    r = np.arange(SSD_BLK)
    same = (r[:, None] // SSD_CHUNK) == (r[None, :] // SSD_CHUNK)
    cum = np.concatenate([same & (r[None, :] <= r[:, None]), same & (r[None, :] >= r[:, None])], axis=0)
    return jnp.asarray(cum, BF16)


def _softplus(x):
    return jnp.maximum(x, 0.0) + jnp.log1p(jnp.exp(-jnp.abs(x)))


def _ssd_kernel(*refs, has_s0):
    it = iter(refs)
    (z_ref, x_ref, b_ref, c_ref, dt_ref, cwx_ref, cwb_ref, cwc_ref, cbx_ref, cbb_ref, cbc_ref,
     dtb_ref, alog_ref, dsk_ref, cum_ref) = (next(it) for _ in range(15))
    s0_ref = next(it) if has_s0 else None
    o_ref = next(it)
    st_ref = None if has_s0 else next(it)
    xs_s, xb_s, bm_s, cm_s, cu_s, dt_s, y_s, sf_s, sb_s = it

    T = x_ref.shape[0]
    nch = T // SSD_CHUNK
    L = SSD_CHUNK

    trow = lax.broadcasted_iota(jnp.int32, (T, 1), 0)

    def conv_silu(v_ref, w_ref, bias_ref):
        v = v_ref[...]
        prev = jnp.where(trow == 0, 0.0, pltpu.roll(v, 1, axis=0))
        nxt = jnp.where(trow == T - 1, 0.0, pltpu.roll(v, T - 1, axis=0))
        y = prev * w_ref[0:1, :] + v * w_ref[1:2, :] + nxt * w_ref[2:3, :] + bias_ref[...]
        return y * jax.nn.sigmoid(y)

    xs = conv_silu(x_ref, cwx_ref, cbx_ref)
    xs_s[...] = xs
    xb_s[...] = xs.astype(BF16)
    bm_s[...] = conv_silu(b_ref, cwb_ref, cbb_ref).astype(BF16)
    cm_s[...] = conv_silu(c_ref, cwc_ref, cbc_ref).astype(BF16)

    lane1 = lax.broadcasted_iota(jnp.int32, (1, LANE), 1)
    a_row = jnp.where(lane1 < 2 * SSD_E, -jnp.exp(alog_ref[...]), 0.0)
    for blk in range(T // SSD_BLK):
        rows = slice(blk * SSD_BLK, (blk + 1) * SSD_BLK)
        dt = _softplus(dt_ref[rows, :] + dtb_ref[...])
        c2 = _dot3(cum_ref[...], dt * a_row)
        cu_s[rows, :] = jnp.where(lane1 < SSD_E, c2[:SSD_BLK], c2[SSD_BLK:])
        dt_s[rows, :] = dt

    ii = lax.broadcasted_iota(jnp.int32, (L, LANE), 0)
    jj = lax.broadcasted_iota(jnp.int32, (L, LANE), 1)
    fwd_half = jj < L
    fwd_half1 = lane1 < L
    tri = (fwd_half & (ii >= jj)) | ((jj >= L) & (ii <= jj - L))
    left = lane1 < SSD_P
    for c in range(nch):
        rows = slice(c * L, (c + 1) * L)
        cum_c, dt_c = cu_s[rows, :], dt_s[rows, :]
        bm_c, cm_c = bm_s[rows, :], cm_s[rows, :]
        cb2 = _dot_nt(cm_c, jnp.concatenate([bm_c, bm_c], axis=0))
        arr = jnp.where(lane1 < 2 * SSD_E, cum_c, dt_c)
        arr_t = jnp.concatenate([arr, arr], axis=0).T
        gs = []
        for e in range(SSD_E):
            row_c = jnp.where(fwd_half1, arr_t[e:e + 1, :], arr_t[SSD_E + e:SSD_E + e + 1, :])
            row_dt = jnp.where(fwd_half1, arr_t[2 * SSD_E + e:2 * SSD_E + e + 1, :],
                               arr_t[3 * SSD_E + e:3 * SSD_E + e + 1, :])
            col_c = jnp.where(fwd_half, jnp.broadcast_to(cum_c[:, e:e + 1], (L, LANE)),
                              jnp.broadcast_to(cum_c[:, SSD_E + e:SSD_E + e + 1], (L, LANE)))
            dec = jnp.exp(jnp.where(tri, col_c - row_c, -jnp.inf))
            gs.append((cb2 * dec * row_dt).astype(BF16))
        for pr in range(SSD_E // 2):
            cols = slice(pr * LANE, (pr + 1) * LANE)
            xp = xb_s[rows, cols]
            xl = jnp.where(left, xp, jnp.zeros_like(xp))
            xr = jnp.where(left, jnp.zeros_like(xp), xp)
            lhs = jnp.concatenate([gs[2 * pr], gs[2 * pr + 1]], axis=1)
            rhs = jnp.concatenate([xl, xl, xr, xr], axis=0)
            y_s[rows, cols] = (jnp.dot(lhs, rhs, preferred_element_type=F32)
                               + dsk_ref[:, cols] * xs_s[rows, cols])

    if has_s0:
        sf_s[...] = s0_ref[0].T
        sb_s[...] = s0_ref[1].T
    else:
        sf_s[...] = jnp.zeros_like(sf_s)
        sb_s[...] = jnp.zeros_like(sb_s)

    def per_head_cols(v, lane0):
        tiles = []
        for t in range(SSD_E // 2):
            a, b = lane0 + 2 * t, lane0 + 2 * t + 1
            tiles.append(jnp.where(left, jnp.broadcast_to(v[:, a:a + 1], (L, LANE)),
                                   jnp.broadcast_to(v[:, b:b + 1], (L, LANE))))
        return jnp.concatenate(tiles, axis=1)

    def chain(c, last, lane0, st_s):
        rows = slice(c * L, (c + 1) * L)
        cum_c = jnp.where((lane1 >= lane0) & (lane1 < lane0 + SSD_E), cu_s[rows, :], 0.0)
        tot = cum_c[last:last + 1, :]
        qd = per_head_cols(jnp.exp(cum_c), lane0)
        w = per_head_cols(jnp.exp(tot - cum_c) * dt_s[rows, :], lane0)
        s = st_s[...]
        y_s[rows, :] += jnp.dot(cm_s[rows, :], s.astype(BF16), preferred_element_type=F32) * qd
        xw = (xs_s[rows, :] * w).astype(BF16)
        st_s[...] = s * qd[last:last + 1, :] + _dot_tn(bm_s[rows, :], xw)

    for n in range(nch):
        chain(n, L - 1, 0, sf_s)
        chain(nch - 1 - n, 0, SSD_E, sb_s)

    z = z_ref[...]
    o_ref[...] = y_s[...] * (z * jax.nn.sigmoid(z))
    if not has_s0:
        st_ref[0] = sf_s[...].T
        st_ref[1] = sb_s[...].T


def _ssd_call(y, prm, T, n_seq, row0, s0):
    has_s0 = s0 is not None
    rb = row0 // T
    col = lambda w, off: pl.BlockSpec((T, w), lambda b, g: (rb + b, off // w + g))
    wcol = lambda rows, w, off: pl.BlockSpec((rows, w), lambda b, g: (0, off // w + g))
    per_g = lambda w: pl.BlockSpec((None, 1, w), lambda b, g: (g, 0, 0))
    cum = _ssd_cum_matrix()
    xoff, boff, coff = 0, SSD_DI, SSD_DI + SSD_G * SSD_N
    in_specs = [col(SSD_GP, 0), col(SSD_GP, SSD_COL_X), col(SSD_N, SSD_COL_B), col(SSD_N, SSD_COL_C),
                col(LANE, SSD_COL_DT),
                wcol(SSD_CONV, SSD_GP, xoff), wcol(SSD_CONV, SSD_N, boff), wcol(SSD_CONV, SSD_N, coff),
                wcol(1, SSD_GP, xoff), wcol(1, SSD_N, boff), wcol(1, SSD_N, coff),
                per_g(LANE), per_g(LANE), per_g(SSD_GP),
                pl.BlockSpec(cum.shape, lambda b, g: (0, 0))]
    args = [y, y, y, y, y, prm["conv_w"], prm["conv_w"], prm["conv_w"], prm["conv_b"], prm["conv_b"], prm["conv_b"],
            prm["dt_bias"], prm["a_log"], prm["d_skip"], cum]
    st_spec = pl.BlockSpec((None, 2, None, SSD_GP, SSD_N), lambda b, g: (b, 0, g, 0, 0))
    out_specs = [pl.BlockSpec((T, SSD_GP), lambda b, g: (b, g))]
    out_shape = [jax.ShapeDtypeStruct((n_seq * T, SSD_DI), F32)]
    if has_s0:
        in_specs.append(st_spec)
        args.append(s0)
    else:
        out_specs.append(st_spec)
        out_shape.append(jax.ShapeDtypeStruct((n_seq, 2, SSD_G, SSD_GP, SSD_N), F32))
    scratch = [pltpu.VMEM((T, SSD_GP), F32), pltpu.VMEM((T, SSD_GP), BF16),
               pltpu.VMEM((T, SSD_N), BF16), pltpu.VMEM((T, SSD_N), BF16),
               pltpu.VMEM((T, LANE), F32), pltpu.VMEM((T, LANE), F32), pltpu.VMEM((T, SSD_GP), F32),
               pltpu.VMEM((SSD_N, SSD_GP), F32), pltpu.VMEM((SSD_N, SSD_GP), F32)]
    outs = pl.pallas_call(
        functools.partial(_ssd_kernel, has_s0=has_s0),
        grid=(n_seq, SSD_G),
        in_specs=in_specs, out_specs=out_specs, out_shape=out_shape,
        scratch_shapes=scratch,
        compiler_params=pltpu.CompilerParams(dimension_semantics=("arbitrary", "arbitrary"),
                                             vmem_limit_bytes=VMEM_LIMIT),
        name="ssd_state" if has_s0 else "ssd",
    )(*args)
    return outs[0], (None if has_s0 else outs[1])


def _ssd_mixer(y, state, conv_w, conv_b, dt_bias, a_log, d_skip):
    def lanes(p):
        pg = p.reshape(2, SSD_G, SSD_E).transpose(1, 0, 2).reshape(SSD_G, 2 * SSD_E)
        return jnp.pad(jnp.concatenate([pg, pg], axis=1), ((0, 0), (0, LANE - 4 * SSD_E))).reshape(SSD_G, 1, LANE)
    prm = dict(conv_w=conv_w, conv_b=conv_b.reshape(1, SSD_XBC), dt_bias=lanes(dt_bias), a_log=lanes(a_log),
               d_skip=jnp.repeat(d_skip, SSD_P).reshape(SSD_G, 1, SSD_GP))
    op, st = _ssd_call(y, prm, SEQ, BATCH, 0, None)
    s0 = state.reshape(DEC_BATCH, 2, SSD_G, SSD_GP, SSD_N)
    os_, _ = _ssd_call(y, prm, DEC_SEQ, DEC_BATCH, N_PROMPT, s0)
    return (op, os_), st.reshape(BATCH, 2, SSD_H, SSD_P, SSD_N)


def kernel(x_prompt, x_sample, state_gla, cache_nat_k, cache_nat_v, state_ssd, c,
           c_ctx, norm_g, w_ada, b_ada, w_ffn_in, w_ffn_out,
           gla_w_in, gla_w_a1, gla_w_a2, gla_b_a, gla_norm_g, gla_w_out,
           nat_w_qkv, nat_rpb, nat_w_out,
           gm_w_in, gm_ln_g, gm_ln_b, gm_w_s, gm_b_s, gm_w_out,
           ssd_w_in, ssd_conv_w, ssd_conv_b, ssd_dt_bias, ssd_a_log, ssd_d, ssd_norm_g, ssd_w_out,
           final_g):
    x = (x_prompt.reshape(N_PROMPT, D_MODEL), x_sample.reshape(N_SAMPLE, D_MODEL))
    mod = _modulation_all(c, c_ctx, w_ada, b_ada)
    new_gla, new_k, new_v, new_ssd = [], [], [], []
    for l in range(DEPTH):
        kind, j = l % N_MIXERS, l // N_MIXERS
        x = _ffn(x, mod[l], norm_g[l, 0], w_ffn_in, w_ffn_out, l, 0, 0)
        mix = None
        if kind == 0:
            w_rank = jnp.pad(jnp.concatenate([gla_w_a1[j, 0], gla_w_a1[j, 1]], axis=1),
                             ((0, 0), (0, LANE - 2 * GLA_RANK)))
            y = _proj_in(x, mod[l], norm_g[l, 1], [(gla_w_in[j], GLA_IN), (w_rank, LANE)], 512)
            o, st = _gla_mixer(y, state_gla[:, j], gla_w_a2[j], gla_b_a[j], gla_norm_g[j])
            new_gla.append(st)
            mix = (o, gla_w_out[j], None)
        elif kind == 1:
            y = _proj_in(x, mod[l], norm_g[l, 1], [(nat_w_qkv[j], 3 * D_MODEL)], 768)
            op, kc, vc = _nat_context(y)
            os_ = _nat_latent(y, _heads_last(cache_nat_k[:, j]), _heads_last(cache_nat_v[:, j]), nat_rpb[j])
            new_k.append(kc)
            new_v.append(vc)
            mix = ((op, os_), nat_w_out[j], None)
        elif kind == 2:
            y = _proj_in(x, mod[l], norm_g[l, 1], [(gm_w_in[j], 2 * GM_DH)], 512, act="gelu")
            x = _gmlp_out(x, y, mod[l], gm_ln_g[j], gm_ln_b[j], gm_w_s[j], gm_b_s[j], gm_w_out[j])
        else:
            y = _proj_in(x, mod[l], norm_g[l, 1],
                         [(ssd_w_in[j], SSD_DI + SSD_XBC), (_ssd_w_dt(ssd_w_in[j]), SSD_G * LANE)], 512)
            o, st = _ssd_mixer(y, state_ssd[:, j], ssd_conv_w[j], ssd_conv_b[j], ssd_dt_bias[j],
                               ssd_a_log[j], ssd_d[j])
            new_ssd.append(st)
            mix = (o, ssd_w_out[j], ssd_norm_g[j])
        x = _ffn(x, mod[l], norm_g[l, 2], w_ffn_in, w_ffn_out, l, 1, 2, mix=mix,
                 final_g=final_g if l == DEPTH - 1 else None)
    y_prompt = x[0].reshape(BATCH, SEQ, D_MODEL)
    y_sample = x[1].reshape(DEC_BATCH, DEC_SEQ, D_MODEL)
    return (y_prompt, y_sample, jnp.stack(new_gla, axis=1), jnp.stack(new_k, axis=1),
            jnp.stack(new_v, axis=1), jnp.stack(new_ssd, axis=1))
```

```python
import functools

import jax
import jax.numpy as jnp
import numpy as np
from jax import lax
from jax.experimental import pallas as pl
from jax.experimental.pallas import tpu as pltpu

D_MODEL = 1024
BATCH = 32
SEQ = 256
DEPTH = 4
DEC_BATCH = 2
DEC_SEQ = 1024
N_PROMPT = BATCH * SEQ
N_SAMPLE = DEC_BATCH * DEC_SEQ
N_TOK = N_PROMPT + N_SAMPLE
N_GROUPS = 1 + DEC_BATCH

GRID_W = 64
N_MIXERS = 4
N_SUB = 3
N_MOD = 3 * N_SUB
D_FF = 2816
EPS = 1e-6
NEG_INF = -1e30
ROPE_THETA = 10000.0
GLA_H, GLA_DK, GLA_DV, GLA_RANK, GLA_TAU, GLA_CHUNK = 4, 128, 256, 16, 16.0, 16
GLA_IN = 2 * GLA_H * GLA_DK + 2 * GLA_H * GLA_DV
NAT_H, NAT_HD, NAT_WH, NAT_WW = 16, 64, 8, 16
GM_DH, GM_G, GM_CHUNK = 1024, 8, 128
GM_CG = GM_DH // GM_G
SSD_DI = 2 * D_MODEL
SSD_P = 64
SSD_H = SSD_DI // SSD_P
SSD_N, SSD_G, SSD_CONV, SSD_CHUNK = 128, 4, 3, 64
SSD_XBC = SSD_DI + 2 * SSD_G * SSD_N

LANE = 128
VMEM_LIMIT = 56 * 1024 * 1024
BF16 = jnp.bfloat16
F32 = jnp.float32

FF_CHUNK = 256
N_FF_CHUNKS = D_FF // FF_CHUNK
TM_FFN = 512
TM_PROJ = 512
ADA_TK = 128


def _group_of_tile(i, tm):
    n_prompt_tiles = N_PROMPT // tm
    return jnp.where(i < n_prompt_tiles, 0, 1 + (i - n_prompt_tiles) // (DEC_SEQ // tm))


def _resident(shape):
    nd = len(shape)
    return pl.BlockSpec(shape, lambda i: (0,) * nd, pipeline_mode=pl.Buffered(1))


def _stream_rows(tm, width, single_buffer_latent=False):
    npt = N_PROMPT // tm
    mode = dict(pipeline_mode=pl.Buffered(1)) if single_buffer_latent else {}
    return (pl.BlockSpec((tm, width), lambda i: (jnp.minimum(i, npt - 1), 0)),
            pl.BlockSpec((tm, width), lambda i: (jnp.maximum(i - npt, 0), 0), **mode))


def _premod(x, g, mod_ref, k):
    shift = mod_ref[3 * k:3 * k + 1, :]
    scale = mod_ref[3 * k + 1:3 * k + 2, :]
    gate = mod_ref[3 * k + 2:3 * k + 3, :]
    ms = jnp.mean(x * x, axis=-1, keepdims=True)
    h = x * lax.rsqrt(ms + EPS) * g
    return h * (1.0 + scale) + shift, gate


def _dot_nt(a, b):
    return lax.dot_general(a, b, (((1,), (1,)), ((), ())), preferred_element_type=F32)


def _dot_tn(a, b):
    return lax.dot_general(a, b, (((0,), (0,)), ((), ())), preferred_element_type=F32)


def _split3(x):
    hi = x.astype(BF16)
    r1 = x - hi.astype(F32)
    mid = r1.astype(BF16)
    lo = (r1 - mid.astype(F32)).astype(BF16)
    return hi, mid, lo


def _dot3(m, x):
    hi, mid, lo = _split3(x)
    d = lambda p: jnp.dot(m, p, preferred_element_type=F32)
    return d(hi) + d(mid) + d(lo)


def _ada_kernel(cond_ref, wa_ref, wb_ref, b_ref, o_ref):
    cnd = cond_ref[...]
    s = (cnd * jax.nn.sigmoid(cnd)).astype(BF16)
    p = jnp.concatenate([jnp.dot(s, w_ref[...].astype(BF16), preferred_element_type=F32)
                         for w_ref in (wa_ref, wb_ref)], axis=1)

    @pl.when(pl.program_id(1) == 0)
    def _():
        o_ref[...] = p + b_ref[...]

    @pl.when(pl.program_id(1) > 0)
    def _():
        o_ref[...] += p


def _modulation_all(c, c_ctx, w_ada, b_ada):
    rows = 8
    nk = D_MODEL // ADA_TK
    cond = jnp.concatenate([c_ctx[None], c, jnp.zeros((rows - N_GROUPS, D_MODEL), F32)], axis=0)
    cond = cond.reshape(rows, nk, ADA_TK).transpose(1, 0, 2)
    n_out = N_MOD * D_MODEL
    out = pl.pallas_call(
        _ada_kernel,
        grid=(DEPTH, nk),
        in_specs=[
            pl.BlockSpec((None, rows, ADA_TK), lambda l, k: (k, 0, 0)),
            pl.BlockSpec((None, ADA_TK, n_out // 2), lambda l, k: (l, k, 0)),
            pl.BlockSpec((None, ADA_TK, n_out // 2), lambda l, k: (l, k, 1)),
            pl.BlockSpec((None, 1, n_out), lambda l, k: (l, 0, 0)),
        ],
        out_specs=pl.BlockSpec((None, rows, n_out), lambda l, k: (l, 0, 0)),
        out_shape=jax.ShapeDtypeStruct((DEPTH, rows, n_out), F32),
        compiler_params=pltpu.CompilerParams(dimension_semantics=("arbitrary", "arbitrary")),
        name="ada_modulation",
    )(cond, w_ada, w_ada, b_ada.reshape(DEPTH, 1, n_out))
    return out[:, :N_GROUPS].reshape(DEPTH, N_GROUPS, N_MOD, D_MODEL)


def _ffn_weight_copies(win_hbm, wout_hbm, st_in, st_out, sem, l, half, j, slot):
    cols = lambda off: pl.ds(off + j * FF_CHUNK, FF_CHUNK)
    return (pltpu.make_async_copy(win_hbm.at[l, half, :, cols(0)], st_in.at[slot, 0], sem.at[slot, 0]),
            pltpu.make_async_copy(win_hbm.at[l, half, :, cols(D_FF)], st_in.at[slot, 1], sem.at[slot, 1]),
            pltpu.make_async_copy(wout_hbm.at[l, half, cols(0), :], st_out.at[slot], sem.at[slot, 2]))


def _ffn_kernel(*refs, l, half, k, split_in, mix, mix_norm, final):
    it = iter(refs)
    x_refs = (next(it), next(it)) if split_in else (next(it),)
    mod_ref, g_ref = next(it), next(it)
    o_in_refs, ng_ref, wmix_ref = ((next(it), next(it)), next(it), next(it)) if mix else (None, None, None)
    win_hbm, wout_hbm = next(it), next(it)
    fg_ref = next(it) if final else None
    out_refs = (next(it), next(it)) if final else (next(it),)
    acc_ref, win_ref, wout_ref, st_in, st_out, sem = it
    i = pl.program_id(0)
    is_prompt = i < N_PROMPT // x_refs[0].shape[0]
    copies = functools.partial(_ffn_weight_copies, win_hbm, wout_hbm, st_in, st_out, sem, l, half)

    def row_tile(fetch_weights):
        if fetch_weights:
            for c in copies(0, 0):
                c.start()
        x = jnp.where(is_prompt, x_refs[0][...], x_refs[1][...]) if split_in else x_refs[0][...]
        if mix:
            o_in = jnp.where(is_prompt, o_in_refs[0][...], o_in_refs[1][...])
            if mix_norm:
                o_in = o_in * lax.rsqrt(jnp.mean(o_in * o_in, axis=-1, keepdims=True) + EPS) * ng_ref[...]
            x = x + mod_ref[5:6, :] * jnp.dot(o_in.astype(BF16), wmix_ref[...], preferred_element_type=F32)
        h, gate = _premod(x, g_ref[...], mod_ref, k)
        hb = h.astype(BF16)
        for j in range(N_FF_CHUNKS):
            gate_cols = slice(j * FF_CHUNK, (j + 1) * FF_CHUNK)
            up_cols = slice(D_FF + j * FF_CHUNK, D_FF + (j + 1) * FF_CHUNK)
            if fetch_weights:
                slot = j % 2
                if j + 1 < N_FF_CHUNKS:
                    for c in copies(j + 1, 1 - slot):
                        c.start()
                for c in copies(j, slot):
                    c.wait()
                store_chunk(j, slot)
            a = jnp.dot(hb, win_ref[:, gate_cols], preferred_element_type=F32)
            u = jnp.dot(hb, win_ref[:, up_cols], preferred_element_type=F32)
            t = (a * jax.nn.sigmoid(a) * u).astype(BF16)
            p = jnp.dot(t, wout_ref[gate_cols, :], preferred_element_type=F32)
            if j == 0:
                acc_ref[...] = p
            else:
                acc_ref[...] += p
        y = x + 0.5 * gate * acc_ref[...]
        if not final:
            out_refs[0][...] = y
        else:
            y = y * lax.rsqrt(jnp.mean(y * y, axis=-1, keepdims=True) + EPS) * fg_ref[...]

            @pl.when(is_prompt)
            def _():
                out_refs[0][...] = y

            out_refs[1][...] = y

    def store_chunk(j, slot):
        win_ref[:, j * FF_CHUNK:(j + 1) * FF_CHUNK] = st_in[slot, 0].astype(BF16)
        win_ref[:, D_FF + j * FF_CHUNK:D_FF + (j + 1) * FF_CHUNK] = st_in[slot, 1].astype(BF16)
        wout_ref[j * FF_CHUNK:(j + 1) * FF_CHUNK, :] = st_out[slot].astype(BF16)

    def fetch_all_weights():
        for c in copies(0, 0):
            c.start()
        for j in range(N_FF_CHUNKS):
            slot = j % 2
            if j + 1 < N_FF_CHUNKS:
                for c in copies(j + 1, 1 - slot):
                    c.start()
            for c in copies(j, slot):
                c.wait()
            store_chunk(j, slot)

    if mix_norm:
        pl.when(i == 0)(fetch_all_weights)
        row_tile(False)
    else:
        pl.when(i == 0)(functools.partial(row_tile, True))
        pl.when(i > 0)(functools.partial(row_tile, False))


def _ffn(x, mod_l, g, w_in_all, w_out_all, l, half, k, mix=None, final_g=None):
    tm = TM_FFN
    split_in = isinstance(x, tuple)
    row = pl.BlockSpec((tm, D_MODEL), lambda i: (i, 0))
    vec = lambda n: pl.BlockSpec((1, n), lambda i: (0, 0))
    in_specs = list(_stream_rows(tm, D_MODEL)) if split_in else [row]
    args = list(x) if split_in else [x]
    in_specs += [pl.BlockSpec((None, N_MOD, D_MODEL), lambda i: (_group_of_tile(i, tm), 0, 0)), vec(D_MODEL)]
    args += [mod_l, g.reshape(1, D_MODEL)]
    if mix is not None:
        o_in, w_mix, norm_g = mix
        kdim = w_mix.shape[0]
        ng = jnp.ones((1, kdim), F32) if norm_g is None else norm_g.reshape(1, kdim)
        in_specs += [*_stream_rows(tm, kdim, single_buffer_latent=kdim > D_MODEL), vec(kdim),
                     _resident((kdim, D_MODEL))]
        args += [*o_in, ng, w_mix.astype(BF16)]
    in_specs += [pl.BlockSpec(memory_space=pl.ANY), pl.BlockSpec(memory_space=pl.ANY)]
    args += [w_in_all, w_out_all]
    if final_g is not None:
        in_specs.append(vec(D_MODEL))
        args.append(final_g.reshape(1, D_MODEL))
        out_specs = list(_stream_rows(tm, D_MODEL))
        out_shape = [jax.ShapeDtypeStruct((N_PROMPT, D_MODEL), F32), jax.ShapeDtypeStruct((N_SAMPLE, D_MODEL), F32)]
    else:
        out_specs, out_shape = row, jax.ShapeDtypeStruct((N_TOK, D_MODEL), F32)
    return pl.pallas_call(
        functools.partial(_ffn_kernel, l=l, half=half, k=k, split_in=split_in, mix=mix is not None,
                          mix_norm=mix is not None and mix[2] is not None, final=final_g is not None),
        grid=(N_TOK // tm,),
        in_specs=in_specs, out_specs=out_specs, out_shape=out_shape,
        scratch_shapes=[pltpu.VMEM((tm, D_MODEL), F32),
                        pltpu.VMEM((D_MODEL, 2 * D_FF), BF16), pltpu.VMEM((D_FF, D_MODEL), BF16),
                        pltpu.VMEM((2, 2, D_MODEL, FF_CHUNK), F32), pltpu.VMEM((2, FF_CHUNK, D_MODEL), F32),
                        pltpu.SemaphoreType.DMA((2, 3))],
        compiler_params=pltpu.CompilerParams(dimension_semantics=("arbitrary",),
                                             vmem_limit_bytes=VMEM_LIMIT),
        name="ffn_swiglu",
    )(*args)


def _proj_in_kernel(x_ref, mod_ref, g_ref, *refs, widths, tn, act):
    w_refs, o_ref = refs[:-1], refs[-1]
    h, _ = _premod(x_ref[...], g_ref[...], mod_ref, 1)
    hb = h.astype(BF16)
    off = 0
    for w_ref, n_use in zip(w_refs, widths):
        for j0 in range(0, n_use, tn):
            wd = min(tn, n_use - j0)
            y = jnp.dot(hb, w_ref[:, j0:j0 + wd], preferred_element_type=F32)
            if act == "gelu":
                y = jax.nn.gelu(y)
            o_ref[:, off + j0:off + j0 + wd] = y
        off += n_use


def _proj_in(x, mod_l, g, pieces, tn, act=None):
    widths = tuple(n for _, n in pieces)
    n_out = sum(widths)
    tm = 2 * TM_PROJ if 2 * (2 * TM_PROJ) * n_out * 4 <= 32 * 1024 * 1024 else TM_PROJ
    return pl.pallas_call(
        functools.partial(_proj_in_kernel, widths=widths, tn=tn, act=act),
        grid=(N_TOK // tm,),
        in_specs=[
            pl.BlockSpec((tm, D_MODEL), lambda i: (i, 0)),
            pl.BlockSpec((None, N_MOD, D_MODEL), lambda i: (_group_of_tile(i, tm), 0, 0)),
            pl.BlockSpec((1, D_MODEL), lambda i: (0, 0)),
            *[_resident(w.shape) for w, _ in pieces],
        ],
        out_specs=pl.BlockSpec((tm, n_out), lambda i: (i, 0)),
        out_shape=jax.ShapeDtypeStruct((N_TOK, n_out), F32),
        compiler_params=pltpu.CompilerParams(dimension_semantics=("arbitrary",),
                                             vmem_limit_bytes=VMEM_LIMIT),
        name="mixer_proj_in",
    )(x, mod_l, g.reshape(1, D_MODEL), *[w.astype(BF16) for w, _ in pieces])


GLA_BLK = 128
GLA_ZCOL = GLA_IN // LANE


def _gla_consts():
    r = np.arange(GLA_BLK)
    same = (r[:, None] // GLA_CHUNK) == (r[None, :] // GLA_CHUNK)
    ri, ci = r[:, None] % GLA_CHUNK, r[None, :] % GLA_CHUNK
    lf = np.concatenate([same & (ci <= ri), same & (ci > ri)], axis=0)
    lb = np.concatenate([same & (ci >= ri), same & (ci < ri)], axis=0)
    rows = np.arange(GLA_CHUNK * GLA_DK)
    sel = (rows[:, None] // GLA_DK) == (np.arange(LANE)[None, :] % GLA_CHUNK)
    return (jnp.asarray(lf, BF16), jnp.asarray(lb, BF16), jnp.asarray(sel, BF16))


def _rope_tables(T):
    half = GLA_DK // 2
    t = np.arange(T)
    inv = ROPE_THETA ** (-np.arange(0, half, 2, dtype=np.float64) / half)
    lane = np.arange(GLA_DK)
    pos = np.where(lane[None, :] < half, (t // GRID_W)[:, None], (t % GRID_W)[:, None])
    ang = pos * inv[lane % (half // 2)][None, :]
    sign = np.where((lane % half) < half // 2, -1.0, 1.0)[None, :]
    return jnp.asarray(np.cos(ang), F32), jnp.asarray(np.sin(ang) * sign, F32)


def _rope_apply(x, cos, sin_signed):
    half = GLA_DK // 2
    lane = lax.broadcasted_iota(jnp.int32, (1, GLA_DK), 1)
    partner = jnp.where((lane % half) < half // 2,
                        pltpu.roll(x, GLA_DK - half // 2, axis=1), pltpu.roll(x, half // 2, axis=1))
    return x * cos + partner * sin_signed


def _gla_kernel(*refs, use_rope, has_s0):
    it = iter(refs)
    q_ref, k_ref, v_ref, r_ref, za_ref, w2_ref, ba_ref, ng_ref, lf_ref, lb_ref, sel_ref = (next(it) for _ in range(11))
    cos_ref, sin_ref = (next(it), next(it)) if use_rope else (None, None)
    s0_ref = next(it) if has_s0 else None
    o_ref = next(it)
    st_ref = None if has_s0 else next(it)
    q_s, k_s, bf_s, bb_s, qdf_s, kdf_s, qdb_s, kdb_s, w_s, sf_s, sb_s = it

    T = q_ref.shape[0]
    nblk, nch = T // GLA_BLK, T // GLA_CHUNK
    hq = lambda h: slice(h * GLA_DK, (h + 1) * GLA_DK)
    hv = lambda h: slice(h * GLA_DV, (h + 1) * GLA_DV)

    for blk in range(nblk):
        rows = slice(blk * GLA_BLK, (blk + 1) * GLA_BLK)
        zab = za_ref[rows, :].astype(BF16)
        for h in range(GLA_H):
            z = jnp.dot(zab, w2_ref[h], preferred_element_type=F32) + ba_ref[h]
            la = (jnp.minimum(z, 0.0) - jnp.log1p(jnp.exp(-jnp.abs(z)))) * (1.0 / GLA_TAU)
            cf = _dot3(lf_ref[...], la[:, :GLA_DK])
            cb = _dot3(lb_ref[...], la[:, GLA_DK:])
            bf, bb = cf[:GLA_BLK], cb[:GLA_BLK]
            q = q_ref[rows, hq(h)] * (GLA_DK ** -0.5)
            k = k_ref[rows, hq(h)]
            if use_rope:
                q = _rope_apply(q, cos_ref[rows, :], sin_ref[rows, :])
                k = _rope_apply(k, cos_ref[rows, :], sin_ref[rows, :])
            q_s[h, rows, :], k_s[h, rows, :] = q, k
            bf_s[h, rows, :], bb_s[h, rows, :] = bf, bb
            qdf_s[h, rows, :] = (q * jnp.exp(bf)).astype(BF16)
            qdb_s[h, rows, :] = (q * jnp.exp(bb)).astype(BF16)
            kdf_s[h, rows, :] = (k * jnp.exp(cf[GLA_BLK:])).astype(BF16)
            kdb_s[h, rows, :] = (k * jnp.exp(cb[GLA_BLK:])).astype(BF16)

    irow = lax.broadcasted_iota(jnp.int32, (GLA_CHUNK, 1), 0)
    lane_c = lax.broadcasted_iota(jnp.int32, (GLA_BLK, LANE), 1) // GLA_CHUNK
    row_c = lax.broadcasted_iota(jnp.int32, (GLA_BLK, LANE), 0) // GLA_CHUNK
    for h in range(GLA_H):
        def intra_chunk(c, carry, h=h):
            rows = pl.ds(pl.multiple_of(c * GLA_CHUNK, GLA_CHUNK), GLA_CHUNK)
            qc, bfc, bbc = q_s[h, rows, :], bf_s[h, rows, :], bb_s[h, rows, :]
            for j in range(GLA_CHUNK):
                row = pl.ds(c * GLA_CHUNK + j, 1)
                e = (jnp.exp(jnp.where(irow >= j, bfc - bf_s[h, row, :], bbc - bb_s[h, row, :]))
                     + jnp.where(irow == j, 1.0, 0.0))
                w_s[rows, j * GLA_DK:(j + 1) * GLA_DK] = (qc * k_s[h, row, :] * e).astype(BF16)
            return carry

        lax.fori_loop(0, nch, intra_chunk, 0)
        a_all = jnp.dot(w_s[...], sel_ref[...], preferred_element_type=F32)
        for blk in range(nblk):
            rows = slice(blk * GLA_BLK, (blk + 1) * GLA_BLK)
            a = jnp.where(lane_c == row_c, a_all[rows, :], 0.0).astype(BF16)
            o_ref[rows, hv(h)] = jnp.dot(a, v_ref[rows, hv(h)].astype(BF16), preferred_element_type=F32)

    for h in range(GLA_H):
        if has_s0:
            sf_s[h] = s0_ref[0, h].T
            sb_s[h] = s0_ref[1, h].T
        else:
            sf_s[h] = jnp.zeros((GLA_DV, GLA_DK), F32)
            sb_s[h] = jnp.zeros((GLA_DV, GLA_DK), F32)

    def chain(h, rows, g_row, qd_s, kd_s, b_s, st_s):
        s = st_s[h]
        o_ref[rows, hv(h)] += _dot_nt(qd_s[h, rows, :], s.astype(BF16))
        u = _dot_tn(v_ref[rows, hv(h)].astype(BF16), kd_s[h, rows, :])
        st_s[h] = s * jnp.exp(b_s[h, g_row, :]) + u

    def inter_chunk(n, carry):
        cf_ = pl.multiple_of(n * GLA_CHUNK, GLA_CHUNK)
        cb_ = pl.multiple_of((nch - 1 - n) * GLA_CHUNK, GLA_CHUNK)
        for h in range(GLA_H):
            chain(h, pl.ds(cf_, GLA_CHUNK), pl.ds(cf_ + GLA_CHUNK - 1, 1), qdf_s, kdf_s, bf_s, sf_s)
            chain(h, pl.ds(cb_, GLA_CHUNK), pl.ds(cb_, 1), qdb_s, kdb_s, bb_s, sb_s)
        return carry

    lax.fori_loop(0, nch, inter_chunk, 0, unroll=8)

    for h in range(GLA_H):
        o = o_ref[:, hv(h)]
        o = o * lax.rsqrt(jnp.mean(o * o, axis=-1, keepdims=True) + EPS) * ng_ref[h]
        r = r_ref[:, hv(h)]
        o_ref[:, hv(h)] = o * (r * jax.nn.sigmoid(r))
        if not has_s0:
            st_ref[0, h] = sf_s[h].T
            st_ref[1, h] = sb_s[h].T


def _gla_call(y, w2, ba, ng, T, n_seq, row0, use_rope, s0):
    has_s0 = s0 is not None
    rb = row0 // T
    nq, nv = GLA_H * GLA_DK, GLA_H * GLA_DV
    mode = dict(pipeline_mode=pl.Buffered(1)) if n_seq <= 2 else {}
    col = lambda w, j: pl.BlockSpec((T, w), lambda b: (rb + b, j), **mode)
    cst = lambda a: pl.BlockSpec(a.shape, lambda b: (0,) * a.ndim)
    lf, lb, sel = _gla_consts()
    in_specs = [col(nq, 0), col(nq, 1), col(nv, 1), col(nv, 2), col(LANE, GLA_ZCOL),
                cst(w2), cst(ba), cst(ng), cst(lf), cst(lb), cst(sel)]
    args = [y, y, y, y, y, w2, ba, ng, lf, lb, sel]
    if use_rope:
        cos, sin = _rope_tables(T)
        in_specs += [cst(cos), cst(sin)]
        args += [cos, sin]
    st_spec = pl.BlockSpec((None, 2, GLA_H, GLA_DK, GLA_DV), lambda b: (b, 0, 0, 0, 0))
    out_specs = [pl.BlockSpec((T, nv), lambda b: (b, 0))]
    out_shape = [jax.ShapeDtypeStruct((n_seq * T, nv), F32)]
    if has_s0:
        in_specs.append(st_spec)
        args.append(s0)
    else:
        out_specs.append(st_spec)
        out_shape.append(jax.ShapeDtypeStruct((n_seq, 2, GLA_H, GLA_DK, GLA_DV), F32))
    scratch = ([pltpu.VMEM((GLA_H, T, GLA_DK), F32)] * 4 + [pltpu.VMEM((GLA_H, T, GLA_DK), BF16)] * 4
               + [pltpu.VMEM((T, GLA_CHUNK * GLA_DK), BF16),
                  pltpu.VMEM((GLA_H, GLA_DV, GLA_DK), F32), pltpu.VMEM((GLA_H, GLA_DV, GLA_DK), F32)])
    outs = pl.pallas_call(
        functools.partial(_gla_kernel, use_rope=use_rope, has_s0=has_s0),
        grid=(n_seq,),
        in_specs=in_specs, out_specs=out_specs, out_shape=out_shape,
        scratch_shapes=scratch,
        compiler_params=pltpu.CompilerParams(dimension_semantics=("arbitrary",),
                                             vmem_limit_bytes=VMEM_LIMIT),
        name="gla_rope" if use_rope else "gla",
    )(*args)
    return outs[0], (None if has_s0 else outs[1])


def _gla_mixer(y, state, w_a2, b_a, norm_g):
    w2 = jnp.zeros((GLA_H, LANE, 2 * GLA_DK), F32)
    for e in range(2):
        we = w_a2[e].reshape(GLA_RANK, GLA_H, GLA_DK).transpose(1, 0, 2)
        w2 = w2.at[:, e * GLA_RANK:(e + 1) * GLA_RANK, e * GLA_DK:(e + 1) * GLA_DK].set(we)
    w2 = w2.astype(BF16)
    ba = b_a.reshape(2, GLA_H, GLA_DK).transpose(1, 0, 2).reshape(GLA_H, 1, 2 * GLA_DK)
    ng = norm_g.reshape(GLA_H, 1, GLA_DV)
    op, st = _gla_call(y, w2, ba, ng, SEQ, BATCH, 0, False, None)
    os_, _ = _gla_call(y, w2, ba, ng, DEC_SEQ, DEC_BATCH, N_PROMPT, True, state)
    return (op, os_), st


N_HEAD_PAIRS = NAT_H // 2
NAT_ROWS = DEC_SEQ // GRID_W
NAT_WIN = NAT_WH * GRID_W
NAT_NDR = 2 * NAT_WH - 1
NAT_NDC = 2 * NAT_WW - 1


def _nat_row_window(r):
    rs = min(max(r - NAT_WH // 2, 0), NAT_ROWS - NAT_WH)
    return rs, r - rs


def _nat_bias_table(rpb):
    qc = np.arange(GRID_W)[:, None]
    kc = np.arange(GRID_W)[None, :]
    c_start = np.clip(qc - NAT_WW // 2, 0, GRID_W - NAT_WW)
    ok = (kc >= c_start) & (kc < c_start + NAT_WW)
    dc = np.clip(kc - qc + NAT_WW - 1, 0, NAT_NDC - 1)
    pick = dc[None] == np.arange(NAT_NDC)[:, None, None]
    pick2 = np.zeros((2, NAT_NDC, GRID_W, 2, GRID_W), np.float32)
    for s in range(2):
        pick2[s, :, :, s, :] = pick
    pick2 = jnp.asarray(pick2.reshape(2 * NAT_NDC, GRID_W, 2 * GRID_W))
    rows2 = jnp.concatenate([rpb[:, :NAT_NDR - 1], rpb[:, 1:]], axis=2)
    t = jnp.einsum('hdy,yql->hdql', rows2, pick2, precision=lax.Precision.HIGHEST)
    return jnp.where(np.tile(ok, (1, 2))[None, None], t, NEG_INF)


def _head_mask(hh):
    lane = lax.broadcasted_iota(jnp.int32, (1, LANE), 1)
    return (lane < NAT_HD) if hh == 0 else (lane >= NAT_HD)


def _nat_ctx_kernel(q_ref, k_ref, v_ref, o_ref, kc_ref, vc_ref):
    for hp in range(N_HEAD_PAIRS):
        cols = slice(hp * LANE, (hp + 1) * LANE)
        q = q_ref[:, cols] * (NAT_HD ** -0.5)
        k, v = k_ref[:, cols], v_ref[:, cols]
        kb, vb = k.astype(BF16), v.astype(BF16)
        q2 = jnp.concatenate([jnp.where(_head_mask(hh), q, 0.0) for hh in range(2)], axis=0).astype(BF16)
        s = _dot_nt(q2, kb)
        p = jnp.exp(s - jnp.max(s, axis=-1, keepdims=True))
        l = jnp.sum(p, axis=-1, keepdims=True)
        o2 = jnp.dot(p.astype(BF16), vb, preferred_element_type=F32) / l
        o_ref[:, cols] = jnp.where(_head_mask(0), o2[:SEQ], o2[SEQ:])
        for hh in range(2):
            kc_ref[2 * hp + hh] = k[:, hh * NAT_HD:(hh + 1) * NAT_HD]
            vc_ref[2 * hp + hh] = v[:, hh * NAT_HD:(hh + 1) * NAT_HD]


def _nat_context(y):
    blk = lambda j: pl.BlockSpec((SEQ, D_MODEL), lambda b: (b, j))
    cache = pl.BlockSpec((None, NAT_H, SEQ, NAT_HD), lambda b: (b, 0, 0, 0))
    cache_shape = jax.ShapeDtypeStruct((BATCH, NAT_H, SEQ, NAT_HD), F32)
    return pl.pallas_call(
        _nat_ctx_kernel,
        grid=(BATCH,),
        in_specs=[blk(0), blk(1), blk(2)],
        out_specs=[pl.BlockSpec((SEQ, D_MODEL), lambda b: (b, 0)), cache, cache],
        out_shape=[jax.ShapeDtypeStruct((N_PROMPT, D_MODEL), F32), cache_shape, cache_shape],
        compiler_params=pltpu.CompilerParams(dimension_semantics=("arbitrary",)),
        name="nat_context",
    )(y, y, y)


def _nat_lat_kernel(q_ref, k_ref, v_ref, ck_ref, cv_ref, tab_ref, o_ref):
    q = q_ref[...] * (NAT_HD ** -0.5)
    qm = [jnp.where(_head_mask(hh), q, 0.0).astype(BF16) for hh in range(2)]
    ckb = ck_ref[...].astype(BF16)
    cvb = cv_ref[...].astype(BF16)
    for r in range(NAT_ROWS):
        rs, off = _nat_row_window(r)
        kw = k_ref[rs * GRID_W:rs * GRID_W + NAT_WIN, :].astype(BF16)
        vw = v_ref[rs * GRID_W:rs * GRID_W + NAT_WIN, :].astype(BF16)
        qr = jnp.concatenate([qm[hh][r * GRID_W:(r + 1) * GRID_W] for hh in range(2)], axis=0)
        bias = jnp.concatenate(
            [jnp.concatenate([tab_ref[hh, w - off + NAT_WH - 1] for w in range(0, NAT_WH, 2)], axis=1)
             for hh in range(2)], axis=0)
        s_lat = _dot_nt(qr, kw) + bias
        s_ctx = _dot_nt(qr, ckb)
        m = jnp.maximum(jnp.max(s_lat, axis=-1, keepdims=True), jnp.max(s_ctx, axis=-1, keepdims=True))
        p_lat = jnp.exp(s_lat - m)
        p_ctx = jnp.exp(s_ctx - m)
        l = jnp.sum(p_lat, axis=-1, keepdims=True) + jnp.sum(p_ctx, axis=-1, keepdims=True)
        o2 = (jnp.dot(p_lat.astype(BF16), vw, preferred_element_type=F32)
              + jnp.dot(p_ctx.astype(BF16), cvb, preferred_element_type=F32)) / l
        o_ref[r * GRID_W:(r + 1) * GRID_W, :] = jnp.where(_head_mask(0), o2[:GRID_W], o2[GRID_W:])


def _nat_latent(y, ck, cv, rpb):
    row0 = N_PROMPT // DEC_SEQ
    blk = lambda off: pl.BlockSpec((DEC_SEQ, LANE), lambda b, hp: (row0 + b, off + hp))
    ctx = pl.BlockSpec((None, ck.shape[1], LANE), lambda b, hp: (b, 0, hp))
    return pl.pallas_call(
        _nat_lat_kernel,
        grid=(DEC_BATCH, N_HEAD_PAIRS),
        in_specs=[blk(0), blk(N_HEAD_PAIRS), blk(2 * N_HEAD_PAIRS), ctx, ctx,
                  pl.BlockSpec((2, NAT_NDR - 1, GRID_W, 2 * GRID_W), lambda b, hp: (hp, 0, 0, 0))],
        out_specs=pl.BlockSpec((DEC_SEQ, LANE), lambda b, hp: (b, hp)),
        out_shape=jax.ShapeDtypeStruct((N_SAMPLE, D_MODEL), F32),
        compiler_params=pltpu.CompilerParams(dimension_semantics=("arbitrary", "arbitrary")),
        name="nat_latent",
    )(y, y, y, ck, cv, _nat_bias_table(rpb))


def _heads_last(t):
    b, h, s, d = t.shape
    return t.transpose(0, 2, 1, 3).reshape(b, s, h * d)


def _gmlp_kernel(x_ref, y_ref, mod_ref, lng_ref, lnb_ref, ws_ref, bs_ref, w_ref, o_ref, t_ref):
    tm = x_ref.shape[0]
    v = y_ref[:, GM_DH:]
    vc = v - jnp.mean(v, axis=-1, keepdims=True)
    vn = vc * lax.rsqrt(jnp.mean(vc * vc, axis=-1, keepdims=True) + EPS) * lng_ref[...] + lnb_ref[...]
    vn = vn.astype(BF16)
    for n in range(tm // GM_CHUNK):
        rows = slice(n * GM_CHUNK, (n + 1) * GM_CHUNK)
        for g in range(GM_G):
            cols = slice(g * GM_CG, (g + 1) * GM_CG)
            sp = jnp.dot(ws_ref[g], vn[rows, cols], preferred_element_type=F32) + bs_ref[:, cols]
            t_ref[rows, cols] = (y_ref[rows, cols] * sp).astype(BF16)
    gate = mod_ref[5:6, :]
    o_ref[...] = x_ref[...] + gate * jnp.dot(t_ref[...], w_ref[...], preferred_element_type=F32)


def _gmlp_out(x, y, mod_l, ln_g, ln_b, w_s, b_s, w_out):
    tm = TM_PROJ
    bias = jnp.repeat(b_s.T, GM_CG, axis=1)
    return pl.pallas_call(
        _gmlp_kernel,
        grid=(N_TOK // tm,),
        in_specs=[
            pl.BlockSpec((tm, D_MODEL), lambda i: (i, 0)),
            pl.BlockSpec((tm, 2 * GM_DH), lambda i: (i, 0)),
            pl.BlockSpec((None, N_MOD, D_MODEL), lambda i: (_group_of_tile(i, tm), 0, 0)),
            pl.BlockSpec((1, GM_DH), lambda i: (0, 0)),
            pl.BlockSpec((1, GM_DH), lambda i: (0, 0)),
            _resident((GM_G, GM_CHUNK, GM_CHUNK)),
            _resident((GM_CHUNK, GM_DH)),
            _resident((GM_DH, D_MODEL)),
        ],
        out_specs=pl.BlockSpec((tm, D_MODEL), lambda i: (i, 0)),
        out_shape=jax.ShapeDtypeStruct((N_TOK, D_MODEL), F32),
        scratch_shapes=[pltpu.VMEM((tm, GM_DH), BF16)],
        compiler_params=pltpu.CompilerParams(dimension_semantics=("arbitrary",),
                                             vmem_limit_bytes=VMEM_LIMIT),
        name="gmlp_gate_out",
    )(x, y, mod_l, ln_g.reshape(1, GM_DH), ln_b.reshape(1, GM_DH), w_s.astype(BF16), bias,
      w_out.astype(BF16))


SSD_E = SSD_H // SSD_G
SSD_GP = SSD_E * SSD_P
SSD_BLK = 2 * SSD_CHUNK
SSD_COL_X = SSD_DI
SSD_COL_B = 2 * SSD_DI
SSD_COL_C = SSD_COL_B + SSD_G * SSD_N
SSD_COL_DT = SSD_COL_C + SSD_G * SSD_N


def _ssd_w_dt(w_in):
    base = SSD_DI + SSD_XBC
    w_dt = jnp.zeros((D_MODEL, SSD_G, LANE), F32)
    for g in range(SSD_G):
        cols = jnp.concatenate([w_in[:, base + g * SSD_E:base + (g + 1) * SSD_E],
                                w_in[:, base + SSD_H + g * SSD_E:base + SSD_H + (g + 1) * SSD_E]], axis=1)
        w_dt = w_dt.at[:, g, :2 * SSD_E].set(cols).at[:, g, 2 * SSD_E:4 * SSD_E].set(cols)
    return w_dt.reshape(D_MODEL, SSD_G * LANE)


def _ssd_cum_matrix():
    r = np.arange(SSD_BLK)
    same = (r[:, None] // SSD_CHUNK) == (r[None, :] // SSD_CHUNK)
    cum = np.concatenate([same & (r[None, :] <= r[:, None]), same & (r[None, :] >= r[:, None])], axis=0)
    return jnp.asarray(cum, BF16)


def _softplus(x):
    return jnp.maximum(x, 0.0) + jnp.log1p(jnp.exp(-jnp.abs(x)))


def _ssd_kernel(*refs, has_s0):
    it = iter(refs)
    (z_ref, x_ref, b_ref, c_ref, dt_ref, cwx_ref, cwb_ref, cwc_ref, cbx_ref, cbb_ref, cbc_ref,
     dtb_ref, alog_ref, dsk_ref, cum_ref) = (next(it) for _ in range(15))
    s0_ref = next(it) if has_s0 else None
    o_ref = next(it)
    st_ref = None if has_s0 else next(it)
    xs_s, xb_s, bm_s, cm_s, cu_s, dt_s, y_s, sf_s, sb_s = it

    T = x_ref.shape[0]
    nch = T // SSD_CHUNK
    L = SSD_CHUNK

    trow = lax.broadcasted_iota(jnp.int32, (T, 1), 0)

    def conv_silu(v_ref, w_ref, bias_ref):
        v = v_ref[...]
        prev = jnp.where(trow == 0, 0.0, pltpu.roll(v, 1, axis=0))
        nxt = jnp.where(trow == T - 1, 0.0, pltpu.roll(v, T - 1, axis=0))
        y = prev * w_ref[0:1, :] + v * w_ref[1:2, :] + nxt * w_ref[2:3, :] + bias_ref[...]
        return y * jax.nn.sigmoid(y)

    xs = conv_silu(x_ref, cwx_ref, cbx_ref)
    xs_s[...] = xs
    xb_s[...] = xs.astype(BF16)
    bm_s[...] = conv_silu(b_ref, cwb_ref, cbb_ref).astype(BF16)
    cm_s[...] = conv_silu(c_ref, cwc_ref, cbc_ref).astype(BF16)

    lane1 = lax.broadcasted_iota(jnp.int32, (1, LANE), 1)
    a_row = jnp.where(lane1 < 2 * SSD_E, -jnp.exp(alog_ref[...]), 0.0)
    for blk in range(T // SSD_BLK):
        rows = slice(blk * SSD_BLK, (blk + 1) * SSD_BLK)
        dt = _softplus(dt_ref[rows, :] + dtb_ref[...])
        c2 = _dot3(cum_ref[...], dt * a_row)
        cu_s[rows, :] = jnp.where(lane1 < SSD_E, c2[:SSD_BLK], c2[SSD_BLK:])
        dt_s[rows, :] = dt

    ii = lax.broadcasted_iota(jnp.int32, (L, LANE), 0)
    jj = lax.broadcasted_iota(jnp.int32, (L, LANE), 1)
    fwd_half = jj < L
    fwd_half1 = lane1 < L
    tri = (fwd_half & (ii >= jj)) | ((jj >= L) & (ii <= jj - L))
    left = lane1 < SSD_P
    for c in range(nch):
        rows = slice(c * L, (c + 1) * L)
        cum_c, dt_c = cu_s[rows, :], dt_s[rows, :]
        bm_c, cm_c = bm_s[rows, :], cm_s[rows, :]
        cb2 = _dot_nt(cm_c, jnp.concatenate([bm_c, bm_c], axis=0))
        arr = jnp.where(lane1 < 2 * SSD_E, cum_c, dt_c)
        arr_t = jnp.concatenate([arr, arr], axis=0).T
        gs = []
        for e in range(SSD_E):
            row_c = jnp.where(fwd_half1, arr_t[e:e + 1, :], arr_t[SSD_E + e:SSD_E + e + 1, :])
            row_dt = jnp.where(fwd_half1, arr_t[2 * SSD_E + e:2 * SSD_E + e + 1, :],
                               arr_t[3 * SSD_E + e:3 * SSD_E + e + 1, :])
            col_c = jnp.where(fwd_half, jnp.broadcast_to(cum_c[:, e:e + 1], (L, LANE)),
                              jnp.broadcast_to(cum_c[:, SSD_E + e:SSD_E + e + 1], (L, LANE)))
            dec = jnp.exp(jnp.where(tri, col_c - row_c, -jnp.inf))
            gs.append((cb2 * dec * row_dt).astype(BF16))
        for pr in range(SSD_E // 2):
            cols = slice(pr * LANE, (pr + 1) * LANE)
            xp = xb_s[rows, cols]
            xl = jnp.where(left, xp, jnp.zeros_like(xp))
            xr = jnp.where(left, jnp.zeros_like(xp), xp)
            lhs = jnp.concatenate([gs[2 * pr], gs[2 * pr + 1]], axis=1)
            rhs = jnp.concatenate([xl, xl, xr, xr], axis=0)
            y_s[rows, cols] = (jnp.dot(lhs, rhs, preferred_element_type=F32)
                               + dsk_ref[:, cols] * xs_s[rows, cols])

    if has_s0:
        sf_s[...] = s0_ref[0].T
        sb_s[...] = s0_ref[1].T
    else:
        sf_s[...] = jnp.zeros_like(sf_s)
        sb_s[...] = jnp.zeros_like(sb_s)

    def per_head_cols(v, lane0):
        tiles = []
        for t in range(SSD_E // 2):
            a, b = lane0 + 2 * t, lane0 + 2 * t + 1
            tiles.append(jnp.where(left, jnp.broadcast_to(v[:, a:a + 1], (L, LANE)),
                                   jnp.broadcast_to(v[:, b:b + 1], (L, LANE))))
        return jnp.concatenate(tiles, axis=1)

    def chain(c, last, lane0, st_s):
        rows = slice(c * L, (c + 1) * L)
        cum_c = jnp.where((lane1 >= lane0) & (lane1 < lane0 + SSD_E), cu_s[rows, :], 0.0)
        tot = cum_c[last:last + 1, :]
        qd = per_head_cols(jnp.exp(cum_c), lane0)
        w = per_head_cols(jnp.exp(tot - cum_c) * dt_s[rows, :], lane0)
        s = st_s[...]
        y_s[rows, :] += jnp.dot(cm_s[rows, :], s.astype(BF16), preferred_element_type=F32) * qd
        xw = (xs_s[rows, :] * w).astype(BF16)
        st_s[...] = s * qd[last:last + 1, :] + _dot_tn(bm_s[rows, :], xw)

    for n in range(nch):
        chain(n, L - 1, 0, sf_s)
        chain(nch - 1 - n, 0, SSD_E, sb_s)

    z = z_ref[...]
    o_ref[...] = y_s[...] * (z * jax.nn.sigmoid(z))
    if not has_s0:
        st_ref[0] = sf_s[...].T
        st_ref[1] = sb_s[...].T


def _ssd_call(y, prm, T, n_seq, row0, s0):
    has_s0 = s0 is not None
    rb = row0 // T
    col = lambda w, off: pl.BlockSpec((T, w), lambda b, g: (rb + b, off // w + g))
    wcol = lambda rows, w, off: pl.BlockSpec((rows, w), lambda b, g: (0, off // w + g))
    per_g = lambda w: pl.BlockSpec((None, 1, w), lambda b, g: (g, 0, 0))
    cum = _ssd_cum_matrix()
    xoff, boff, coff = 0, SSD_DI, SSD_DI + SSD_G * SSD_N
    in_specs = [col(SSD_GP, 0), col(SSD_GP, SSD_COL_X), col(SSD_N, SSD_COL_B), col(SSD_N, SSD_COL_C),
                col(LANE, SSD_COL_DT),
                wcol(SSD_CONV, SSD_GP, xoff), wcol(SSD_CONV, SSD_N, boff), wcol(SSD_CONV, SSD_N, coff),
                wcol(1, SSD_GP, xoff), wcol(1, SSD_N, boff), wcol(1, SSD_N, coff),
                per_g(LANE), per_g(LANE), per_g(SSD_GP),
                pl.BlockSpec(cum.shape, lambda b, g: (0, 0))]
    args = [y, y, y, y, y, prm["conv_w"], prm["conv_w"], prm["conv_w"], prm["conv_b"], prm["conv_b"], prm["conv_b"],
            prm["dt_bias"], prm["a_log"], prm["d_skip"], cum]
    st_spec = pl.BlockSpec((None, 2, None, SSD_GP, SSD_N), lambda b, g: (b, 0, g, 0, 0))
    out_specs = [pl.BlockSpec((T, SSD_GP), lambda b, g: (b, g))]
    out_shape = [jax.ShapeDtypeStruct((n_seq * T, SSD_DI), F32)]
    if has_s0:
        in_specs.append(st_spec)
        args.append(s0)
    else:
        out_specs.append(st_spec)
        out_shape.append(jax.ShapeDtypeStruct((n_seq, 2, SSD_G, SSD_GP, SSD_N), F32))
    scratch = [pltpu.VMEM((T, SSD_GP), F32), pltpu.VMEM((T, SSD_GP), BF16),
               pltpu.VMEM((T, SSD_N), BF16), pltpu.VMEM((T, SSD_N), BF16),
               pltpu.VMEM((T, LANE), F32), pltpu.VMEM((T, LANE), F32), pltpu.VMEM((T, SSD_GP), F32),
               pltpu.VMEM((SSD_N, SSD_GP), F32), pltpu.VMEM((SSD_N, SSD_GP), F32)]
    outs = pl.pallas_call(
        functools.partial(_ssd_kernel, has_s0=has_s0),
        grid=(n_seq, SSD_G),
        in_specs=in_specs, out_specs=out_specs, out_shape=out_shape,
        scratch_shapes=scratch,
        compiler_params=pltpu.CompilerParams(dimension_semantics=("arbitrary", "arbitrary"),
                                             vmem_limit_bytes=VMEM_LIMIT),
        name="ssd_state" if has_s0 else "ssd",
    )(*args)
    return outs[0], (None if has_s0 else outs[1])


def _ssd_mixer(y, state, conv_w, conv_b, dt_bias, a_log, d_skip):
    def lanes(p):
        pg = p.reshape(2, SSD_G, SSD_E).transpose(1, 0, 2).reshape(SSD_G, 2 * SSD_E)
        return jnp.pad(jnp.concatenate([pg, pg], axis=1), ((0, 0), (0, LANE - 4 * SSD_E))).reshape(SSD_G, 1, LANE)
    prm = dict(conv_w=conv_w, conv_b=conv_b.reshape(1, SSD_XBC), dt_bias=lanes(dt_bias), a_log=lanes(a_log),
               d_skip=jnp.repeat(d_skip, SSD_P).reshape(SSD_G, 1, SSD_GP))
    op, st = _ssd_call(y, prm, SEQ, BATCH, 0, None)
    s0 = state.reshape(DEC_BATCH, 2, SSD_G, SSD_GP, SSD_N)
    os_, _ = _ssd_call(y, prm, DEC_SEQ, DEC_BATCH, N_PROMPT, s0)
    return (op, os_), st.reshape(BATCH, 2, SSD_H, SSD_P, SSD_N)


def kernel(x_prompt, x_sample, state_gla, cache_nat_k, cache_nat_v, state_ssd, c,
           c_ctx, norm_g, w_ada, b_ada, w_ffn_in, w_ffn_out,
           gla_w_in, gla_w_a1, gla_w_a2, gla_b_a, gla_norm_g, gla_w_out,
           nat_w_qkv, nat_rpb, nat_w_out,
           gm_w_in, gm_ln_g, gm_ln_b, gm_w_s, gm_b_s, gm_w_out,
           ssd_w_in, ssd_conv_w, ssd_conv_b, ssd_dt_bias, ssd_a_log, ssd_d, ssd_norm_g, ssd_w_out,
           final_g):
    x = (x_prompt.reshape(N_PROMPT, D_MODEL), x_sample.reshape(N_SAMPLE, D_MODEL))
    mod = _modulation_all(c, c_ctx, w_ada, b_ada)
    new_gla, new_k, new_v, new_ssd = [], [], [], []
    for l in range(DEPTH):
        kind, j = l % N_MIXERS, l // N_MIXERS
        x = _ffn(x, mod[l], norm_g[l, 0], w_ffn_in, w_ffn_out, l, 0, 0)
        mix = None
        if kind == 0:
            w_rank = jnp.pad(jnp.concatenate([gla_w_a1[j, 0], gla_w_a1[j, 1]], axis=1),
                             ((0, 0), (0, LANE - 2 * GLA_RANK)))
            y = _proj_in(x, mod[l], norm_g[l, 1], [(gla_w_in[j], GLA_IN), (w_rank, LANE)], 512)
            o, st = _gla_mixer(y, state_gla[:, j], gla_w_a2[j], gla_b_a[j], gla_norm_g[j])
            new_gla.append(st)
            mix = (o, gla_w_out[j], None)
        elif kind == 1:
            y = _proj_in(x, mod[l], norm_g[l, 1], [(nat_w_qkv[j], 3 * D_MODEL)], 768)
            op, kc, vc = _nat_context(y)
            os_ = _nat_latent(y, _heads_last(cache_nat_k[:, j]), _heads_last(cache_nat_v[:, j]), nat_rpb[j])
            new_k.append(kc)
            new_v.append(vc)
            mix = ((op, os_), nat_w_out[j], None)
        elif kind == 2:
            y = _proj_in(x, mod[l], norm_g[l, 1], [(gm_w_in[j], 2 * GM_DH)], 512, act="gelu")
            x = _gmlp_out(x, y, mod[l], gm_ln_g[j], gm_ln_b[j], gm_w_s[j], gm_b_s[j], gm_w_out[j])
        else:
            y = _proj_in(x, mod[l], norm_g[l, 1],
                         [(ssd_w_in[j], SSD_DI + SSD_XBC), (_ssd_w_dt(ssd_w_in[j]), SSD_G * LANE)], 512)
            o, st = _ssd_mixer(y, state_ssd[:, j], ssd_conv_w[j], ssd_conv_b[j], ssd_dt_bias[j],
                               ssd_a_log[j], ssd_d[j])
            new_ssd.append(st)
            mix = (o, ssd_w_out[j], ssd_norm_g[j])
        x = _ffn(x, mod[l], norm_g[l, 2], w_ffn_in, w_ffn_out, l, 1, 2, mix=mix,
                 final_g=final_g if l == DEPTH - 1 else None)
    y_prompt = x[0].reshape(BATCH, SEQ, D_MODEL)
    y_sample = x[1].reshape(DEC_BATCH, DEC_SEQ, D_MODEL)
    return (y_prompt, y_sample, jnp.stack(new_gla, axis=1), jnp.stack(new_k, axis=1),
            jnp.stack(new_v, axis=1), jnp.stack(new_ssd, axis=1))
```

```python
import functools

import jax
import jax.numpy as jnp
import numpy as np
from jax import lax
from jax.experimental import pallas as pl
from jax.experimental.pallas import tpu as pltpu

D_MODEL = 1024
BATCH = 32
SEQ = 256
DEPTH = 4
DEC_BATCH = 2
DEC_SEQ = 1024
N_PROMPT = BATCH * SEQ
N_SAMPLE = DEC_BATCH * DEC_SEQ
N_TOK = N_PROMPT + N_SAMPLE
N_GROUPS = 1 + DEC_BATCH

GRID_W = 64
N_MIXERS = 4
N_SUB = 3
N_MOD = 3 * N_SUB
D_FF = 2816
EPS = 1e-6
NEG_INF = -1e30
ROPE_THETA = 10000.0
GLA_H, GLA_DK, GLA_DV, GLA_RANK, GLA_TAU, GLA_CHUNK = 4, 128, 256, 16, 16.0, 16
GLA_IN = 2 * GLA_H * GLA_DK + 2 * GLA_H * GLA_DV
NAT_H, NAT_HD, NAT_WH, NAT_WW = 16, 64, 8, 16
GM_DH, GM_G, GM_CHUNK = 1024, 8, 128
GM_CG = GM_DH // GM_G
SSD_DI = 2 * D_MODEL
SSD_P = 64
SSD_H = SSD_DI // SSD_P
SSD_N, SSD_G, SSD_CONV, SSD_CHUNK = 128, 4, 3, 64
SSD_XBC = SSD_DI + 2 * SSD_G * SSD_N

LANE = 128
VMEM_LIMIT = 56 * 1024 * 1024
BF16 = jnp.bfloat16
F32 = jnp.float32

FF_CHUNK = 256
N_FF_CHUNKS = D_FF // FF_CHUNK
TM_FFN = 512
TM_PROJ = 512
ADA_TK = 128


def _group_of_tile(i, tm):
    n_prompt_tiles = N_PROMPT // tm
    return jnp.where(i < n_prompt_tiles, 0, 1 + (i - n_prompt_tiles) // (DEC_SEQ // tm))


def _resident(shape):
    nd = len(shape)
    return pl.BlockSpec(shape, lambda i: (0,) * nd, pipeline_mode=pl.Buffered(1))


def _stream_rows(tm, width, single_buffer_latent=False):
    npt = N_PROMPT // tm
    mode = dict(pipeline_mode=pl.Buffered(1)) if single_buffer_latent else {}
    return (pl.BlockSpec((tm, width), lambda i: (jnp.minimum(i, npt - 1), 0)),
            pl.BlockSpec((tm, width), lambda i: (jnp.maximum(i - npt, 0), 0), **mode))


def _premod(x, g, mod_ref, k):
    shift = mod_ref[3 * k:3 * k + 1, :]
    scale = mod_ref[3 * k + 1:3 * k + 2, :]
    gate = mod_ref[3 * k + 2:3 * k + 3, :]
    ms = jnp.mean(x * x, axis=-1, keepdims=True)
    h = x * lax.rsqrt(ms + EPS) * g
    return h * (1.0 + scale) + shift, gate


def _dot_nt(a, b):
    return lax.dot_general(a, b, (((1,), (1,)), ((), ())), preferred_element_type=F32)


def _dot_tn(a, b):
    return lax.dot_general(a, b, (((0,), (0,)), ((), ())), preferred_element_type=F32)


def _split3(x):
    hi = x.astype(BF16)
    r1 = x - hi.astype(F32)
    mid = r1.astype(BF16)
    lo = (r1 - mid.astype(F32)).astype(BF16)
    return hi, mid, lo


def _dot3(m, x):
    hi, mid, lo = _split3(x)
    d = lambda p: jnp.dot(m, p, preferred_element_type=F32)
    return d(hi) + d(mid) + d(lo)


def _ada_kernel(cond_ref, wa_ref, wb_ref, b_ref, o_ref):
    cnd = cond_ref[...]
    s = (cnd * jax.nn.sigmoid(cnd)).astype(BF16)
    p = jnp.concatenate([jnp.dot(s, w_ref[...].astype(BF16), preferred_element_type=F32)
                         for w_ref in (wa_ref, wb_ref)], axis=1)

    @pl.when(pl.program_id(1) == 0)
    def _():
        o_ref[...] = p + b_ref[...]

    @pl.when(pl.program_id(1) > 0)
    def _():
        o_ref[...] += p


def _modulation_all(c, c_ctx, w_ada, b_ada):
    rows = 8
    nk = D_MODEL // ADA_TK
    cond = jnp.concatenate([c_ctx[None], c, jnp.zeros((rows - N_GROUPS, D_MODEL), F32)], axis=0)
    cond = cond.reshape(rows, nk, ADA_TK).transpose(1, 0, 2)
    n_out = N_MOD * D_MODEL
    out = pl.pallas_call(
        _ada_kernel,
        grid=(DEPTH, nk),
        in_specs=[
            pl.BlockSpec((None, rows, ADA_TK), lambda l, k: (k, 0, 0)),
            pl.BlockSpec((None, ADA_TK, n_out // 2), lambda l, k: (l, k, 0)),
            pl.BlockSpec((None, ADA_TK, n_out // 2), lambda l, k: (l, k, 1)),
            pl.BlockSpec((None, 1, n_out), lambda l, k: (l, 0, 0)),
        ],
        out_specs=pl.BlockSpec((None, rows, n_out), lambda l, k: (l, 0, 0)),
        out_shape=jax.ShapeDtypeStruct((DEPTH, rows, n_out), F32),
        compiler_params=pltpu.CompilerParams(dimension_semantics=("arbitrary", "arbitrary")),
        name="ada_modulation",
    )(cond, w_ada, w_ada, b_ada.reshape(DEPTH, 1, n_out))
    return out[:, :N_GROUPS].reshape(DEPTH, N_GROUPS, N_MOD, D_MODEL)


def _ffn_weight_copies(win_hbm, wout_hbm, st_in, st_out, sem, l, half, j, slot):
    cols = lambda off: pl.ds(off + j * FF_CHUNK, FF_CHUNK)
    return (pltpu.make_async_copy(win_hbm.at[l, half, :, cols(0)], st_in.at[slot, 0], sem.at[slot, 0]),
            pltpu.make_async_copy(win_hbm.at[l, half, :, cols(D_FF)], st_in.at[slot, 1], sem.at[slot, 1]),
            pltpu.make_async_copy(wout_hbm.at[l, half, cols(0), :], st_out.at[slot], sem.at[slot, 2]))


def _ffn_kernel(*refs, l, half, k, split_in, mix, mix_norm, final):
    it = iter(refs)
    x_refs = (next(it), next(it)) if split_in else (next(it),)
    mod_ref, g_ref = next(it), next(it)
    o_in_refs, ng_ref, wmix_ref = ((next(it), next(it)), next(it), next(it)) if mix else (None, None, None)
    win_hbm, wout_hbm = next(it), next(it)
    fg_ref = next(it) if final else None
    out_refs = (next(it), next(it)) if final else (next(it),)
    acc_ref, win_ref, wout_ref, st_in, st_out, sem = it
    i = pl.program_id(0)
    is_prompt = i < N_PROMPT // x_refs[0].shape[0]
    copies = functools.partial(_ffn_weight_copies, win_hbm, wout_hbm, st_in, st_out, sem, l, half)

    def row_tile(fetch_weights):
        if fetch_weights:
            for c in copies(0, 0):
                c.start()
        x = jnp.where(is_prompt, x_refs[0][...], x_refs[1][...]) if split_in else x_refs[0][...]
        if mix:
            o_in = jnp.where(is_prompt, o_in_refs[0][...], o_in_refs[1][...])
            if mix_norm:
                o_in = o_in * lax.rsqrt(jnp.mean(o_in * o_in, axis=-1, keepdims=True) + EPS) * ng_ref[...]
            x = x + mod_ref[5:6, :] * jnp.dot(o_in.astype(BF16), wmix_ref[...], preferred_element_type=F32)
        h, gate = _premod(x, g_ref[...], mod_ref, k)
        hb = h.astype(BF16)
        for j in range(N_FF_CHUNKS):
            gate_cols = slice(j * FF_CHUNK, (j + 1) * FF_CHUNK)
            up_cols = slice(D_FF + j * FF_CHUNK, D_FF + (j + 1) * FF_CHUNK)
            if fetch_weights:
                slot = j % 2
                if j + 1 < N_FF_CHUNKS:
                    for c in copies(j + 1, 1 - slot):
                        c.start()
                for c in copies(j, slot):
                    c.wait()
                store_chunk(j, slot)
            a = jnp.dot(hb, win_ref[:, gate_cols], preferred_element_type=F32)
            u = jnp.dot(hb, win_ref[:, up_cols], preferred_element_type=F32)
            t = (a * jax.nn.sigmoid(a) * u).astype(BF16)
            p = jnp.dot(t, wout_ref[gate_cols, :], preferred_element_type=F32)
            if j == 0:
                acc_ref[...] = p
            else:
                acc_ref[...] += p
        y = x + 0.5 * gate * acc_ref[...]
        if not final:
            out_refs[0][...] = y
        else:
            y = y * lax.rsqrt(jnp.mean(y * y, axis=-1, keepdims=True) + EPS) * fg_ref[...]

            @pl.when(is_prompt)
            def _():
                out_refs[0][...] = y

            @pl.when(jnp.logical_not(is_prompt))
            def _():
                out_refs[1][...] = y

    def store_chunk(j, slot):
        win_ref[:, j * FF_CHUNK:(j + 1) * FF_CHUNK] = st_in[slot, 0].astype(BF16)
        win_ref[:, D_FF + j * FF_CHUNK:D_FF + (j + 1) * FF_CHUNK] = st_in[slot, 1].astype(BF16)
        wout_ref[j * FF_CHUNK:(j + 1) * FF_CHUNK, :] = st_out[slot].astype(BF16)

    def fetch_all_weights():
        for c in copies(0, 0):
            c.start()
        for j in range(N_FF_CHUNKS):
            slot = j % 2
            if j + 1 < N_FF_CHUNKS:
                for c in copies(j + 1, 1 - slot):
                    c.start()
            for c in copies(j, slot):
                c.wait()
            store_chunk(j, slot)

    if mix_norm:
        pl.when(i == 0)(fetch_all_weights)
        row_tile(False)
    else:
        pl.when(i == 0)(functools.partial(row_tile, True))
        pl.when(i > 0)(functools.partial(row_tile, False))


def _ffn(x, mod_l, g, w_in_all, w_out_all, l, half, k, mix=None, final_g=None):
    tm = TM_FFN
    split_in = isinstance(x, tuple)
    row = pl.BlockSpec((tm, D_MODEL), lambda i: (i, 0))
    vec = lambda n: pl.BlockSpec((1, n), lambda i: (0, 0))
    in_specs = list(_stream_rows(tm, D_MODEL)) if split_in else [row]
    args = list(x) if split_in else [x]
    in_specs += [pl.BlockSpec((None, N_MOD, D_MODEL), lambda i: (_group_of_tile(i, tm), 0, 0)), vec(D_MODEL)]
    args += [mod_l, g.reshape(1, D_MODEL)]
    if mix is not None:
        o_in, w_mix, norm_g = mix
        kdim = w_mix.shape[0]
        ng = jnp.ones((1, kdim), F32) if norm_g is None else norm_g.reshape(1, kdim)
        in_specs += [*_stream_rows(tm, kdim, single_buffer_latent=kdim > D_MODEL), vec(kdim),
                     _resident((kdim, D_MODEL))]
        args += [*o_in, ng, w_mix.astype(BF16)]
    in_specs += [pl.BlockSpec(memory_space=pl.ANY), pl.BlockSpec(memory_space=pl.ANY)]
    args += [w_in_all, w_out_all]
    if final_g is not None:
        in_specs.append(vec(D_MODEL))
        args.append(final_g.reshape(1, D_MODEL))
        out_specs = list(_stream_rows(tm, D_MODEL))
        out_shape = [jax.ShapeDtypeStruct((N_PROMPT, D_MODEL), F32), jax.ShapeDtypeStruct((N_SAMPLE, D_MODEL), F32)]
    else:
        out_specs, out_shape = row, jax.ShapeDtypeStruct((N_TOK, D_MODEL), F32)
    return pl.pallas_call(
        functools.partial(_ffn_kernel, l=l, half=half, k=k, split_in=split_in, mix=mix is not None,
                          mix_norm=mix is not None and mix[2] is not None, final=final_g is not None),
        grid=(N_TOK // tm,),
        in_specs=in_specs, out_specs=out_specs, out_shape=out_shape,
        scratch_shapes=[pltpu.VMEM((tm, D_MODEL), F32),
                        pltpu.VMEM((D_MODEL, 2 * D_FF), BF16), pltpu.VMEM((D_FF, D_MODEL), BF16),
                        pltpu.VMEM((2, 2, D_MODEL, FF_CHUNK), F32), pltpu.VMEM((2, FF_CHUNK, D_MODEL), F32),
                        pltpu.SemaphoreType.DMA((2, 3))],
        compiler_params=pltpu.CompilerParams(dimension_semantics=("arbitrary",),
                                             vmem_limit_bytes=VMEM_LIMIT),
        name="ffn_swiglu",
    )(*args)


def _proj_in_kernel(x_ref, mod_ref, g_ref, *refs, widths, tn, act):
    w_refs, o_ref = refs[:-1], refs[-1]
    h, _ = _premod(x_ref[...], g_ref[...], mod_ref, 1)
    hb = h.astype(BF16)
    off = 0
    for w_ref, n_use in zip(w_refs, widths):
        for j0 in range(0, n_use, tn):
            wd = min(tn, n_use - j0)
            y = jnp.dot(hb, w_ref[:, j0:j0 + wd].astype(BF16), preferred_element_type=F32)
            if act == "gelu":
                y = jax.nn.gelu(y)
            o_ref[:, off + j0:off + j0 + wd] = y
        off += n_use


def _proj_in(x, mod_l, g, pieces, tn, act=None):
    tm = TM_PROJ
    widths = tuple(n for _, n in pieces)
    n_out = sum(widths)
    return pl.pallas_call(
        functools.partial(_proj_in_kernel, widths=widths, tn=tn, act=act),
        grid=(N_TOK // tm,),
        in_specs=[
            pl.BlockSpec((tm, D_MODEL), lambda i: (i, 0)),
            pl.BlockSpec((None, N_MOD, D_MODEL), lambda i: (_group_of_tile(i, tm), 0, 0)),
            pl.BlockSpec((1, D_MODEL), lambda i: (0, 0)),
            *[_resident(w.shape) for w, _ in pieces],
        ],
        out_specs=pl.BlockSpec((tm, n_out), lambda i: (i, 0)),
        out_shape=jax.ShapeDtypeStruct((N_TOK, n_out), F32),
        compiler_params=pltpu.CompilerParams(dimension_semantics=("arbitrary",),
                                             vmem_limit_bytes=VMEM_LIMIT),
        name="mixer_proj_in",
    )(x, mod_l, g.reshape(1, D_MODEL), *[w for w, _ in pieces])


GLA_BLK = 128
GLA_ZCOL = GLA_IN // LANE


def _gla_consts():
    r = np.arange(GLA_BLK)
    same = (r[:, None] // GLA_CHUNK) == (r[None, :] // GLA_CHUNK)
    ri, ci = r[:, None] % GLA_CHUNK, r[None, :] % GLA_CHUNK
    lf = np.concatenate([same & (ci <= ri), same & (ci > ri)], axis=0)
    lb = np.concatenate([same & (ci >= ri), same & (ci < ri)], axis=0)
    rows = np.arange(GLA_CHUNK * GLA_DK)
    sel = (rows[:, None] // GLA_DK) == (np.arange(LANE)[None, :] % GLA_CHUNK)
    return (jnp.asarray(lf, BF16), jnp.asarray(lb, BF16), jnp.asarray(sel, BF16))


def _rope_tables(T):
    half = GLA_DK // 2
    t = np.arange(T)
    inv = ROPE_THETA ** (-np.arange(0, half, 2, dtype=np.float64) / half)
    lane = np.arange(GLA_DK)
    pos = np.where(lane[None, :] < half, (t // GRID_W)[:, None], (t % GRID_W)[:, None])
    ang = pos * inv[lane % (half // 2)][None, :]
    sign = np.where((lane % half) < half // 2, -1.0, 1.0)[None, :]
    return jnp.asarray(np.cos(ang), F32), jnp.asarray(np.sin(ang) * sign, F32)


def _rope_apply(x, cos, sin_signed):
    half = GLA_DK // 2
    lane = lax.broadcasted_iota(jnp.int32, (1, GLA_DK), 1)
    partner = jnp.where((lane % half) < half // 2,
                        pltpu.roll(x, GLA_DK - half // 2, axis=1), pltpu.roll(x, half // 2, axis=1))
    return x * cos + partner * sin_signed


def _gla_kernel(*refs, use_rope, has_s0):
    it = iter(refs)
    q_ref, k_ref, v_ref, r_ref, za_ref, w2_ref, ba_ref, ng_ref, lf_ref, lb_ref, sel_ref = (next(it) for _ in range(11))
    cos_ref, sin_ref = (next(it), next(it)) if use_rope else (None, None)
    s0_ref = next(it) if has_s0 else None
    o_ref = next(it)
    st_ref = None if has_s0 else next(it)
    q_s, k_s, bf_s, bb_s, qdf_s, kdf_s, qdb_s, kdb_s, w_s, sf_s, sb_s = it

    T = q_ref.shape[0]
    nblk, nch = T // GLA_BLK, T // GLA_CHUNK
    hq = lambda h: slice(h * GLA_DK, (h + 1) * GLA_DK)
    hv = lambda h: slice(h * GLA_DV, (h + 1) * GLA_DV)

    for blk in range(nblk):
        rows = slice(blk * GLA_BLK, (blk + 1) * GLA_BLK)
        zab = za_ref[rows, :].astype(BF16)
        for h in range(GLA_H):
            z = jnp.dot(zab, w2_ref[h], preferred_element_type=F32) + ba_ref[h]
            la = (jnp.minimum(z, 0.0) - jnp.log1p(jnp.exp(-jnp.abs(z)))) * (1.0 / GLA_TAU)
            cf = _dot3(lf_ref[...], la[:, :GLA_DK])
            cb = _dot3(lb_ref[...], la[:, GLA_DK:])
            bf, bb = cf[:GLA_BLK], cb[:GLA_BLK]
            q = q_ref[rows, hq(h)] * (GLA_DK ** -0.5)
            k = k_ref[rows, hq(h)]
            if use_rope:
                q = _rope_apply(q, cos_ref[rows, :], sin_ref[rows, :])
                k = _rope_apply(k, cos_ref[rows, :], sin_ref[rows, :])
            q_s[h, rows, :], k_s[h, rows, :] = q, k
            bf_s[h, rows, :], bb_s[h, rows, :] = bf, bb
            qdf_s[h, rows, :] = (q * jnp.exp(bf)).astype(BF16)
            qdb_s[h, rows, :] = (q * jnp.exp(bb)).astype(BF16)
            kdf_s[h, rows, :] = (k * jnp.exp(cf[GLA_BLK:])).astype(BF16)
            kdb_s[h, rows, :] = (k * jnp.exp(cb[GLA_BLK:])).astype(BF16)

    irow = lax.broadcasted_iota(jnp.int32, (GLA_CHUNK, 1), 0)
    lane_c = lax.broadcasted_iota(jnp.int32, (GLA_BLK, LANE), 1) // GLA_CHUNK
    row_c = lax.broadcasted_iota(jnp.int32, (GLA_BLK, LANE), 0) // GLA_CHUNK
    for h in range(GLA_H):
        def intra_chunk(c, carry, h=h):
            rows = pl.ds(pl.multiple_of(c * GLA_CHUNK, GLA_CHUNK), GLA_CHUNK)
            qc, bfc, bbc = q_s[h, rows, :], bf_s[h, rows, :], bb_s[h, rows, :]
            for j in range(GLA_CHUNK):
                row = pl.ds(c * GLA_CHUNK + j, 1)
                e = (jnp.exp(jnp.where(irow >= j, bfc - bf_s[h, row, :], bbc - bb_s[h, row, :]))
                     + jnp.where(irow == j, 1.0, 0.0))
                w_s[rows, j * GLA_DK:(j + 1) * GLA_DK] = (qc * k_s[h, row, :] * e).astype(BF16)
            return carry

        lax.fori_loop(0, nch, intra_chunk, 0)
        a_all = jnp.dot(w_s[...], sel_ref[...], preferred_element_type=F32)
        for blk in range(nblk):
            rows = slice(blk * GLA_BLK, (blk + 1) * GLA_BLK)
            a = jnp.where(lane_c == row_c, a_all[rows, :], 0.0).astype(BF16)
            o_ref[rows, hv(h)] = jnp.dot(a, v_ref[rows, hv(h)].astype(BF16), preferred_element_type=F32)

    for h in range(GLA_H):
        if has_s0:
            sf_s[h] = s0_ref[0, h].T
            sb_s[h] = s0_ref[1, h].T
        else:
            sf_s[h] = jnp.zeros((GLA_DV, GLA_DK), F32)
            sb_s[h] = jnp.zeros((GLA_DV, GLA_DK), F32)

    def chain(h, rows, g_row, qd_s, kd_s, b_s, st_s):
        s = st_s[h]
        o_ref[rows, hv(h)] += _dot_nt(qd_s[h, rows, :], s.astype(BF16))
        u = _dot_tn(v_ref[rows, hv(h)].astype(BF16), kd_s[h, rows, :])
        st_s[h] = s * jnp.exp(b_s[h, g_row, :]) + u

    def inter_chunk(n, carry):
        cf_ = pl.multiple_of(n * GLA_CHUNK, GLA_CHUNK)
        cb_ = pl.multiple_of((nch - 1 - n) * GLA_CHUNK, GLA_CHUNK)
        for h in range(GLA_H):
            chain(h, pl.ds(cf_, GLA_CHUNK), pl.ds(cf_ + GLA_CHUNK - 1, 1), qdf_s, kdf_s, bf_s, sf_s)
            chain(h, pl.ds(cb_, GLA_CHUNK), pl.ds(cb_, 1), qdb_s, kdb_s, bb_s, sb_s)
        return carry

    lax.fori_loop(0, nch, inter_chunk, 0, unroll=8)

    for h in range(GLA_H):
        o = o_ref[:, hv(h)]
        o = o * lax.rsqrt(jnp.mean(o * o, axis=-1, keepdims=True) + EPS) * ng_ref[h]
        r = r_ref[:, hv(h)]
        o_ref[:, hv(h)] = o * (r * jax.nn.sigmoid(r))
        if not has_s0:
            st_ref[0, h] = sf_s[h].T
            st_ref[1, h] = sb_s[h].T


def _gla_call(y, w2, ba, ng, T, n_seq, row0, use_rope, s0):
    has_s0 = s0 is not None
    rb = row0 // T
    nq, nv = GLA_H * GLA_DK, GLA_H * GLA_DV
    mode = dict(pipeline_mode=pl.Buffered(1)) if n_seq <= 2 else {}
    col = lambda w, j: pl.BlockSpec((T, w), lambda b: (rb + b, j), **mode)
    cst = lambda a: pl.BlockSpec(a.shape, lambda b: (0,) * a.ndim)
    lf, lb, sel = _gla_consts()
    in_specs = [col(nq, 0), col(nq, 1), col(nv, 1), col(nv, 2), col(LANE, GLA_ZCOL),
                cst(w2), cst(ba), cst(ng), cst(lf), cst(lb), cst(sel)]
    args = [y, y, y, y, y, w2, ba, ng, lf, lb, sel]
    if use_rope:
        cos, sin = _rope_tables(T)
        in_specs += [cst(cos), cst(sin)]
        args += [cos, sin]
    st_spec = pl.BlockSpec((None, 2, GLA_H, GLA_DK, GLA_DV), lambda b: (b, 0, 0, 0, 0))
    out_specs = [pl.BlockSpec((T, nv), lambda b: (b, 0))]
    out_shape = [jax.ShapeDtypeStruct((n_seq * T, nv), F32)]
    if has_s0:
        in_specs.append(st_spec)
        args.append(s0)
    else:
        out_specs.append(st_spec)
        out_shape.append(jax.ShapeDtypeStruct((n_seq, 2, GLA_H, GLA_DK, GLA_DV), F32))
    scratch = ([pltpu.VMEM((GLA_H, T, GLA_DK), F32)] * 4 + [pltpu.VMEM((GLA_H, T, GLA_DK), BF16)] * 4
               + [pltpu.VMEM((T, GLA_CHUNK * GLA_DK), BF16),
                  pltpu.VMEM((GLA_H, GLA_DV, GLA_DK), F32), pltpu.VMEM((GLA_H, GLA_DV, GLA_DK), F32)])
    outs = pl.pallas_call(
        functools.partial(_gla_kernel, use_rope=use_rope, has_s0=has_s0),
        grid=(n_seq,),
        in_specs=in_specs, out_specs=out_specs, out_shape=out_shape,
        scratch_shapes=scratch,
        compiler_params=pltpu.CompilerParams(dimension_semantics=("arbitrary",),
                                             vmem_limit_bytes=VMEM_LIMIT),
        name="gla_rope" if use_rope else "gla",
    )(*args)
    return outs[0], (None if has_s0 else outs[1])


def _gla_mixer(y, state, w_a2, b_a, norm_g):
    w2 = jnp.zeros((GLA_H, LANE, 2 * GLA_DK), F32)
    for e in range(2):
        we = w_a2[e].reshape(GLA_RANK, GLA_H, GLA_DK).transpose(1, 0, 2)
        w2 = w2.at[:, e * GLA_RANK:(e + 1) * GLA_RANK, e * GLA_DK:(e + 1) * GLA_DK].set(we)
    w2 = w2.astype(BF16)
    ba = b_a.reshape(2, GLA_H, GLA_DK).transpose(1, 0, 2).reshape(GLA_H, 1, 2 * GLA_DK)
    ng = norm_g.reshape(GLA_H, 1, GLA_DV)
    op, st = _gla_call(y, w2, ba, ng, SEQ, BATCH, 0, False, None)
    os_, _ = _gla_call(y, w2, ba, ng, DEC_SEQ, DEC_BATCH, N_PROMPT, True, state)
    return (op, os_), st


N_HEAD_PAIRS = NAT_H // 2
NAT_ROWS = DEC_SEQ // GRID_W
NAT_WIN = NAT_WH * GRID_W
NAT_NDR = 2 * NAT_WH - 1
NAT_NDC = 2 * NAT_WW - 1


def _nat_row_window(r):
    rs = min(max(r - NAT_WH // 2, 0), NAT_ROWS - NAT_WH)
    return rs, r - rs


def _nat_bias_table(rpb):
    qc = np.arange(GRID_W)[:, None]
    kc = np.arange(GRID_W)[None, :]
    c_start = np.clip(qc - NAT_WW // 2, 0, GRID_W - NAT_WW)
    ok = (kc >= c_start) & (kc < c_start + NAT_WW)
    dc = np.clip(kc - qc + NAT_WW - 1, 0, NAT_NDC - 1)
    pick = dc[None] == np.arange(NAT_NDC)[:, None, None]
    pick2 = np.zeros((2, NAT_NDC, GRID_W, 2, GRID_W), np.float32)
    for s in range(2):
        pick2[s, :, :, s, :] = pick
    pick2 = jnp.asarray(pick2.reshape(2 * NAT_NDC, GRID_W, 2 * GRID_W))
    rows2 = jnp.concatenate([rpb[:, :NAT_NDR - 1], rpb[:, 1:]], axis=2)
    t = jnp.einsum('hdy,yql->hdql', rows2, pick2, precision=lax.Precision.HIGHEST)
    return jnp.where(np.tile(ok, (1, 2))[None, None], t, NEG_INF)


def _head_mask(hh):
    lane = lax.broadcasted_iota(jnp.int32, (1, LANE), 1)
    return (lane < NAT_HD) if hh == 0 else (lane >= NAT_HD)


def _nat_ctx_kernel(q_ref, k_ref, v_ref, o_ref, kc_ref, vc_ref):
    for hp in range(N_HEAD_PAIRS):
        cols = slice(hp * LANE, (hp + 1) * LANE)
        q = q_ref[:, cols] * (NAT_HD ** -0.5)
        k, v = k_ref[:, cols], v_ref[:, cols]
        kb, vb = k.astype(BF16), v.astype(BF16)
        q2 = jnp.concatenate([jnp.where(_head_mask(hh), q, 0.0) for hh in range(2)], axis=0).astype(BF16)
        s = _dot_nt(q2, kb)
        p = jnp.exp(s - jnp.max(s, axis=-1, keepdims=True))
        l = jnp.sum(p, axis=-1, keepdims=True)
        o2 = jnp.dot(p.astype(BF16), vb, preferred_element_type=F32) / l
        o_ref[:, cols] = jnp.where(_head_mask(0), o2[:SEQ], o2[SEQ:])
        for hh in range(2):
            kc_ref[2 * hp + hh] = k[:, hh * NAT_HD:(hh + 1) * NAT_HD]
            vc_ref[2 * hp + hh] = v[:, hh * NAT_HD:(hh + 1) * NAT_HD]


def _nat_context(y):
    blk = lambda j: pl.BlockSpec((SEQ, D_MODEL), lambda b: (b, j))
    cache = pl.BlockSpec((None, NAT_H, SEQ, NAT_HD), lambda b: (b, 0, 0, 0))
    cache_shape = jax.ShapeDtypeStruct((BATCH, NAT_H, SEQ, NAT_HD), F32)
    return pl.pallas_call(
        _nat_ctx_kernel,
        grid=(BATCH,),
        in_specs=[blk(0), blk(1), blk(2)],
        out_specs=[pl.BlockSpec((SEQ, D_MODEL), lambda b: (b, 0)), cache, cache],
        out_shape=[jax.ShapeDtypeStruct((N_PROMPT, D_MODEL), F32), cache_shape, cache_shape],
        compiler_params=pltpu.CompilerParams(dimension_semantics=("arbitrary",)),
        name="nat_context",
    )(y, y, y)


def _nat_lat_kernel(q_ref, k_ref, v_ref, ck_ref, cv_ref, tab_ref, o_ref):
    q = q_ref[...] * (NAT_HD ** -0.5)
    qm = [jnp.where(_head_mask(hh), q, 0.0).astype(BF16) for hh in range(2)]
    ckb = ck_ref[...].astype(BF16)
    cvb = cv_ref[...].astype(BF16)
    for r in range(NAT_ROWS):
        rs, off = _nat_row_window(r)
        kw = k_ref[rs * GRID_W:rs * GRID_W + NAT_WIN, :].astype(BF16)
        vw = v_ref[rs * GRID_W:rs * GRID_W + NAT_WIN, :].astype(BF16)
        qr = jnp.concatenate([qm[hh][r * GRID_W:(r + 1) * GRID_W] for hh in range(2)], axis=0)
        bias = jnp.concatenate(
            [jnp.concatenate([tab_ref[hh, w - off + NAT_WH - 1] for w in range(0, NAT_WH, 2)], axis=1)
             for hh in range(2)], axis=0)
        s_lat = _dot_nt(qr, kw) + bias
        s_ctx = _dot_nt(qr, ckb)
        m = jnp.maximum(jnp.max(s_lat, axis=-1, keepdims=True), jnp.max(s_ctx, axis=-1, keepdims=True))
        p_lat = jnp.exp(s_lat - m)
        p_ctx = jnp.exp(s_ctx - m)
        l = jnp.sum(p_lat, axis=-1, keepdims=True) + jnp.sum(p_ctx, axis=-1, keepdims=True)
        o2 = (jnp.dot(p_lat.astype(BF16), vw, preferred_element_type=F32)
              + jnp.dot(p_ctx.astype(BF16), cvb, preferred_element_type=F32)) / l
        o_ref[r * GRID_W:(r + 1) * GRID_W, :] = jnp.where(_head_mask(0), o2[:GRID_W], o2[GRID_W:])


def _nat_latent(y, ck, cv, rpb):
    row0 = N_PROMPT // DEC_SEQ
    blk = lambda off: pl.BlockSpec((DEC_SEQ, LANE), lambda b, hp: (row0 + b, off + hp))
    ctx = pl.BlockSpec((None, ck.shape[1], LANE), lambda b, hp: (b, 0, hp))
    return pl.pallas_call(
        _nat_lat_kernel,
        grid=(DEC_BATCH, N_HEAD_PAIRS),
        in_specs=[blk(0), blk(N_HEAD_PAIRS), blk(2 * N_HEAD_PAIRS), ctx, ctx,
                  pl.BlockSpec((2, NAT_NDR - 1, GRID_W, 2 * GRID_W), lambda b, hp: (hp, 0, 0, 0))],
        out_specs=pl.BlockSpec((DEC_SEQ, LANE), lambda b, hp: (b, hp)),
        out_shape=jax.ShapeDtypeStruct((N_SAMPLE, D_MODEL), F32),
        compiler_params=pltpu.CompilerParams(dimension_semantics=("arbitrary", "arbitrary")),
        name="nat_latent",
    )(y, y, y, ck, cv, _nat_bias_table(rpb))


def _heads_last(t):
    b, h, s, d = t.shape
    return t.transpose(0, 2, 1, 3).reshape(b, s, h * d)


def _gmlp_kernel(x_ref, y_ref, mod_ref, lng_ref, lnb_ref, ws_ref, bs_ref, w_ref, o_ref, t_ref):
    tm = x_ref.shape[0]
    v = y_ref[:, GM_DH:]
    vc = v - jnp.mean(v, axis=-1, keepdims=True)
    vn = vc * lax.rsqrt(jnp.mean(vc * vc, axis=-1, keepdims=True) + EPS) * lng_ref[...] + lnb_ref[...]
    vn = vn.astype(BF16)
    for n in range(tm // GM_CHUNK):
        rows = slice(n * GM_CHUNK, (n + 1) * GM_CHUNK)
        for g in range(GM_G):
            cols = slice(g * GM_CG, (g + 1) * GM_CG)
            sp = jnp.dot(ws_ref[g], vn[rows, cols], preferred_element_type=F32) + bs_ref[:, cols]
            t_ref[rows, cols] = (y_ref[rows, cols] * sp).astype(BF16)
    gate = mod_ref[5:6, :]
    o_ref[...] = x_ref[...] + gate * jnp.dot(t_ref[...], w_ref[...], preferred_element_type=F32)


def _gmlp_out(x, y, mod_l, ln_g, ln_b, w_s, b_s, w_out):
    tm = TM_PROJ
    bias = jnp.repeat(b_s.T, GM_CG, axis=1)
    return pl.pallas_call(
        _gmlp_kernel,
        grid=(N_TOK // tm,),
        in_specs=[
            pl.BlockSpec((tm, D_MODEL), lambda i: (i, 0)),
            pl.BlockSpec((tm, 2 * GM_DH), lambda i: (i, 0)),
            pl.BlockSpec((None, N_MOD, D_MODEL), lambda i: (_group_of_tile(i, tm), 0, 0)),
            pl.BlockSpec((1, GM_DH), lambda i: (0, 0)),
            pl.BlockSpec((1, GM_DH), lambda i: (0, 0)),
            _resident((GM_G, GM_CHUNK, GM_CHUNK)),
            _resident((GM_CHUNK, GM_DH)),
            _resident((GM_DH, D_MODEL)),
        ],
        out_specs=pl.BlockSpec((tm, D_MODEL), lambda i: (i, 0)),
        out_shape=jax.ShapeDtypeStruct((N_TOK, D_MODEL), F32),
        scratch_shapes=[pltpu.VMEM((tm, GM_DH), BF16)],
        compiler_params=pltpu.CompilerParams(dimension_semantics=("arbitrary",),
                                             vmem_limit_bytes=VMEM_LIMIT),
        name="gmlp_gate_out",
    )(x, y, mod_l, ln_g.reshape(1, GM_DH), ln_b.reshape(1, GM_DH), w_s.astype(BF16), bias,
      w_out.astype(BF16))


SSD_E = SSD_H // SSD_G
SSD_GP = SSD_E * SSD_P
SSD_BLK = 2 * SSD_CHUNK
SSD_COL_X = SSD_DI
SSD_COL_B = 2 * SSD_DI
SSD_COL_C = SSD_COL_B + SSD_G * SSD_N
SSD_COL_DT = SSD_COL_C + SSD_G * SSD_N


def _ssd_w_dt(w_in):
    base = SSD_DI + SSD_XBC
    w_dt = jnp.zeros((D_MODEL, SSD_G, LANE), F32)
    for g in range(SSD_G):
        cols = jnp.concatenate([w_in[:, base + g * SSD_E:base + (g + 1) * SSD_E],
                                w_in[:, base + SSD_H + g * SSD_E:base + SSD_H + (g + 1) * SSD_E]], axis=1)
        w_dt = w_dt.at[:, g, :2 * SSD_E].set(cols).at[:, g, 2 * SSD_E:4 * SSD_E].set(cols)
    return w_dt.reshape(D_MODEL, SSD_G * LANE)


def _ssd_cum_matrix():
    r = np.arange(SSD_BLK)
    same = (r[:, None] // SSD_CHUNK) == (r[None, :] // SSD_CHUNK)
    cum = np.concatenate([same & (r[None, :] <= r[:, None]), same & (r[None, :] >= r[:, None])], axis=0)
    return jnp.asarray(cum, BF16)


def _softplus(x):
    return jnp.maximum(x, 0.0) + jnp.log1p(jnp.exp(-jnp.abs(x)))


def _ssd_kernel(*refs, has_s0):
    it = iter(refs)
    (z_ref, x_ref, b_ref, c_ref, dt_ref, cwx_ref, cwb_ref, cwc_ref, cbx_ref, cbb_ref, cbc_ref,
     dtb_ref, alog_ref, dsk_ref, cum_ref) = (next(it) for _ in range(15))
    s0_ref = next(it) if has_s0 else None
    o_ref = next(it)
    st_ref = None if has_s0 else next(it)
    xs_s, xb_s, bm_s, cm_s, cu_s, dt_s, y_s, sf_s, sb_s = it

    T = x_ref.shape[0]
    nch = T // SSD_CHUNK
    L = SSD_CHUNK

    trow = lax.broadcasted_iota(jnp.int32, (T, 1), 0)

    def conv_silu(v_ref, w_ref, bias_ref):
        v = v_ref[...]
        prev = jnp.where(trow == 0, 0.0, pltpu.roll(v, 1, axis=0))
        nxt = jnp.where(trow == T - 1, 0.0, pltpu.roll(v, T - 1, axis=0))
        y = prev * w_ref[0:1, :] + v * w_ref[1:2, :] + nxt * w_ref[2:3, :] + bias_ref[...]
        return y * jax.nn.sigmoid(y)

    xs = conv_silu(x_ref, cwx_ref, cbx_ref)
    xs_s[...] = xs
    xb_s[...] = xs.astype(BF16)
    bm_s[...] = conv_silu(b_ref, cwb_ref, cbb_ref).astype(BF16)
    cm_s[...] = conv_silu(c_ref, cwc_ref, cbc_ref).astype(BF16)

    lane1 = lax.broadcasted_iota(jnp.int32, (1, LANE), 1)
    a_row = jnp.where(lane1 < 2 * SSD_E, -jnp.exp(alog_ref[...]), 0.0)
    for blk in range(T // SSD_BLK):
        rows = slice(blk * SSD_BLK, (blk + 1) * SSD_BLK)
        dt = _softplus(dt_ref[rows, :] + dtb_ref[...])
        c2 = _dot3(cum_ref[...], dt * a_row)
        cu_s[rows, :] = jnp.where(lane1 < SSD_E, c2[:SSD_BLK], c2[SSD_BLK:])
        dt_s[rows, :] = dt

    ii = lax.broadcasted_iota(jnp.int32, (L, LANE), 0)
    jj = lax.broadcasted_iota(jnp.int32, (L, LANE), 1)
    fwd_half = jj < L
    fwd_half1 = lane1 < L
    tri = (fwd_half & (ii >= jj)) | ((jj >= L) & (ii <= jj - L))
    left = lane1 < SSD_P
    for c in range(nch):
        rows = slice(c * L, (c + 1) * L)
        cum_c, dt_c = cu_s[rows, :], dt_s[rows, :]
        bm_c, cm_c = bm_s[rows, :], cm_s[rows, :]
        cb2 = _dot_nt(cm_c, jnp.concatenate([bm_c, bm_c], axis=0))
        arr = jnp.where(lane1 < 2 * SSD_E, cum_c, dt_c)
        arr_t = jnp.concatenate([arr, arr], axis=0).T
        gs = []
        for e in range(SSD_E):
            row_c = jnp.where(fwd_half1, arr_t[e:e + 1, :], arr_t[SSD_E + e:SSD_E + e + 1, :])
            row_dt = jnp.where(fwd_half1, arr_t[2 * SSD_E + e:2 * SSD_E + e + 1, :],
                               arr_t[3 * SSD_E + e:3 * SSD_E + e + 1, :])
            col_c = jnp.where(fwd_half, jnp.broadcast_to(cum_c[:, e:e + 1], (L, LANE)),
                              jnp.broadcast_to(cum_c[:, SSD_E + e:SSD_E + e + 1], (L, LANE)))
            dec = jnp.exp(jnp.where(tri, col_c - row_c, -jnp.inf))
            gs.append((cb2 * dec * row_dt).astype(BF16))
        for pr in range(SSD_E // 2):
            cols = slice(pr * LANE, (pr + 1) * LANE)
            xp = xb_s[rows, cols]
            xl = jnp.where(left, xp, jnp.zeros_like(xp))
            xr = jnp.where(left, jnp.zeros_like(xp), xp)
            lhs = jnp.concatenate([gs[2 * pr], gs[2 * pr + 1]], axis=1)
            rhs = jnp.concatenate([xl, xl, xr, xr], axis=0)
            y_s[rows, cols] = (jnp.dot(lhs, rhs, preferred_element_type=F32)
                               + dsk_ref[:, cols] * xs_s[rows, cols])

    if has_s0:
        sf_s[...] = s0_ref[0].T
        sb_s[...] = s0_ref[1].T
    else:
        sf_s[...] = jnp.zeros_like(sf_s)
        sb_s[...] = jnp.zeros_like(sb_s)

    def per_head_cols(v, lane0):
        tiles = []
        for t in range(SSD_E // 2):
            a, b = lane0 + 2 * t, lane0 + 2 * t + 1
            tiles.append(jnp.where(left, jnp.broadcast_to(v[:, a:a + 1], (L, LANE)),
                                   jnp.broadcast_to(v[:, b:b + 1], (L, LANE))))
        return jnp.concatenate(tiles, axis=1)

    def chain(c, last, lane0, st_s):
        rows = slice(c * L, (c + 1) * L)
        cum_c = jnp.where((lane1 >= lane0) & (lane1 < lane0 + SSD_E), cu_s[rows, :], 0.0)
        tot = cum_c[last:last + 1, :]
        qd = per_head_cols(jnp.exp(cum_c), lane0)
        w = per_head_cols(jnp.exp(tot - cum_c) * dt_s[rows, :], lane0)
        s = st_s[...]
        y_s[rows, :] += jnp.dot(cm_s[rows, :], s.astype(BF16), preferred_element_type=F32) * qd
        xw = (xs_s[rows, :] * w).astype(BF16)
        st_s[...] = s * qd[last:last + 1, :] + _dot_tn(bm_s[rows, :], xw)

    for n in range(nch):
        chain(n, L - 1, 0, sf_s)
        chain(nch - 1 - n, 0, SSD_E, sb_s)

    z = z_ref[...]
    o_ref[...] = y_s[...] * (z * jax.nn.sigmoid(z))
    if not has_s0:
        st_ref[0] = sf_s[...].T
        st_ref[1] = sb_s[...].T


def _ssd_call(y, prm, T, n_seq, row0, s0):
    has_s0 = s0 is not None
    rb = row0 // T
    col = lambda w, off: pl.BlockSpec((T, w), lambda b, g: (rb + b, off // w + g))
    wcol = lambda rows, w, off: pl.BlockSpec((rows, w), lambda b, g: (0, off // w + g))
    per_g = lambda w: pl.BlockSpec((None, 1, w), lambda b, g: (g, 0, 0))
    cum = _ssd_cum_matrix()
    xoff, boff, coff = 0, SSD_DI, SSD_DI + SSD_G * SSD_N
    in_specs = [col(SSD_GP, 0), col(SSD_GP, SSD_COL_X), col(SSD_N, SSD_COL_B), col(SSD_N, SSD_COL_C),
                col(LANE, SSD_COL_DT),
                wcol(SSD_CONV, SSD_GP, xoff), wcol(SSD_CONV, SSD_N, boff), wcol(SSD_CONV, SSD_N, coff),
                wcol(1, SSD_GP, xoff), wcol(1, SSD_N, boff), wcol(1, SSD_N, coff),
                per_g(LANE), per_g(LANE), per_g(SSD_GP),
                pl.BlockSpec(cum.shape, lambda b, g: (0, 0))]
    args = [y, y, y, y, y, prm["conv_w"], prm["conv_w"], prm["conv_w"], prm["conv_b"], prm["conv_b"], prm["conv_b"],
            prm["dt_bias"], prm["a_log"], prm["d_skip"], cum]
    st_spec = pl.BlockSpec((None, 2, None, SSD_GP, SSD_N), lambda b, g: (b, 0, g, 0, 0))
    out_specs = [pl.BlockSpec((T, SSD_GP), lambda b, g: (b, g))]
    out_shape = [jax.ShapeDtypeStruct((n_seq * T, SSD_DI), F32)]
    if has_s0:
        in_specs.append(st_spec)
        args.append(s0)
    else:
        out_specs.append(st_spec)
        out_shape.append(jax.ShapeDtypeStruct((n_seq, 2, SSD_G, SSD_GP, SSD_N), F32))
    scratch = [pltpu.VMEM((T, SSD_GP), F32), pltpu.VMEM((T, SSD_GP), BF16),
               pltpu.VMEM((T, SSD_N), BF16), pltpu.VMEM((T, SSD_N), BF16),
               pltpu.VMEM((T, LANE), F32), pltpu.VMEM((T, LANE), F32), pltpu.VMEM((T, SSD_GP), F32),
               pltpu.VMEM((SSD_N, SSD_GP), F32), pltpu.VMEM((SSD_N, SSD_GP), F32)]
    outs = pl.pallas_call(
        functools.partial(_ssd_kernel, has_s0=has_s0),
        grid=(n_seq, SSD_G),
        in_specs=in_specs, out_specs=out_specs, out_shape=out_shape,
        scratch_shapes=scratch,
        compiler_params=pltpu.CompilerParams(dimension_semantics=("arbitrary", "arbitrary"),
                                             vmem_limit_bytes=VMEM_LIMIT),
        name="ssd_state" if has_s0 else "ssd",
    )(*args)
    return outs[0], (None if has_s0 else outs[1])


def _ssd_mixer(y, state, conv_w, conv_b, dt_bias, a_log, d_skip):
    def lanes(p):
        pg = p.reshape(2, SSD_G, SSD_E).transpose(1, 0, 2).reshape(SSD_G, 2 * SSD_E)
        return jnp.pad(jnp.concatenate([pg, pg], axis=1), ((0, 0), (0, LANE - 4 * SSD_E))).reshape(SSD_G, 1, LANE)
    prm = dict(conv_w=conv_w, conv_b=conv_b.reshape(1, SSD_XBC), dt_bias=lanes(dt_bias), a_log=lanes(a_log),
               d_skip=jnp.repeat(d_skip, SSD_P).reshape(SSD_G, 1, SSD_GP))
    op, st = _ssd_call(y, prm, SEQ, BATCH, 0, None)
    s0 = state.reshape(DEC_BATCH, 2, SSD_G, SSD_GP, SSD_N)
    os_, _ = _ssd_call(y, prm, DEC_SEQ, DEC_BATCH, N_PROMPT, s0)
    return (op, os_), st.reshape(BATCH, 2, SSD_H, SSD_P, SSD_N)


def kernel(x_prompt, x_sample, state_gla, cache_nat_k, cache_nat_v, state_ssd, c,
           c_ctx, norm_g, w_ada, b_ada, w_ffn_in, w_ffn_out,
           gla_w_in, gla_w_a1, gla_w_a2, gla_b_a, gla_norm_g, gla_w_out,
           nat_w_qkv, nat_rpb, nat_w_out,
           gm_w_in, gm_ln_g, gm_ln_b, gm_w_s, gm_b_s, gm_w_out,
           ssd_w_in, ssd_conv_w, ssd_conv_b, ssd_dt_bias, ssd_a_log, ssd_d, ssd_norm_g, ssd_w_out,
           final_g):
    x = (x_prompt.reshape(N_PROMPT, D_MODEL), x_sample.reshape(N_SAMPLE, D_MODEL))
    mod = _modulation_all(c, c_ctx, w_ada, b_ada)
    new_gla, new_k, new_v, new_ssd = [], [], [], []
    for l in range(DEPTH):
        kind, j = l % N_MIXERS, l // N_MIXERS
        x = _ffn(x, mod[l], norm_g[l, 0], w_ffn_in, w_ffn_out, l, 0, 0)
        mix = None
        if kind == 0:
            w_rank = jnp.pad(jnp.concatenate([gla_w_a1[j, 0], gla_w_a1[j, 1]], axis=1),
                             ((0, 0), (0, LANE - 2 * GLA_RANK)))
            y = _proj_in(x, mod[l], norm_g[l, 1], [(gla_w_in[j], GLA_IN), (w_rank, LANE)], 512)
            o, st = _gla_mixer(y, state_gla[:, j], gla_w_a2[j], gla_b_a[j], gla_norm_g[j])
            new_gla.append(st)
            mix = (o, gla_w_out[j], None)
        elif kind == 1:
            y = _proj_in(x, mod[l], norm_g[l, 1], [(nat_w_qkv[j], 3 * D_MODEL)], 768)
            op, kc, vc = _nat_context(y)
            os_ = _nat_latent(y, _heads_last(cache_nat_k[:, j]), _heads_last(cache_nat_v[:, j]), nat_rpb[j])
            new_k.append(kc)
            new_v.append(vc)
            mix = ((op, os_), nat_w_out[j], None)
        elif kind == 2:
            y = _proj_in(x, mod[l], norm_g[l, 1], [(gm_w_in[j], 2 * GM_DH)], 512, act="gelu")
            x = _gmlp_out(x, y, mod[l], gm_ln_g[j], gm_ln_b[j], gm_w_s[j], gm_b_s[j], gm_w_out[j])
        else:
            y = _proj_in(x, mod[l], norm_g[l, 1],
                         [(ssd_w_in[j], SSD_DI + SSD_XBC), (_ssd_w_dt(ssd_w_in[j]), SSD_G * LANE)], 512)
            o, st = _ssd_mixer(y, state_ssd[:, j], ssd_conv_w[j], ssd_conv_b[j], ssd_dt_bias[j],
                               ssd_a_log[j], ssd_d[j])
            new_ssd.append(st)
            mix = (o, ssd_w_out[j], ssd_norm_g[j])
        x = _ffn(x, mod[l], norm_g[l, 2], w_ffn_in, w_ffn_out, l, 1, 2, mix=mix,
                 final_g=final_g if l == DEPTH - 1 else None)
    y_prompt = x[0].reshape(BATCH, SEQ, D_MODEL)
    y_sample = x[1].reshape(DEC_BATCH, DEC_SEQ, D_MODEL)
    return (y_prompt, y_sample, jnp.stack(new_gla, axis=1), jnp.stack(new_k, axis=1),
            jnp.stack(new_v, axis=1), jnp.stack(new_ssd, axis=1))
```

```python
import functools

import jax
import jax.numpy as jnp
import numpy as np
from jax import lax
from jax.experimental import pallas as pl
from jax.experimental.pallas import tpu as pltpu

D_MODEL = 1024
BATCH = 32
SEQ = 256
DEPTH = 4
DEC_BATCH = 2
DEC_SEQ = 1024
N_PROMPT = BATCH * SEQ
N_SAMPLE = DEC_BATCH * DEC_SEQ
N_TOK = N_PROMPT + N_SAMPLE
N_GROUPS = 1 + DEC_BATCH

GRID_W = 64
N_MIXERS = 4
N_SUB = 3
N_MOD = 3 * N_SUB
D_FF = 2816
EPS = 1e-6
NEG_INF = -1e30
ROPE_THETA = 10000.0
GLA_H, GLA_DK, GLA_DV, GLA_RANK, GLA_TAU, GLA_CHUNK = 4, 128, 256, 16, 16.0, 16
GLA_IN = 2 * GLA_H * GLA_DK + 2 * GLA_H * GLA_DV
NAT_H, NAT_HD, NAT_WH, NAT_WW = 16, 64, 8, 16
GM_DH, GM_G, GM_CHUNK = 1024, 8, 128
GM_CG = GM_DH // GM_G
SSD_DI = 2 * D_MODEL
SSD_P = 64
SSD_H = SSD_DI // SSD_P
SSD_N, SSD_G, SSD_CONV, SSD_CHUNK = 128, 4, 3, 64
SSD_XBC = SSD_DI + 2 * SSD_G * SSD_N

LANE = 128
VMEM_LIMIT = 56 * 1024 * 1024
BF16 = jnp.bfloat16
F32 = jnp.float32

FF_CHUNK = 256
N_FF_CHUNKS = D_FF // FF_CHUNK
TM_FFN = 512
TM_PROJ = 512
ADA_TK = 128


def _group_of_tile(i, tm):
    n_prompt_tiles = N_PROMPT // tm
    return jnp.where(i < n_prompt_tiles, 0, 1 + (i - n_prompt_tiles) // (DEC_SEQ // tm))


def _resident(shape):
    nd = len(shape)
    return pl.BlockSpec(shape, lambda i: (0,) * nd, pipeline_mode=pl.Buffered(1))


def _stream_rows(tm, width, single_buffer_latent=False):
    npt = N_PROMPT // tm
    mode = dict(pipeline_mode=pl.Buffered(1)) if single_buffer_latent else {}
    return (pl.BlockSpec((tm, width), lambda i: (jnp.minimum(i, npt - 1), 0)),
            pl.BlockSpec((tm, width), lambda i: (jnp.maximum(i - npt, 0), 0), **mode))


def _premod(x, g, mod_ref, k):
    shift = mod_ref[3 * k:3 * k + 1, :]
    scale = mod_ref[3 * k + 1:3 * k + 2, :]
    gate = mod_ref[3 * k + 2:3 * k + 3, :]
    ms = jnp.mean(x * x, axis=-1, keepdims=True)
    h = x * lax.rsqrt(ms + EPS) * g
    return h * (1.0 + scale) + shift, gate


def _dot_nt(a, b):
    return lax.dot_general(a, b, (((1,), (1,)), ((), ())), preferred_element_type=F32)


def _dot_tn(a, b):
    return lax.dot_general(a, b, (((0,), (0,)), ((), ())), preferred_element_type=F32)


def _split3(x):
    hi = x.astype(BF16)
    r1 = x - hi.astype(F32)
    mid = r1.astype(BF16)
    lo = (r1 - mid.astype(F32)).astype(BF16)
    return hi, mid, lo


def _dot3(m, x):
    hi, mid, lo = _split3(x)
    d = lambda p: jnp.dot(m, p, preferred_element_type=F32)
    return d(hi) + d(mid) + d(lo)


def _ada_kernel(cond_ref, wa_ref, wb_ref, b_ref, o_ref):
    cnd = cond_ref[...]
    s = (cnd * jax.nn.sigmoid(cnd)).astype(BF16)
    p = jnp.concatenate([jnp.dot(s, w_ref[...].astype(BF16), preferred_element_type=F32)
                         for w_ref in (wa_ref, wb_ref)], axis=1)

    @pl.when(pl.program_id(1) == 0)
    def _():
        o_ref[...] = p + b_ref[...]

    @pl.when(pl.program_id(1) > 0)
    def _():
        o_ref[...] += p


def _modulation_all(c, c_ctx, w_ada, b_ada):
    rows = 8
    nk = D_MODEL // ADA_TK
    cond = jnp.concatenate([c_ctx[None], c, jnp.zeros((rows - N_GROUPS, D_MODEL), F32)], axis=0)
    cond = cond.reshape(rows, nk, ADA_TK).transpose(1, 0, 2)
    n_out = N_MOD * D_MODEL
    out = pl.pallas_call(
        _ada_kernel,
        grid=(DEPTH, nk),
        in_specs=[
            pl.BlockSpec((None, rows, ADA_TK), lambda l, k: (k, 0, 0)),
            pl.BlockSpec((None, ADA_TK, n_out // 2), lambda l, k: (l, k, 0)),
            pl.BlockSpec((None, ADA_TK, n_out // 2), lambda l, k: (l, k, 1)),
            pl.BlockSpec((None, 1, n_out), lambda l, k: (l, 0, 0)),
        ],
        out_specs=pl.BlockSpec((None, rows, n_out), lambda l, k: (l, 0, 0)),
        out_shape=jax.ShapeDtypeStruct((DEPTH, rows, n_out), F32),
        compiler_params=pltpu.CompilerParams(dimension_semantics=("arbitrary", "arbitrary")),
        name="ada_modulation",
    )(cond, w_ada, w_ada, b_ada.reshape(DEPTH, 1, n_out))
    return out[:, :N_GROUPS].reshape(DEPTH, N_GROUPS, N_MOD, D_MODEL)


def _ffn_weight_copies(win_hbm, wout_hbm, st_in, st_out, sem, l, half, j, slot):
    cols = lambda off: pl.ds(off + j * FF_CHUNK, FF_CHUNK)
    return (pltpu.make_async_copy(win_hbm.at[l, half, :, cols(0)], st_in.at[slot, 0], sem.at[slot, 0]),
            pltpu.make_async_copy(win_hbm.at[l, half, :, cols(D_FF)], st_in.at[slot, 1], sem.at[slot, 1]),
            pltpu.make_async_copy(wout_hbm.at[l, half, cols(0), :], st_out.at[slot], sem.at[slot, 2]))


def _ffn_kernel(*refs, l, half, k, split_in, mix, mix_norm, final):
    it = iter(refs)
    x_refs = (next(it), next(it)) if split_in else (next(it),)
    mod_ref, g_ref = next(it), next(it)
    o_in_refs, ng_ref, wmix_ref = ((next(it), next(it)), next(it), next(it)) if mix else (None, None, None)
    win_hbm, wout_hbm = next(it), next(it)
    fg_ref = next(it) if final else None
    out_refs = (next(it), next(it)) if final else (next(it),)
    acc_ref, win_ref, wout_ref, st_in, st_out, sem = it
    i = pl.program_id(0)
    is_prompt = i < N_PROMPT // x_refs[0].shape[0]
    copies = functools.partial(_ffn_weight_copies, win_hbm, wout_hbm, st_in, st_out, sem, l, half)

    def row_tile(fetch_weights):
        if fetch_weights:
            for c in copies(0, 0):
                c.start()
        x = jnp.where(is_prompt, x_refs[0][...], x_refs[1][...]) if split_in else x_refs[0][...]
        if mix:
            o_in = jnp.where(is_prompt, o_in_refs[0][...], o_in_refs[1][...])
            if mix_norm:
                o_in = o_in * lax.rsqrt(jnp.mean(o_in * o_in, axis=-1, keepdims=True) + EPS) * ng_ref[...]
            x = x + mod_ref[5:6, :] * jnp.dot(o_in.astype(BF16), wmix_ref[...], preferred_element_type=F32)
        h, gate = _premod(x, g_ref[...], mod_ref, k)
        hb = h.astype(BF16)
        for j in range(N_FF_CHUNKS):
            gate_cols = slice(j * FF_CHUNK, (j + 1) * FF_CHUNK)
            up_cols = slice(D_FF + j * FF_CHUNK, D_FF + (j + 1) * FF_CHUNK)
            if fetch_weights:
                slot = j % 2
                if j + 1 < N_FF_CHUNKS:
                    for c in copies(j + 1, 1 - slot):
                        c.start()
                for c in copies(j, slot):
                    c.wait()
                store_chunk(j, slot)
            a = jnp.dot(hb, win_ref[:, gate_cols], preferred_element_type=F32)
            u = jnp.dot(hb, win_ref[:, up_cols], preferred_element_type=F32)
            t = (a * jax.nn.sigmoid(a) * u).astype(BF16)
            p = jnp.dot(t, wout_ref[gate_cols, :], preferred_element_type=F32)
            if j == 0:
                acc_ref[...] = p
            else:
                acc_ref[...] += p
        y = x + 0.5 * gate * acc_ref[...]
        if not final:
            out_refs[0][...] = y
        else:
            y = y * lax.rsqrt(jnp.mean(y * y, axis=-1, keepdims=True) + EPS) * fg_ref[...]

            @pl.when(is_prompt)
            def _():
                out_refs[0][...] = y

            @pl.when(jnp.logical_not(is_prompt))
            def _():
                out_refs[1][...] = y

    def store_chunk(j, slot):
        win_ref[:, j * FF_CHUNK:(j + 1) * FF_CHUNK] = st_in[slot, 0].astype(BF16)
        win_ref[:, D_FF + j * FF_CHUNK:D_FF + (j + 1) * FF_CHUNK] = st_in[slot, 1].astype(BF16)
        wout_ref[j * FF_CHUNK:(j + 1) * FF_CHUNK, :] = st_out[slot].astype(BF16)

    def fetch_all_weights():
        for c in copies(0, 0):
            c.start()
        for j in range(N_FF_CHUNKS):
            slot = j % 2
            if j + 1 < N_FF_CHUNKS:
                for c in copies(j + 1, 1 - slot):
                    c.start()
            for c in copies(j, slot):
                c.wait()
            store_chunk(j, slot)

    if mix_norm:
        pl.when(i == 0)(fetch_all_weights)
        row_tile(False)
    else:
        pl.when(i == 0)(functools.partial(row_tile, True))
        pl.when(i > 0)(functools.partial(row_tile, False))


def _ffn(x, mod_l, g, w_in_all, w_out_all, l, half, k, mix=None, final_g=None):
    tm = TM_FFN
    split_in = isinstance(x, tuple)
    row = pl.BlockSpec((tm, D_MODEL), lambda i: (i, 0))
    vec = lambda n: pl.BlockSpec((1, n), lambda i: (0, 0))
    in_specs = list(_stream_rows(tm, D_MODEL)) if split_in else [row]
    args = list(x) if split_in else [x]
    in_specs += [pl.BlockSpec((None, N_MOD, D_MODEL), lambda i: (_group_of_tile(i, tm), 0, 0)), vec(D_MODEL)]
    args += [mod_l, g.reshape(1, D_MODEL)]
    if mix is not None:
        o_in, w_mix, norm_g = mix
        kdim = w_mix.shape[0]
        ng = jnp.ones((1, kdim), F32) if norm_g is None else norm_g.reshape(1, kdim)
        in_specs += [*_stream_rows(tm, kdim, single_buffer_latent=kdim > D_MODEL), vec(kdim),
                     _resident((kdim, D_MODEL))]
        args += [*o_in, ng, w_mix.astype(BF16)]
    in_specs += [pl.BlockSpec(memory_space=pl.ANY), pl.BlockSpec(memory_space=pl.ANY)]
    args += [w_in_all, w_out_all]
    if final_g is not None:
        in_specs.append(vec(D_MODEL))
        args.append(final_g.reshape(1, D_MODEL))
        out_specs = list(_stream_rows(tm, D_MODEL))
        out_shape = [jax.ShapeDtypeStruct((N_PROMPT, D_MODEL), F32), jax.ShapeDtypeStruct((N_SAMPLE, D_MODEL), F32)]
    else:
        out_specs, out_shape = row, jax.ShapeDtypeStruct((N_TOK, D_MODEL), F32)
    return pl.pallas_call(
        functools.partial(_ffn_kernel, l=l, half=half, k=k, split_in=split_in, mix=mix is not None,
                          mix_norm=mix is not None and mix[2] is not None, final=final_g is not None),
        grid=(N_TOK // tm,),
        in_specs=in_specs, out_specs=out_specs, out_shape=out_shape,
        scratch_shapes=[pltpu.VMEM((tm, D_MODEL), F32),
                        pltpu.VMEM((D_MODEL, 2 * D_FF), BF16), pltpu.VMEM((D_FF, D_MODEL), BF16),
                        pltpu.VMEM((2, 2, D_MODEL, FF_CHUNK), F32), pltpu.VMEM((2, FF_CHUNK, D_MODEL), F32),
                        pltpu.SemaphoreType.DMA((2, 3))],
        compiler_params=pltpu.CompilerParams(dimension_semantics=("arbitrary",),
                                             vmem_limit_bytes=VMEM_LIMIT),
        name="ffn_swiglu",
    )(*args)


def _proj_in_kernel(x_ref, mod_ref, g_ref, *refs, widths, tn, act):
    w_refs, o_ref = refs[:-1], refs[-1]
    h, _ = _premod(x_ref[...], g_ref[...], mod_ref, 1)
    hb = h.astype(BF16)
    off = 0
    for w_ref, n_use in zip(w_refs, widths):
        for j0 in range(0, n_use, tn):
            wd = min(tn, n_use - j0)
            y = jnp.dot(hb, w_ref[:, j0:j0 + wd].astype(BF16), preferred_element_type=F32)
            if act == "gelu":
                y = jax.nn.gelu(y)
            o_ref[:, off + j0:off + j0 + wd] = y
        off += n_use


def _proj_in(x, mod_l, g, pieces, tn, act=None):
    widths = tuple(n for _, n in pieces)
    n_out = sum(widths)
    tm = 2 * TM_PROJ if 2 * (2 * TM_PROJ) * n_out * 4 <= 32 * 1024 * 1024 else TM_PROJ
    return pl.pallas_call(
        functools.partial(_proj_in_kernel, widths=widths, tn=tn, act=act),
        grid=(N_TOK // tm,),
        in_specs=[
            pl.BlockSpec((tm, D_MODEL), lambda i: (i, 0)),
            pl.BlockSpec((None, N_MOD, D_MODEL), lambda i: (_group_of_tile(i, tm), 0, 0)),
            pl.BlockSpec((1, D_MODEL), lambda i: (0, 0)),
            *[_resident(w.shape) for w, _ in pieces],
        ],
        out_specs=pl.BlockSpec((tm, n_out), lambda i: (i, 0)),
        out_shape=jax.ShapeDtypeStruct((N_TOK, n_out), F32),
        compiler_params=pltpu.CompilerParams(dimension_semantics=("arbitrary",),
                                             vmem_limit_bytes=VMEM_LIMIT),
        name="mixer_proj_in",
    )(x, mod_l, g.reshape(1, D_MODEL), *[w for w, _ in pieces])


GLA_BLK = 128
GLA_ZCOL = GLA_IN // LANE


def _gla_consts():
    r = np.arange(GLA_BLK)
    same = (r[:, None] // GLA_CHUNK) == (r[None, :] // GLA_CHUNK)
    ri, ci = r[:, None] % GLA_CHUNK, r[None, :] % GLA_CHUNK
    lf = np.concatenate([same & (ci <= ri), same & (ci > ri)], axis=0)
    lb = np.concatenate([same & (ci >= ri), same & (ci < ri)], axis=0)
    rows = np.arange(GLA_CHUNK * GLA_DK)
    sel = (rows[:, None] // GLA_DK) == (np.arange(LANE)[None, :] % GLA_CHUNK)
    return (jnp.asarray(lf, BF16), jnp.asarray(lb, BF16), jnp.asarray(sel, BF16))


def _rope_tables(T):
    half = GLA_DK // 2
    t = np.arange(T)
    inv = ROPE_THETA ** (-np.arange(0, half, 2, dtype=np.float64) / half)
    lane = np.arange(GLA_DK)
    pos = np.where(lane[None, :] < half, (t // GRID_W)[:, None], (t % GRID_W)[:, None])
    ang = pos * inv[lane % (half // 2)][None, :]
    sign = np.where((lane % half) < half // 2, -1.0, 1.0)[None, :]
    return jnp.asarray(np.cos(ang), F32), jnp.asarray(np.sin(ang) * sign, F32)


def _rope_apply(x, cos, sin_signed):
    half = GLA_DK // 2
    lane = lax.broadcasted_iota(jnp.int32, (1, GLA_DK), 1)
    partner = jnp.where((lane % half) < half // 2,
                        pltpu.roll(x, GLA_DK - half // 2, axis=1), pltpu.roll(x, half // 2, axis=1))
    return x * cos + partner * sin_signed


def _gla_kernel(*refs, use_rope, has_s0):
    it = iter(refs)
    q_ref, k_ref, v_ref, r_ref, za_ref, w2_ref, ba_ref, ng_ref, lf_ref, lb_ref, sel_ref = (next(it) for _ in range(11))
    cos_ref, sin_ref = (next(it), next(it)) if use_rope else (None, None)
    s0_ref = next(it) if has_s0 else None
    o_ref = next(it)
    st_ref = None if has_s0 else next(it)
    q_s, k_s, bf_s, bb_s, qdf_s, kdf_s, qdb_s, kdb_s, w_s, sf_s, sb_s = it

    T = q_ref.shape[0]
    nblk, nch = T // GLA_BLK, T // GLA_CHUNK
    hq = lambda h: slice(h * GLA_DK, (h + 1) * GLA_DK)
    hv = lambda h: slice(h * GLA_DV, (h + 1) * GLA_DV)

    for blk in range(nblk):
        rows = slice(blk * GLA_BLK, (blk + 1) * GLA_BLK)
        zab = za_ref[rows, :].astype(BF16)
        for h in range(GLA_H):
            z = jnp.dot(zab, w2_ref[h], preferred_element_type=F32) + ba_ref[h]
            la = (jnp.minimum(z, 0.0) - jnp.log1p(jnp.exp(-jnp.abs(z)))) * (1.0 / GLA_TAU)
            cf = _dot3(lf_ref[...], la[:, :GLA_DK])
            cb = _dot3(lb_ref[...], la[:, GLA_DK:])
            bf, bb = cf[:GLA_BLK], cb[:GLA_BLK]
            q = q_ref[rows, hq(h)] * (GLA_DK ** -0.5)
            k = k_ref[rows, hq(h)]
            if use_rope:
                q = _rope_apply(q, cos_ref[rows, :], sin_ref[rows, :])
                k = _rope_apply(k, cos_ref[rows, :], sin_ref[rows, :])
            q_s[h, rows, :], k_s[h, rows, :] = q, k
            bf_s[h, rows, :], bb_s[h, rows, :] = bf, bb
            qdf_s[h, rows, :] = (q * jnp.exp(bf)).astype(BF16)
            qdb_s[h, rows, :] = (q * jnp.exp(bb)).astype(BF16)
            kdf_s[h, rows, :] = (k * jnp.exp(cf[GLA_BLK:])).astype(BF16)
            kdb_s[h, rows, :] = (k * jnp.exp(cb[GLA_BLK:])).astype(BF16)

    irow = lax.broadcasted_iota(jnp.int32, (GLA_CHUNK, 1), 0)
    lane_c = lax.broadcasted_iota(jnp.int32, (GLA_BLK, LANE), 1) // GLA_CHUNK
    row_c = lax.broadcasted_iota(jnp.int32, (GLA_BLK, LANE), 0) // GLA_CHUNK
    for h in range(GLA_H):
        def intra_chunk(c, carry, h=h):
            rows = pl.ds(pl.multiple_of(c * GLA_CHUNK, GLA_CHUNK), GLA_CHUNK)
            qc, bfc, bbc = q_s[h, rows, :], bf_s[h, rows, :], bb_s[h, rows, :]
            for j in range(GLA_CHUNK):
                row = pl.ds(c * GLA_CHUNK + j, 1)
                e = (jnp.exp(jnp.where(irow >= j, bfc - bf_s[h, row, :], bbc - bb_s[h, row, :]))
                     + jnp.where(irow == j, 1.0, 0.0))
                w_s[rows, j * GLA_DK:(j + 1) * GLA_DK] = (qc * k_s[h, row, :] * e).astype(BF16)
            return carry

        lax.fori_loop(0, nch, intra_chunk, 0)
        a_all = jnp.dot(w_s[...], sel_ref[...], preferred_element_type=F32)
        for blk in range(nblk):
            rows = slice(blk * GLA_BLK, (blk + 1) * GLA_BLK)
            a = jnp.where(lane_c == row_c, a_all[rows, :], 0.0).astype(BF16)
            o_ref[rows, hv(h)] = jnp.dot(a, v_ref[rows, hv(h)].astype(BF16), preferred_element_type=F32)

    for h in range(GLA_H):
        if has_s0:
            sf_s[h] = s0_ref[0, h].T
            sb_s[h] = s0_ref[1, h].T
        else:
            sf_s[h] = jnp.zeros((GLA_DV, GLA_DK), F32)
            sb_s[h] = jnp.zeros((GLA_DV, GLA_DK), F32)

    def chain(h, rows, g_row, qd_s, kd_s, b_s, st_s):
        s = st_s[h]
        o_ref[rows, hv(h)] += _dot_nt(qd_s[h, rows, :], s.astype(BF16))
        u = _dot_tn(v_ref[rows, hv(h)].astype(BF16), kd_s[h, rows, :])
        st_s[h] = s * jnp.exp(b_s[h, g_row, :]) + u

    def inter_chunk(n, carry):
        cf_ = pl.multiple_of(n * GLA_CHUNK, GLA_CHUNK)
        cb_ = pl.multiple_of((nch - 1 - n) * GLA_CHUNK, GLA_CHUNK)
        for h in range(GLA_H):
            chain(h, pl.ds(cf_, GLA_CHUNK), pl.ds(cf_ + GLA_CHUNK - 1, 1), qdf_s, kdf_s, bf_s, sf_s)
            chain(h, pl.ds(cb_, GLA_CHUNK), pl.ds(cb_, 1), qdb_s, kdb_s, bb_s, sb_s)
        return carry

    lax.fori_loop(0, nch, inter_chunk, 0, unroll=8)

    for h in range(GLA_H):
        o = o_ref[:, hv(h)]
        o = o * lax.rsqrt(jnp.mean(o * o, axis=-1, keepdims=True) + EPS) * ng_ref[h]
        r = r_ref[:, hv(h)]
        o_ref[:, hv(h)] = o * (r * jax.nn.sigmoid(r))
        if not has_s0:
            st_ref[0, h] = sf_s[h].T
            st_ref[1, h] = sb_s[h].T


def _gla_call(y, w2, ba, ng, T, n_seq, row0, use_rope, s0):
    has_s0 = s0 is not None
    rb = row0 // T
    nq, nv = GLA_H * GLA_DK, GLA_H * GLA_DV
    mode = dict(pipeline_mode=pl.Buffered(1)) if n_seq <= 2 else {}
    col = lambda w, j: pl.BlockSpec((T, w), lambda b: (rb + b, j), **mode)
    cst = lambda a: pl.BlockSpec(a.shape, lambda b: (0,) * a.ndim)
    lf, lb, sel = _gla_consts()
    in_specs = [col(nq, 0), col(nq, 1), col(nv, 1), col(nv, 2), col(LANE, GLA_ZCOL),
                cst(w2), cst(ba), cst(ng), cst(lf), cst(lb), cst(sel)]
    args = [y, y, y, y, y, w2, ba, ng, lf, lb, sel]
    if use_rope:
        cos, sin = _rope_tables(T)
        in_specs += [cst(cos), cst(sin)]
        args += [cos, sin]
    st_spec = pl.BlockSpec((None, 2, GLA_H, GLA_DK, GLA_DV), lambda b: (b, 0, 0, 0, 0))
    out_specs = [pl.BlockSpec((T, nv), lambda b: (b, 0))]
    out_shape = [jax.ShapeDtypeStruct((n_seq * T, nv), F32)]
    if has_s0:
        in_specs.append(st_spec)
        args.append(s0)
    else:
        out_specs.append(st_spec)
        out_shape.append(jax.ShapeDtypeStruct((n_seq, 2, GLA_H, GLA_DK, GLA_DV), F32))
    scratch = ([pltpu.VMEM((GLA_H, T, GLA_DK), F32)] * 4 + [pltpu.VMEM((GLA_H, T, GLA_DK), BF16)] * 4
               + [pltpu.VMEM((T, GLA_CHUNK * GLA_DK), BF16),
                  pltpu.VMEM((GLA_H, GLA_DV, GLA_DK), F32), pltpu.VMEM((GLA_H, GLA_DV, GLA_DK), F32)])
    outs = pl.pallas_call(
        functools.partial(_gla_kernel, use_rope=use_rope, has_s0=has_s0),
        grid=(n_seq,),
        in_specs=in_specs, out_specs=out_specs, out_shape=out_shape,
        scratch_shapes=scratch,
        compiler_params=pltpu.CompilerParams(dimension_semantics=("arbitrary",),
                                             vmem_limit_bytes=VMEM_LIMIT),
        name="gla_rope" if use_rope else "gla",
    )(*args)
    return outs[0], (None if has_s0 else outs[1])


def _gla_mixer(y, state, w_a2, b_a, norm_g):
    w2 = jnp.zeros((GLA_H, LANE, 2 * GLA_DK), F32)
    for e in range(2):
        we = w_a2[e].reshape(GLA_RANK, GLA_H, GLA_DK).transpose(1, 0, 2)
        w2 = w2.at[:, e * GLA_RANK:(e + 1) * GLA_RANK, e * GLA_DK:(e + 1) * GLA_DK].set(we)
    w2 = w2.astype(BF16)
    ba = b_a.reshape(2, GLA_H, GLA_DK).transpose(1, 0, 2).reshape(GLA_H, 1, 2 * GLA_DK)
    ng = norm_g.reshape(GLA_H, 1, GLA_DV)
    op, st = _gla_call(y, w2, ba, ng, SEQ, BATCH, 0, False, None)
    os_, _ = _gla_call(y, w2, ba, ng, DEC_SEQ, DEC_BATCH, N_PROMPT, True, state)
    return (op, os_), st


N_HEAD_PAIRS = NAT_H // 2
NAT_ROWS = DEC_SEQ // GRID_W
NAT_WIN = NAT_WH * GRID_W
NAT_NDR = 2 * NAT_WH - 1
NAT_NDC = 2 * NAT_WW - 1


def _nat_row_window(r):
    rs = min(max(r - NAT_WH // 2, 0), NAT_ROWS - NAT_WH)
    return rs, r - rs


def _nat_bias_table(rpb):
    qc = np.arange(GRID_W)[:, None]
    kc = np.arange(GRID_W)[None, :]
    c_start = np.clip(qc - NAT_WW // 2, 0, GRID_W - NAT_WW)
    ok = (kc >= c_start) & (kc < c_start + NAT_WW)
    dc = np.clip(kc - qc + NAT_WW - 1, 0, NAT_NDC - 1)
    pick = dc[None] == np.arange(NAT_NDC)[:, None, None]
    pick2 = np.zeros((2, NAT_NDC, GRID_W, 2, GRID_W), np.float32)
    for s in range(2):
        pick2[s, :, :, s, :] = pick
    pick2 = jnp.asarray(pick2.reshape(2 * NAT_NDC, GRID_W, 2 * GRID_W))
    rows2 = jnp.concatenate([rpb[:, :NAT_NDR - 1], rpb[:, 1:]], axis=2)
    t = jnp.einsum('hdy,yql->hdql', rows2, pick2, precision=lax.Precision.HIGHEST)
    return jnp.where(np.tile(ok, (1, 2))[None, None], t, NEG_INF)


def _head_mask(hh):
    lane = lax.broadcasted_iota(jnp.int32, (1, LANE), 1)
    return (lane < NAT_HD) if hh == 0 else (lane >= NAT_HD)


def _nat_ctx_kernel(q_ref, k_ref, v_ref, o_ref, kc_ref, vc_ref):
    for hp in range(N_HEAD_PAIRS):
        cols = slice(hp * LANE, (hp + 1) * LANE)
        q = q_ref[:, cols] * (NAT_HD ** -0.5)
        k, v = k_ref[:, cols], v_ref[:, cols]
        kb, vb = k.astype(BF16), v.astype(BF16)
        q2 = jnp.concatenate([jnp.where(_head_mask(hh), q, 0.0) for hh in range(2)], axis=0).astype(BF16)
        s = _dot_nt(q2, kb)
        p = jnp.exp(s - jnp.max(s, axis=-1, keepdims=True))
        l = jnp.sum(p, axis=-1, keepdims=True)
        o2 = jnp.dot(p.astype(BF16), vb, preferred_element_type=F32) / l
        o_ref[:, cols] = jnp.where(_head_mask(0), o2[:SEQ], o2[SEQ:])
        for hh in range(2):
            kc_ref[2 * hp + hh] = k[:, hh * NAT_HD:(hh + 1) * NAT_HD]
            vc_ref[2 * hp + hh] = v[:, hh * NAT_HD:(hh + 1) * NAT_HD]


def _nat_context(y):
    blk = lambda j: pl.BlockSpec((SEQ, D_MODEL), lambda b: (b, j))
    cache = pl.BlockSpec((None, NAT_H, SEQ, NAT_HD), lambda b: (b, 0, 0, 0))
    cache_shape = jax.ShapeDtypeStruct((BATCH, NAT_H, SEQ, NAT_HD), F32)
    return pl.pallas_call(
        _nat_ctx_kernel,
        grid=(BATCH,),
        in_specs=[blk(0), blk(1), blk(2)],
        out_specs=[pl.BlockSpec((SEQ, D_MODEL), lambda b: (b, 0)), cache, cache],
        out_shape=[jax.ShapeDtypeStruct((N_PROMPT, D_MODEL), F32), cache_shape, cache_shape],
        compiler_params=pltpu.CompilerParams(dimension_semantics=("arbitrary",)),
        name="nat_context",
    )(y, y, y)


def _nat_lat_kernel(q_ref, k_ref, v_ref, ck_ref, cv_ref, tab_ref, o_ref):
    q = q_ref[...] * (NAT_HD ** -0.5)
    qm = [jnp.where(_head_mask(hh), q, 0.0).astype(BF16) for hh in range(2)]
    ckb = ck_ref[...].astype(BF16)
    cvb = cv_ref[...].astype(BF16)
    for r in range(NAT_ROWS):
        rs, off = _nat_row_window(r)
        kw = k_ref[rs * GRID_W:rs * GRID_W + NAT_WIN, :].astype(BF16)
        vw = v_ref[rs * GRID_W:rs * GRID_W + NAT_WIN, :].astype(BF16)
        qr = jnp.concatenate([qm[hh][r * GRID_W:(r + 1) * GRID_W] for hh in range(2)], axis=0)
        bias = jnp.concatenate(
            [jnp.concatenate([tab_ref[hh, w - off + NAT_WH - 1] for w in range(0, NAT_WH, 2)], axis=1)
             for hh in range(2)], axis=0)
        s_lat = _dot_nt(qr, kw) + bias
        s_ctx = _dot_nt(qr, ckb)
        m = jnp.maximum(jnp.max(s_lat, axis=-1, keepdims=True), jnp.max(s_ctx, axis=-1, keepdims=True))
        p_lat = jnp.exp(s_lat - m)
        p_ctx = jnp.exp(s_ctx - m)
        l = jnp.sum(p_lat, axis=-1, keepdims=True) + jnp.sum(p_ctx, axis=-1, keepdims=True)
        o2 = (jnp.dot(p_lat.astype(BF16), vw, preferred_element_type=F32)
              + jnp.dot(p_ctx.astype(BF16), cvb, preferred_element_type=F32)) / l
        o_ref[r * GRID_W:(r + 1) * GRID_W, :] = jnp.where(_head_mask(0), o2[:GRID_W], o2[GRID_W:])


def _nat_latent(y, ck, cv, rpb):
    row0 = N_PROMPT // DEC_SEQ
    blk = lambda off: pl.BlockSpec((DEC_SEQ, LANE), lambda b, hp: (row0 + b, off + hp))
    ctx = pl.BlockSpec((None, ck.shape[1], LANE), lambda b, hp: (b, 0, hp))
    return pl.pallas_call(
        _nat_lat_kernel,
        grid=(DEC_BATCH, N_HEAD_PAIRS),
        in_specs=[blk(0), blk(N_HEAD_PAIRS), blk(2 * N_HEAD_PAIRS), ctx, ctx,
                  pl.BlockSpec((2, NAT_NDR - 1, GRID_W, 2 * GRID_W), lambda b, hp: (hp, 0, 0, 0))],
        out_specs=pl.BlockSpec((DEC_SEQ, LANE), lambda b, hp: (b, hp)),
        out_shape=jax.ShapeDtypeStruct((N_SAMPLE, D_MODEL), F32),
        compiler_params=pltpu.CompilerParams(dimension_semantics=("arbitrary", "arbitrary")),
        name="nat_latent",
    )(y, y, y, ck, cv, _nat_bias_table(rpb))


def _heads_last(t):
    b, h, s, d = t.shape
    return t.transpose(0, 2, 1, 3).reshape(b, s, h * d)


def _gmlp_kernel(x_ref, y_ref, mod_ref, lng_ref, lnb_ref, ws_ref, bs_ref, w_ref, o_ref, t_ref):
    tm = x_ref.shape[0]
    v = y_ref[:, GM_DH:]
    vc = v - jnp.mean(v, axis=-1, keepdims=True)
    vn = vc * lax.rsqrt(jnp.mean(vc * vc, axis=-1, keepdims=True) + EPS) * lng_ref[...] + lnb_ref[...]
    vn = vn.astype(BF16)
    for n in range(tm // GM_CHUNK):
        rows = slice(n * GM_CHUNK, (n + 1) * GM_CHUNK)
        for g in range(GM_G):
            cols = slice(g * GM_CG, (g + 1) * GM_CG)
            sp = jnp.dot(ws_ref[g], vn[rows, cols], preferred_element_type=F32) + bs_ref[:, cols]
            t_ref[rows, cols] = (y_ref[rows, cols] * sp).astype(BF16)
    gate = mod_ref[5:6, :]
    o_ref[...] = x_ref[...] + gate * jnp.dot(t_ref[...], w_ref[...], preferred_element_type=F32)


def _gmlp_out(x, y, mod_l, ln_g, ln_b, w_s, b_s, w_out):
    tm = TM_PROJ
    bias = jnp.repeat(b_s.T, GM_CG, axis=1)
    return pl.pallas_call(
        _gmlp_kernel,
        grid=(N_TOK // tm,),
        in_specs=[
            pl.BlockSpec((tm, D_MODEL), lambda i: (i, 0)),
            pl.BlockSpec((tm, 2 * GM_DH), lambda i: (i, 0)),
            pl.BlockSpec((None, N_MOD, D_MODEL), lambda i: (_group_of_tile(i, tm), 0, 0)),
            pl.BlockSpec((1, GM_DH), lambda i: (0, 0)),
            pl.BlockSpec((1, GM_DH), lambda i: (0, 0)),
            _resident((GM_G, GM_CHUNK, GM_CHUNK)),
            _resident((GM_CHUNK, GM_DH)),
            _resident((GM_DH, D_MODEL)),
        ],
        out_specs=pl.BlockSpec((tm, D_MODEL), lambda i: (i, 0)),
        out_shape=jax.ShapeDtypeStruct((N_TOK, D_MODEL), F32),
        scratch_shapes=[pltpu.VMEM((tm, GM_DH), BF16)],
        compiler_params=pltpu.CompilerParams(dimension_semantics=("arbitrary",),
                                             vmem_limit_bytes=VMEM_LIMIT),
        name="gmlp_gate_out",
    )(x, y, mod_l, ln_g.reshape(1, GM_DH), ln_b.reshape(1, GM_DH), w_s.astype(BF16), bias,
      w_out.astype(BF16))


SSD_E = SSD_H // SSD_G
SSD_GP = SSD_E * SSD_P
SSD_BLK = 2 * SSD_CHUNK
SSD_COL_X = SSD_DI
SSD_COL_B = 2 * SSD_DI
SSD_COL_C = SSD_COL_B + SSD_G * SSD_N
SSD_COL_DT = SSD_COL_C + SSD_G * SSD_N


def _ssd_w_dt(w_in):
    base = SSD_DI + SSD_XBC
    w_dt = jnp.zeros((D_MODEL, SSD_G, LANE), F32)
    for g in range(SSD_G):
        cols = jnp.concatenate([w_in[:, base + g * SSD_E:base + (g + 1) * SSD_E],
                                w_in[:, base + SSD_H + g * SSD_E:base + SSD_H + (g + 1) * SSD_E]], axis=1)
        w_dt = w_dt.at[:, g, :2 * SSD_E].set(cols).at[:, g, 2 * SSD_E:4 * SSD_E].set(cols)
    return w_dt.reshape(D_MODEL, SSD_G * LANE)


def _ssd_cum_matrix():
    r = np.arange(SSD_BLK)
    same = (r[:, None] // SSD_CHUNK) == (r[None, :] // SSD_CHUNK)
    cum = np.concatenate([same & (r[None, :] <= r[:, None]), same & (r[None, :] >= r[:, None])], axis=0)
    return jnp.asarray(cum, BF16)


def _softplus(x):
    return jnp.maximum(x, 0.0) + jnp.log1p(jnp.exp(-jnp.abs(x)))


def _ssd_kernel(*refs, has_s0):
    it = iter(refs)
    (z_ref, x_ref, b_ref, c_ref, dt_ref, cwx_ref, cwb_ref, cwc_ref, cbx_ref, cbb_ref, cbc_ref,
     dtb_ref, alog_ref, dsk_ref, cum_ref) = (next(it) for _ in range(15))
    s0_ref = next(it) if has_s0 else None
    o_ref = next(it)
    st_ref = None if has_s0 else next(it)
    xs_s, xb_s, bm_s, cm_s, cu_s, dt_s, y_s, sf_s, sb_s = it

    T = x_ref.shape[0]
    nch = T // SSD_CHUNK
    L = SSD_CHUNK

    trow = lax.broadcasted_iota(jnp.int32, (T, 1), 0)

    def conv_silu(v_ref, w_ref, bias_ref):
        v = v_ref[...]
        prev = jnp.where(trow == 0, 0.0, pltpu.roll(v, 1, axis=0))
        nxt = jnp.where(trow == T - 1, 0.0, pltpu.roll(v, T - 1, axis=0))
        y = prev * w_ref[0:1, :] + v * w_ref[1:2, :] + nxt * w_ref[2:3, :] + bias_ref[...]
        return y * jax.nn.sigmoid(y)

    xs = conv_silu(x_ref, cwx_ref, cbx_ref)
    xs_s[...] = xs
    xb_s[...] = xs.astype(BF16)
    bm_s[...] = conv_silu(b_ref, cwb_ref, cbb_ref).astype(BF16)
    cm_s[...] = conv_silu(c_ref, cwc_ref, cbc_ref).astype(BF16)

    lane1 = lax.broadcasted_iota(jnp.int32, (1, LANE), 1)
    a_row = jnp.where(lane1 < 2 * SSD_E, -jnp.exp(alog_ref[...]), 0.0)
    for blk in range(T // SSD_BLK):
        rows = slice(blk * SSD_BLK, (blk + 1) * SSD_BLK)
        dt = _softplus(dt_ref[rows, :] + dtb_ref[...])
        c2 = _dot3(cum_ref[...], dt * a_row)
        cu_s[rows, :] = jnp.where(lane1 < SSD_E, c2[:SSD_BLK], c2[SSD_BLK:])
        dt_s[rows, :] = dt

    ii = lax.broadcasted_iota(jnp.int32, (L, LANE), 0)
    jj = lax.broadcasted_iota(jnp.int32, (L, LANE), 1)
    fwd_half = jj < L
    fwd_half1 = lane1 < L
    tri = (fwd_half & (ii >= jj)) | ((jj >= L) & (ii <= jj - L))
    left = lane1 < SSD_P
    for c in range(nch):
        rows = slice(c * L, (c + 1) * L)
        cum_c, dt_c = cu_s[rows, :], dt_s[rows, :]
        bm_c, cm_c = bm_s[rows, :], cm_s[rows, :]
        cb2 = _dot_nt(cm_c, jnp.concatenate([bm_c, bm_c], axis=0))
        arr = jnp.where(lane1 < 2 * SSD_E, cum_c, dt_c)
        arr_t = jnp.concatenate([arr, arr], axis=0).T
        gs = []
        for e in range(SSD_E):
            row_c = jnp.where(fwd_half1, arr_t[e:e + 1, :], arr_t[SSD_E + e:SSD_E + e + 1, :])
            row_dt = jnp.where(fwd_half1, arr_t[2 * SSD_E + e:2 * SSD_E + e + 1, :],
                               arr_t[3 * SSD_E + e:3 * SSD_E + e + 1, :])
            col_c = jnp.where(fwd_half, jnp.broadcast_to(cum_c[:, e:e + 1], (L, LANE)),
                              jnp.broadcast_to(cum_c[:, SSD_E + e:SSD_E + e + 1], (L, LANE)))
            dec = jnp.exp(jnp.where(tri, col_c - row_c, -jnp.inf))
            gs.append((cb2 * dec * row_dt).astype(BF16))
        for pr in range(SSD_E // 2):
            cols = slice(pr * LANE, (pr + 1) * LANE)
            xp = xb_s[rows, cols]
            xl = jnp.where(left, xp, jnp.zeros_like(xp))
            xr = jnp.where(left, jnp.zeros_like(xp), xp)
            lhs = jnp.concatenate([gs[2 * pr], gs[2 * pr + 1]], axis=1)
            rhs = jnp.concatenate([xl, xl, xr, xr], axis=0)
            y_s[rows, cols] = (jnp.dot(lhs, rhs, preferred_element_type=F32)
                               + dsk_ref[:, cols] * xs_s[rows, cols])

    if has_s0:
        sf_s[...] = s0_ref[0].T
        sb_s[...] = s0_ref[1].T
    else:
        sf_s[...] = jnp.zeros_like(sf_s)
        sb_s[...] = jnp.zeros_like(sb_s)

    def per_head_cols(v, lane0):
        tiles = []
        for t in range(SSD_E // 2):
            a, b = lane0 + 2 * t, lane0 + 2 * t + 1
            tiles.append(jnp.where(left, jnp.broadcast_to(v[:, a:a + 1], (L, LANE)),
                                   jnp.broadcast_to(v[:, b:b + 1], (L, LANE))))
        return jnp.concatenate(tiles, axis=1)

    def chain(c, last, lane0, st_s):
        rows = slice(c * L, (c + 1) * L)
        cum_c = jnp.where((lane1 >= lane0) & (lane1 < lane0 + SSD_E), cu_s[rows, :], 0.0)
        tot = cum_c[last:last + 1, :]
        qd = per_head_cols(jnp.exp(cum_c), lane0)
        w = per_head_cols(jnp.exp(tot - cum_c) * dt_s[rows, :], lane0)
        s = st_s[...]
        y_s[rows, :] += jnp.dot(cm_s[rows, :], s.astype(BF16), preferred_element_type=F32) * qd
        xw = (xs_s[rows, :] * w).astype(BF16)
        st_s[...] = s * qd[last:last + 1, :] + _dot_tn(bm_s[rows, :], xw)

    for n in range(nch):
        chain(n, L - 1, 0, sf_s)
        chain(nch - 1 - n, 0, SSD_E, sb_s)

    z = z_ref[...]
    o_ref[...] = y_s[...] * (z * jax.nn.sigmoid(z))
    if not has_s0:
        st_ref[0] = sf_s[...].T
        st_ref[1] = sb_s[...].T


def _ssd_call(y, prm, T, n_seq, row0, s0):
    has_s0 = s0 is not None
    rb = row0 // T
    col = lambda w, off: pl.BlockSpec((T, w), lambda b, g: (rb + b, off // w + g))
    wcol = lambda rows, w, off: pl.BlockSpec((rows, w), lambda b, g: (0, off // w + g))
    per_g = lambda w: pl.BlockSpec((None, 1, w), lambda b, g: (g, 0, 0))
    cum = _ssd_cum_matrix()
    xoff, boff, coff = 0, SSD_DI, SSD_DI + SSD_G * SSD_N
    in_specs = [col(SSD_GP, 0), col(SSD_GP, SSD_COL_X), col(SSD_N, SSD_COL_B), col(SSD_N, SSD_COL_C),
                col(LANE, SSD_COL_DT),
                wcol(SSD_CONV, SSD_GP, xoff), wcol(SSD_CONV, SSD_N, boff), wcol(SSD_CONV, SSD_N, coff),
                wcol(1, SSD_GP, xoff), wcol(1, SSD_N, boff), wcol(1, SSD_N, coff),
                per_g(LANE), per_g(LANE), per_g(SSD_GP),
                pl.BlockSpec(cum.shape, lambda b, g: (0, 0))]
    args = [y, y, y, y, y, prm["conv_w"], prm["conv_w"], prm["conv_w"], prm["conv_b"], prm["conv_b"], prm["conv_b"],
            prm["dt_bias"], prm["a_log"], prm["d_skip"], cum]
    st_spec = pl.BlockSpec((None, 2, None, SSD_GP, SSD_N), lambda b, g: (b, 0, g, 0, 0))
    out_specs = [pl.BlockSpec((T, SSD_GP), lambda b, g: (b, g))]
    out_shape = [jax.ShapeDtypeStruct((n_seq * T, SSD_DI), F32)]
    if has_s0:
        in_specs.append(st_spec)
        args.append(s0)
    else:
        out_specs.append(st_spec)
        out_shape.append(jax.ShapeDtypeStruct((n_seq, 2, SSD_G, SSD_GP, SSD_N), F32))
    scratch = [pltpu.VMEM((T, SSD_GP), F32), pltpu.VMEM((T, SSD_GP), BF16),
               pltpu.VMEM((T, SSD_N), BF16), pltpu.VMEM((T, SSD_N), BF16),
               pltpu.VMEM((T, LANE), F32), pltpu.VMEM((T, LANE), F32), pltpu.VMEM((T, SSD_GP), F32),
               pltpu.VMEM((SSD_N, SSD_GP), F32), pltpu.VMEM((SSD_N, SSD_GP), F32)]
    outs = pl.pallas_call(
        functools.partial(_ssd_kernel, has_s0=has_s0),
        grid=(n_seq, SSD_G),
        in_specs=in_specs, out_specs=out_specs, out_shape=out_shape,
        scratch_shapes=scratch,
        compiler_params=pltpu.CompilerParams(dimension_semantics=("arbitrary", "arbitrary"),
                                             vmem_limit_bytes=VMEM_LIMIT),
        name="ssd_state" if has_s0 else "ssd",
    )(*args)
    return outs[0], (None if has_s0 else outs[1])


def _ssd_mixer(y, state, conv_w, conv_b, dt_bias, a_log, d_skip):
    def lanes(p):
        pg = p.reshape(2, SSD_G, SSD_E).transpose(1, 0, 2).reshape(SSD_G, 2 * SSD_E)
        return jnp.pad(jnp.concatenate([pg, pg], axis=1), ((0, 0), (0, LANE - 4 * SSD_E))).reshape(SSD_G, 1, LANE)
    prm = dict(conv_w=conv_w, conv_b=conv_b.reshape(1, SSD_XBC), dt_bias=lanes(dt_bias), a_log=lanes(a_log),
               d_skip=jnp.repeat(d_skip, SSD_P).reshape(SSD_G, 1, SSD_GP))
    op, st = _ssd_call(y, prm, SEQ, BATCH, 0, None)
    s0 = state.reshape(DEC_BATCH, 2, SSD_G, SSD_GP, SSD_N)
    os_, _ = _ssd_call(y, prm, DEC_SEQ, DEC_BATCH, N_PROMPT, s0)
    return (op, os_), st.reshape(BATCH, 2, SSD_H, SSD_P, SSD_N)


def kernel(x_prompt, x_sample, state_gla, cache_nat_k, cache_nat_v, state_ssd, c,
           c_ctx, norm_g, w_ada, b_ada, w_ffn_in, w_ffn_out,
           gla_w_in, gla_w_a1, gla_w_a2, gla_b_a, gla_norm_g, gla_w_out,
           nat_w_qkv, nat_rpb, nat_w_out,
           gm_w_in, gm_ln_g, gm_ln_b, gm_w_s, gm_b_s, gm_w_out,
           ssd_w_in, ssd_conv_w, ssd_conv_b, ssd_dt_bias, ssd_a_log, ssd_d, ssd_norm_g, ssd_w_out,
           final_g):
    x = (x_prompt.reshape(N_PROMPT, D_MODEL), x_sample.reshape(N_SAMPLE, D_MODEL))
    mod = _modulation_all(c, c_ctx, w_ada, b_ada)
    new_gla, new_k, new_v, new_ssd = [], [], [], []
    for l in range(DEPTH):
        kind, j = l % N_MIXERS, l // N_MIXERS
        x = _ffn(x, mod[l], norm_g[l, 0], w_ffn_in, w_ffn_out, l, 0, 0)
        mix = None
        if kind == 0:
            w_rank = jnp.pad(jnp.concatenate([gla_w_a1[j, 0], gla_w_a1[j, 1]], axis=1),
                             ((0, 0), (0, LANE - 2 * GLA_RANK)))
            y = _proj_in(x, mod[l], norm_g[l, 1], [(gla_w_in[j], GLA_IN), (w_rank, LANE)], 512)
            o, st = _gla_mixer(y, state_gla[:, j], gla_w_a2[j], gla_b_a[j], gla_norm_g[j])
            new_gla.append(st)
            mix = (o, gla_w_out[j], None)
        elif kind == 1:
            y = _proj_in(x, mod[l], norm_g[l, 1], [(nat_w_qkv[j], 3 * D_MODEL)], 768)
            op, kc, vc = _nat_context(y)
            os_ = _nat_latent(y, _heads_last(cache_nat_k[:, j]), _heads_last(cache_nat_v[:, j]), nat_rpb[j])
            new_k.append(kc)
            new_v.append(vc)
            mix = ((op, os_), nat_w_out[j], None)
        elif kind == 2:
            y = _proj_in(x, mod[l], norm_g[l, 1], [(gm_w_in[j], 2 * GM_DH)], 512, act="gelu")
            x = _gmlp_out(x, y, mod[l], gm_ln_g[j], gm_ln_b[j], gm_w_s[j], gm_b_s[j], gm_w_out[j])
        else:
            y = _proj_in(x, mod[l], norm_g[l, 1],
                         [(ssd_w_in[j][:, :SSD_DI + SSD_XBC], SSD_DI + SSD_XBC),
                          (_ssd_w_dt(ssd_w_in[j]), SSD_G * LANE)], 512)
            o, st = _ssd_mixer(y, state_ssd[:, j], ssd_conv_w[j], ssd_conv_b[j], ssd_dt_bias[j],
                               ssd_a_log[j], ssd_d[j])
            new_ssd.append(st)
            mix = (o, ssd_w_out[j], ssd_norm_g[j])
        x = _ffn(x, mod[l], norm_g[l, 2], w_ffn_in, w_ffn_out, l, 1, 2, mix=mix,
                 final_g=final_g if l == DEPTH - 1 else None)
    y_prompt = x[0].reshape(BATCH, SEQ, D_MODEL)
    y_sample = x[1].reshape(DEC_BATCH, DEC_SEQ, D_MODEL)
    return (y_prompt, y_sample, jnp.stack(new_gla, axis=1), jnp.stack(new_k, axis=1),
            jnp.stack(new_v, axis=1), jnp.stack(new_ssd, axis=1))
```
